```python
import jax, jax.numpy as jnp
from jax import lax
import numpy as np

D_MODEL = 1024
BATCH = 4
SEQ = 4096
DEPTH = 1

NSA_HEADS = 8
NSA_KV_GROUPS = 2
NSA_HPG = NSA_HEADS // NSA_KV_GROUPS
NSA_HEAD_DIM = 64
NSA_WIDTH = NSA_HEADS * NSA_HEAD_DIM
NSA_KV_WIDTH = NSA_KV_GROUPS * NSA_HEAD_DIM
CMP_BLOCK = 32
CMP_STRIDE = 16
CMP_HIDDEN = 128
SLC_BLOCK = 64
SLC_TOPK = 16
WINDOW = 512
N_BRANCH = 3
FORCED_SCORE = 1.0e4
MLA_HEADS = 8
MLA_NOPE_DIM = 64
MLA_ROPE_DIM = 32
MLA_V_DIM = 64
MLA_WIDTH = MLA_HEADS * MLA_V_DIM
MLA_Q_RANK = 256
MLA_KV_RANK = 128
MIX_WIDTH = NSA_WIDTH + MLA_WIDTH
Q_BLOCK = 128
ROPE_THETA = 10000.0
NORM_EPS = 1e-6
NEG_INF = -1e30

IN_SIZES = (
    NSA_WIDTH,
    NSA_KV_WIDTH, NSA_KV_WIDTH,
    NSA_KV_WIDTH, NSA_KV_WIDTH,
    NSA_KV_WIDTH, NSA_KV_WIDTH,
    NSA_HEADS * N_BRANCH,
    NSA_WIDTH,
    MLA_Q_RANK,
    MLA_KV_RANK,
    MLA_ROPE_DIM,
    MLA_WIDTH,
)
IN_WIDTH = sum(IN_SIZES)
IN_OFFSETS = tuple(int(o) for o in np.cumsum(IN_SIZES)[:-1])

kernel_name = "hymba_nsa_mla_adaln_block"


def rms_norm(x, g):
    xf = x.astype(jnp.float32)
    y = xf * lax.rsqrt(jnp.mean(xf * xf, axis=-1, keepdims=True) + NORM_EPS)
    return (y * g.astype(jnp.float32)).astype(x.dtype)


def rope(x, pos):
    half = x.shape[-1] // 2
    inv = ROPE_THETA ** (-jnp.arange(half, dtype=jnp.float32) / half)
    ang = pos.astype(jnp.float32)[..., None] * inv
    cos = jnp.cos(ang)[:, :, None, :]
    sin = jnp.sin(ang)[:, :, None, :]
    xf = x.astype(jnp.float32)
    x1, x2 = xf[..., :half], xf[..., half:]
    out = jnp.concatenate([x1 * cos - x2 * sin, x2 * cos + x1 * sin], axis=-1)
    return out.astype(x.dtype)


def masked_softmax(s, mask):
    s = jnp.where(mask, s.astype(jnp.float32), NEG_INF)
    p = jax.nn.softmax(s, axis=-1)
    return jnp.where(mask, p, 0.0)


def compress(kv, cmp_pos, w1, w2):
    B, S, G, D = kv.shape
    r = CMP_BLOCK // CMP_STRIDE
    ns = S // CMP_STRIDE
    sub = kv.reshape(B, ns, CMP_STRIDE, G, D)
    nc = ns - r + 1
    blocks = jnp.concatenate([sub[:, j:j + nc] for j in range(r)], axis=2)
    blocks = blocks + cmp_pos[None, None, :, None, :]
    flat = blocks.transpose(0, 1, 3, 2, 4).reshape(B, nc, G, CMP_BLOCK * D)
    return jax.nn.silu(flat @ w1) @ w2


def cmp_to_slc_matrix(nc, nslc):
    start = np.arange(nc)[:, None] * CMP_STRIDE
    bstart = np.arange(nslc)[None, :] * SLC_BLOCK
    ov = np.minimum(start + CMP_BLOCK, bstart + SLC_BLOCK) - np.maximum(start, bstart)
    return (np.clip(ov, 0, None) / CMP_BLOCK).astype(np.float32)


def nsa_attention(q, k_cmp, v_cmp, k_slc, v_slc, k_win, v_win, gate_logits, positions,
                  cmp_pos, cmp_k_w1, cmp_k_w2, cmp_v_w1, cmp_v_w2):
    B, S = q.shape[:2]
    G, HPG, dk = NSA_KV_GROUPS, NSA_HPG, NSA_HEAD_DIM
    scale = dk ** -0.5
    nqb = S // Q_BLOCK
    t = np.arange(S)
    qg = rope(q.reshape(B, S, NSA_HEADS, dk), positions).reshape(B, S, G, HPG, dk)

    kc = compress(k_cmp.reshape(B, S, G, dk), cmp_pos, cmp_k_w1, cmp_k_w2)
    vc = compress(v_cmp.reshape(B, S, G, dk), cmp_pos, cmp_v_w1, cmp_v_w2)
    nc = kc.shape[1]
    cmp_end = np.arange(nc) * CMP_STRIDE + CMP_BLOCK - 1
    kc = rope(kc, positions[:, cmp_end])
    s_cmp = jnp.einsum('bsghd,bngd->bghsn', qg, kc).astype(jnp.float32) * scale
    p_cmp = masked_softmax(s_cmp, cmp_end[None, :] <= t[:, None])
    o_cmp = jnp.einsum('bghsn,bngd->bsghd', p_cmp.astype(vc.dtype), vc)

    nslc = S // SLC_BLOCK
    m = jnp.asarray(cmp_to_slc_matrix(nc, nslc))
    imp = jnp.einsum('bghsn,nj->bgsj', p_cmp, m)
    blk = np.arange(nslc)[None, :]
    cur = (t // SLC_BLOCK)[:, None]
    forced = (blk == 0) | (blk == cur) | (blk == cur - 1)
    imp = jnp.where(forced, FORCED_SCORE, jnp.where(blk > cur, -FORCED_SCORE, imp))
    n_sel = min(SLC_TOPK, nslc)
    _, sel_idx = lax.top_k(imp, n_sel)

    kb = rope(k_slc.reshape(B, S, G, dk), positions).reshape(B, nslc, SLC_BLOCK, G, dk).transpose(0, 3, 1, 2, 4)
    vb = v_slc.reshape(B, nslc, SLC_BLOCK, G, dk).transpose(0, 3, 1, 2, 4)
    q_chunks = qg.reshape(B, nqb, Q_BLOCK, G, HPG, dk).transpose(1, 0, 3, 2, 4, 5)
    idx_chunks = sel_idx.reshape(B, G, nqb, Q_BLOCK, n_sel).transpose(2, 0, 1, 3, 4)
    t_chunks = jnp.arange(S, dtype=jnp.int32).reshape(nqb, Q_BLOCK)
    gather = jax.vmap(jax.vmap(lambda blocks, ix: blocks[ix]))
    sb_offsets = jnp.arange(SLC_BLOCK, dtype=jnp.int32)

    def slc_block(args):
        qc, ic, tc = args
        kg = gather(kb, ic)
        vg = gather(vb, ic)
        s = jnp.einsum('bgqhd,bgqnkd->bgqhnk', qc, kg).astype(jnp.float32) * scale
        key_pos = ic[..., None] * SLC_BLOCK + sb_offsets
        mask = (key_pos <= tc[None, None, :, None, None]).reshape(B, G, Q_BLOCK, 1, n_sel * SLC_BLOCK)
        p = masked_softmax(s.reshape(B, G, Q_BLOCK, HPG, n_sel * SLC_BLOCK), mask)
        p = p.reshape(B, G, Q_BLOCK, HPG, n_sel, SLC_BLOCK)
        return jnp.einsum('bgqhnk,bgqnkd->bqghd', p.astype(vg.dtype), vg)

    o_slc = lax.map(slc_block, (q_chunks, idx_chunks, t_chunks))
    o_slc = o_slc.transpose(1, 0, 2, 3, 4, 5).reshape(B, S, G, HPG, dk)

    n_back = WINDOW // Q_BLOCK

    def band(a):
        ap = jnp.pad(a, ((0, 0), (WINDOW, 0), (0, 0), (0, 0)))
        ab = ap.reshape(B, nqb + n_back, Q_BLOCK, G, dk)
        return jnp.concatenate([ab[:, j:j + nqb] for j in range(n_back + 1)], axis=2)

    kband = band(rope(k_win.reshape(B, S, G, dk), positions))
    vband = band(v_win.reshape(B, S, G, dk))
    qw = qg.reshape(B, nqb, Q_BLOCK, G, HPG, dk)
    s_win = jnp.einsum('bnqghd,bnkgd->bnghqk', qw, kband).astype(jnp.float32) * scale
    q_idx = np.arange(nqb)[:, None] * Q_BLOCK + np.arange(Q_BLOCK)[None, :]
    k_idx = np.arange(nqb)[:, None] * Q_BLOCK - WINDOW + np.arange((n_back + 1) * Q_BLOCK)[None, :]
    diff = q_idx[:, :, None] - k_idx[:, None, :]
    win_mask = (diff >= 0) & (diff < WINDOW) & (k_idx[:, None, :] >= 0)
    p_win = masked_softmax(s_win, win_mask[None, :, None, None])
    o_win = jnp.einsum('bnghqk,bnkgd->bnqghd', p_win.astype(vband.dtype), vband).reshape(B, S, G, HPG, dk)

    g = jax.nn.sigmoid(gate_logits.astype(jnp.float32)).reshape(B, S, G, HPG, N_BRANCH).astype(q.dtype)
    o = g[..., 0:1] * o_cmp + g[..., 1:2] * o_slc + g[..., 2:3] * o_win
    return o.reshape(B, S, NSA_WIDTH)


def mla_attention(c_q, c_kv, k_rope, positions, q_norm_g, w_q_up, kv_norm_g, w_kv_up):
    B, S = c_q.shape[:2]
    H = MLA_HEADS
    nqb = S // Q_BLOCK
    q = (rms_norm(c_q, q_norm_g) @ w_q_up).reshape(B, S, H, MLA_NOPE_DIM + MLA_ROPE_DIM)
    q_nope = q[..., :MLA_NOPE_DIM]
    q_rot = rope(q[..., MLA_NOPE_DIM:], positions)
    kv = (rms_norm(c_kv, kv_norm_g) @ w_kv_up).reshape(B, S, H, MLA_NOPE_DIM + MLA_V_DIM)
    k_nope, v = kv[..., :MLA_NOPE_DIM], kv[..., MLA_NOPE_DIM:]
    k_rot = rope(k_rope[:, :, None, :], positions)[:, :, 0]
    scale = (MLA_NOPE_DIM + MLA_ROPE_DIM) ** -0.5
    qn_c = q_nope.reshape(B, nqb, Q_BLOCK, H, MLA_NOPE_DIM).transpose(1, 0, 2, 3, 4)
    qr_c = q_rot.reshape(B, nqb, Q_BLOCK, H, MLA_ROPE_DIM).transpose(1, 0, 2, 3, 4)
    t_chunks = jnp.arange(S, dtype=jnp.int32).reshape(nqb, Q_BLOCK)
    key_idx = jnp.arange(S, dtype=jnp.int32)

    def mla_block(args):
        qn, qr, tc = args
        s = (jnp.einsum('bqhd,bkhd->bhqk', qn, k_nope)
             + jnp.einsum('bqhd,bkd->bhqk', qr, k_rot)).astype(jnp.float32) * scale
        p = masked_softmax(s, key_idx[None, :] <= tc[:, None])
        return jnp.einsum('bhqk,bkhd->bqhd', p.astype(v.dtype), v)

    o = lax.map(mla_block, (qn_c, qr_c, t_chunks))
    return o.transpose(1, 0, 2, 3, 4).reshape(B, S, MLA_WIDTH)


def setup_inputs(seed: int = 0) -> dict:
    key = jax.random.key(seed)
    ks = jax.random.split(key, 20)
    f32 = jnp.float32
    L, D = DEPTH, D_MODEL

    def nrm(k, shape, fan_in, gain=1.0):
        return jax.random.normal(k, shape, f32) * (gain * fan_in ** -0.5)

    def gain(k, shape):
        return 1.0 + 0.02 * jax.random.normal(k, shape, f32)

    x = jax.random.normal(ks[0], (BATCH, SEQ, D), f32)
    c = jax.random.normal(ks[1], (BATCH, D), f32)
    offset = jax.random.randint(ks[2], (BATCH, 1), 0, 1024, dtype=jnp.int32)
    positions = offset + jnp.arange(SEQ, dtype=jnp.int32)[None, :]
    return {
        "x": x,
        "c": c,
        "positions": positions,
        "ada_w": nrm(ks[3], (L, D, 3 * D), D, 0.5),
        "ada_b": 0.01 * jax.random.normal(ks[4], (L, 3 * D), f32),
        "norm_g": gain(ks[5], (L, D)),
        "w_in": nrm(ks[6], (L, D, IN_WIDTH), D),
        "cmp_pos": 0.02 * jax.random.normal(ks[7], (L, CMP_BLOCK, NSA_HEAD_DIM), f32),
        "cmp_k_w1": nrm(ks[8], (L, CMP_BLOCK * NSA_HEAD_DIM, CMP_HIDDEN), CMP_BLOCK * NSA_HEAD_DIM),
        "cmp_k_w2": nrm(ks[9], (L, CMP_HIDDEN, NSA_HEAD_DIM), CMP_HIDDEN),
        "cmp_v_w1": nrm(ks[10], (L, CMP_BLOCK * NSA_HEAD_DIM, CMP_HIDDEN), CMP_BLOCK * NSA_HEAD_DIM),
        "cmp_v_w2": nrm(ks[11], (L, CMP_HIDDEN, NSA_HEAD_DIM), CMP_HIDDEN),
        "q_norm_g": gain(ks[12], (L, MLA_Q_RANK)),
        "w_q_up": nrm(ks[13], (L, MLA_Q_RANK, MLA_HEADS * (MLA_NOPE_DIM + MLA_ROPE_DIM)), MLA_Q_RANK),
        "kv_norm_g": gain(ks[14], (L, MLA_KV_RANK)),
        "w_kv_up": nrm(ks[15], (L, MLA_KV_RANK, MLA_HEADS * (MLA_NOPE_DIM + MLA_V_DIM)), MLA_KV_RANK),
        "w_out": nrm(ks[16], (L, MIX_WIDTH, D), MIX_WIDTH),
        "final_norm_g": gain(ks[17], (D,)),
    }


def reference(x, c, positions, ada_w, ada_b, norm_g, w_in, cmp_pos, cmp_k_w1, cmp_k_w2,
              cmp_v_w1, cmp_v_w2, q_norm_g, w_q_up, kv_norm_g, w_kv_up, w_out, final_norm_g):
    for l in range(DEPTH):
        mod = jax.nn.silu(c) @ ada_w[l] + ada_b[l]
        shift, scl, gate = jnp.split(mod, 3, axis=-1)
        h = rms_norm(x, norm_g[l]) * (1.0 + scl[:, None, :]) + shift[:, None, :]

        proj = h @ w_in[l]
        (q_n, kc_n, vc_n, ks_n, vs_n, kw_n, vw_n, gl_n, z_nsa,
         cq_m, ckv_m, kr_m, z_mla) = jnp.split(proj, IN_OFFSETS, axis=-1)

        o_nsa = nsa_attention(q_n, kc_n, vc_n, ks_n, vs_n, kw_n, vw_n, gl_n, positions,
                              cmp_pos[l], cmp_k_w1[l], cmp_k_w2[l], cmp_v_w1[l], cmp_v_w2[l])
        o_mla = mla_attention(cq_m, ckv_m, kr_m, positions,
                              q_norm_g[l], w_q_up[l], kv_norm_g[l], w_kv_up[l])

        mixed = jnp.concatenate([o_nsa * jax.nn.silu(z_nsa), o_mla * jax.nn.silu(z_mla)], axis=-1)
        x = x + gate[:, None, :] * (mixed @ w_out[l])
    return rms_norm(x, final_norm_g)
```

```python
import functools

import numpy as np
import jax
import jax.numpy as jnp
from jax import lax
from jax.experimental import pallas as pl
from jax.experimental.pallas import tpu as pltpu

F32 = jnp.float32
BF16 = jnp.bfloat16
I32 = jnp.int32

NSA_HEADS = 8
NSA_KV_GROUPS = 2
NSA_HPG = NSA_HEADS // NSA_KV_GROUPS
NSA_HEAD_DIM = 64
NSA_WIDTH = NSA_HEADS * NSA_HEAD_DIM
NSA_KV_WIDTH = NSA_KV_GROUPS * NSA_HEAD_DIM
CMP_BLOCK = 32
CMP_STRIDE = 16
CMP_HIDDEN = 128
SLC_BLOCK = 64
SLC_TOPK = 16
WINDOW = 512
N_BRANCH = 3
FORCED_SCORE = 1.0e4
MLA_HEADS = 8
MLA_NOPE_DIM = 64
MLA_ROPE_DIM = 32
MLA_V_DIM = 64
MLA_WIDTH = MLA_HEADS * MLA_V_DIM
MLA_Q_RANK = 256
MLA_KV_RANK = 128
MIX_WIDTH = NSA_WIDTH + MLA_WIDTH
ROPE_THETA = 10000.0
NORM_EPS = 1e-6
NEG_INF = -1e30
IN_SIZES = (NSA_WIDTH, NSA_KV_WIDTH, NSA_KV_WIDTH, NSA_KV_WIDTH, NSA_KV_WIDTH, NSA_KV_WIDTH, NSA_KV_WIDTH,
            NSA_HEADS * N_BRANCH, NSA_WIDTH, MLA_Q_RANK, MLA_KV_RANK, MLA_ROPE_DIM, MLA_WIDTH)
IN_OFFSETS = tuple(int(o) for o in np.cumsum(IN_SIZES)[:-1])

LANES = 128
SUBLANES = 8
VMEM_LIMIT_BYTES = 56 * 1024 * 1024

PROJ_TILE = 512
ATT_TILE = 256
SEL_BIAS = -30000.0
GATE_ROWS = 16
MLA_QK = MLA_KV_RANK + MLA_ROPE_DIM
SLC_SHIFT = SLC_BLOCK.bit_length() - 1
assert 1 << SLC_SHIFT == SLC_BLOCK

NT = (((1,), (1,)), ((), ()))
TN = (((0,), (0,)), ((), ()))


def _silu(v):
    return v * jax.nn.sigmoid(v)


def _rms(v, g):
    ms = jnp.mean(v * v, axis=-1, keepdims=True)
    return v * lax.rsqrt(ms + NORM_EPS) * g


def _rope_tok(v, c, s_signed, half, lane):
    up = pltpu.roll(v, LANES - half, axis=1)
    dn = pltpu.roll(v, half, axis=1)
    return v * c + jnp.where(lane < half, up, dn) * s_signed


def _flash_update(s_t, v_t, m_ref, l_ref, acc_ref):
    m_prev = m_ref[...]
    m_new = jnp.maximum(m_prev, jnp.max(s_t, axis=0, keepdims=True))
    alpha = jnp.exp(m_prev - m_new)
    p = jnp.exp(s_t - m_new)
    l_ref[...] = alpha * l_ref[...] + jnp.sum(p, axis=0, keepdims=True)
    acc_ref[...] = alpha * acc_ref[...] + jnp.dot(v_t, p.astype(BF16), preferred_element_type=F32)
    m_ref[...] = m_new


def _adaln_kernel(c_ref, w_ref, b_ref, o_ref):
    sc = _silu(c_ref[...])
    o_ref[...] = jnp.dot(sc.astype(BF16), w_ref[...].astype(BF16), preferred_element_type=F32) + b_ref[...]


def _adaln(c_pad, w, b):
    bp, d = c_pad.shape
    n = w.shape[1] // d
    return pl.pallas_call(
        _adaln_kernel,
        grid=(n,),
        in_specs=[pl.BlockSpec((bp, d), lambda j: (0, 0)),
                  pl.BlockSpec((d, d), lambda j: (0, j)),
                  pl.BlockSpec((1, d), lambda j: (0, j))],
        out_specs=pl.BlockSpec((bp, d), lambda j: (0, j)),
        out_shape=jax.ShapeDtypeStruct((bp, n * d), F32),
        name="adaln",
    )(c_pad, w, b)


def _tok_table_kernel(p_ref, inv_ref, sign_ref, *out_refs):
    p = p_ref[...]
    for k in range(len(out_refs) // 2):
        ang = p * inv_ref[k:k + 1, :]
        out_refs[2 * k][...] = jnp.cos(ang)
        out_refs[2 * k + 1][...] = jnp.sin(ang) * sign_ref[k:k + 1, :]


def _tok_tables(pos_b, inv_lanes, sign_lanes, tile):
    r = pos_b.shape[0]
    npat = inv_lanes.shape[0]
    spec = pl.BlockSpec((tile, LANES), lambda i: (i, 0))
    cst = pl.BlockSpec((npat, LANES), lambda i: (0, 0))
    return pl.pallas_call(
        _tok_table_kernel,
        grid=(r // tile,),
        in_specs=[spec, cst, cst],
        out_specs=[spec] * (2 * npat),
        out_shape=[jax.ShapeDtypeStruct((r, LANES), F32)] * (2 * npat),
        name="rope_tok_tables",
    )(pos_b, inv_lanes, sign_lanes)


def _row_table_kernel(p_ref, inva_ref, invb_ref, ca_ref, sa_ref, cb_ref, sb_ref):
    p = p_ref[0]
    ang = inva_ref[...] * p
    ca_ref[0] = jnp.cos(ang)
    sa_ref[0] = jnp.sin(ang)
    ang = invb_ref[...] * p
    cb_ref[0] = jnp.cos(ang)
    sb_ref[0] = jnp.sin(ang)


def _row_tables(pos_row, inva_b, invb_b, tile):
    b, _, s = pos_row.shape
    fa, fb = inva_b.shape[0], invb_b.shape[0]
    return pl.pallas_call(
        _row_table_kernel,
        grid=(b, s // tile),
        in_specs=[pl.BlockSpec((1, 1, tile), lambda bi, i: (bi, 0, i)),
                  pl.BlockSpec((fa, tile), lambda bi, i: (0, 0)),
                  pl.BlockSpec((fb, tile), lambda bi, i: (0, 0))],
        out_specs=[pl.BlockSpec((1, fa, tile), lambda bi, i: (bi, 0, i))] * 2
        + [pl.BlockSpec((1, fb, tile), lambda bi, i: (bi, 0, i))] * 2,
        out_shape=[jax.ShapeDtypeStruct((b, fa, s), F32)] * 2 + [jax.ShapeDtypeStruct((b, fb, s), F32)] * 2,
        name="rope_row_tables",
    )(pos_row, inva_b, invb_b)


TOK_KS, TOK_KW, TOK_KC, TOK_VC, TOK_CQ, TOK_CKV, TOK_KR, TOK_COLS = 0, 256, 512, 640, 768, 1024, 1152, 1280
TR_Q, TR_VS, TR_VW, TR_G, TR_ZN, TR_ZM, TR_ROWS = 0, 512, 640, 768, 800, 1312, 1824


def _proj_kernel(x_ref, mod_ref, ng_ref, wtok_ref, wtr_ref, ct_ref, st_ref, cm_ref, sm_ref,
                 ctn_ref, stn_ref, ctm_ref, stm_ref, qng_ref, kvng_ref, wqt_ref, wk_ref,
                 qt_ref, kaug_ref, kwin_ref, vts_ref, vtw_ref, kcmp_ref, vcmp_ref, gt_ref,
                 ztn_ref, ztm_ref, qtm_ref, kmla_ref, vtm_ref, *, tm, tk, scale_nsa, scale_mla):
    i = pl.program_id(1)
    mod = mod_ref[0]
    h = _rms(x_ref[0], ng_ref[...]) * (1.0 + mod[1:2]) + mod[0:1]
    hb = h.astype(BF16)
    tok = jnp.dot(hb, wtok_ref[...], preferred_element_type=F32)
    tr = lax.dot_general(wtr_ref[...], hb, NT, preferred_element_type=F32)

    lane = lax.broadcasted_iota(I32, (tm, LANES), 1)
    row = lax.broadcasted_iota(I32, (tm, LANES), 0)
    blk = (i * tm + row) >> SLC_SHIFT
    onehot = (lane - NSA_HEAD_DIM == blk).astype(F32)
    ct, st = ct_ref[...], st_ref[...]
    half_n = NSA_HEAD_DIM // 2
    for g in range(NSA_KV_GROUPS):
        ks = _rope_tok(tok[:, TOK_KS + LANES * g:TOK_KS + LANES * (g + 1)], ct, st, half_n, lane)
        kaug_ref[0, g] = (ks + onehot).astype(BF16)
        kw = _rope_tok(tok[:, TOK_KW + LANES * g:TOK_KW + LANES * (g + 1)], ct, st, half_n, lane)
        kwin_ref[0, g] = kw.astype(BF16)
    kcmp_ref[0] = tok[:, TOK_KC:TOK_KC + LANES]
    vcmp_ref[0] = tok[:, TOK_VC:TOK_VC + LANES]

    ckvn = _rms(tok[:, TOK_CKV:TOK_CKV + MLA_KV_RANK], kvng_ref[...])
    krr = _rope_tok(tok[:, TOK_KR:TOK_KR + LANES], cm_ref[...], sm_ref[...], MLA_ROPE_DIM // 2, lane)
    kmla_ref[0, :, 0:MLA_KV_RANK] = ckvn.astype(BF16)
    kmla_ref[0, :, MLA_KV_RANK:MLA_QK] = krr[:, 0:MLA_ROPE_DIM].astype(BF16)
    ckvt = ckvn.T.astype(BF16)
    for ii in range(tm // tk):
        vtm_ref[0, ii] = ckvt[:, ii * tk:(ii + 1) * tk]

    cqn = _rms(tok[:, TOK_CQ:TOK_CQ + MLA_Q_RANK], qng_ref[...]).astype(BF16)
    qm = lax.dot_general(wqt_ref[...], cqn, NT, preferred_element_type=F32)
    nq = MLA_HEADS * MLA_NOPE_DIM
    hr = MLA_ROPE_DIM // 2
    x1 = qm[nq:nq + MLA_HEADS * hr].reshape(MLA_HEADS, hr, tm)
    x2 = qm[nq + MLA_HEADS * hr:nq + 2 * MLA_HEADS * hr].reshape(MLA_HEADS, hr, tm)
    cm_t, sm_t = ctm_ref[0][None], stm_ref[0][None]
    o1 = (x1 * cm_t - x2 * sm_t) * scale_mla
    o2 = (x2 * cm_t + x1 * sm_t) * scale_mla
    for hd in range(MLA_HEADS):
        qn_h = qm[hd * MLA_NOPE_DIM:(hd + 1) * MLA_NOPE_DIM].astype(BF16)
        qabs = jnp.dot(wk_ref[hd], qn_h, preferred_element_type=F32)
        qtm_ref[0, hd, 0:MLA_KV_RANK, :] = (qabs * scale_mla).astype(BF16)
        qtm_ref[0, hd, MLA_KV_RANK:MLA_KV_RANK + hr, :] = o1[hd].astype(BF16)
        qtm_ref[0, hd, MLA_KV_RANK + hr:MLA_QK, :] = o2[hd].astype(BF16)

    hq = NSA_HEADS * half_n
    q1 = tr[TR_Q:TR_Q + hq].reshape(NSA_HEADS, half_n, tm)
    q2 = tr[TR_Q + hq:TR_Q + 2 * hq].reshape(NSA_HEADS, half_n, tm)
    cn_t, sn_t = ctn_ref[0][None], stn_ref[0][None]
    qt_ref[0, 0:hq, :] = ((q1 * cn_t - q2 * sn_t) * scale_nsa).reshape(hq, tm).astype(BF16)
    qt_ref[0, hq:2 * hq, :] = ((q2 * cn_t + q1 * sn_t) * scale_nsa).reshape(hq, tm).astype(BF16)

    vts = tr[TR_VS:TR_VS + NSA_KV_WIDTH].astype(BF16)
    vtw = tr[TR_VW:TR_VW + NSA_KV_WIDTH].astype(BF16)
    for ii in range(tm // tk):
        vts_ref[0, ii] = vts[:, ii * tk:(ii + 1) * tk]
        vtw_ref[0, ii] = vtw[:, ii * tk:(ii + 1) * tk]
    gt_ref[0] = jax.nn.sigmoid(tr[TR_G:TR_G + NSA_KV_GROUPS * GATE_ROWS])
    ztn_ref[0] = _silu(tr[TR_ZN:TR_ZN + NSA_WIDTH]).astype(BF16)
    ztm_ref[0] = _silu(tr[TR_ZM:TR_ZM + MLA_WIDTH]).astype(BF16)


def _proj(x, mod3, ng, wtok, wtr, tabs_tok, tabs_row, qng, kvng, wqt, wk, *, tm, tk):
    b, s, d = x.shape
    ct, st, cm, sm = tabs_tok
    ctn, stn, ctm, stm = tabs_row
    nt = s // tk
    tile_tok = pl.BlockSpec((tm, LANES), lambda bi, i: (bi * (s // tm) + i, 0))

    def full(a):
        return pl.BlockSpec(a.shape, lambda bi, i, _n=a.ndim: (0,) * _n)

    def rowtab(a):
        return pl.BlockSpec((1, a.shape[1], tm), lambda bi, i: (bi, 0, i))

    in_specs = [pl.BlockSpec((1, tm, d), lambda bi, i: (bi, i, 0)),
                pl.BlockSpec((1, 3, d), lambda bi, i: (bi, 0, 0)),
                full(ng), full(wtok), full(wtr),
                tile_tok, tile_tok, tile_tok, tile_tok,
                rowtab(ctn), rowtab(stn), rowtab(ctm), rowtab(stm),
                full(qng), full(kvng), full(wqt), full(wk)]
    out_shape = [
        jax.ShapeDtypeStruct((b, NSA_WIDTH, s), BF16),
        jax.ShapeDtypeStruct((b, NSA_KV_GROUPS, s, LANES), BF16),
        jax.ShapeDtypeStruct((b, NSA_KV_GROUPS, s, LANES), BF16),
        jax.ShapeDtypeStruct((b, nt, NSA_KV_WIDTH, tk), BF16),
        jax.ShapeDtypeStruct((b, nt, NSA_KV_WIDTH, tk), BF16),
        jax.ShapeDtypeStruct((b, s, NSA_KV_WIDTH), F32),
        jax.ShapeDtypeStruct((b, s, NSA_KV_WIDTH), F32),
        jax.ShapeDtypeStruct((b, NSA_KV_GROUPS * GATE_ROWS, s), F32),
        jax.ShapeDtypeStruct((b, NSA_WIDTH, s), BF16),
        jax.ShapeDtypeStruct((b, MLA_WIDTH, s), BF16),
        jax.ShapeDtypeStruct((b, MLA_HEADS, MLA_QK, s), BF16),
        jax.ShapeDtypeStruct((b, s, MLA_QK), BF16),
        jax.ShapeDtypeStruct((b, nt, MLA_KV_RANK, tk), BF16),
    ]
    out_specs = [
        pl.BlockSpec((1, NSA_WIDTH, tm), lambda bi, i: (bi, 0, i)),
        pl.BlockSpec((1, NSA_KV_GROUPS, tm, LANES), lambda bi, i: (bi, 0, i, 0)),
        pl.BlockSpec((1, NSA_KV_GROUPS, tm, LANES), lambda bi, i: (bi, 0, i, 0)),
        pl.BlockSpec((1, tm // tk, NSA_KV_WIDTH, tk), lambda bi, i: (bi, i, 0, 0)),
        pl.BlockSpec((1, tm // tk, NSA_KV_WIDTH, tk), lambda bi, i: (bi, i, 0, 0)),
        pl.BlockSpec((1, tm, NSA_KV_WIDTH), lambda bi, i: (bi, i, 0)),
        pl.BlockSpec((1, tm, NSA_KV_WIDTH), lambda bi, i: (bi, i, 0)),
        pl.BlockSpec((1, NSA_KV_GROUPS * GATE_ROWS, tm), lambda bi, i: (bi, 0, i)),
        pl.BlockSpec((1, NSA_WIDTH, tm), lambda bi, i: (bi, 0, i)),
        pl.BlockSpec((1, MLA_WIDTH, tm), lambda bi, i: (bi, 0, i)),
        pl.BlockSpec((1, MLA_HEADS, MLA_QK, tm), lambda bi, i: (bi, 0, 0, i)),
        pl.BlockSpec((1, tm, MLA_QK), lambda bi, i: (bi, i, 0)),
        pl.BlockSpec((1, tm // tk, MLA_KV_RANK, tk), lambda bi, i: (bi, i, 0, 0)),
    ]
    kern = functools.partial(_proj_kernel, tm=tm, tk=tk, scale_nsa=NSA_HEAD_DIM ** -0.5,
                             scale_mla=(MLA_NOPE_DIM + MLA_ROPE_DIM) ** -0.5)
    return pl.pallas_call(
        kern, grid=(b, s // tm), in_specs=in_specs, out_specs=out_specs, out_shape=out_shape,
        compiler_params=pltpu.CompilerParams(vmem_limit_bytes=VMEM_LIMIT_BYTES),
        name="proj",
    )(x, mod3, ng, wtok, wtr, ct, st, cm, sm, ctn, stn, ctm, stm, qng, kvng, wqt, wk)


def _compress_kernel(k_ref, v_ref, ptop_ref, pbot_ref, wk1_ref, wv1_ref, wk2_ref, wv2t_ref, cc_ref, sc_ref,
                     kc_ref, vct_ref):
    ncp = k_ref.shape[1]
    gw = NSA_KV_GROUPS * CMP_HIDDEN
    lane = lax.broadcasted_iota(I32, (ncp, LANES), 1)

    def hidden(r_ref, w1_ref):
        r = r_ref[0]
        a = jnp.dot((r + ptop_ref[...]).astype(BF16), w1_ref[:, 0:gw], preferred_element_type=F32)
        bt = jnp.dot((r + pbot_ref[...]).astype(BF16), w1_ref[:, gw:2 * gw], preferred_element_type=F32)
        return _silu(a + pltpu.roll(bt, ncp - 1, axis=0))

    hk = hidden(k_ref, wk1_ref)
    hv = hidden(v_ref, wv1_ref)
    for g in range(NSA_KV_GROUPS):
        hkg = hk[:, g * CMP_HIDDEN:(g + 1) * CMP_HIDDEN].astype(BF16)
        kc = jnp.dot(hkg, wk2_ref[...], preferred_element_type=F32)
        kc_ref[0, g] = _rope_tok(kc, cc_ref[...], sc_ref[...], NSA_HEAD_DIM // 2, lane).astype(BF16)
        hvg = hv[:, g * CMP_HIDDEN:(g + 1) * CMP_HIDDEN].astype(BF16)
        vct_ref[0, g] = lax.dot_general(wv2t_ref[...], hvg, NT, preferred_element_type=F32).astype(BF16)


def _compress(kcmp_r, vcmp_r, ptop, pbot, wk1, wv1, wk2, wv2t, cc, sc):
    b, ncp, width = kcmp_r.shape

    def full(a):
        return pl.BlockSpec(a.shape, lambda bi, _n=a.ndim: (0,) * _n)

    blk = pl.BlockSpec((1, ncp, width), lambda bi: (bi, 0, 0))
    tab = pl.BlockSpec((ncp, LANES), lambda bi: (bi, 0))
    return pl.pallas_call(
        _compress_kernel, grid=(b,),
        in_specs=[blk, blk, full(ptop), full(pbot), full(wk1), full(wv1), full(wk2), full(wv2t), tab, tab],
        out_specs=[pl.BlockSpec((1, NSA_KV_GROUPS, ncp, LANES), lambda bi: (bi, 0, 0, 0)),
                   pl.BlockSpec((1, NSA_KV_GROUPS, NSA_HEAD_DIM, ncp), lambda bi: (bi, 0, 0, 0))],
        out_shape=[jax.ShapeDtypeStruct((b, NSA_KV_GROUPS, ncp, LANES), BF16),
                   jax.ShapeDtypeStruct((b, NSA_KV_GROUPS, NSA_HEAD_DIM, ncp), BF16)],
        compiler_params=pltpu.CompilerParams(vmem_limit_bytes=VMEM_LIMIT_BYTES),
        name="compress",
    )(kcmp_r, vcmp_r, ptop, pbot, wk1, wv1, wk2, wv2t, cc, sc)


def _nsa_kernel(q1_ref, q2_ref, kc_ref, vct_ref, kaug_ref, vts_ref, kwin_ref, vtw_ref, g_ref, z_ref, mt_ref,
                o_ref, qaug_ref, m_ref, l_ref, acc_ref, tot_ref, *, tq, tk, n_sel):
    i = pl.program_id(2)
    hpg, dk, half = NSA_HPG, NSA_HEAD_DIM, NSA_HEAD_DIM // 2
    n = hpg * tq
    for hh in range(hpg):
        qaug_ref[0:half, hh * tq:(hh + 1) * tq] = q1_ref[0, hh * half:(hh + 1) * half, :]
        qaug_ref[half:dk, hh * tq:(hh + 1) * tq] = q2_ref[0, hh * half:(hh + 1) * half, :]

    def gate_row(branch):
        return jnp.concatenate([g_ref[0, hh * N_BRANCH + branch:hh * N_BRANCH + branch + 1, :]
                                for hh in range(hpg)], axis=1)

    ncp = kc_ref.shape[2]
    s_c = jnp.dot(kc_ref[0, 0, :, 0:dk], qaug_ref[0:dk, :], preferred_element_type=F32)
    n_idx = lax.broadcasted_iota(I32, (ncp, n), 0)
    t_q = i * tq + (lax.broadcasted_iota(I32, (ncp, n), 1) & (tq - 1))
    valid = n_idx * CMP_STRIDE + (CMP_BLOCK - 1) <= t_q
    s_c = jnp.where(valid, s_c, NEG_INF)
    e = jnp.where(valid, jnp.exp(s_c - jnp.max(s_c, axis=0, keepdims=True)), 0.0)
    l_c = jnp.sum(e, axis=0, keepdims=True)
    p_c = e * jnp.where(l_c > 0.0, 1.0 / l_c, 0.0)
    o_c = jnp.dot(vct_ref[0, 0], p_c.astype(BF16), preferred_element_type=F32)
    tot_ref[...] = gate_row(0) * o_c

    psum = p_c[:, 0:tq]
    for hh in range(1, hpg):
        psum = psum + p_c[:, hh * tq:(hh + 1) * tq]
    hi = psum.astype(BF16)
    lo = (psum - hi.astype(F32)).astype(BF16)
    mt = mt_ref[...]
    imp = jnp.dot(mt, hi, preferred_element_type=F32) + jnp.dot(mt, lo, preferred_element_type=F32)
    nb = mt.shape[0]
    j_idx = lax.broadcasted_iota(I32, (nb, tq), 0)
    cur = (i * tq + lax.broadcasted_iota(I32, (nb, tq), 1)) >> SLC_SHIFT
    forced = (j_idx == 0) | (j_idx == cur) | (j_idx == cur - 1)
    imp = jnp.where(forced, FORCED_SCORE, jnp.where(j_idx > cur, -FORCED_SCORE, imp))
    rank = jnp.zeros((nb, tq), I32)
    for r in range(nb):
        row = imp[r:r + 1, :]
        rank = rank + ((row > imp) | ((row == imp) & (j_idx > r))).astype(I32)
    bias = jnp.where(rank < n_sel, 0.0, SEL_BIAS).astype(BF16)
    for hh in range(hpg):
        qaug_ref[dk:dk + nb, hh * tq:(hh + 1) * tq] = bias

    row_k = lax.broadcasted_iota(I32, (tk, n), 0)
    col_q = lax.broadcasted_iota(I32, (tk, n), 1) & (tq - 1)

    def reset():
        m_ref[...] = jnp.full_like(m_ref, NEG_INF)
        l_ref[...] = jnp.zeros_like(l_ref)
        acc_ref[...] = jnp.zeros_like(acc_ref)

    reset()

    def slc_body(j, carry):
        kt = kaug_ref[0, 0, pl.ds(pl.multiple_of(j * tk, tk), tk), :]
        s_t = jnp.dot(kt, qaug_ref[...], preferred_element_type=F32)
        _flash_update(s_t, vts_ref[0, j], m_ref, l_ref, acc_ref)
        return carry

    lax.fori_loop(0, i, slc_body, 0)
    kt = kaug_ref[0, 0, pl.ds(pl.multiple_of(i * tk, tk), tk), :]
    s_t = jnp.dot(kt, qaug_ref[...], preferred_element_type=F32)
    _flash_update(jnp.where(row_k <= col_q, s_t, NEG_INF), vts_ref[0, i], m_ref, l_ref, acc_ref)
    tot_ref[...] = tot_ref[...] + gate_row(1) * (acc_ref[...] * (1.0 / l_ref[...]))

    reset()
    kt = kwin_ref[0, 0, pl.ds(pl.multiple_of(i * tk, tk), tk), :]
    s_t = jnp.dot(kt, qaug_ref[...], preferred_element_type=F32)
    _flash_update(jnp.where(row_k <= col_q, s_t, NEG_INF), vtw_ref[0, i], m_ref, l_ref, acc_ref)
    n_back = WINDOW // tk
    for back in range(1, n_back + 1):
        @pl.when(i >= back)
        def _(back=back):
            kt_b = kwin_ref[0, 0, pl.ds(pl.multiple_of((i - back) * tk, tk), tk), :]
            s_b = jnp.dot(kt_b, qaug_ref[...], preferred_element_type=F32)
            if back == n_back:
                s_b = jnp.where(row_k > col_q, s_b, NEG_INF)
            _flash_update(s_b, vtw_ref[0, i - back], m_ref, l_ref, acc_ref)
    tot = tot_ref[...] + gate_row(2) * (acc_ref[...] * (1.0 / l_ref[...]))

    for hh in range(hpg):
        zz = z_ref[0, hh * dk:(hh + 1) * dk, :].astype(F32)
        o_ref[0, hh * dk:(hh + 1) * dk, :] = (tot[:, hh * tq:(hh + 1) * tq] * zz).astype(BF16)


def _nsa(qt, kc, vct, kaug, vts, kwin, vtw, gt, ztn, mt, *, tq, tk, n_sel):
    b, _, s = qt.shape
    nt = s // tk
    ncp = kc.shape[2]
    hq = NSA_HPG * (NSA_HEAD_DIM // 2)
    nb = mt.shape[0]
    n = NSA_HPG * tq
    in_specs = [
        pl.BlockSpec((1, hq, tq), lambda bi, g, i: (bi, g, i)),
        pl.BlockSpec((1, hq, tq), lambda bi, g, i: (bi, NSA_KV_GROUPS + g, i)),
        pl.BlockSpec((1, 1, ncp, LANES), lambda bi, g, i: (bi, g, 0, 0)),
        pl.BlockSpec((1, 1, NSA_HEAD_DIM, ncp), lambda bi, g, i: (bi, g, 0, 0)),
        pl.BlockSpec((1, 1, s, LANES), lambda bi, g, i: (bi, g, 0, 0)),
        pl.BlockSpec((1, nt, NSA_HEAD_DIM, tk), lambda bi, g, i: (bi, 0, g, 0)),
        pl.BlockSpec((1, 1, s, LANES), lambda bi, g, i: (bi, g, 0, 0)),
        pl.BlockSpec((1, nt, NSA_HEAD_DIM, tk), lambda bi, g, i: (bi, 0, g, 0)),
        pl.BlockSpec((1, GATE_ROWS, tq), lambda bi, g, i: (bi, g, i)),
        pl.BlockSpec((1, NSA_HPG * NSA_HEAD_DIM, tq), lambda bi, g, i: (bi, g, i)),
        pl.BlockSpec(mt.shape, lambda bi, g, i: (0, 0)),
    ]
    kern = functools.partial(_nsa_kernel, tq=tq, tk=tk, n_sel=n_sel)
    return pl.pallas_call(
        kern, grid=(b, NSA_KV_GROUPS, s // tq), in_specs=in_specs,
        out_specs=pl.BlockSpec((1, NSA_HPG * NSA_HEAD_DIM, tq), lambda bi, g, i: (bi, g, i)),
        out_shape=jax.ShapeDtypeStruct((b, NSA_WIDTH, s), BF16),
        scratch_shapes=[pltpu.VMEM((NSA_HEAD_DIM + nb, n), BF16),
                        pltpu.VMEM((1, n), F32), pltpu.VMEM((1, n), F32),
                        pltpu.VMEM((NSA_HEAD_DIM, n), F32), pltpu.VMEM((NSA_HEAD_DIM, n), F32)],
        compiler_params=pltpu.CompilerParams(vmem_limit_bytes=VMEM_LIMIT_BYTES),
        name="nsa",
    )(qt, qt, kc, vct, kaug, vts, kwin, vtw, gt, ztn, mt)


def _mla_kernel(q_ref, k_ref, vt_ref, z_ref, wvt_ref, o_ref, qs_ref, m_ref, l_ref, acc_ref, *, tq, tk):
    i = pl.program_id(1)
    n = MLA_HEADS * tq
    for hd in range(MLA_HEADS):
        qs_ref[:, hd * tq:(hd + 1) * tq] = q_ref[0, hd]
    m_ref[...] = jnp.full_like(m_ref, NEG_INF)
    l_ref[...] = jnp.zeros_like(l_ref)
    acc_ref[...] = jnp.zeros_like(acc_ref)

    def body(j, carry):
        kt = k_ref[0, pl.ds(pl.multiple_of(j * tk, tk), tk), :]
        s_t = jnp.dot(kt, qs_ref[...], preferred_element_type=F32)
        _flash_update(s_t, vt_ref[0, j], m_ref, l_ref, acc_ref)
        return carry

    lax.fori_loop(0, i, body, 0)
    kt = k_ref[0, pl.ds(pl.multiple_of(i * tk, tk), tk), :]
    s_t = jnp.dot(kt, qs_ref[...], preferred_element_type=F32)
    row_k = lax.broadcasted_iota(I32, (tk, n), 0)
    col_q = lax.broadcasted_iota(I32, (tk, n), 1) & (tq - 1)
    _flash_update(jnp.where(row_k <= col_q, s_t, NEG_INF), vt_ref[0, i], m_ref, l_ref, acc_ref)
    o_lat = (acc_ref[...] * (1.0 / l_ref[...])).astype(BF16)
    for hd in range(MLA_HEADS):
        o_h = jnp.dot(wvt_ref[hd], o_lat[:, hd * tq:(hd + 1) * tq], preferred_element_type=F32)
        zz = z_ref[0, hd * MLA_V_DIM:(hd + 1) * MLA_V_DIM, :].astype(F32)
        o_ref[0, hd * MLA_V_DIM:(hd + 1) * MLA_V_DIM, :] = (o_h * zz).astype(BF16)


def _mla(qtm, kmla, vtm, ztm, wvt, *, tq, tk):
    b, s, _ = kmla.shape
    nt = s // tk
    n = MLA_HEADS * tq
    kern = functools.partial(_mla_kernel, tq=tq, tk=tk)
    return pl.pallas_call(
        kern, grid=(b, s // tq),
        in_specs=[pl.BlockSpec((1, MLA_HEADS, MLA_QK, tq), lambda bi, i: (bi, 0, 0, i)),
                  pl.BlockSpec((1, s, MLA_QK), lambda bi, i: (bi, 0, 0)),
                  pl.BlockSpec((1, nt, MLA_KV_RANK, tk), lambda bi, i: (bi, 0, 0, 0)),
                  pl.BlockSpec((1, MLA_WIDTH, tq), lambda bi, i: (bi, 0, i)),
                  pl.BlockSpec(wvt.shape, lambda bi, i: (0, 0, 0))],
        out_specs=pl.BlockSpec((1, MLA_WIDTH, tq), lambda bi, i: (bi, 0, i)),
        out_shape=jax.ShapeDtypeStruct((b, MLA_WIDTH, s), BF16),
        scratch_shapes=[pltpu.VMEM((MLA_QK, n), BF16), pltpu.VMEM((1, n), F32), pltpu.VMEM((1, n), F32),
                        pltpu.VMEM((MLA_KV_RANK, n), F32)],
        compiler_params=pltpu.CompilerParams(vmem_limit_bytes=VMEM_LIMIT_BYTES),
        name="mla",
    )(qtm, kmla, vtm, ztm, wvt)


def _out_kernel(x_ref, mn_ref, mm_ref, w_ref, mod_ref, fg_ref, o_ref, *, final):
    y = lax.dot_general(mn_ref[0], w_ref[0:NSA_WIDTH, :], TN, preferred_element_type=F32)
    y = y + lax.dot_general(mm_ref[0], w_ref[NSA_WIDTH:MIX_WIDTH, :], TN, preferred_element_type=F32)
    x2 = x_ref[0] + mod_ref[0][2:3] * y
    o_ref[0] = _rms(x2, fg_ref[...]) if final else x2


def _out(x, mn, mm, w_out, mod3, fg, *, tm, final):
    b, s, d = x.shape
    return pl.pallas_call(
        functools.partial(_out_kernel, final=final), grid=(b, s // tm),
        in_specs=[pl.BlockSpec((1, tm, d), lambda bi, i: (bi, i, 0)),
                  pl.BlockSpec((1, NSA_WIDTH, tm), lambda bi, i: (bi, 0, i)),
                  pl.BlockSpec((1, MLA_WIDTH, tm), lambda bi, i: (bi, 0, i)),
                  pl.BlockSpec(w_out.shape, lambda bi, i: (0, 0)),
                  pl.BlockSpec((1, 3, d), lambda bi, i: (bi, 0, 0)),
                  pl.BlockSpec((1, d), lambda bi, i: (0, 0))],
        out_specs=pl.BlockSpec((1, tm, d), lambda bi, i: (bi, i, 0)),
        out_shape=jax.ShapeDtypeStruct((b, s, d), F32),
        compiler_params=pltpu.CompilerParams(vmem_limit_bytes=VMEM_LIMIT_BYTES),
        name="out_proj",
    )(x, mn, mm, w_out, mod3, fg)


def _cmp_to_slc_t(ncp, nc, nslc, nb):
    start = np.arange(nc)[:, None] * CMP_STRIDE
    bstart = np.arange(nslc)[None, :] * SLC_BLOCK
    ov = np.minimum(start + CMP_BLOCK, bstart + SLC_BLOCK) - np.maximum(start, bstart)
    m = (np.clip(ov, 0, None) / CMP_BLOCK).astype(np.float32)
    out = np.zeros((nb, ncp), np.float32)
    out[:nslc, :nc] = m.T
    return out


def _layout_w_in(w):
    d = w.shape[0]
    (q_n, kc_n, vc_n, ks_n, vs_n, kw_n, vw_n, gl_n, z_n, cq_m, ckv_m, kr_m, z_m) = jnp.split(w, IN_OFFSETS, axis=-1)
    dk = NSA_HEAD_DIM
    z64 = jnp.zeros((d, LANES - dk), w.dtype)
    wtok = jnp.concatenate(
        [ks_n[:, :dk], z64, ks_n[:, dk:], z64, kw_n[:, :dk], z64, kw_n[:, dk:], z64,
         kc_n, vc_n, cq_m, ckv_m, kr_m, jnp.zeros((d, LANES - MLA_ROPE_DIM), w.dtype)], axis=1)
    qr = q_n.reshape(d, NSA_HEADS, 2, dk // 2)
    q_perm = jnp.concatenate([qr[:, :, 0, :].reshape(d, -1), qr[:, :, 1, :].reshape(d, -1)], axis=1)
    gl = gl_n.reshape(d, NSA_KV_GROUPS, NSA_HPG * N_BRANCH)
    gl = jnp.pad(gl, ((0, 0), (0, 0), (0, GATE_ROWS - NSA_HPG * N_BRANCH))).reshape(d, -1)
    wtr = jnp.concatenate([q_perm, vs_n, vw_n, gl, z_n, z_m], axis=1).T
    assert wtok.shape[1] == TOK_COLS and wtr.shape[0] == TR_ROWS
    return wtok.astype(BF16), wtr.astype(BF16)


def _layout_w1(w1):
    hid = w1.shape[1]
    w1r = w1.reshape(2, CMP_STRIDE, NSA_HEAD_DIM, hid)
    eye = jnp.eye(NSA_KV_GROUPS, dtype=w1.dtype)
    halves = [jnp.einsum('ldh,pg->lpdgh', w1r[k], eye).reshape(CMP_STRIDE * NSA_KV_WIDTH, NSA_KV_GROUPS * hid)
              for k in range(2)]
    return jnp.concatenate(halves, axis=1).astype(BF16)


def kernel(x, c, positions, ada_w, ada_b, norm_g, w_in, cmp_pos, cmp_k_w1, cmp_k_w2, cmp_v_w1, cmp_v_w2,
           q_norm_g, w_q_up, kv_norm_g, w_kv_up, w_out, final_norm_g):
    b, s, d = x.shape
    depth = ada_w.shape[0]
    tm, tq = PROJ_TILE, ATT_TILE
    tk = tq
    assert s % tm == 0 and tm % tk == 0 and WINDOW % tk == 0 and (tq & (tq - 1)) == 0
    assert CMP_BLOCK == 2 * CMP_STRIDE and s % SLC_BLOCK == 0
    nslc = s // SLC_BLOCK
    nb = LANES - NSA_HEAD_DIM
    assert nslc <= nb
    ncp = s // CMP_STRIDE
    nc = ncp - 1

    half_n, half_m = NSA_HEAD_DIM // 2, MLA_ROPE_DIM // 2
    inv_n = ROPE_THETA ** (-jnp.arange(half_n, dtype=F32) / half_n)
    inv_m = ROPE_THETA ** (-jnp.arange(half_m, dtype=F32) / half_m)

    def lane_pattern(inv, half):
        z = jnp.zeros((LANES - 2 * half,), F32)
        ones = jnp.ones((half,), F32)
        return jnp.concatenate([inv, inv, z]), jnp.concatenate([-ones, ones, z])

    inv_ln, sign_ln = lane_pattern(inv_n, half_n)
    inv_lm, sign_lm = lane_pattern(inv_m, half_m)
    pos_f = positions.astype(F32)
    pos_b = jnp.broadcast_to(pos_f.reshape(b * s, 1), (b * s, LANES))
    ct, st, cm, sm = _tok_tables(pos_b, jnp.stack([inv_ln, inv_lm]), jnp.stack([sign_ln, sign_lm]), tm)
    cmp_end = np.minimum(np.arange(ncp) * CMP_STRIDE + CMP_BLOCK - 1, s - 1)
    pos_c = jnp.broadcast_to(pos_f[:, cmp_end].reshape(b * ncp, 1), (b * ncp, LANES))
    cc, sc = _tok_tables(pos_c, inv_ln[None], sign_ln[None], ncp)
    tabs_row = _row_tables(pos_f.reshape(b, 1, s), jnp.broadcast_to(inv_n[:, None], (half_n, tm)),
                           jnp.broadcast_to(inv_m[:, None], (half_m, tm)), tm)

    mt = jnp.asarray(_cmp_to_slc_t(ncp, nc, nslc, nb), dtype=BF16)
    bp = -(-b // SUBLANES) * SUBLANES
    c_pad = jnp.pad(c, ((0, bp - b), (0, 0)))

    for l in range(depth):
        mod = _adaln(c_pad, ada_w[l], ada_b[l].reshape(1, -1))
        mod3 = mod[:b].reshape(b, 3, d)
        wtok, wtr = _layout_w_in(w_in[l])
        wq = w_q_up[l].reshape(MLA_Q_RANK, MLA_HEADS, MLA_NOPE_DIM + MLA_ROPE_DIM)
        wqt = jnp.concatenate([wq[:, :, :MLA_NOPE_DIM].reshape(MLA_Q_RANK, -1),
                               wq[:, :, MLA_NOPE_DIM:MLA_NOPE_DIM + half_m].reshape(MLA_Q_RANK, -1),
                               wq[:, :, MLA_NOPE_DIM + half_m:].reshape(MLA_Q_RANK, -1)], axis=1).T.astype(BF16)
        wkv = w_kv_up[l].reshape(MLA_KV_RANK, MLA_HEADS, MLA_NOPE_DIM + MLA_V_DIM)
        wk = wkv[:, :, :MLA_NOPE_DIM].transpose(1, 0, 2).astype(BF16)
        wvt = wkv[:, :, MLA_NOPE_DIM:].transpose(1, 2, 0).astype(BF16)

        (qt, kaug, kwin, vts, vtw, kcmp, vcmp, gt, ztn, ztm, qtm, kmla, vtm) = _proj(
            x, mod3, norm_g[l].reshape(1, d), wtok, wtr, (ct, st, cm, sm), tabs_row,
            q_norm_g[l].reshape(1, -1), kv_norm_g[l].reshape(1, -1), wqt, wk, tm=tm, tk=tk)

        pos_l = cmp_pos[l]
        ptop = jnp.broadcast_to(pos_l[:CMP_STRIDE, None, :], (CMP_STRIDE, NSA_KV_GROUPS, NSA_HEAD_DIM)).reshape(1, -1)
        pbot = jnp.broadcast_to(pos_l[CMP_STRIDE:, None, :], (CMP_STRIDE, NSA_KV_GROUPS, NSA_HEAD_DIM)).reshape(1, -1)
        wk2 = jnp.pad(cmp_k_w2[l], ((0, 0), (0, LANES - NSA_HEAD_DIM))).astype(BF16)
        kc, vct = _compress(kcmp.reshape(b, ncp, -1), vcmp.reshape(b, ncp, -1), ptop, pbot,
                            _layout_w1(cmp_k_w1[l]), _layout_w1(cmp_v_w1[l]), wk2,
                            cmp_v_w2[l].T.astype(BF16), cc, sc)

        mix_n = _nsa(qt, kc, vct, kaug, vts, kwin, vtw, gt, ztn, mt, tq=tq, tk=tk, n_sel=min(SLC_TOPK, nslc))
        mix_m = _mla(qtm, kmla, vtm, ztm, wvt, tq=tq, tk=tk)
        x = _out(x, mix_n, mix_m, w_out[l].astype(BF16), mod3, final_norm_g.reshape(1, d), tm=tm,
                 final=(l == depth - 1))
    return x
```

```python
import functools

import numpy as np
import jax
import jax.numpy as jnp
from jax import lax
from jax.experimental import pallas as pl
from jax.experimental.pallas import tpu as pltpu

F32 = jnp.float32
BF16 = jnp.bfloat16
I32 = jnp.int32

NSA_HEADS = 8
NSA_KV_GROUPS = 2
NSA_HPG = NSA_HEADS // NSA_KV_GROUPS
NSA_HEAD_DIM = 64
NSA_WIDTH = NSA_HEADS * NSA_HEAD_DIM
NSA_KV_WIDTH = NSA_KV_GROUPS * NSA_HEAD_DIM
CMP_BLOCK = 32
CMP_STRIDE = 16
CMP_HIDDEN = 128
SLC_BLOCK = 64
SLC_TOPK = 16
WINDOW = 512
N_BRANCH = 3
FORCED_SCORE = 1.0e4
MLA_HEADS = 8
MLA_NOPE_DIM = 64
MLA_ROPE_DIM = 32
MLA_V_DIM = 64
MLA_WIDTH = MLA_HEADS * MLA_V_DIM
MLA_Q_RANK = 256
MLA_KV_RANK = 128
MIX_WIDTH = NSA_WIDTH + MLA_WIDTH
ROPE_THETA = 10000.0
NORM_EPS = 1e-6
NEG_INF = -1e30
IN_SIZES = (NSA_WIDTH, NSA_KV_WIDTH, NSA_KV_WIDTH, NSA_KV_WIDTH, NSA_KV_WIDTH, NSA_KV_WIDTH, NSA_KV_WIDTH,
            NSA_HEADS * N_BRANCH, NSA_WIDTH, MLA_Q_RANK, MLA_KV_RANK, MLA_ROPE_DIM, MLA_WIDTH)
IN_OFFSETS = tuple(int(o) for o in np.cumsum(IN_SIZES)[:-1])

LANES = 128
SUBLANES = 8
VMEM_LIMIT_BYTES = 56 * 1024 * 1024

PROJ_TILE = 512
ATT_TILE = 256
SEL_BIAS = -30000.0
GATE_ROWS = 16
MLA_QK = MLA_KV_RANK + MLA_ROPE_DIM
LOG2E = float(np.log2(np.e))
SLC_SHIFT = SLC_BLOCK.bit_length() - 1
assert 1 << SLC_SHIFT == SLC_BLOCK

NT = (((1,), (1,)), ((), ()))
TN = (((0,), (0,)), ((), ()))


def _silu(v):
    return v * jax.nn.sigmoid(v)


def _rms(v, g):
    ms = jnp.mean(v * v, axis=-1, keepdims=True)
    return v * lax.rsqrt(ms + NORM_EPS) * g


def _rope_tok(v, c, s_signed, half, lane):
    up = pltpu.roll(v, LANES - half, axis=1)
    dn = pltpu.roll(v, half, axis=1)
    return v * c + jnp.where(lane < half, up, dn) * s_signed


def _chain_update(s_t, v_t, m_ref, l_ref, acc_ref, cols, keep=None):
    if keep is not None:
        s_t = jnp.where(keep, s_t, NEG_INF)
    m_prev = m_ref[:, cols]
    m_new = jnp.maximum(m_prev, jnp.max(s_t, axis=0, keepdims=True))
    alpha = jnp.exp2(m_prev - m_new)
    p = jnp.exp2(s_t - m_new)
    l_ref[:, cols] = alpha * l_ref[:, cols] + jnp.sum(p, axis=0, keepdims=True)
    acc_ref[:, cols] = alpha * acc_ref[:, cols] + jnp.dot(v_t, p.astype(BF16), preferred_element_type=F32)
    m_ref[:, cols] = m_new


def _causal_sweep(k_tile, v_tile, q_ref, sa_ref, sb_ref, m_ref, l_ref, acc_ref, last, n_chains, tq, keep_last):
    def phase(k_next, s_next_ref, s_cur_ref, v_t, keep=None, lookahead=2):
        for n in range(n_chains + lookahead):
            if k_next is not None and n < n_chains:
                cols = slice(n * tq, (n + 1) * tq)
                s_next_ref[:, cols] = jnp.dot(k_next, q_ref[:, cols], preferred_element_type=F32)
            if n >= lookahead:
                cols = slice((n - lookahead) * tq, (n - lookahead + 1) * tq)
                _chain_update(s_cur_ref[:, cols], v_t, m_ref, l_ref, acc_ref, cols, keep=keep)

    k_0 = k_tile(0)
    for ch in range(n_chains):
        sa_ref[:, ch * tq:(ch + 1) * tq] = jnp.dot(k_0, q_ref[:, ch * tq:(ch + 1) * tq], preferred_element_type=F32)

    def body(jj, carry):
        j = 2 * jj
        phase(k_tile(j + 1), sb_ref, sa_ref, v_tile(j))
        phase(k_tile(j + 2), sa_ref, sb_ref, v_tile(j + 1))
        return carry

    lax.fori_loop(0, last >> 1, body, 0)

    @pl.when((last & 1) == 0)
    def _():
        phase(None, None, sa_ref, v_tile(last), keep=keep_last)

    @pl.when((last & 1) == 1)
    def _():
        phase(k_tile(last), sb_ref, sa_ref, v_tile(last - 1))
        phase(None, None, sb_ref, v_tile(last), keep=keep_last)


def _tile_list(items, q_ref, m_ref, l_ref, acc_ref, tq, lookahead):
    pending = {}
    for n in range(len(items) + lookahead):
        if n < len(items):
            k_t, _, ch, _ = items[n]
            pending[n] = jnp.dot(k_t, q_ref[:, ch * tq:(ch + 1) * tq], preferred_element_type=F32)
        if n >= lookahead:
            _, v_t, ch, keep = items[n - lookahead]
            _chain_update(pending.pop(n - lookahead), v_t, m_ref, l_ref, acc_ref,
                          slice(ch * tq, (ch + 1) * tq), keep=keep)


def _adaln_kernel(c_ref, w_ref, b_ref, o_ref):
    sc = _silu(c_ref[...])
    o_ref[...] = jnp.dot(sc.astype(BF16), w_ref[...].astype(BF16), preferred_element_type=F32) + b_ref[...]


def _adaln(c_pad, w, b):
    bp, d = c_pad.shape
    n = w.shape[1] // d
    return pl.pallas_call(
        _adaln_kernel,
        grid=(n,),
        in_specs=[pl.BlockSpec((bp, d), lambda j: (0, 0)),
                  pl.BlockSpec((d, d), lambda j: (0, j)),
                  pl.BlockSpec((1, d), lambda j: (0, j))],
        out_specs=pl.BlockSpec((bp, d), lambda j: (0, j)),
        out_shape=jax.ShapeDtypeStruct((bp, n * d), F32),
        name="adaln",
    )(c_pad, w, b)


def _tok_table_kernel(p_ref, inv_ref, sign_ref, *out_refs):
    p = p_ref[...]
    for k in range(len(out_refs) // 2):
        ang = p * inv_ref[k:k + 1, :]
        out_refs[2 * k][...] = jnp.cos(ang)
        out_refs[2 * k + 1][...] = jnp.sin(ang) * sign_ref[k:k + 1, :]


def _tok_tables(pos_b, inv_lanes, sign_lanes, tile):
    r = pos_b.shape[0]
    npat = inv_lanes.shape[0]
    spec = pl.BlockSpec((tile, LANES), lambda i: (i, 0))
    cst = pl.BlockSpec((npat, LANES), lambda i: (0, 0))
    return pl.pallas_call(
        _tok_table_kernel,
        grid=(r // tile,),
        in_specs=[spec, cst, cst],
        out_specs=[spec] * (2 * npat),
        out_shape=[jax.ShapeDtypeStruct((r, LANES), F32)] * (2 * npat),
        name="rope_tok_tables",
    )(pos_b, inv_lanes, sign_lanes)


def _row_table_kernel(p_ref, inva_ref, invb_ref, ca_ref, sa_ref, cb_ref, sb_ref):
    p = p_ref[0]
    ang = inva_ref[...] * p
    ca_ref[0] = jnp.cos(ang)
    sa_ref[0] = jnp.sin(ang)
    ang = invb_ref[...] * p
    cb_ref[0] = jnp.cos(ang)
    sb_ref[0] = jnp.sin(ang)


def _row_tables(pos_row, inva_b, invb_b, tile):
    b, _, s = pos_row.shape
    fa, fb = inva_b.shape[0], invb_b.shape[0]
    return pl.pallas_call(
        _row_table_kernel,
        grid=(b, s // tile),
        in_specs=[pl.BlockSpec((1, 1, tile), lambda bi, i: (bi, 0, i)),
                  pl.BlockSpec((fa, tile), lambda bi, i: (0, 0)),
                  pl.BlockSpec((fb, tile), lambda bi, i: (0, 0))],
        out_specs=[pl.BlockSpec((1, fa, tile), lambda bi, i: (bi, 0, i))] * 2
        + [pl.BlockSpec((1, fb, tile), lambda bi, i: (bi, 0, i))] * 2,
        out_shape=[jax.ShapeDtypeStruct((b, fa, s), F32)] * 2 + [jax.ShapeDtypeStruct((b, fb, s), F32)] * 2,
        name="rope_row_tables",
    )(pos_row, inva_b, invb_b)


TOK_KS, TOK_KW, TOK_KC, TOK_VC, TOK_CQ, TOK_CKV, TOK_KR, TOK_COLS = 0, 256, 512, 640, 768, 1024, 1152, 1280
TR_Q, TR_VS, TR_VW, TR_G, TR_ZN, TR_ZM, TR_ROWS = 0, 512, 640, 768, 800, 1312, 1824


def _proj_kernel(x_ref, mod_ref, ng_ref, wtok_ref, wtr_ref, ct_ref, st_ref, cm_ref, sm_ref,
                 ctn_ref, stn_ref, ctm_ref, stm_ref, qng_ref, kvng_ref, wqt_ref, wk_ref,
                 qt_ref, kaug_ref, kwin_ref, vts_ref, vtw_ref, kcmp_ref, vcmp_ref, gt_ref,
                 ztn_ref, ztm_ref, qtm_ref, kmla_ref, vtm_ref, *, tm, tk, scale_nsa, scale_mla):
    i = pl.program_id(1)
    mod = mod_ref[0]
    h = _rms(x_ref[0], ng_ref[...]) * (1.0 + mod[1:2]) + mod[0:1]
    hb = h.astype(BF16)
    tok = jnp.dot(hb, wtok_ref[...], preferred_element_type=F32)
    tr = lax.dot_general(wtr_ref[...], hb, NT, preferred_element_type=F32)

    lane = lax.broadcasted_iota(I32, (tm, LANES), 1)
    row = lax.broadcasted_iota(I32, (tm, LANES), 0)
    blk = (i * tm + row) >> SLC_SHIFT
    onehot = (lane - NSA_HEAD_DIM == blk).astype(F32)
    ct, st = ct_ref[...], st_ref[...]
    half_n = NSA_HEAD_DIM // 2
    for g in range(NSA_KV_GROUPS):
        ks = _rope_tok(tok[:, TOK_KS + LANES * g:TOK_KS + LANES * (g + 1)], ct, st, half_n, lane)
        kaug_ref[0, g] = (ks + onehot).astype(BF16)
        kw = _rope_tok(tok[:, TOK_KW + LANES * g:TOK_KW + LANES * (g + 1)], ct, st, half_n, lane)
        kwin_ref[0, g] = kw.astype(BF16)
    kcmp_ref[0] = tok[:, TOK_KC:TOK_KC + LANES]
    vcmp_ref[0] = tok[:, TOK_VC:TOK_VC + LANES]

    ckvn = _rms(tok[:, TOK_CKV:TOK_CKV + MLA_KV_RANK], kvng_ref[...])
    krr = _rope_tok(tok[:, TOK_KR:TOK_KR + LANES], cm_ref[...], sm_ref[...], MLA_ROPE_DIM // 2, lane)
    kmla_ref[0, :, 0:MLA_KV_RANK] = ckvn.astype(BF16)
    kmla_ref[0, :, MLA_KV_RANK:MLA_QK] = krr[:, 0:MLA_ROPE_DIM].astype(BF16)
    ckvt = ckvn.T.astype(BF16)
    for ii in range(tm // tk):
        vtm_ref[0, ii] = ckvt[:, ii * tk:(ii + 1) * tk]

    cqn = _rms(tok[:, TOK_CQ:TOK_CQ + MLA_Q_RANK], qng_ref[...]).astype(BF16)
    qm = lax.dot_general(wqt_ref[...], cqn, NT, preferred_element_type=F32)
    nq = MLA_HEADS * MLA_NOPE_DIM
    hr = MLA_ROPE_DIM // 2
    x1 = qm[nq:nq + MLA_HEADS * hr].reshape(MLA_HEADS, hr, tm)
    x2 = qm[nq + MLA_HEADS * hr:nq + 2 * MLA_HEADS * hr].reshape(MLA_HEADS, hr, tm)
    cm_t, sm_t = ctm_ref[0][None], stm_ref[0][None]
    o1 = (x1 * cm_t - x2 * sm_t) * scale_mla
    o2 = (x2 * cm_t + x1 * sm_t) * scale_mla
    for hd in range(MLA_HEADS):
        qn_h = qm[hd * MLA_NOPE_DIM:(hd + 1) * MLA_NOPE_DIM].astype(BF16)
        qabs = jnp.dot(wk_ref[hd], qn_h, preferred_element_type=F32)
        qtm_ref[0, hd, 0:MLA_KV_RANK, :] = (qabs * scale_mla).astype(BF16)
        qtm_ref[0, hd, MLA_KV_RANK:MLA_KV_RANK + hr, :] = o1[hd].astype(BF16)
        qtm_ref[0, hd, MLA_KV_RANK + hr:MLA_QK, :] = o2[hd].astype(BF16)

    hq = NSA_HEADS * half_n
    q1 = tr[TR_Q:TR_Q + hq].reshape(NSA_HEADS, half_n, tm)
    q2 = tr[TR_Q + hq:TR_Q + 2 * hq].reshape(NSA_HEADS, half_n, tm)
    cn_t, sn_t = ctn_ref[0][None], stn_ref[0][None]
    qt_ref[0, 0:hq, :] = ((q1 * cn_t - q2 * sn_t) * scale_nsa).reshape(hq, tm).astype(BF16)
    qt_ref[0, hq:2 * hq, :] = ((q2 * cn_t + q1 * sn_t) * scale_nsa).reshape(hq, tm).astype(BF16)

    vts = tr[TR_VS:TR_VS + NSA_KV_WIDTH].astype(BF16)
    vtw = tr[TR_VW:TR_VW + NSA_KV_WIDTH].astype(BF16)
    for ii in range(tm // tk):
        vts_ref[0, ii] = vts[:, ii * tk:(ii + 1) * tk]
        vtw_ref[0, ii] = vtw[:, ii * tk:(ii + 1) * tk]
    gt_ref[0] = jax.nn.sigmoid(tr[TR_G:TR_G + NSA_KV_GROUPS * GATE_ROWS])
    ztn_ref[0] = _silu(tr[TR_ZN:TR_ZN + NSA_WIDTH]).astype(BF16)
    ztm_ref[0] = _silu(tr[TR_ZM:TR_ZM + MLA_WIDTH]).astype(BF16)


def _proj(x, mod3, ng, wtok, wtr, tabs_tok, tabs_row, qng, kvng, wqt, wk, *, tm, tk):
    b, s, d = x.shape
    ct, st, cm, sm = tabs_tok
    ctn, stn, ctm, stm = tabs_row
    nt = s // tk
    tile_tok = pl.BlockSpec((tm, LANES), lambda bi, i: (bi * (s // tm) + i, 0))

    def full(a):
        return pl.BlockSpec(a.shape, lambda bi, i, _n=a.ndim: (0,) * _n)

    def rowtab(a):
        return pl.BlockSpec((1, a.shape[1], tm), lambda bi, i: (bi, 0, i))

    in_specs = [pl.BlockSpec((1, tm, d), lambda bi, i: (bi, i, 0)),
                pl.BlockSpec((1, 3, d), lambda bi, i: (bi, 0, 0)),
                full(ng), full(wtok), full(wtr),
                tile_tok, tile_tok, tile_tok, tile_tok,
                rowtab(ctn), rowtab(stn), rowtab(ctm), rowtab(stm),
                full(qng), full(kvng), full(wqt), full(wk)]
    out_shape = [
        jax.ShapeDtypeStruct((b, NSA_WIDTH, s), BF16),
        jax.ShapeDtypeStruct((b, NSA_KV_GROUPS, s, LANES), BF16),
        jax.ShapeDtypeStruct((b, NSA_KV_GROUPS, s, LANES), BF16),
        jax.ShapeDtypeStruct((b, nt, NSA_KV_WIDTH, tk), BF16),
        jax.ShapeDtypeStruct((b, nt, NSA_KV_WIDTH, tk), BF16),
        jax.ShapeDtypeStruct((b, s, NSA_KV_WIDTH), F32),
        jax.ShapeDtypeStruct((b, s, NSA_KV_WIDTH), F32),
        jax.ShapeDtypeStruct((b, NSA_KV_GROUPS * GATE_ROWS, s), F32),
        jax.ShapeDtypeStruct((b, NSA_WIDTH, s), BF16),
        jax.ShapeDtypeStruct((b, MLA_WIDTH, s), BF16),
        jax.ShapeDtypeStruct((b, MLA_HEADS, MLA_QK, s), BF16),
        jax.ShapeDtypeStruct((b, s, MLA_QK), BF16),
        jax.ShapeDtypeStruct((b, nt, MLA_KV_RANK, tk), BF16),
    ]
    out_specs = [
        pl.BlockSpec((1, NSA_WIDTH, tm), lambda bi, i: (bi, 0, i)),
        pl.BlockSpec((1, NSA_KV_GROUPS, tm, LANES), lambda bi, i: (bi, 0, i, 0)),
        pl.BlockSpec((1, NSA_KV_GROUPS, tm, LANES), lambda bi, i: (bi, 0, i, 0)),
        pl.BlockSpec((1, tm // tk, NSA_KV_WIDTH, tk), lambda bi, i: (bi, i, 0, 0)),
        pl.BlockSpec((1, tm // tk, NSA_KV_WIDTH, tk), lambda bi, i: (bi, i, 0, 0)),
        pl.BlockSpec((1, tm, NSA_KV_WIDTH), lambda bi, i: (bi, i, 0)),
        pl.BlockSpec((1, tm, NSA_KV_WIDTH), lambda bi, i: (bi, i, 0)),
        pl.BlockSpec((1, NSA_KV_GROUPS * GATE_ROWS, tm), lambda bi, i: (bi, 0, i)),
        pl.BlockSpec((1, NSA_WIDTH, tm), lambda bi, i: (bi, 0, i)),
        pl.BlockSpec((1, MLA_WIDTH, tm), lambda bi, i: (bi, 0, i)),
        pl.BlockSpec((1, MLA_HEADS, MLA_QK, tm), lambda bi, i: (bi, 0, 0, i)),
        pl.BlockSpec((1, tm, MLA_QK), lambda bi, i: (bi, i, 0)),
        pl.BlockSpec((1, tm // tk, MLA_KV_RANK, tk), lambda bi, i: (bi, i, 0, 0)),
    ]
    kern = functools.partial(_proj_kernel, tm=tm, tk=tk, scale_nsa=NSA_HEAD_DIM ** -0.5 * LOG2E,
                             scale_mla=(MLA_NOPE_DIM + MLA_ROPE_DIM) ** -0.5 * LOG2E)
    return pl.pallas_call(
        kern, grid=(b, s // tm), in_specs=in_specs, out_specs=out_specs, out_shape=out_shape,
        compiler_params=pltpu.CompilerParams(vmem_limit_bytes=VMEM_LIMIT_BYTES),
        name="proj",
    )(x, mod3, ng, wtok, wtr, ct, st, cm, sm, ctn, stn, ctm, stm, qng, kvng, wqt, wk)


def _compress_kernel(k_ref, v_ref, ptop_ref, pbot_ref, wk1_ref, wv1_ref, wk2_ref, wv2t_ref, cc_ref, sc_ref,
                     kc_ref, vct_ref):
    ncp = k_ref.shape[1]
    gw = NSA_KV_GROUPS * CMP_HIDDEN
    lane = lax.broadcasted_iota(I32, (ncp, LANES), 1)

    def hidden(r_ref, w1_ref):
        r = r_ref[0]
        a = jnp.dot((r + ptop_ref[...]).astype(BF16), w1_ref[:, 0:gw], preferred_element_type=F32)
        bt = jnp.dot((r + pbot_ref[...]).astype(BF16), w1_ref[:, gw:2 * gw], preferred_element_type=F32)
        return _silu(a + pltpu.roll(bt, ncp - 1, axis=0))

    hk = hidden(k_ref, wk1_ref)
    hv = hidden(v_ref, wv1_ref)
    for g in range(NSA_KV_GROUPS):
        hkg = hk[:, g * CMP_HIDDEN:(g + 1) * CMP_HIDDEN].astype(BF16)
        kc = jnp.dot(hkg, wk2_ref[...], preferred_element_type=F32)
        kc_ref[0, g] = _rope_tok(kc, cc_ref[...], sc_ref[...], NSA_HEAD_DIM // 2, lane).astype(BF16)
        hvg = hv[:, g * CMP_HIDDEN:(g + 1) * CMP_HIDDEN].astype(BF16)
        vct_ref[0, g] = lax.dot_general(wv2t_ref[...], hvg, NT, preferred_element_type=F32).astype(BF16)


def _compress(kcmp_r, vcmp_r, ptop, pbot, wk1, wv1, wk2, wv2t, cc, sc):
    b, ncp, width = kcmp_r.shape

    def full(a):
        return pl.BlockSpec(a.shape, lambda bi, _n=a.ndim: (0,) * _n)

    blk = pl.BlockSpec((1, ncp, width), lambda bi: (bi, 0, 0))
    tab = pl.BlockSpec((ncp, LANES), lambda bi: (bi, 0))
    return pl.pallas_call(
        _compress_kernel, grid=(b,),
        in_specs=[blk, blk, full(ptop), full(pbot), full(wk1), full(wv1), full(wk2), full(wv2t), tab, tab],
        out_specs=[pl.BlockSpec((1, NSA_KV_GROUPS, ncp, LANES), lambda bi: (bi, 0, 0, 0)),
                   pl.BlockSpec((1, NSA_KV_GROUPS, NSA_HEAD_DIM, ncp), lambda bi: (bi, 0, 0, 0))],
        out_shape=[jax.ShapeDtypeStruct((b, NSA_KV_GROUPS, ncp, LANES), BF16),
                   jax.ShapeDtypeStruct((b, NSA_KV_GROUPS, NSA_HEAD_DIM, ncp), BF16)],
        compiler_params=pltpu.CompilerParams(vmem_limit_bytes=VMEM_LIMIT_BYTES),
        name="compress",
    )(kcmp_r, vcmp_r, ptop, pbot, wk1, wv1, wk2, wv2t, cc, sc)


def _nsa_kernel(q1_ref, q2_ref, kc_ref, vct_ref, kaug_ref, vts_ref, kwin_ref, vtw_ref, g_ref, z_ref, mt_ref,
                o_ref, qaug_ref, sa_ref, sb_ref, m_ref, l_ref, acc_ref, tot_ref, *, tq, tk, n_sel):
    i = pl.program_id(2)
    hpg, dk, half = NSA_HPG, NSA_HEAD_DIM, NSA_HEAD_DIM // 2
    n = hpg * tq
    for hh in range(hpg):
        qaug_ref[0:half, hh * tq:(hh + 1) * tq] = q1_ref[0, hh * half:(hh + 1) * half, :]
        qaug_ref[half:dk, hh * tq:(hh + 1) * tq] = q2_ref[0, hh * half:(hh + 1) * half, :]

    def gate(hh, branch):
        return g_ref[0, hh * N_BRANCH + branch:hh * N_BRANCH + branch + 1, :]

    ncp = kc_ref.shape[2]
    n_idx = lax.broadcasted_iota(I32, (ncp, tq), 0)
    t_q = i * tq + lax.broadcasted_iota(I32, (ncp, tq), 1)
    valid = n_idx * CMP_STRIDE + (CMP_BLOCK - 1) <= t_q
    kc = kc_ref[0, 0, :, 0:dk]
    s_cmp = [jnp.dot(kc, qaug_ref[0:dk, hh * tq:(hh + 1) * tq], preferred_element_type=F32) for hh in range(hpg)]
    psum = None
    for hh in range(hpg):
        cols = slice(hh * tq, (hh + 1) * tq)
        s_c = jnp.where(valid, s_cmp[hh], NEG_INF)
        e = jnp.where(valid, jnp.exp2(s_c - jnp.max(s_c, axis=0, keepdims=True)), 0.0)
        l_c = jnp.sum(e, axis=0, keepdims=True)
        p_c = e * jnp.where(l_c > 0.0, 1.0 / l_c, 0.0)
        o_c = jnp.dot(vct_ref[0, 0], p_c.astype(BF16), preferred_element_type=F32)
        tot_ref[:, cols] = gate(hh, 0) * o_c
        psum = p_c if psum is None else psum + p_c

    row_k = lax.broadcasted_iota(I32, (tk, tq), 0)
    col_q = lax.broadcasted_iota(I32, (tk, tq), 1)
    causal = row_k <= col_q

    def reset():
        m_ref[...] = jnp.full_like(m_ref, NEG_INF)
        l_ref[...] = jnp.zeros_like(l_ref)
        acc_ref[...] = jnp.zeros_like(acc_ref)

    def add_branch(branch):
        for hh in range(hpg):
            cols = slice(hh * tq, (hh + 1) * tq)
            tot_ref[:, cols] = tot_ref[:, cols] + gate(hh, branch) * (acc_ref[:, cols] * (1.0 / l_ref[:, cols]))

    hi = psum.astype(BF16)
    lo = (psum - hi.astype(F32)).astype(BF16)
    mt = mt_ref[...]
    imp = jnp.dot(mt, hi, preferred_element_type=F32) + jnp.dot(mt, lo, preferred_element_type=F32)

    reset()
    n_back = WINDOW // tk
    items = []
    for back in range(n_back + 1):
        jb = jnp.maximum(i - back, 0)
        kt_b = kwin_ref[0, 0, pl.ds(pl.multiple_of(jb * tk, tk), tk), 0:dk]
        if back == 0:
            keep = causal
        elif back == n_back:
            keep = (row_k > col_q) & (i >= back)
        else:
            keep = jnp.broadcast_to(i >= back, (tk, tq))
        items += [(kt_b, vtw_ref[0, jb], hh, keep) for hh in range(hpg)]
    _tile_list(items, qaug_ref.at[0:dk, :], m_ref, l_ref, acc_ref, tq, lookahead=2)
    add_branch(2)

    nb = mt.shape[0]
    j_idx = lax.broadcasted_iota(I32, (nb, tq), 0)
    cur = (i * tq + lax.broadcasted_iota(I32, (nb, tq), 1)) >> SLC_SHIFT
    forced = (j_idx == 0) | (j_idx == cur) | (j_idx == cur - 1)
    imp = jnp.where(forced, FORCED_SCORE, jnp.where(j_idx > cur, -FORCED_SCORE, imp))
    rank = jnp.zeros((nb, tq), I32)
    for r in range(nb):
        row = imp[r:r + 1, :]
        rank = rank + ((row > imp) | ((row == imp) & (j_idx > r))).astype(I32)
    bias = jnp.where(rank < n_sel, 0.0, SEL_BIAS).astype(BF16)
    for hh in range(hpg):
        qaug_ref[dk:dk + nb, hh * tq:(hh + 1) * tq] = bias

    reset()
    _causal_sweep(lambda j: kaug_ref[0, 0, pl.ds(pl.multiple_of(j * tk, tk), tk), :], lambda j: vts_ref[0, j],
                  qaug_ref, sa_ref, sb_ref, m_ref, l_ref, acc_ref, i, hpg, tq, causal)
    add_branch(1)

    for hh in range(hpg):
        zz = z_ref[0, hh * dk:(hh + 1) * dk, :].astype(F32)
        o_ref[0, hh * dk:(hh + 1) * dk, :] = (tot_ref[:, hh * tq:(hh + 1) * tq] * zz).astype(BF16)


def _nsa(qt, kc, vct, kaug, vts, kwin, vtw, gt, ztn, mt, *, tq, tk, n_sel):
    b, _, s = qt.shape
    nt = s // tk
    ncp = kc.shape[2]
    hq = NSA_HPG * (NSA_HEAD_DIM // 2)
    nb = mt.shape[0]
    n = NSA_HPG * tq
    in_specs = [
        pl.BlockSpec((1, hq, tq), lambda bi, g, i: (bi, g, i)),
        pl.BlockSpec((1, hq, tq), lambda bi, g, i: (bi, NSA_KV_GROUPS + g, i)),
        pl.BlockSpec((1, 1, ncp, LANES), lambda bi, g, i: (bi, g, 0, 0)),
        pl.BlockSpec((1, 1, NSA_HEAD_DIM, ncp), lambda bi, g, i: (bi, g, 0, 0)),
        pl.BlockSpec((1, 1, s, LANES), lambda bi, g, i: (bi, g, 0, 0)),
        pl.BlockSpec((1, nt, NSA_HEAD_DIM, tk), lambda bi, g, i: (bi, 0, g, 0)),
        pl.BlockSpec((1, 1, s, LANES), lambda bi, g, i: (bi, g, 0, 0)),
        pl.BlockSpec((1, nt, NSA_HEAD_DIM, tk), lambda bi, g, i: (bi, 0, g, 0)),
        pl.BlockSpec((1, GATE_ROWS, tq), lambda bi, g, i: (bi, g, i)),
        pl.BlockSpec((1, NSA_HPG * NSA_HEAD_DIM, tq), lambda bi, g, i: (bi, g, i)),
        pl.BlockSpec(mt.shape, lambda bi, g, i: (0, 0)),
    ]
    kern = functools.partial(_nsa_kernel, tq=tq, tk=tk, n_sel=n_sel)
    return pl.pallas_call(
        kern, grid=(b, NSA_KV_GROUPS, s // tq), in_specs=in_specs,
        out_specs=pl.BlockSpec((1, NSA_HPG * NSA_HEAD_DIM, tq), lambda bi, g, i: (bi, g, i)),
        out_shape=jax.ShapeDtypeStruct((b, NSA_WIDTH, s), BF16),
        scratch_shapes=[pltpu.VMEM((NSA_HEAD_DIM + nb, n), BF16), pltpu.VMEM((tk, n), F32), pltpu.VMEM((tk, n), F32),
                        pltpu.VMEM((1, n), F32), pltpu.VMEM((1, n), F32),
                        pltpu.VMEM((NSA_HEAD_DIM, n), F32), pltpu.VMEM((NSA_HEAD_DIM, n), F32)],
        compiler_params=pltpu.CompilerParams(vmem_limit_bytes=VMEM_LIMIT_BYTES),
        name="nsa",
    )(qt, qt, kc, vct, kaug, vts, kwin, vtw, gt, ztn, mt)


def _mla_kernel(q_ref, k_ref, vt_ref, z_ref, wvt_ref, o_ref, qs_ref, sa_ref, sb_ref, m_ref, l_ref, acc_ref,
                *, tq, tk):
    i = pl.program_id(1)
    n = MLA_HEADS * tq
    for hd in range(MLA_HEADS):
        qs_ref[:, hd * tq:(hd + 1) * tq] = q_ref[0, hd]
    m_ref[...] = jnp.full_like(m_ref, NEG_INF)
    l_ref[...] = jnp.zeros_like(l_ref)
    acc_ref[...] = jnp.zeros_like(acc_ref)

    causal = lax.broadcasted_iota(I32, (tk, tq), 0) <= lax.broadcasted_iota(I32, (tk, tq), 1)
    _causal_sweep(lambda j: k_ref[0, pl.ds(pl.multiple_of(j * tk, tk), tk), :], lambda j: vt_ref[0, j],
                  qs_ref, sa_ref, sb_ref, m_ref, l_ref, acc_ref, i, MLA_HEADS, tq, causal)
    o_lat = (acc_ref[...] * (1.0 / l_ref[...])).astype(BF16)
    for hd in range(MLA_HEADS):
        o_h = jnp.dot(wvt_ref[hd], o_lat[:, hd * tq:(hd + 1) * tq], preferred_element_type=F32)
        zz = z_ref[0, hd * MLA_V_DIM:(hd + 1) * MLA_V_DIM, :].astype(F32)
        o_ref[0, hd * MLA_V_DIM:(hd + 1) * MLA_V_DIM, :] = (o_h * zz).astype(BF16)


def _mla(qtm, kmla, vtm, ztm, wvt, *, tq, tk):
    b, s, _ = kmla.shape
    nt = s // tk
    n = MLA_HEADS * tq
    kern = functools.partial(_mla_kernel, tq=tq, tk=tk)
    return pl.pallas_call(
        kern, grid=(b, s // tq),
        in_specs=[pl.BlockSpec((1, MLA_HEADS, MLA_QK, tq), lambda bi, i: (bi, 0, 0, i)),
                  pl.BlockSpec((1, s, MLA_QK), lambda bi, i: (bi, 0, 0)),
                  pl.BlockSpec((1, nt, MLA_KV_RANK, tk), lambda bi, i: (bi, 0, 0, 0)),
                  pl.BlockSpec((1, MLA_WIDTH, tq), lambda bi, i: (bi, 0, i)),
                  pl.BlockSpec(wvt.shape, lambda bi, i: (0, 0, 0))],
        out_specs=pl.BlockSpec((1, MLA_WIDTH, tq), lambda bi, i: (bi, 0, i)),
        out_shape=jax.ShapeDtypeStruct((b, MLA_WIDTH, s), BF16),
        scratch_shapes=[pltpu.VMEM((MLA_QK, n), BF16), pltpu.VMEM((tk, n), F32), pltpu.VMEM((tk, n), F32),
                        pltpu.VMEM((1, n), F32), pltpu.VMEM((1, n), F32),
                        pltpu.VMEM((MLA_KV_RANK, n), F32)],
        compiler_params=pltpu.CompilerParams(vmem_limit_bytes=VMEM_LIMIT_BYTES),
        name="mla",
    )(qtm, kmla, vtm, ztm, wvt)


def _out_kernel(x_ref, mn_ref, mm_ref, w_ref, mod_ref, fg_ref, o_ref, *, final):
    y = lax.dot_general(mn_ref[0], w_ref[0:NSA_WIDTH, :], TN, preferred_element_type=F32)
    y = y + lax.dot_general(mm_ref[0], w_ref[NSA_WIDTH:MIX_WIDTH, :], TN, preferred_element_type=F32)
    x2 = x_ref[0] + mod_ref[0][2:3] * y
    o_ref[0] = _rms(x2, fg_ref[...]) if final else x2


def _out(x, mn, mm, w_out, mod3, fg, *, tm, final):
    b, s, d = x.shape
    return pl.pallas_call(
        functools.partial(_out_kernel, final=final), grid=(b, s // tm),
        in_specs=[pl.BlockSpec((1, tm, d), lambda bi, i: (bi, i, 0)),
                  pl.BlockSpec((1, NSA_WIDTH, tm), lambda bi, i: (bi, 0, i)),
                  pl.BlockSpec((1, MLA_WIDTH, tm), lambda bi, i: (bi, 0, i)),
                  pl.BlockSpec(w_out.shape, lambda bi, i: (0, 0)),
                  pl.BlockSpec((1, 3, d), lambda bi, i: (bi, 0, 0)),
                  pl.BlockSpec((1, d), lambda bi, i: (0, 0))],
        out_specs=pl.BlockSpec((1, tm, d), lambda bi, i: (bi, i, 0)),
        out_shape=jax.ShapeDtypeStruct((b, s, d), F32),
        compiler_params=pltpu.CompilerParams(vmem_limit_bytes=VMEM_LIMIT_BYTES),
        name="out_proj",
    )(x, mn, mm, w_out, mod3, fg)


def _cmp_to_slc_t(ncp, nc, nslc, nb):
    start = np.arange(nc)[:, None] * CMP_STRIDE
    bstart = np.arange(nslc)[None, :] * SLC_BLOCK
    ov = np.minimum(start + CMP_BLOCK, bstart + SLC_BLOCK) - np.maximum(start, bstart)
    m = (np.clip(ov, 0, None) / CMP_BLOCK).astype(np.float32)
    out = np.zeros((nb, ncp), np.float32)
    out[:nslc, :nc] = m.T
    return out


def _layout_w_in(w):
    d = w.shape[0]
    (q_n, kc_n, vc_n, ks_n, vs_n, kw_n, vw_n, gl_n, z_n, cq_m, ckv_m, kr_m, z_m) = jnp.split(w, IN_OFFSETS, axis=-1)
    dk = NSA_HEAD_DIM
    z64 = jnp.zeros((d, LANES - dk), w.dtype)
    wtok = jnp.concatenate(
        [ks_n[:, :dk], z64, ks_n[:, dk:], z64, kw_n[:, :dk], z64, kw_n[:, dk:], z64,
         kc_n, vc_n, cq_m, ckv_m, kr_m, jnp.zeros((d, LANES - MLA_ROPE_DIM), w.dtype)], axis=1)
    qr = q_n.reshape(d, NSA_HEADS, 2, dk // 2)
    q_perm = jnp.concatenate([qr[:, :, 0, :].reshape(d, -1), qr[:, :, 1, :].reshape(d, -1)], axis=1)
    gl = gl_n.reshape(d, NSA_KV_GROUPS, NSA_HPG * N_BRANCH)
    gl = jnp.pad(gl, ((0, 0), (0, 0), (0, GATE_ROWS - NSA_HPG * N_BRANCH))).reshape(d, -1)
    wtr = jnp.concatenate([q_perm, vs_n, vw_n, gl, z_n, z_m], axis=1).T
    assert wtok.shape[1] == TOK_COLS and wtr.shape[0] == TR_ROWS
    return wtok.astype(BF16), wtr.astype(BF16)


def _layout_w1(w1):
    hid = w1.shape[1]
    w1r = w1.reshape(2, CMP_STRIDE, NSA_HEAD_DIM, hid)
    eye = jnp.eye(NSA_KV_GROUPS, dtype=w1.dtype)
    halves = [jnp.einsum('ldh,pg->lpdgh', w1r[k], eye).reshape(CMP_STRIDE * NSA_KV_WIDTH, NSA_KV_GROUPS * hid)
              for k in range(2)]
    return jnp.concatenate(halves, axis=1).astype(BF16)


def kernel(x, c, positions, ada_w, ada_b, norm_g, w_in, cmp_pos, cmp_k_w1, cmp_k_w2, cmp_v_w1, cmp_v_w2,
           q_norm_g, w_q_up, kv_norm_g, w_kv_up, w_out, final_norm_g):
    b, s, d = x.shape
    depth = ada_w.shape[0]
    tm, tq = PROJ_TILE, ATT_TILE
    tk = tq
    assert s % tm == 0 and tm % tk == 0 and WINDOW % tk == 0 and (tq & (tq - 1)) == 0
    assert CMP_BLOCK == 2 * CMP_STRIDE and s % SLC_BLOCK == 0
    nslc = s // SLC_BLOCK
    nb = LANES - NSA_HEAD_DIM
    assert nslc <= nb
    ncp = s // CMP_STRIDE
    nc = ncp - 1

    half_n, half_m = NSA_HEAD_DIM // 2, MLA_ROPE_DIM // 2
    inv_n = ROPE_THETA ** (-jnp.arange(half_n, dtype=F32) / half_n)
    inv_m = ROPE_THETA ** (-jnp.arange(half_m, dtype=F32) / half_m)

    def lane_pattern(inv, half):
        z = jnp.zeros((LANES - 2 * half,), F32)
        ones = jnp.ones((half,), F32)
        return jnp.concatenate([inv, inv, z]), jnp.concatenate([-ones, ones, z])

    inv_ln, sign_ln = lane_pattern(inv_n, half_n)
    inv_lm, sign_lm = lane_pattern(inv_m, half_m)
    pos_f = positions.astype(F32)
    pos_b = jnp.broadcast_to(pos_f.reshape(b * s, 1), (b * s, LANES))
    ct, st, cm, sm = _tok_tables(pos_b, jnp.stack([inv_ln, inv_lm]), jnp.stack([sign_ln, sign_lm]), tm)
    cmp_end = np.minimum(np.arange(ncp) * CMP_STRIDE + CMP_BLOCK - 1, s - 1)
    pos_c = jnp.broadcast_to(pos_f[:, cmp_end].reshape(b * ncp, 1), (b * ncp, LANES))
    cc, sc = _tok_tables(pos_c, inv_ln[None], sign_ln[None], ncp)
    tabs_row = _row_tables(pos_f.reshape(b, 1, s), jnp.broadcast_to(inv_n[:, None], (half_n, tm)),
                           jnp.broadcast_to(inv_m[:, None], (half_m, tm)), tm)

    mt = jnp.asarray(_cmp_to_slc_t(ncp, nc, nslc, nb), dtype=BF16)
    bp = -(-b // SUBLANES) * SUBLANES
    c_pad = jnp.pad(c, ((0, bp - b), (0, 0)))

    for l in range(depth):
        mod = _adaln(c_pad, ada_w[l], ada_b[l].reshape(1, -1))
        mod3 = mod[:b].reshape(b, 3, d)
        wtok, wtr = _layout_w_in(w_in[l])
        wq = w_q_up[l].reshape(MLA_Q_RANK, MLA_HEADS, MLA_NOPE_DIM + MLA_ROPE_DIM)
        wqt = jnp.concatenate([wq[:, :, :MLA_NOPE_DIM].reshape(MLA_Q_RANK, -1),
                               wq[:, :, MLA_NOPE_DIM:MLA_NOPE_DIM + half_m].reshape(MLA_Q_RANK, -1),
                               wq[:, :, MLA_NOPE_DIM + half_m:].reshape(MLA_Q_RANK, -1)], axis=1).T.astype(BF16)
        wkv = w_kv_up[l].reshape(MLA_KV_RANK, MLA_HEADS, MLA_NOPE_DIM + MLA_V_DIM)
        wk = wkv[:, :, :MLA_NOPE_DIM].transpose(1, 0, 2).astype(BF16)
        wvt = wkv[:, :, MLA_NOPE_DIM:].transpose(1, 2, 0).astype(BF16)

        (qt, kaug, kwin, vts, vtw, kcmp, vcmp, gt, ztn, ztm, qtm, kmla, vtm) = _proj(
            x, mod3, norm_g[l].reshape(1, d), wtok, wtr, (ct, st, cm, sm), tabs_row,
            q_norm_g[l].reshape(1, -1), kv_norm_g[l].reshape(1, -1), wqt, wk, tm=tm, tk=tk)

        pos_l = cmp_pos[l]
        ptop = jnp.broadcast_to(pos_l[:CMP_STRIDE, None, :], (CMP_STRIDE, NSA_KV_GROUPS, NSA_HEAD_DIM)).reshape(1, -1)
        pbot = jnp.broadcast_to(pos_l[CMP_STRIDE:, None, :], (CMP_STRIDE, NSA_KV_GROUPS, NSA_HEAD_DIM)).reshape(1, -1)
        wk2 = jnp.pad(cmp_k_w2[l], ((0, 0), (0, LANES - NSA_HEAD_DIM))).astype(BF16)
        kc, vct = _compress(kcmp.reshape(b, ncp, -1), vcmp.reshape(b, ncp, -1), ptop, pbot,
                            _layout_w1(cmp_k_w1[l]), _layout_w1(cmp_v_w1[l]), wk2,
                            cmp_v_w2[l].T.astype(BF16), cc, sc)

        mix_n = _nsa(qt, kc, vct, kaug, vts, kwin, vtw, gt, ztn, mt, tq=tq, tk=tk, n_sel=min(SLC_TOPK, nslc))
        mix_m = _mla(qtm, kmla, vtm, ztm, wvt, tq=tq, tk=tk)
        x = _out(x, mix_n, mix_m, w_out[l].astype(BF16), mod3, final_norm_g.reshape(1, d), tm=tm,
                 final=(l == depth - 1))
    return x
```

```python
import functools

import numpy as np
import jax
import jax.numpy as jnp
from jax import lax
from jax.experimental import pallas as pl
from jax.experimental.pallas import tpu as pltpu

F32 = jnp.float32
BF16 = jnp.bfloat16
I32 = jnp.int32

NSA_HEADS = 8
NSA_KV_GROUPS = 2
NSA_HPG = NSA_HEADS // NSA_KV_GROUPS
NSA_HEAD_DIM = 64
NSA_WIDTH = NSA_HEADS * NSA_HEAD_DIM
NSA_KV_WIDTH = NSA_KV_GROUPS * NSA_HEAD_DIM
CMP_BLOCK = 32
CMP_STRIDE = 16
CMP_HIDDEN = 128
SLC_BLOCK = 64
SLC_TOPK = 16
WINDOW = 512
N_BRANCH = 3
FORCED_SCORE = 1.0e4
MLA_HEADS = 8
MLA_NOPE_DIM = 64
MLA_ROPE_DIM = 32
MLA_V_DIM = 64
MLA_WIDTH = MLA_HEADS * MLA_V_DIM
MLA_Q_RANK = 256
MLA_KV_RANK = 128
MIX_WIDTH = NSA_WIDTH + MLA_WIDTH
ROPE_THETA = 10000.0
NORM_EPS = 1e-6
NEG_INF = -1e30
IN_SIZES = (NSA_WIDTH, NSA_KV_WIDTH, NSA_KV_WIDTH, NSA_KV_WIDTH, NSA_KV_WIDTH, NSA_KV_WIDTH, NSA_KV_WIDTH,
            NSA_HEADS * N_BRANCH, NSA_WIDTH, MLA_Q_RANK, MLA_KV_RANK, MLA_ROPE_DIM, MLA_WIDTH)
IN_OFFSETS = tuple(int(o) for o in np.cumsum(IN_SIZES)[:-1])

LANES = 128
SUBLANES = 8
VMEM_LIMIT_BYTES = 56 * 1024 * 1024

PROJ_TILE = 512
ATT_TILE = 256
SEL_BIAS = -30000.0
GATE_ROWS = 16
ONES_ROWS = 16
MLA_QK = MLA_KV_RANK + MLA_ROPE_DIM
LOG2E = float(np.log2(np.e))
SLC_SHIFT = SLC_BLOCK.bit_length() - 1
CMP_SHIFT = CMP_STRIDE.bit_length() - 1
assert 1 << SLC_SHIFT == SLC_BLOCK and 1 << CMP_SHIFT == CMP_STRIDE

NT = (((1,), (1,)), ((), ()))
TN = (((0,), (0,)), ((), ()))


def _silu(v):
    return v * jax.nn.sigmoid(v)


def _rms(v, g):
    ms = jnp.mean(v * v, axis=-1, keepdims=True)
    return v * lax.rsqrt(ms + NORM_EPS) * g


def _rope_tok(v, c, s_signed, half, lane, base=0):
    up = pltpu.roll(v, LANES - half, axis=1)
    dn = pltpu.roll(v, half, axis=1)
    return v * c + jnp.where(lane < base + half, up, dn) * s_signed


def _rope_lanes(pos_b, inv_ref, sign_ref):
    ang = pos_b * inv_ref[...]
    return jnp.cos(ang), jnp.sin(ang) * sign_ref[...]


def _chain_update(s_t, v_t, m_ref, acc_ref, cols, keep=None):
    if keep is not None:
        s_t = jnp.where(keep, s_t, NEG_INF)
    m_prev = m_ref[:, cols]
    m_new = jnp.maximum(m_prev, jnp.max(s_t, axis=0, keepdims=True))
    alpha = jnp.exp2(m_prev - m_new)
    p = jnp.exp2(s_t - m_new)
    acc_ref[:, cols] = alpha * acc_ref[:, cols] + jnp.dot(v_t, p.astype(BF16), preferred_element_type=F32)
    m_ref[:, cols] = m_new


def _causal_sweep(k_tile, v_tile, q_ref, sa_ref, sb_ref, m_ref, acc_ref, last, n_chains, tq, keep_last):
    def phase(k_next, s_next_ref, s_cur_ref, j_cur, keep=None, lookahead=2):
        for n in range(n_chains + lookahead):
            if k_next is not None and n < n_chains:
                cols = slice(n * tq, (n + 1) * tq)
                s_next_ref[:, cols] = jnp.dot(k_next, q_ref[:, cols], preferred_element_type=F32)
            if n >= lookahead:
                ch = n - lookahead
                cols = slice(ch * tq, (ch + 1) * tq)
                _chain_update(s_cur_ref[:, cols], v_tile(j_cur, ch), m_ref, acc_ref, cols, keep=keep)

    k_0 = k_tile(0)
    for ch in range(n_chains):
        sa_ref[:, ch * tq:(ch + 1) * tq] = jnp.dot(k_0, q_ref[:, ch * tq:(ch + 1) * tq], preferred_element_type=F32)

    def body(jj, carry):
        j = 2 * jj
        phase(k_tile(j + 1), sb_ref, sa_ref, j)
        phase(k_tile(j + 2), sa_ref, sb_ref, j + 1)
        return carry

    lax.fori_loop(0, last >> 1, body, 0)

    @pl.when((last & 1) == 0)
    def _():
        phase(None, None, sa_ref, last, keep=keep_last)

    @pl.when((last & 1) == 1)
    def _():
        phase(k_tile(last), sb_ref, sa_ref, last - 1)
        phase(None, None, sb_ref, last, keep=keep_last)


def _tile_list(items, q_ref, m_ref, acc_ref, tq, lookahead):
    pending = {}
    for n in range(len(items) + lookahead):
        if n < len(items):
            k_t, _, ch, _ = items[n]
            pending[n] = jnp.dot(k_t, q_ref[:, ch * tq:(ch + 1) * tq], preferred_element_type=F32)
        if n >= lookahead:
            _, v_t, ch, keep = items[n - lookahead]
            _chain_update(pending.pop(n - lookahead), v_t, m_ref, acc_ref,
                          slice(ch * tq, (ch + 1) * tq), keep=keep)


def _adaln_kernel(c_ref, w_ref, b_ref, o_ref):
    sc = _silu(c_ref[...])
    o_ref[...] = jnp.dot(sc.astype(BF16), w_ref[...].astype(BF16), preferred_element_type=F32) + b_ref[...]


def _adaln(c_pad, w, b):
    bp, d = c_pad.shape
    n = w.shape[1] // d
    return pl.pallas_call(
        _adaln_kernel,
        grid=(n,),
        in_specs=[pl.BlockSpec((bp, d), lambda j: (0, 0)),
                  pl.BlockSpec((d, d), lambda j: (0, j)),
                  pl.BlockSpec((1, d), lambda j: (0, j))],
        out_specs=pl.BlockSpec((bp, d), lambda j: (0, j)),
        out_shape=jax.ShapeDtypeStruct((bp, n * d), F32),
        name="adaln",
    )(c_pad, w, b)


TOK_KS, TOK_KW, TOK_KC, TOK_VC, TOK_CQ, TOK_CKV, TOK_KR, TOK_COLS = 0, 256, 512, 640, 768, 1024, 1152, 1280
KR_LANE = NSA_HEAD_DIM
TR_Q, TR_VS, TR_VW, TR_G, TR_ZN, TR_ZM, TR_ROWS = 0, 512, 640, 768, 800, 1312, 1824


def _proj_kernel(x_ref, mod_ref, ng_ref, wtok_ref, wtr_ref, posb_ref, posr_ref, invl_ref, signl_ref,
                 invn_ref, invm_ref, qng_ref, kvng_ref, wqt_ref, wk_ref, wv_ref,
                 qt_ref, kaug_ref, kwin_ref, vts_ref, vtw_ref, kcmp_ref, vcmp_ref, gt_ref,
                 ztn_ref, ztm_ref, qtm_ref, kmla_ref, vtm_ref, *, tm, tk, scale_nsa, scale_mla):
    i = pl.program_id(1)
    mod = mod_ref[0]
    h = _rms(x_ref[0], ng_ref[...]) * (1.0 + mod[1:2]) + mod[0:1]
    hb = h.astype(BF16)
    tok = jnp.dot(hb, wtok_ref[...], preferred_element_type=F32)
    tr = lax.dot_general(wtr_ref[...], hb, NT, preferred_element_type=F32)

    lane = lax.broadcasted_iota(I32, (tm, LANES), 1)
    row = lax.broadcasted_iota(I32, (tm, LANES), 0)
    blk = (i * tm + row) >> SLC_SHIFT
    onehot = (lane - NSA_HEAD_DIM == blk).astype(F32)
    ct, st = _rope_lanes(posb_ref[...], invl_ref, signl_ref)
    half_n = NSA_HEAD_DIM // 2
    for g in range(NSA_KV_GROUPS):
        ks = _rope_tok(tok[:, TOK_KS + LANES * g:TOK_KS + LANES * (g + 1)], ct, st, half_n, lane)
        kaug_ref[0, g] = jnp.where(lane >= NSA_HEAD_DIM, onehot, ks).astype(BF16)
        kw = _rope_tok(tok[:, TOK_KW + LANES * g:TOK_KW + LANES * (g + 1)], ct, st, half_n, lane)
        kwin_ref[0, g] = kw.astype(BF16)
    kcmp_ref[0] = tok[:, TOK_KC:TOK_KC + LANES]
    vcmp_ref[0] = tok[:, TOK_VC:TOK_VC + LANES]

    ckvn = _rms(tok[:, TOK_CKV:TOK_CKV + MLA_KV_RANK], kvng_ref[...])
    krr = _rope_tok(tok[:, TOK_KR:TOK_KR + LANES], ct, st, MLA_ROPE_DIM // 2, lane, base=KR_LANE)
    kmla_ref[0, :, MLA_KV_RANK:MLA_QK] = krr[:, KR_LANE:KR_LANE + MLA_ROPE_DIM].astype(BF16)
    ckvb = ckvn.astype(BF16)
    kmla_ref[0, :, 0:MLA_KV_RANK] = ckvb
    vtm = lax.dot_general(wv_ref[...], ckvb, NT, preferred_element_type=F32).astype(BF16)
    ones = jnp.ones((ONES_ROWS, tk), BF16)
    vr = MLA_V_DIM + ONES_ROWS
    for ii in range(tm // tk):
        for hd in range(MLA_HEADS):
            vtm_ref[0, ii, hd * vr:hd * vr + MLA_V_DIM, :] = vtm[hd * MLA_V_DIM:(hd + 1) * MLA_V_DIM,
                                                                 ii * tk:(ii + 1) * tk]
            vtm_ref[0, ii, hd * vr + MLA_V_DIM:(hd + 1) * vr, :] = ones

    cqn = _rms(tok[:, TOK_CQ:TOK_CQ + MLA_Q_RANK], qng_ref[...]).astype(BF16)
    qm = lax.dot_general(wqt_ref[...], cqn, NT, preferred_element_type=F32)
    nq = MLA_HEADS * MLA_NOPE_DIM
    hr = MLA_ROPE_DIM // 2
    x1 = qm[nq:nq + MLA_HEADS * hr].reshape(MLA_HEADS, hr, tm)
    x2 = qm[nq + MLA_HEADS * hr:nq + 2 * MLA_HEADS * hr].reshape(MLA_HEADS, hr, tm)
    ang_m = invm_ref[...] * posr_ref[0]
    cm_t, sm_t = jnp.cos(ang_m)[None], jnp.sin(ang_m)[None]
    o1 = (x1 * cm_t - x2 * sm_t) * scale_mla
    o2 = (x2 * cm_t + x1 * sm_t) * scale_mla
    for hd in range(MLA_HEADS):
        qn_h = qm[hd * MLA_NOPE_DIM:(hd + 1) * MLA_NOPE_DIM].astype(BF16)
        qabs = jnp.dot(wk_ref[hd], qn_h, preferred_element_type=F32)
        qtm_ref[0, hd, 0:MLA_KV_RANK, :] = (qabs * scale_mla).astype(BF16)
        qtm_ref[0, hd, MLA_KV_RANK:MLA_KV_RANK + hr, :] = o1[hd].astype(BF16)
        qtm_ref[0, hd, MLA_KV_RANK + hr:MLA_QK, :] = o2[hd].astype(BF16)

    hq = NSA_HEADS * half_n
    q1 = tr[TR_Q:TR_Q + hq].reshape(NSA_HEADS, half_n, tm)
    q2 = tr[TR_Q + hq:TR_Q + 2 * hq].reshape(NSA_HEADS, half_n, tm)
    ang_n = invn_ref[...] * posr_ref[0]
    cn_t, sn_t = jnp.cos(ang_n)[None], jnp.sin(ang_n)[None]
    qt_ref[0, 0:hq, :] = ((q1 * cn_t - q2 * sn_t) * scale_nsa).reshape(hq, tm).astype(BF16)
    qt_ref[0, hq:2 * hq, :] = ((q2 * cn_t + q1 * sn_t) * scale_nsa).reshape(hq, tm).astype(BF16)

    vts = tr[TR_VS:TR_VS + NSA_KV_WIDTH].astype(BF16)
    vtw = tr[TR_VW:TR_VW + NSA_KV_WIDTH].astype(BF16)
    dk = NSA_HEAD_DIM
    gr = dk + ONES_ROWS
    for ii in range(tm // tk):
        for g in range(NSA_KV_GROUPS):
            vts_ref[0, ii, g * gr:g * gr + dk, :] = vts[g * dk:(g + 1) * dk, ii * tk:(ii + 1) * tk]
            vtw_ref[0, ii, g * gr:g * gr + dk, :] = vtw[g * dk:(g + 1) * dk, ii * tk:(ii + 1) * tk]
            vts_ref[0, ii, g * gr + dk:(g + 1) * gr, :] = ones
            vtw_ref[0, ii, g * gr + dk:(g + 1) * gr, :] = ones
    gt_ref[0] = jax.nn.sigmoid(tr[TR_G:TR_G + NSA_KV_GROUPS * GATE_ROWS])
    ztn_ref[0] = _silu(tr[TR_ZN:TR_ZN + NSA_WIDTH]).astype(BF16)
    ztm_ref[0] = _silu(tr[TR_ZM:TR_ZM + MLA_WIDTH]).astype(BF16)


def _proj(x, mod3, ng, wtok, wtr, pos_b, pos_row, inv_l, sign_l, inv_nb, inv_mb, qng, kvng, wqt, wk, wv, *, tm, tk):
    b, s, d = x.shape
    nt = s // tk
    v_rows_n = NSA_KV_GROUPS * (NSA_HEAD_DIM + ONES_ROWS)
    v_rows_m = MLA_HEADS * (MLA_V_DIM + ONES_ROWS)
    tile_tok = pl.BlockSpec((tm, LANES), lambda bi, i: (bi * (s // tm) + i, 0))

    def full(a):
        return pl.BlockSpec(a.shape, lambda bi, i, _n=a.ndim: (0,) * _n)

    in_specs = [pl.BlockSpec((1, tm, d), lambda bi, i: (bi, i, 0)),
                pl.BlockSpec((1, 3, d), lambda bi, i: (bi, 0, 0)),
                full(ng), full(wtok), full(wtr),
                tile_tok, pl.BlockSpec((1, 1, tm), lambda bi, i: (bi, 0, i)),
                full(inv_l), full(sign_l), full(inv_nb), full(inv_mb),
                full(qng), full(kvng), full(wqt), full(wk), full(wv)]
    out_shape = [
        jax.ShapeDtypeStruct((b, NSA_WIDTH, s), BF16),
        jax.ShapeDtypeStruct((b, NSA_KV_GROUPS, s, LANES), BF16),
        jax.ShapeDtypeStruct((b, NSA_KV_GROUPS, s, LANES), BF16),
        jax.ShapeDtypeStruct((b, nt, v_rows_n, tk), BF16),
        jax.ShapeDtypeStruct((b, nt, v_rows_n, tk), BF16),
        jax.ShapeDtypeStruct((b, s, NSA_KV_WIDTH), F32),
        jax.ShapeDtypeStruct((b, s, NSA_KV_WIDTH), F32),
        jax.ShapeDtypeStruct((b, NSA_KV_GROUPS * GATE_ROWS, s), F32),
        jax.ShapeDtypeStruct((b, NSA_WIDTH, s), BF16),
        jax.ShapeDtypeStruct((b, MLA_WIDTH, s), BF16),
        jax.ShapeDtypeStruct((b, MLA_HEADS, MLA_QK, s), BF16),
        jax.ShapeDtypeStruct((b, s, MLA_QK), BF16),
        jax.ShapeDtypeStruct((b, nt, v_rows_m, tk), BF16),
    ]
    out_specs = [
        pl.BlockSpec((1, NSA_WIDTH, tm), lambda bi, i: (bi, 0, i)),
        pl.BlockSpec((1, NSA_KV_GROUPS, tm, LANES), lambda bi, i: (bi, 0, i, 0)),
        pl.BlockSpec((1, NSA_KV_GROUPS, tm, LANES), lambda bi, i: (bi, 0, i, 0)),
        pl.BlockSpec((1, tm // tk, v_rows_n, tk), lambda bi, i: (bi, i, 0, 0)),
        pl.BlockSpec((1, tm // tk, v_rows_n, tk), lambda bi, i: (bi, i, 0, 0)),
        pl.BlockSpec((1, tm, NSA_KV_WIDTH), lambda bi, i: (bi, i, 0)),
        pl.BlockSpec((1, tm, NSA_KV_WIDTH), lambda bi, i: (bi, i, 0)),
        pl.BlockSpec((1, NSA_KV_GROUPS * GATE_ROWS, tm), lambda bi, i: (bi, 0, i)),
        pl.BlockSpec((1, NSA_WIDTH, tm), lambda bi, i: (bi, 0, i)),
        pl.BlockSpec((1, MLA_WIDTH, tm), lambda bi, i: (bi, 0, i)),
        pl.BlockSpec((1, MLA_HEADS, MLA_QK, tm), lambda bi, i: (bi, 0, 0, i)),
        pl.BlockSpec((1, tm, MLA_QK), lambda bi, i: (bi, i, 0)),
        pl.BlockSpec((1, tm // tk, v_rows_m, tk), lambda bi, i: (bi, i, 0, 0)),
    ]
    kern = functools.partial(_proj_kernel, tm=tm, tk=tk, scale_nsa=NSA_HEAD_DIM ** -0.5 * LOG2E,
                             scale_mla=(MLA_NOPE_DIM + MLA_ROPE_DIM) ** -0.5 * LOG2E)
    return pl.pallas_call(
        kern, grid=(b, s // tm), in_specs=in_specs, out_specs=out_specs, out_shape=out_shape,
        compiler_params=pltpu.CompilerParams(vmem_limit_bytes=VMEM_LIMIT_BYTES),
        name="proj",
    )(x, mod3, ng, wtok, wtr, pos_b, pos_row, inv_l, sign_l, inv_nb, inv_mb, qng, kvng, wqt, wk, wv)


def _compress_kernel(k_ref, v_ref, ptop_ref, pbot_ref, wk1_ref, wv1_ref, wk2_ref, wv2t_ref, posc_ref, invl_ref,
                     signl_ref, kc_ref, vct_ref):
    ncp = k_ref.shape[1]
    gw = NSA_KV_GROUPS * CMP_HIDDEN
    lane = lax.broadcasted_iota(I32, (ncp, LANES), 1)

    def hidden(r_ref, w1_ref):
        r = r_ref[0]
        a = jnp.dot((r + ptop_ref[...]).astype(BF16), w1_ref[:, 0:gw], preferred_element_type=F32)
        bt = jnp.dot((r + pbot_ref[...]).astype(BF16), w1_ref[:, gw:2 * gw], preferred_element_type=F32)
        return _silu(a + pltpu.roll(bt, ncp - 1, axis=0))

    cc, sc = _rope_lanes(posc_ref[...], invl_ref, signl_ref)
    hk = hidden(k_ref, wk1_ref)
    hv = hidden(v_ref, wv1_ref)
    for g in range(NSA_KV_GROUPS):
        hkg = hk[:, g * CMP_HIDDEN:(g + 1) * CMP_HIDDEN].astype(BF16)
        kc = jnp.dot(hkg, wk2_ref[...], preferred_element_type=F32)
        kc_ref[0, g] = _rope_tok(kc, cc, sc, NSA_HEAD_DIM // 2, lane).astype(BF16)
        hvg = hv[:, g * CMP_HIDDEN:(g + 1) * CMP_HIDDEN].astype(BF16)
        vct_ref[0, g] = lax.dot_general(wv2t_ref[...], hvg, NT, preferred_element_type=F32).astype(BF16)


def _compress(kcmp_r, vcmp_r, ptop, pbot, wk1, wv1, wk2, wv2t, pos_c, inv_l, sign_l):
    b, ncp, width = kcmp_r.shape

    def full(a):
        return pl.BlockSpec(a.shape, lambda bi, _n=a.ndim: (0,) * _n)

    blk = pl.BlockSpec((1, ncp, width), lambda bi: (bi, 0, 0))
    tab = pl.BlockSpec((ncp, LANES), lambda bi: (bi, 0))
    return pl.pallas_call(
        _compress_kernel, grid=(b,),
        in_specs=[blk, blk, full(ptop), full(pbot), full(wk1), full(wv1), full(wk2), full(wv2t), tab,
                  full(inv_l), full(sign_l)],
        out_specs=[pl.BlockSpec((1, NSA_KV_GROUPS, ncp, LANES), lambda bi: (bi, 0, 0, 0)),
                   pl.BlockSpec((1, NSA_KV_GROUPS, NSA_HEAD_DIM, ncp), lambda bi: (bi, 0, 0, 0))],
        out_shape=[jax.ShapeDtypeStruct((b, NSA_KV_GROUPS, ncp, LANES), BF16),
                   jax.ShapeDtypeStruct((b, NSA_KV_GROUPS, NSA_HEAD_DIM, ncp), BF16)],
        compiler_params=pltpu.CompilerParams(vmem_limit_bytes=VMEM_LIMIT_BYTES),
        name="compress",
    )(kcmp_r, vcmp_r, ptop, pbot, wk1, wv1, wk2, wv2t, pos_c, inv_l, sign_l)


def _nsa_kernel(q1_ref, q2_ref, kc_ref, vct_ref, kaug_ref, vts_ref, kwin_ref, vtw_ref, g_ref, z_ref, mt_ref,
                o_ref, qaug_ref, sa_ref, sb_ref, m_ref, acc_ref, tot_ref, imp_ref, rank_ref, *, tq, tk, n_sel):
    i = pl.program_id(2)
    hpg, dk, half = NSA_HPG, NSA_HEAD_DIM, NSA_HEAD_DIM // 2
    for hh in range(hpg):
        qaug_ref[0:half, hh * tq:(hh + 1) * tq] = q1_ref[0, hh * half:(hh + 1) * half, :]
        qaug_ref[half:dk, hh * tq:(hh + 1) * tq] = q2_ref[0, hh * half:(hh + 1) * half, :]

    def gate(hh, branch):
        return g_ref[0, hh * N_BRANCH + branch:hh * N_BRANCH + branch + 1, :]

    ncp = kc_ref.shape[2]
    t_row = i * tq + lax.broadcasted_iota(I32, (1, tq), 1)
    last_n = (t_row - (CMP_BLOCK - 1)) >> CMP_SHIFT
    valid = lax.broadcasted_iota(I32, (ncp, tq), 0) <= last_n
    col_ok = last_n >= 0
    kc = kc_ref[0, 0, :, 0:dk]
    s_cmp = [jnp.dot(kc, qaug_ref[0:dk, hh * tq:(hh + 1) * tq], preferred_element_type=F32) for hh in range(hpg)]
    psum = None
    for hh in range(hpg):
        cols = slice(hh * tq, (hh + 1) * tq)
        s_c = jnp.where(valid, s_cmp[hh], NEG_INF)
        e = jnp.exp2(s_c - jnp.max(s_c, axis=0, keepdims=True))
        l_c = jnp.sum(e, axis=0, keepdims=True)
        p_c = e * jnp.where(col_ok, 1.0 / l_c, 0.0)
        o_c = jnp.dot(vct_ref[0, 0], p_c.astype(BF16), preferred_element_type=F32)
        tot_ref[:, cols] = gate(hh, 0) * o_c
        psum = p_c if psum is None else psum + p_c

    row_k = lax.broadcasted_iota(I32, (tk, tq), 0)
    col_q = lax.broadcasted_iota(I32, (tk, tq), 1)
    causal = row_k <= col_q

    def reset():
        m_ref[...] = jnp.full_like(m_ref, NEG_INF)
        acc_ref[...] = jnp.zeros_like(acc_ref)

    def add_branch(branch):
        for hh in range(hpg):
            cols = slice(hh * tq, (hh + 1) * tq)
            inv_l = 1.0 / acc_ref[dk:dk + 1, cols]
            tot_ref[:, cols] = tot_ref[:, cols] + (gate(hh, branch) * inv_l) * acc_ref[0:dk, cols]

    hi = psum.astype(BF16)
    lo = (psum - hi.astype(F32)).astype(BF16)
    mt = mt_ref[...]
    imp = jnp.dot(mt, hi, preferred_element_type=F32) + jnp.dot(mt, lo, preferred_element_type=F32)

    reset()
    n_back = WINDOW // tk
    items = []
    for back in range(n_back + 1):
        jb = jnp.maximum(i - back, 0)
        kt_b = kwin_ref[0, 0, pl.ds(pl.multiple_of(jb * tk, tk), tk), 0:dk]
        if back == 0:
            keep = causal
        elif back == n_back:
            keep = (row_k > col_q) & (i >= back)
        else:
            keep = jnp.broadcast_to(i >= back, (tk, tq))
        items += [(kt_b, vtw_ref[0, jb], hh, keep) for hh in range(hpg)]
    _tile_list(items, qaug_ref.at[0:dk, :], m_ref, acc_ref, tq, lookahead=2)
    add_branch(2)

    nb = mt.shape[0]
    j_idx = lax.broadcasted_iota(I32, (nb, tq), 0)
    cur = (i * tq + lax.broadcasted_iota(I32, (nb, tq), 1)) >> SLC_SHIFT
    forced = (j_idx == 0) | (j_idx == cur) | (j_idx == cur - 1)
    imp_ref[...] = jnp.where(forced, FORCED_SCORE, jnp.where(j_idx > cur, -FORCED_SCORE, imp))
    rank_ref[...] = jnp.zeros_like(rank_ref)
    sub = lax.broadcasted_iota(I32, (SUBLANES, tq), 0)
    last_group = ((i + 1) * (tq // SLC_BLOCK) - 1) // SUBLANES

    def count(c, v):
        blk = imp_ref[v * SUBLANES:(v + 1) * SUBLANES, :]
        cnt = rank_ref[v * SUBLANES:(v + 1) * SUBLANES, :]
        for rr in range(SUBLANES):
            row = imp_ref[c * SUBLANES + rr:c * SUBLANES + rr + 1, :]
            if c < v:
                beats = row >= blk
            elif c > v:
                beats = row > blk
            else:
                beats = (row > blk) | ((row == blk) & (sub > rr))
            cnt = cnt + beats.astype(I32)
        rank_ref[v * SUBLANES:(v + 1) * SUBLANES, :] = cnt

    for lvl in range(nb // SUBLANES):
        @pl.when(lvl <= last_group)
        def _(lvl=lvl):
            for v in range(lvl + 1):
                count(lvl, v)
            for c in range(lvl):
                count(c, lvl)

    bias = jnp.where(rank_ref[...] < n_sel, 0.0, SEL_BIAS).astype(BF16)
    for hh in range(hpg):
        qaug_ref[dk:dk + nb, hh * tq:(hh + 1) * tq] = bias

    reset()
    _causal_sweep(lambda j: kaug_ref[0, 0, pl.ds(pl.multiple_of(j * tk, tk), tk), :], lambda j, ch: vts_ref[0, j],
                  qaug_ref, sa_ref, sb_ref, m_ref, acc_ref, i, hpg, tq, causal)
    add_branch(1)

    for hh in range(hpg):
        zz = z_ref[0, hh * dk:(hh + 1) * dk, :].astype(F32)
        o_ref[0, hh * dk:(hh + 1) * dk, :] = (tot_ref[:, hh * tq:(hh + 1) * tq] * zz).astype(BF16)


def _nsa(qt, kc, vct, kaug, vts, kwin, vtw, gt, ztn, mt, *, tq, tk, n_sel):
    b, _, s = qt.shape
    nt = s // tk
    ncp = kc.shape[2]
    hq = NSA_HPG * (NSA_HEAD_DIM // 2)
    nb = mt.shape[0]
    n = NSA_HPG * tq
    vr = NSA_HEAD_DIM + ONES_ROWS
    in_specs = [
        pl.BlockSpec((1, hq, tq), lambda bi, g, i: (bi, g, i)),
        pl.BlockSpec((1, hq, tq), lambda bi, g, i: (bi, NSA_KV_GROUPS + g, i)),
        pl.BlockSpec((1, 1, ncp, LANES), lambda bi, g, i: (bi, g, 0, 0)),
        pl.BlockSpec((1, 1, NSA_HEAD_DIM, ncp), lambda bi, g, i: (bi, g, 0, 0)),
        pl.BlockSpec((1, 1, s, LANES), lambda bi, g, i: (bi, g, 0, 0)),
        pl.BlockSpec((1, nt, vr, tk), lambda bi, g, i: (bi, 0, g, 0)),
        pl.BlockSpec((1, 1, s, LANES), lambda bi, g, i: (bi, g, 0, 0)),
        pl.BlockSpec((1, nt, vr, tk), lambda bi, g, i: (bi, 0, g, 0)),
        pl.BlockSpec((1, GATE_ROWS, tq), lambda bi, g, i: (bi, g, i)),
        pl.BlockSpec((1, NSA_HPG * NSA_HEAD_DIM, tq), lambda bi, g, i: (bi, g, i)),
        pl.BlockSpec(mt.shape, lambda bi, g, i: (0, 0)),
    ]
    kern = functools.partial(_nsa_kernel, tq=tq, tk=tk, n_sel=n_sel)
    return pl.pallas_call(
        kern, grid=(b, NSA_KV_GROUPS, s // tq), in_specs=in_specs,
        out_specs=pl.BlockSpec((1, NSA_HPG * NSA_HEAD_DIM, tq), lambda bi, g, i: (bi, g, i)),
        out_shape=jax.ShapeDtypeStruct((b, NSA_WIDTH, s), BF16),
        scratch_shapes=[pltpu.VMEM((NSA_HEAD_DIM + nb, n), BF16), pltpu.VMEM((tk, n), F32), pltpu.VMEM((tk, n), F32),
                        pltpu.VMEM((1, n), F32), pltpu.VMEM((vr, n), F32), pltpu.VMEM((NSA_HEAD_DIM, n), F32),
                        pltpu.VMEM((nb, tq), F32), pltpu.VMEM((nb, tq), I32)],
        compiler_params=pltpu.CompilerParams(vmem_limit_bytes=VMEM_LIMIT_BYTES),
        name="nsa",
    )(qt, qt, kc, vct, kaug, vts, kwin, vtw, gt, ztn, mt)


def _mla_kernel(q_ref, k_ref, vt_ref, z_ref, o_ref, qs_ref, sa_ref, sb_ref, m_ref, acc_ref, *, tq, tk):
    i = pl.program_id(1)
    dv, vr = MLA_V_DIM, MLA_V_DIM + ONES_ROWS
    for hd in range(MLA_HEADS):
        qs_ref[:, hd * tq:(hd + 1) * tq] = q_ref[0, hd]
    m_ref[...] = jnp.full_like(m_ref, NEG_INF)
    acc_ref[...] = jnp.zeros_like(acc_ref)

    causal = lax.broadcasted_iota(I32, (tk, tq), 0) <= lax.broadcasted_iota(I32, (tk, tq), 1)
    _causal_sweep(lambda j: k_ref[0, pl.ds(pl.multiple_of(j * tk, tk), tk), :],
                  lambda j, hd: vt_ref[0, j, hd * vr:(hd + 1) * vr, :],
                  qs_ref, sa_ref, sb_ref, m_ref, acc_ref, i, MLA_HEADS, tq, causal)
    for hd in range(MLA_HEADS):
        cols = slice(hd * tq, (hd + 1) * tq)
        o_h = acc_ref[0:dv, cols] * (1.0 / acc_ref[dv:dv + 1, cols])
        zz = z_ref[0, hd * dv:(hd + 1) * dv, :].astype(F32)
        o_ref[0, hd * dv:(hd + 1) * dv, :] = (o_h * zz).astype(BF16)


def _mla(qtm, kmla, vtm, ztm, *, tq, tk):
    b, s, _ = kmla.shape
    nt = s // tk
    n = MLA_HEADS * tq
    vrows = MLA_HEADS * (MLA_V_DIM + ONES_ROWS)
    kern = functools.partial(_mla_kernel, tq=tq, tk=tk)
    return pl.pallas_call(
        kern, grid=(b, s // tq),
        in_specs=[pl.BlockSpec((1, MLA_HEADS, MLA_QK, tq), lambda bi, i: (bi, 0, 0, i)),
                  pl.BlockSpec((1, s, MLA_QK), lambda bi, i: (bi, 0, 0)),
                  pl.BlockSpec((1, nt, vrows, tk), lambda bi, i: (bi, 0, 0, 0)),
                  pl.BlockSpec((1, MLA_WIDTH, tq), lambda bi, i: (bi, 0, i))],
        out_specs=pl.BlockSpec((1, MLA_WIDTH, tq), lambda bi, i: (bi, 0, i)),
        out_shape=jax.ShapeDtypeStruct((b, MLA_WIDTH, s), BF16),
        scratch_shapes=[pltpu.VMEM((MLA_QK, n), BF16), pltpu.VMEM((tk, n), F32), pltpu.VMEM((tk, n), F32),
                        pltpu.VMEM((1, n), F32), pltpu.VMEM((MLA_V_DIM + ONES_ROWS, n), F32)],
        compiler_params=pltpu.CompilerParams(vmem_limit_bytes=VMEM_LIMIT_BYTES),
        name="mla",
    )(qtm, kmla, vtm, ztm)


def _out_kernel(x_ref, mn_ref, mm_ref, w_ref, mod_ref, fg_ref, o_ref, *, final):
    y = lax.dot_general(mn_ref[0], w_ref[0:NSA_WIDTH, :], TN, preferred_element_type=F32)
    y = y + lax.dot_general(mm_ref[0], w_ref[NSA_WIDTH:MIX_WIDTH, :], TN, preferred_element_type=F32)
    x2 = x_ref[0] + mod_ref[0][2:3] * y
    o_ref[0] = _rms(x2, fg_ref[...]) if final else x2


def _out(x, mn, mm, w_out, mod3, fg, *, tm, final):
    b, s, d = x.shape
    return pl.pallas_call(
        functools.partial(_out_kernel, final=final), grid=(b, s // tm),
        in_specs=[pl.BlockSpec((1, tm, d), lambda bi, i: (bi, i, 0)),
                  pl.BlockSpec((1, NSA_WIDTH, tm), lambda bi, i: (bi, 0, i)),
                  pl.BlockSpec((1, MLA_WIDTH, tm), lambda bi, i: (bi, 0, i)),
                  pl.BlockSpec(w_out.shape, lambda bi, i: (0, 0)),
                  pl.BlockSpec((1, 3, d), lambda bi, i: (bi, 0, 0)),
                  pl.BlockSpec((1, d), lambda bi, i: (0, 0))],
        out_specs=pl.BlockSpec((1, tm, d), lambda bi, i: (bi, i, 0)),
        out_shape=jax.ShapeDtypeStruct((b, s, d), F32),
        compiler_params=pltpu.CompilerParams(vmem_limit_bytes=VMEM_LIMIT_BYTES),
        name="out_proj",
    )(x, mn, mm, w_out, mod3, fg)


def _cmp_to_slc_t(ncp, nc, nslc, nb):
    start = np.arange(nc)[:, None] * CMP_STRIDE
    bstart = np.arange(nslc)[None, :] * SLC_BLOCK
    ov = np.minimum(start + CMP_BLOCK, bstart + SLC_BLOCK) - np.maximum(start, bstart)
    m = (np.clip(ov, 0, None) / CMP_BLOCK).astype(np.float32)
    out = np.zeros((nb, ncp), np.float32)
    out[:nslc, :nc] = m.T
    return out


def _layout_w_in(w):
    d = w.shape[0]
    (q_n, kc_n, vc_n, ks_n, vs_n, kw_n, vw_n, gl_n, z_n, cq_m, ckv_m, kr_m, z_m) = jnp.split(w, IN_OFFSETS, axis=-1)
    dk = NSA_HEAD_DIM
    z64 = jnp.zeros((d, LANES - dk), w.dtype)
    wtok = jnp.concatenate(
        [ks_n[:, :dk], z64, ks_n[:, dk:], z64, kw_n[:, :dk], z64, kw_n[:, dk:], z64,
         kc_n, vc_n, cq_m, ckv_m, jnp.zeros((d, KR_LANE), w.dtype), kr_m,
         jnp.zeros((d, LANES - KR_LANE - MLA_ROPE_DIM), w.dtype)], axis=1)
    qr = q_n.reshape(d, NSA_HEADS, 2, dk // 2)
    q_perm = jnp.concatenate([qr[:, :, 0, :].reshape(d, -1), qr[:, :, 1, :].reshape(d, -1)], axis=1)
    gl = gl_n.reshape(d, NSA_KV_GROUPS, NSA_HPG * N_BRANCH)
    gl = jnp.pad(gl, ((0, 0), (0, 0), (0, GATE_ROWS - NSA_HPG * N_BRANCH))).reshape(d, -1)
    wtr = jnp.concatenate([q_perm, vs_n, vw_n, gl, z_n, z_m], axis=1).T
    assert wtok.shape[1] == TOK_COLS and wtr.shape[0] == TR_ROWS
    return wtok.astype(BF16), wtr.astype(BF16)


def _layout_w1(w1):
    hid = w1.shape[1]
    w1r = w1.reshape(2, CMP_STRIDE, NSA_HEAD_DIM, hid)
    eye = jnp.eye(NSA_KV_GROUPS, dtype=w1.dtype)
    halves = [jnp.einsum('ldh,pg->lpdgh', w1r[k], eye).reshape(CMP_STRIDE * NSA_KV_WIDTH, NSA_KV_GROUPS * hid)
              for k in range(2)]
    return jnp.concatenate(halves, axis=1).astype(BF16)


def kernel(x, c, positions, ada_w, ada_b, norm_g, w_in, cmp_pos, cmp_k_w1, cmp_k_w2, cmp_v_w1, cmp_v_w2,
           q_norm_g, w_q_up, kv_norm_g, w_kv_up, w_out, final_norm_g):
    b, s, d = x.shape
    depth = ada_w.shape[0]
    tm, tq = PROJ_TILE, ATT_TILE
    tk = tq
    assert s % tm == 0 and tm % tk == 0 and WINDOW % tk == 0 and (tq & (tq - 1)) == 0
    assert CMP_BLOCK == 2 * CMP_STRIDE and s % SLC_BLOCK == 0
    nslc = s // SLC_BLOCK
    nb = LANES - NSA_HEAD_DIM
    assert nslc <= nb
    ncp = s // CMP_STRIDE
    nc = ncp - 1

    half_n, half_m = NSA_HEAD_DIM // 2, MLA_ROPE_DIM // 2
    inv_n = ROPE_THETA ** (-jnp.arange(half_n, dtype=F32) / half_n)
    inv_m = ROPE_THETA ** (-jnp.arange(half_m, dtype=F32) / half_m)
    ones_n, ones_m = jnp.ones((half_n,), F32), jnp.ones((half_m,), F32)
    pad = jnp.zeros((LANES - NSA_HEAD_DIM - MLA_ROPE_DIM,), F32)
    inv_l = jnp.concatenate([inv_n, inv_n, inv_m, inv_m, pad])[None]
    sign_l = jnp.concatenate([-ones_n, ones_n, -ones_m, ones_m, pad])[None]
    pos_f = positions.astype(F32)
    pos_b = jnp.broadcast_to(pos_f.reshape(b * s, 1), (b * s, LANES))
    pos_row = pos_f.reshape(b, 1, s)
    cmp_end = np.minimum(np.arange(ncp) * CMP_STRIDE + CMP_BLOCK - 1, s - 1)
    pos_c = jnp.broadcast_to(pos_f[:, cmp_end].reshape(b * ncp, 1), (b * ncp, LANES))
    inv_nb = jnp.broadcast_to(inv_n[:, None], (half_n, tm))
    inv_mb = jnp.broadcast_to(inv_m[:, None], (half_m, tm))

    mt = jnp.asarray(_cmp_to_slc_t(ncp, nc, nslc, nb), dtype=BF16)
    bp = -(-b // SUBLANES) * SUBLANES
    c_pad = jnp.pad(c, ((0, bp - b), (0, 0)))

    for l in range(depth):
        mod = _adaln(c_pad, ada_w[l], ada_b[l].reshape(1, -1))
        mod3 = mod[:b].reshape(b, 3, d)
        wtok, wtr = _layout_w_in(w_in[l])
        wq = w_q_up[l].reshape(MLA_Q_RANK, MLA_HEADS, MLA_NOPE_DIM + MLA_ROPE_DIM)
        wqt = jnp.concatenate([wq[:, :, :MLA_NOPE_DIM].reshape(MLA_Q_RANK, -1),
                               wq[:, :, MLA_NOPE_DIM:MLA_NOPE_DIM + half_m].reshape(MLA_Q_RANK, -1),
                               wq[:, :, MLA_NOPE_DIM + half_m:].reshape(MLA_Q_RANK, -1)], axis=1).T.astype(BF16)
        wkv = w_kv_up[l].reshape(MLA_KV_RANK, MLA_HEADS, MLA_NOPE_DIM + MLA_V_DIM)
        wk = wkv[:, :, :MLA_NOPE_DIM].transpose(1, 0, 2).astype(BF16)
        wv = wkv[:, :, MLA_NOPE_DIM:].transpose(1, 2, 0).reshape(MLA_WIDTH, MLA_KV_RANK).astype(BF16)

        (qt, kaug, kwin, vts, vtw, kcmp, vcmp, gt, ztn, ztm, qtm, kmla, vtm) = _proj(
            x, mod3, norm_g[l].reshape(1, d), wtok, wtr, pos_b, pos_row, inv_l, sign_l, inv_nb, inv_mb,
            q_norm_g[l].reshape(1, -1), kv_norm_g[l].reshape(1, -1), wqt, wk, wv, tm=tm, tk=tk)

        pos_l = cmp_pos[l]
        ptop = jnp.broadcast_to(pos_l[:CMP_STRIDE, None, :], (CMP_STRIDE, NSA_KV_GROUPS, NSA_HEAD_DIM)).reshape(1, -1)
        pbot = jnp.broadcast_to(pos_l[CMP_STRIDE:, None, :], (CMP_STRIDE, NSA_KV_GROUPS, NSA_HEAD_DIM)).reshape(1, -1)
        wk2 = jnp.pad(cmp_k_w2[l], ((0, 0), (0, LANES - NSA_HEAD_DIM))).astype(BF16)
        kc, vct = _compress(kcmp.reshape(b, ncp, -1), vcmp.reshape(b, ncp, -1), ptop, pbot,
                            _layout_w1(cmp_k_w1[l]), _layout_w1(cmp_v_w1[l]), wk2,
                            cmp_v_w2[l].T.astype(BF16), pos_c, inv_l, sign_l)

        mix_n = _nsa(qt, kc, vct, kaug, vts, kwin, vtw, gt, ztn, mt, tq=tq, tk=tk, n_sel=min(SLC_TOPK, nslc))
        mix_m = _mla(qtm, kmla, vtm, ztm, tq=tq, tk=tk)
        x = _out(x, mix_n, mix_m, w_out[l].astype(BF16), mod3, final_norm_g.reshape(1, d), tm=tm,
                 final=(l == depth - 1))
    return x
```

```python
import functools

import numpy as np
import jax
import jax.numpy as jnp
from jax import lax
from jax.experimental import pallas as pl
from jax.experimental.pallas import tpu as pltpu

F32 = jnp.float32
BF16 = jnp.bfloat16
I32 = jnp.int32

NSA_HEADS = 8
NSA_KV_GROUPS = 2
NSA_HPG = NSA_HEADS // NSA_KV_GROUPS
NSA_HEAD_DIM = 64
NSA_WIDTH = NSA_HEADS * NSA_HEAD_DIM
NSA_KV_WIDTH = NSA_KV_GROUPS * NSA_HEAD_DIM
CMP_BLOCK = 32
CMP_STRIDE = 16
CMP_HIDDEN = 128
SLC_BLOCK = 64
SLC_TOPK = 16
WINDOW = 512
N_BRANCH = 3
FORCED_SCORE = 1.0e4
MLA_HEADS = 8
MLA_NOPE_DIM = 64
MLA_ROPE_DIM = 32
MLA_V_DIM = 64
MLA_WIDTH = MLA_HEADS * MLA_V_DIM
MLA_Q_RANK = 256
MLA_KV_RANK = 128
MIX_WIDTH = NSA_WIDTH + MLA_WIDTH
ROPE_THETA = 10000.0
NORM_EPS = 1e-6
NEG_INF = -1e30
IN_SIZES = (NSA_WIDTH, NSA_KV_WIDTH, NSA_KV_WIDTH, NSA_KV_WIDTH, NSA_KV_WIDTH, NSA_KV_WIDTH, NSA_KV_WIDTH,
            NSA_HEADS * N_BRANCH, NSA_WIDTH, MLA_Q_RANK, MLA_KV_RANK, MLA_ROPE_DIM, MLA_WIDTH)
IN_OFFSETS = tuple(int(o) for o in np.cumsum(IN_SIZES)[:-1])

LANES = 128
SUBLANES = 8
VMEM_LIMIT_BYTES = 56 * 1024 * 1024

PROJ_TILE = 512
ATT_TILE = 256
SEL_BIAS = -30000.0
GATE_ROWS = 16
ONES_ROWS = 16
MLA_QK = MLA_KV_RANK + MLA_ROPE_DIM
LOG2E = float(np.log2(np.e))
SLC_SHIFT = SLC_BLOCK.bit_length() - 1
CMP_SHIFT = CMP_STRIDE.bit_length() - 1
assert 1 << SLC_SHIFT == SLC_BLOCK and 1 << CMP_SHIFT == CMP_STRIDE

NT = (((1,), (1,)), ((), ()))
TN = (((0,), (0,)), ((), ()))


def _silu(v):
    return v * jax.nn.sigmoid(v)


def _rms(v, g):
    ms = jnp.mean(v * v, axis=-1, keepdims=True)
    return v * lax.rsqrt(ms + NORM_EPS) * g


def _rope_tok(v, c, s_signed, half, lane, base=0):
    up = pltpu.roll(v, LANES - half, axis=1)
    dn = pltpu.roll(v, half, axis=1)
    return v * c + jnp.where(lane < base + half, up, dn) * s_signed


def _rope_lanes(pos_b, inv_ref, sign_ref):
    ang = pos_b * inv_ref[...]
    return jnp.cos(ang), jnp.sin(ang) * sign_ref[...]


def _chain_update(s_t, v_t, m_ref, acc_ref, ch, keep=None):
    if keep is not None:
        s_t = jnp.where(keep, s_t, NEG_INF)
    m_prev = m_ref[ch]
    m_new = jnp.maximum(m_prev, jnp.max(s_t, axis=0, keepdims=True))
    alpha = jnp.exp2(m_prev - m_new)
    p = jnp.exp2(s_t - m_new)
    acc_ref[ch] = alpha * acc_ref[ch] + jnp.dot(v_t, p.astype(BF16), preferred_element_type=F32)
    m_ref[ch] = m_new


def _causal_sweep(k_tile, v_tile, q_ref, sa_ref, sb_ref, m_ref, acc_ref, last, n_chains, keep_last):
    def phase(k_next, s_next_ref, s_cur_ref, j_cur, keep=None, lookahead=2):
        for n in range(n_chains + lookahead):
            if k_next is not None and n < n_chains:
                s_next_ref[n] = jnp.dot(k_next, q_ref[n], preferred_element_type=F32)
            if n >= lookahead:
                ch = n - lookahead
                _chain_update(s_cur_ref[ch], v_tile(j_cur, ch), m_ref, acc_ref, ch, keep=keep)

    k_0 = k_tile(0)
    for ch in range(n_chains):
        sa_ref[ch] = jnp.dot(k_0, q_ref[ch], preferred_element_type=F32)

    def body(jj, carry):
        j = 2 * jj
        phase(k_tile(j + 1), sb_ref, sa_ref, j)
        phase(k_tile(j + 2), sa_ref, sb_ref, j + 1)
        return carry

    lax.fori_loop(0, last >> 1, body, 0)

    @pl.when((last & 1) == 0)
    def _():
        phase(None, None, sa_ref, last, keep=keep_last)

    @pl.when((last & 1) == 1)
    def _():
        phase(k_tile(last), sb_ref, sa_ref, last - 1)
        phase(None, None, sb_ref, last, keep=keep_last)


def _pipeline(stages, lookahead):
    pending = {}
    for n in range(len(stages) + lookahead):
        if n < len(stages) and stages[n][0] is not None:
            pending[n] = stages[n][0]()
        if n >= lookahead:
            stages[n - lookahead][1](pending.pop(n - lookahead, None))


def _adaln_kernel(c_ref, w_ref, b_ref, o_ref):
    sc = _silu(c_ref[...])
    o_ref[...] = jnp.dot(sc.astype(BF16), w_ref[...].astype(BF16), preferred_element_type=F32) + b_ref[...]


def _adaln(c_pad, w, b):
    bp, d = c_pad.shape
    n = w.shape[1] // d
    return pl.pallas_call(
        _adaln_kernel,
        grid=(n,),
        in_specs=[pl.BlockSpec((bp, d), lambda j: (0, 0)),
                  pl.BlockSpec((d, d), lambda j: (0, j)),
                  pl.BlockSpec((1, d), lambda j: (0, j))],
        out_specs=pl.BlockSpec((bp, d), lambda j: (0, j)),
        out_shape=jax.ShapeDtypeStruct((bp, n * d), F32),
        name="adaln",
    )(c_pad, w, b)


TOK_KS, TOK_KW, TOK_KC, TOK_VC, TOK_CQ, TOK_CKV, TOK_KR, TOK_COLS = 0, 256, 512, 640, 768, 1024, 1152, 1280
KR_LANE = NSA_HEAD_DIM
TR_Q, TR_VS, TR_VW, TR_G, TR_ZN, TR_ZM, TR_ROWS = 0, 512, 640, 768, 800, 1312, 1824


def _proj_kernel(x_ref, mod_ref, ng_ref, wtok_ref, wtr_ref, posb_ref, posr_ref, invl_ref, signl_ref,
                 invn_ref, invm_ref, qng_ref, kvng_ref, wqt_ref, wk_ref, wv_ref,
                 qt_ref, kaug_ref, kwin_ref, vts_ref, vtw_ref, kcmp_ref, vcmp_ref, gt_ref,
                 ztn_ref, ztm_ref, qtm_ref, kmla_ref, vtm_ref, *, tm, tk, scale_nsa, scale_mla):
    i = pl.program_id(1)
    mod = mod_ref[0]
    h = _rms(x_ref[0], ng_ref[...]) * (1.0 + mod[1:2]) + mod[0:1]
    hb = h.astype(BF16)
    tok = jnp.dot(hb, wtok_ref[...], preferred_element_type=F32)
    tr = lax.dot_general(wtr_ref[...], hb, NT, preferred_element_type=F32)

    lane = lax.broadcasted_iota(I32, (tm, LANES), 1)
    row = lax.broadcasted_iota(I32, (tm, LANES), 0)
    blk = (i * tm + row) >> SLC_SHIFT
    onehot = (lane - NSA_HEAD_DIM == blk).astype(F32)
    ct, st = _rope_lanes(posb_ref[...], invl_ref, signl_ref)
    half_n = NSA_HEAD_DIM // 2
    for g in range(NSA_KV_GROUPS):
        ks = _rope_tok(tok[:, TOK_KS + LANES * g:TOK_KS + LANES * (g + 1)], ct, st, half_n, lane)
        kaug_ref[0, g] = jnp.where(lane >= NSA_HEAD_DIM, onehot, ks).astype(BF16)
        kw = _rope_tok(tok[:, TOK_KW + LANES * g:TOK_KW + LANES * (g + 1)], ct, st, half_n, lane)
        kwin_ref[0, g] = kw.astype(BF16)
    kcmp_ref[0] = tok[:, TOK_KC:TOK_KC + LANES]
    vcmp_ref[0] = tok[:, TOK_VC:TOK_VC + LANES]

    ckvn = _rms(tok[:, TOK_CKV:TOK_CKV + MLA_KV_RANK], kvng_ref[...])
    krr = _rope_tok(tok[:, TOK_KR:TOK_KR + LANES], ct, st, MLA_ROPE_DIM // 2, lane, base=KR_LANE)
    kmla_ref[0, :, MLA_KV_RANK:MLA_QK] = krr[:, KR_LANE:KR_LANE + MLA_ROPE_DIM].astype(BF16)
    ckvb = ckvn.astype(BF16)
    kmla_ref[0, :, 0:MLA_KV_RANK] = ckvb
    vtm = lax.dot_general(wv_ref[...], ckvb, NT, preferred_element_type=F32).astype(BF16)
    ones = jnp.ones((ONES_ROWS, tk), BF16)
    vr = MLA_V_DIM + ONES_ROWS
    for ii in range(tm // tk):
        for hd in range(MLA_HEADS):
            vtm_ref[0, ii, hd * vr:hd * vr + MLA_V_DIM, :] = vtm[hd * MLA_V_DIM:(hd + 1) * MLA_V_DIM,
                                                                 ii * tk:(ii + 1) * tk]
            vtm_ref[0, ii, hd * vr + MLA_V_DIM:(hd + 1) * vr, :] = ones

    cqn = _rms(tok[:, TOK_CQ:TOK_CQ + MLA_Q_RANK], qng_ref[...]).astype(BF16)
    qm = lax.dot_general(wqt_ref[...], cqn, NT, preferred_element_type=F32)
    nq = MLA_HEADS * MLA_NOPE_DIM
    hr = MLA_ROPE_DIM // 2
    x1 = qm[nq:nq + MLA_HEADS * hr].reshape(MLA_HEADS, hr, tm)
    x2 = qm[nq + MLA_HEADS * hr:nq + 2 * MLA_HEADS * hr].reshape(MLA_HEADS, hr, tm)
    ang_m = invm_ref[...] * posr_ref[0]
    cm_t, sm_t = jnp.cos(ang_m)[None], jnp.sin(ang_m)[None]
    o1 = (x1 * cm_t - x2 * sm_t) * scale_mla
    o2 = (x2 * cm_t + x1 * sm_t) * scale_mla
    for hd in range(MLA_HEADS):
        qn_h = qm[hd * MLA_NOPE_DIM:(hd + 1) * MLA_NOPE_DIM].astype(BF16)
        qabs = jnp.dot(wk_ref[hd], qn_h, preferred_element_type=F32)
        qtm_ref[0, hd, 0:MLA_KV_RANK, :] = (qabs * scale_mla).astype(BF16)
        qtm_ref[0, hd, MLA_KV_RANK:MLA_KV_RANK + hr, :] = o1[hd].astype(BF16)
        qtm_ref[0, hd, MLA_KV_RANK + hr:MLA_QK, :] = o2[hd].astype(BF16)

    hq = NSA_HEADS * half_n
    q1 = tr[TR_Q:TR_Q + hq].reshape(NSA_HEADS, half_n, tm)
    q2 = tr[TR_Q + hq:TR_Q + 2 * hq].reshape(NSA_HEADS, half_n, tm)
    ang_n = invn_ref[...] * posr_ref[0]
    cn_t, sn_t = jnp.cos(ang_n)[None], jnp.sin(ang_n)[None]
    qt_ref[0, 0:hq, :] = ((q1 * cn_t - q2 * sn_t) * scale_nsa).reshape(hq, tm).astype(BF16)
    qt_ref[0, hq:2 * hq, :] = ((q2 * cn_t + q1 * sn_t) * scale_nsa).reshape(hq, tm).astype(BF16)

    vts = tr[TR_VS:TR_VS + NSA_KV_WIDTH].astype(BF16)
    vtw = tr[TR_VW:TR_VW + NSA_KV_WIDTH].astype(BF16)
    dk = NSA_HEAD_DIM
    gr = dk + ONES_ROWS
    for ii in range(tm // tk):
        for g in range(NSA_KV_GROUPS):
            vts_ref[0, ii, g * gr:g * gr + dk, :] = vts[g * dk:(g + 1) * dk, ii * tk:(ii + 1) * tk]
            vtw_ref[0, ii, g * gr:g * gr + dk, :] = vtw[g * dk:(g + 1) * dk, ii * tk:(ii + 1) * tk]
            vts_ref[0, ii, g * gr + dk:(g + 1) * gr, :] = ones
            vtw_ref[0, ii, g * gr + dk:(g + 1) * gr, :] = ones
    gt_ref[0] = jax.nn.sigmoid(tr[TR_G:TR_G + NSA_KV_GROUPS * GATE_ROWS])
    ztn_ref[0] = _silu(tr[TR_ZN:TR_ZN + NSA_WIDTH]).astype(BF16)
    ztm_ref[0] = _silu(tr[TR_ZM:TR_ZM + MLA_WIDTH]).astype(BF16)


def _proj(x, mod3, ng, wtok, wtr, pos_b, pos_row, inv_l, sign_l, inv_nb, inv_mb, qng, kvng, wqt, wk, wv, *, tm, tk):
    b, s, d = x.shape
    nt = s // tk
    v_rows_n = NSA_KV_GROUPS * (NSA_HEAD_DIM + ONES_ROWS)
    v_rows_m = MLA_HEADS * (MLA_V_DIM + ONES_ROWS)
    tile_tok = pl.BlockSpec((tm, LANES), lambda bi, i: (bi * (s // tm) + i, 0))

    def full(a):
        return pl.BlockSpec(a.shape, lambda bi, i, _n=a.ndim: (0,) * _n)

    in_specs = [pl.BlockSpec((1, tm, d), lambda bi, i: (bi, i, 0)),
                pl.BlockSpec((1, 3, d), lambda bi, i: (bi, 0, 0)),
                full(ng), full(wtok), full(wtr),
                tile_tok, pl.BlockSpec((1, 1, tm), lambda bi, i: (bi, 0, i)),
                full(inv_l), full(sign_l), full(inv_nb), full(inv_mb),
                full(qng), full(kvng), full(wqt), full(wk), full(wv)]
    out_shape = [
        jax.ShapeDtypeStruct((b, NSA_WIDTH, s), BF16),
        jax.ShapeDtypeStruct((b, NSA_KV_GROUPS, s, LANES), BF16),
        jax.ShapeDtypeStruct((b, NSA_KV_GROUPS, s, LANES), BF16),
        jax.ShapeDtypeStruct((b, nt, v_rows_n, tk), BF16),
        jax.ShapeDtypeStruct((b, nt, v_rows_n, tk), BF16),
        jax.ShapeDtypeStruct((b, s, NSA_KV_WIDTH), F32),
        jax.ShapeDtypeStruct((b, s, NSA_KV_WIDTH), F32),
        jax.ShapeDtypeStruct((b, NSA_KV_GROUPS * GATE_ROWS, s), F32),
        jax.ShapeDtypeStruct((b, NSA_WIDTH, s), BF16),
        jax.ShapeDtypeStruct((b, MLA_WIDTH, s), BF16),
        jax.ShapeDtypeStruct((b, MLA_HEADS, MLA_QK, s), BF16),
        jax.ShapeDtypeStruct((b, s, MLA_QK), BF16),
        jax.ShapeDtypeStruct((b, nt, v_rows_m, tk), BF16),
    ]
    out_specs = [
        pl.BlockSpec((1, NSA_WIDTH, tm), lambda bi, i: (bi, 0, i)),
        pl.BlockSpec((1, NSA_KV_GROUPS, tm, LANES), lambda bi, i: (bi, 0, i, 0)),
        pl.BlockSpec((1, NSA_KV_GROUPS, tm, LANES), lambda bi, i: (bi, 0, i, 0)),
        pl.BlockSpec((1, tm // tk, v_rows_n, tk), lambda bi, i: (bi, i, 0, 0)),
        pl.BlockSpec((1, tm // tk, v_rows_n, tk), lambda bi, i: (bi, i, 0, 0)),
        pl.BlockSpec((1, tm, NSA_KV_WIDTH), lambda bi, i: (bi, i, 0)),
        pl.BlockSpec((1, tm, NSA_KV_WIDTH), lambda bi, i: (bi, i, 0)),
        pl.BlockSpec((1, NSA_KV_GROUPS * GATE_ROWS, tm), lambda bi, i: (bi, 0, i)),
        pl.BlockSpec((1, NSA_WIDTH, tm), lambda bi, i: (bi, 0, i)),
        pl.BlockSpec((1, MLA_WIDTH, tm), lambda bi, i: (bi, 0, i)),
        pl.BlockSpec((1, MLA_HEADS, MLA_QK, tm), lambda bi, i: (bi, 0, 0, i)),
        pl.BlockSpec((1, tm, MLA_QK), lambda bi, i: (bi, i, 0)),
        pl.BlockSpec((1, tm // tk, v_rows_m, tk), lambda bi, i: (bi, i, 0, 0)),
    ]
    kern = functools.partial(_proj_kernel, tm=tm, tk=tk, scale_nsa=NSA_HEAD_DIM ** -0.5 * LOG2E,
                             scale_mla=(MLA_NOPE_DIM + MLA_ROPE_DIM) ** -0.5 * LOG2E)
    return pl.pallas_call(
        kern, grid=(b, s // tm), in_specs=in_specs, out_specs=out_specs, out_shape=out_shape,
        compiler_params=pltpu.CompilerParams(vmem_limit_bytes=VMEM_LIMIT_BYTES),
        name="proj",
    )(x, mod3, ng, wtok, wtr, pos_b, pos_row, inv_l, sign_l, inv_nb, inv_mb, qng, kvng, wqt, wk, wv)


def _compress_kernel(k_ref, v_ref, ptop_ref, pbot_ref, wk1_ref, wv1_ref, wk2_ref, wv2t_ref, posc_ref, invl_ref,
                     signl_ref, kc_ref, vct_ref):
    ncp = k_ref.shape[1]
    gw = NSA_KV_GROUPS * CMP_HIDDEN
    lane = lax.broadcasted_iota(I32, (ncp, LANES), 1)

    def hidden(r_ref, w1_ref):
        r = r_ref[0]
        a = jnp.dot((r + ptop_ref[...]).astype(BF16), w1_ref[:, 0:gw], preferred_element_type=F32)
        bt = jnp.dot((r + pbot_ref[...]).astype(BF16), w1_ref[:, gw:2 * gw], preferred_element_type=F32)
        return _silu(a + pltpu.roll(bt, ncp - 1, axis=0))

    cc, sc = _rope_lanes(posc_ref[...], invl_ref, signl_ref)
    hk = hidden(k_ref, wk1_ref)
    hv = hidden(v_ref, wv1_ref)
    for g in range(NSA_KV_GROUPS):
        hkg = hk[:, g * CMP_HIDDEN:(g + 1) * CMP_HIDDEN].astype(BF16)
        kc = jnp.dot(hkg, wk2_ref[...], preferred_element_type=F32)
        kc_ref[0, g] = _rope_tok(kc, cc, sc, NSA_HEAD_DIM // 2, lane).astype(BF16)
        hvg = hv[:, g * CMP_HIDDEN:(g + 1) * CMP_HIDDEN].astype(BF16)
        vct_ref[0, g] = lax.dot_general(wv2t_ref[...], hvg, NT, preferred_element_type=F32).astype(BF16)


def _compress(kcmp_r, vcmp_r, ptop, pbot, wk1, wv1, wk2, wv2t, pos_c, inv_l, sign_l):
    b, ncp, width = kcmp_r.shape

    def full(a):
        return pl.BlockSpec(a.shape, lambda bi, _n=a.ndim: (0,) * _n)

    blk = pl.BlockSpec((1, ncp, width), lambda bi: (bi, 0, 0))
    tab = pl.BlockSpec((ncp, LANES), lambda bi: (bi, 0))
    return pl.pallas_call(
        _compress_kernel, grid=(b,),
        in_specs=[blk, blk, full(ptop), full(pbot), full(wk1), full(wv1), full(wk2), full(wv2t), tab,
                  full(inv_l), full(sign_l)],
        out_specs=[pl.BlockSpec((1, NSA_KV_GROUPS, ncp, LANES), lambda bi: (bi, 0, 0, 0)),
                   pl.BlockSpec((1, NSA_KV_GROUPS, NSA_HEAD_DIM, ncp), lambda bi: (bi, 0, 0, 0))],
        out_shape=[jax.ShapeDtypeStruct((b, NSA_KV_GROUPS, ncp, LANES), BF16),
                   jax.ShapeDtypeStruct((b, NSA_KV_GROUPS, NSA_HEAD_DIM, ncp), BF16)],
        compiler_params=pltpu.CompilerParams(vmem_limit_bytes=VMEM_LIMIT_BYTES),
        name="compress",
    )(kcmp_r, vcmp_r, ptop, pbot, wk1, wv1, wk2, wv2t, pos_c, inv_l, sign_l)


def _nsa_kernel(q1_ref, q2_ref, kc_ref, vct_ref, kaug_ref, vts_ref, kwin_ref, vtw_ref, g_ref, z_ref, mt_ref,
                o_ref, qaug_ref, sa_ref, sb_ref, m_ref, acc_ref, tot_ref, imp_ref, rank_ref, *, tq, tk, n_sel):
    i = pl.program_id(2)
    hpg, dk, half = NSA_HPG, NSA_HEAD_DIM, NSA_HEAD_DIM // 2
    for hh in range(hpg):
        qaug_ref[hh, 0:half, :] = q1_ref[0, hh * half:(hh + 1) * half, :]
        qaug_ref[hh, half:dk, :] = q2_ref[0, hh * half:(hh + 1) * half, :]

    def gate(hh, branch):
        return g_ref[0, hh * N_BRANCH + branch:hh * N_BRANCH + branch + 1, :]

    row_k = lax.broadcasted_iota(I32, (tk, tq), 0)
    col_q = lax.broadcasted_iota(I32, (tk, tq), 1)
    causal = row_k <= col_q

    def reset():
        m_ref[...] = jnp.full_like(m_ref, NEG_INF)
        acc_ref[...] = jnp.zeros_like(acc_ref)

    def add_branch(branch):
        for hh in range(hpg):
            inv_l = 1.0 / acc_ref[hh, dk:dk + 1, :]
            tot_ref[hh] = tot_ref[hh] + (gate(hh, branch) * inv_l) * acc_ref[hh, 0:dk, :]

    def q_cols(hh):
        return qaug_ref[hh, 0:dk, :]

    ncp = kc_ref.shape[2]
    t_row = i * tq + lax.broadcasted_iota(I32, (1, tq), 1)
    last_n = (t_row - (CMP_BLOCK - 1)) >> CMP_SHIFT
    valid = lax.broadcasted_iota(I32, (ncp, tq), 0) <= last_n
    col_ok = last_n >= 0
    kc = kc_ref[0, 0, :, 0:dk]
    p_heads = []

    def cmp_update(hh, s_c):
        s_c = jnp.where(valid, s_c, NEG_INF)
        e = jnp.exp2(s_c - jnp.max(s_c, axis=0, keepdims=True))
        l_c = jnp.sum(e, axis=0, keepdims=True)
        p_c = e * jnp.where(col_ok, 1.0 / l_c, 0.0)
        o_c = jnp.dot(vct_ref[0, 0], p_c.astype(BF16), preferred_element_type=F32)
        tot_ref[hh] = gate(hh, 0) * o_c
        p_heads.append(p_c)

    nb = mt_ref.shape[0]

    def importance(_):
        psum = functools.reduce(lambda a, b: a + b, p_heads)
        hi = psum.astype(BF16)
        lo = (psum - hi.astype(F32)).astype(BF16)
        mt = mt_ref[...]
        imp = jnp.dot(mt, hi, preferred_element_type=F32) + jnp.dot(mt, lo, preferred_element_type=F32)
        j_idx = lax.broadcasted_iota(I32, (nb, tq), 0)
        cur = (i * tq + lax.broadcasted_iota(I32, (nb, tq), 1)) >> SLC_SHIFT
        forced = (j_idx == 0) | (j_idx == cur) | (j_idx == cur - 1)
        imp_ref[...] = jnp.where(forced, FORCED_SCORE, jnp.where(j_idx > cur, -FORCED_SCORE, imp))

    stages = [(functools.partial(jnp.dot, kc, q_cols(hh), preferred_element_type=F32),
               functools.partial(cmp_update, hh)) for hh in range(hpg)]
    stages.append((None, importance))

    reset()
    n_back = WINDOW // tk
    for back in range(n_back + 1):
        jb = jnp.maximum(i - back, 0)
        kt_b = kwin_ref[0, 0, pl.ds(pl.multiple_of(jb * tk, tk), tk), 0:dk]
        if back == 0:
            keep = causal
        elif back == n_back:
            keep = (row_k > col_q) & (i >= back)
        else:
            keep = jnp.broadcast_to(i >= back, (tk, tq))
        for hh in range(hpg):
            stages.append((functools.partial(jnp.dot, kt_b, q_cols(hh), preferred_element_type=F32),
                           functools.partial(_chain_update, v_t=vtw_ref[0, jb], m_ref=m_ref, acc_ref=acc_ref,
                                             ch=hh, keep=keep)))
    _pipeline(stages, lookahead=3)
    add_branch(2)

    rank_ref[...] = jnp.zeros_like(rank_ref)
    sub = lax.broadcasted_iota(I32, (SUBLANES, tq), 0)
    last_group = ((i + 1) * (tq // SLC_BLOCK) - 1) // SUBLANES

    def count(c, v):
        blk = imp_ref[v * SUBLANES:(v + 1) * SUBLANES, :]
        cnt = rank_ref[v * SUBLANES:(v + 1) * SUBLANES, :]
        for rr in range(SUBLANES):
            row = imp_ref[c * SUBLANES + rr:c * SUBLANES + rr + 1, :]
            if c < v:
                beats = row >= blk
            elif c > v:
                beats = row > blk
            else:
                beats = (row > blk) | ((row == blk) & (sub > rr))
            cnt = cnt + beats.astype(I32)
        rank_ref[v * SUBLANES:(v + 1) * SUBLANES, :] = cnt

    for lvl in range(nb // SUBLANES):
        @pl.when(lvl <= last_group)
        def _(lvl=lvl):
            for v in range(lvl + 1):
                count(lvl, v)
            for c in range(lvl):
                count(c, lvl)

    bias = jnp.where(rank_ref[...] < n_sel, 0.0, SEL_BIAS).astype(BF16)
    for hh in range(hpg):
        qaug_ref[hh, dk:dk + nb, :] = bias

    reset()
    _causal_sweep(lambda j: kaug_ref[0, 0, pl.ds(pl.multiple_of(j * tk, tk), tk), :], lambda j, ch: vts_ref[0, j],
                  qaug_ref, sa_ref, sb_ref, m_ref, acc_ref, i, hpg, causal)
    add_branch(1)

    for hh in range(hpg):
        zz = z_ref[0, hh * dk:(hh + 1) * dk, :].astype(F32)
        o_ref[0, hh * dk:(hh + 1) * dk, :] = (tot_ref[hh] * zz).astype(BF16)


def _nsa(qt, kc, vct, kaug, vts, kwin, vtw, gt, ztn, mt, *, tq, tk, n_sel):
    b, _, s = qt.shape
    nt = s // tk
    ncp = kc.shape[2]
    hq = NSA_HPG * (NSA_HEAD_DIM // 2)
    nb = mt.shape[0]
    hpg = NSA_HPG
    vr = NSA_HEAD_DIM + ONES_ROWS
    in_specs = [
        pl.BlockSpec((1, hq, tq), lambda bi, g, i: (bi, g, i)),
        pl.BlockSpec((1, hq, tq), lambda bi, g, i: (bi, NSA_KV_GROUPS + g, i)),
        pl.BlockSpec((1, 1, ncp, LANES), lambda bi, g, i: (bi, g, 0, 0)),
        pl.BlockSpec((1, 1, NSA_HEAD_DIM, ncp), lambda bi, g, i: (bi, g, 0, 0)),
        pl.BlockSpec((1, 1, s, LANES), lambda bi, g, i: (bi, g, 0, 0)),
        pl.BlockSpec((1, nt, vr, tk), lambda bi, g, i: (bi, 0, g, 0)),
        pl.BlockSpec((1, 1, s, LANES), lambda bi, g, i: (bi, g, 0, 0)),
        pl.BlockSpec((1, nt, vr, tk), lambda bi, g, i: (bi, 0, g, 0)),
        pl.BlockSpec((1, GATE_ROWS, tq), lambda bi, g, i: (bi, g, i)),
        pl.BlockSpec((1, NSA_HPG * NSA_HEAD_DIM, tq), lambda bi, g, i: (bi, g, i)),
        pl.BlockSpec(mt.shape, lambda bi, g, i: (0, 0)),
    ]
    kern = functools.partial(_nsa_kernel, tq=tq, tk=tk, n_sel=n_sel)
    return pl.pallas_call(
        kern, grid=(b, NSA_KV_GROUPS, s // tq), in_specs=in_specs,
        out_specs=pl.BlockSpec((1, NSA_HPG * NSA_HEAD_DIM, tq), lambda bi, g, i: (bi, g, i)),
        out_shape=jax.ShapeDtypeStruct((b, NSA_WIDTH, s), BF16),
        scratch_shapes=[pltpu.VMEM((hpg, NSA_HEAD_DIM + nb, tq), BF16),
                        pltpu.VMEM((hpg, tk, tq), F32), pltpu.VMEM((hpg, tk, tq), F32),
                        pltpu.VMEM((hpg, 1, tq), F32), pltpu.VMEM((hpg, vr, tq), F32),
                        pltpu.VMEM((hpg, NSA_HEAD_DIM, tq), F32),
                        pltpu.VMEM((nb, tq), F32), pltpu.VMEM((nb, tq), I32)],
        compiler_params=pltpu.CompilerParams(vmem_limit_bytes=VMEM_LIMIT_BYTES),
        name="nsa",
    )(qt, qt, kc, vct, kaug, vts, kwin, vtw, gt, ztn, mt)


def _mla_kernel(q_ref, k_ref, vt_ref, z_ref, o_ref, sa_ref, sb_ref, m_ref, acc_ref, *, tq, tk):
    i = pl.program_id(1)
    dv, vr = MLA_V_DIM, MLA_V_DIM + ONES_ROWS
    m_ref[...] = jnp.full_like(m_ref, NEG_INF)
    acc_ref[...] = jnp.zeros_like(acc_ref)

    causal = lax.broadcasted_iota(I32, (tk, tq), 0) <= lax.broadcasted_iota(I32, (tk, tq), 1)
    _causal_sweep(lambda j: k_ref[0, pl.ds(pl.multiple_of(j * tk, tk), tk), :],
                  lambda j, hd: vt_ref[0, j, hd * vr:(hd + 1) * vr, :],
                  q_ref.at[0], sa_ref, sb_ref, m_ref, acc_ref, i, MLA_HEADS, causal)
    for hd in range(MLA_HEADS):
        o_h = acc_ref[hd, 0:dv, :] * (1.0 / acc_ref[hd, dv:dv + 1, :])
        zz = z_ref[0, hd * dv:(hd + 1) * dv, :].astype(F32)
        o_ref[0, hd * dv:(hd + 1) * dv, :] = (o_h * zz).astype(BF16)


def _mla(qtm, kmla, vtm, ztm, *, tq, tk):
    b, s, _ = kmla.shape
    nt = s // tk
    vrows = MLA_HEADS * (MLA_V_DIM + ONES_ROWS)
    kern = functools.partial(_mla_kernel, tq=tq, tk=tk)
    return pl.pallas_call(
        kern, grid=(b, s // tq),
        in_specs=[pl.BlockSpec((1, MLA_HEADS, MLA_QK, tq), lambda bi, i: (bi, 0, 0, i)),
                  pl.BlockSpec((1, s, MLA_QK), lambda bi, i: (bi, 0, 0)),
                  pl.BlockSpec((1, nt, vrows, tk), lambda bi, i: (bi, 0, 0, 0)),
                  pl.BlockSpec((1, MLA_WIDTH, tq), lambda bi, i: (bi, 0, i))],
        out_specs=pl.BlockSpec((1, MLA_WIDTH, tq), lambda bi, i: (bi, 0, i)),
        out_shape=jax.ShapeDtypeStruct((b, MLA_WIDTH, s), BF16),
        scratch_shapes=[pltpu.VMEM((MLA_HEADS, tk, tq), F32), pltpu.VMEM((MLA_HEADS, tk, tq), F32),
                        pltpu.VMEM((MLA_HEADS, 1, tq), F32), pltpu.VMEM((MLA_HEADS, MLA_V_DIM + ONES_ROWS, tq), F32)],
        compiler_params=pltpu.CompilerParams(vmem_limit_bytes=VMEM_LIMIT_BYTES),
        name="mla",
    )(qtm, kmla, vtm, ztm)


def _out_kernel(x_ref, mn_ref, mm_ref, w_ref, mod_ref, fg_ref, o_ref, *, final):
    y = lax.dot_general(mn_ref[0], w_ref[0:NSA_WIDTH, :], TN, preferred_element_type=F32)
    y = y + lax.dot_general(mm_ref[0], w_ref[NSA_WIDTH:MIX_WIDTH, :], TN, preferred_element_type=F32)
    x2 = x_ref[0] + mod_ref[0][2:3] * y
    o_ref[0] = _rms(x2, fg_ref[...]) if final else x2


def _out(x, mn, mm, w_out, mod3, fg, *, tm, final):
    b, s, d = x.shape
    return pl.pallas_call(
        functools.partial(_out_kernel, final=final), grid=(b, s // tm),
        in_specs=[pl.BlockSpec((1, tm, d), lambda bi, i: (bi, i, 0)),
                  pl.BlockSpec((1, NSA_WIDTH, tm), lambda bi, i: (bi, 0, i)),
                  pl.BlockSpec((1, MLA_WIDTH, tm), lambda bi, i: (bi, 0, i)),
                  pl.BlockSpec(w_out.shape, lambda bi, i: (0, 0)),
                  pl.BlockSpec((1, 3, d), lambda bi, i: (bi, 0, 0)),
                  pl.BlockSpec((1, d), lambda bi, i: (0, 0))],
        out_specs=pl.BlockSpec((1, tm, d), lambda bi, i: (bi, i, 0)),
        out_shape=jax.ShapeDtypeStruct((b, s, d), F32),
        compiler_params=pltpu.CompilerParams(vmem_limit_bytes=VMEM_LIMIT_BYTES),
        name="out_proj",
    )(x, mn, mm, w_out, mod3, fg)


def _cmp_to_slc_t(ncp, nc, nslc, nb):
    start = np.arange(nc)[:, None] * CMP_STRIDE
    bstart = np.arange(nslc)[None, :] * SLC_BLOCK
    ov = np.minimum(start + CMP_BLOCK, bstart + SLC_BLOCK) - np.maximum(start, bstart)
    m = (np.clip(ov, 0, None) / CMP_BLOCK).astype(np.float32)
    out = np.zeros((nb, ncp), np.float32)
    out[:nslc, :nc] = m.T
    return out


def _layout_w_in(w):
    d = w.shape[0]
    (q_n, kc_n, vc_n, ks_n, vs_n, kw_n, vw_n, gl_n, z_n, cq_m, ckv_m, kr_m, z_m) = jnp.split(w, IN_OFFSETS, axis=-1)
    dk = NSA_HEAD_DIM
    z64 = jnp.zeros((d, LANES - dk), w.dtype)
    wtok = jnp.concatenate(
        [ks_n[:, :dk], z64, ks_n[:, dk:], z64, kw_n[:, :dk], z64, kw_n[:, dk:], z64,
         kc_n, vc_n, cq_m, ckv_m, jnp.zeros((d, KR_LANE), w.dtype), kr_m,
         jnp.zeros((d, LANES - KR_LANE - MLA_ROPE_DIM), w.dtype)], axis=1)
    qr = q_n.reshape(d, NSA_HEADS, 2, dk // 2)
    q_perm = jnp.concatenate([qr[:, :, 0, :].reshape(d, -1), qr[:, :, 1, :].reshape(d, -1)], axis=1)
    gl = gl_n.reshape(d, NSA_KV_GROUPS, NSA_HPG * N_BRANCH)
    gl = jnp.pad(gl, ((0, 0), (0, 0), (0, GATE_ROWS - NSA_HPG * N_BRANCH))).reshape(d, -1)
    wtr = jnp.concatenate([q_perm, vs_n, vw_n, gl, z_n, z_m], axis=1).T
    assert wtok.shape[1] == TOK_COLS and wtr.shape[0] == TR_ROWS
    return wtok.astype(BF16), wtr.astype(BF16)


def _layout_w1(w1):
    hid = w1.shape[1]
    w1r = w1.reshape(2, CMP_STRIDE, NSA_HEAD_DIM, hid)
    eye = jnp.eye(NSA_KV_GROUPS, dtype=w1.dtype)
    halves = [jnp.einsum('ldh,pg->lpdgh', w1r[k], eye).reshape(CMP_STRIDE * NSA_KV_WIDTH, NSA_KV_GROUPS * hid)
              for k in range(2)]
    return jnp.concatenate(halves, axis=1).astype(BF16)


def kernel(x, c, positions, ada_w, ada_b, norm_g, w_in, cmp_pos, cmp_k_w1, cmp_k_w2, cmp_v_w1, cmp_v_w2,
           q_norm_g, w_q_up, kv_norm_g, w_kv_up, w_out, final_norm_g):
    b, s, d = x.shape
    depth = ada_w.shape[0]
    tm, tq = PROJ_TILE, ATT_TILE
    tk = tq
    assert s % tm == 0 and tm % tk == 0 and WINDOW % tk == 0 and (tq & (tq - 1)) == 0
    assert CMP_BLOCK == 2 * CMP_STRIDE and s % SLC_BLOCK == 0
    nslc = s // SLC_BLOCK
    nb = LANES - NSA_HEAD_DIM
    assert nslc <= nb
    ncp = s // CMP_STRIDE
    nc = ncp - 1

    half_n, half_m = NSA_HEAD_DIM // 2, MLA_ROPE_DIM // 2
    inv_n = ROPE_THETA ** (-jnp.arange(half_n, dtype=F32) / half_n)
    inv_m = ROPE_THETA ** (-jnp.arange(half_m, dtype=F32) / half_m)
    ones_n, ones_m = jnp.ones((half_n,), F32), jnp.ones((half_m,), F32)
    pad = jnp.zeros((LANES - NSA_HEAD_DIM - MLA_ROPE_DIM,), F32)
    inv_l = jnp.concatenate([inv_n, inv_n, inv_m, inv_m, pad])[None]
    sign_l = jnp.concatenate([-ones_n, ones_n, -ones_m, ones_m, pad])[None]
    pos_f = positions.astype(F32)
    pos_b = jnp.broadcast_to(pos_f.reshape(b * s, 1), (b * s, LANES))
    pos_row = pos_f.reshape(b, 1, s)
    cmp_end = np.minimum(np.arange(ncp) * CMP_STRIDE + CMP_BLOCK - 1, s - 1)
    pos_c = jnp.broadcast_to(pos_f[:, cmp_end].reshape(b * ncp, 1), (b * ncp, LANES))
    inv_nb = jnp.broadcast_to(inv_n[:, None], (half_n, tm))
    inv_mb = jnp.broadcast_to(inv_m[:, None], (half_m, tm))

    mt = jnp.asarray(_cmp_to_slc_t(ncp, nc, nslc, nb), dtype=BF16)
    bp = -(-b // SUBLANES) * SUBLANES
    c_pad = jnp.pad(c, ((0, bp - b), (0, 0)))

    for l in range(depth):
        mod = _adaln(c_pad, ada_w[l], ada_b[l].reshape(1, -1))
        mod3 = mod[:b].reshape(b, 3, d)
        wtok, wtr = _layout_w_in(w_in[l])
        wq = w_q_up[l].reshape(MLA_Q_RANK, MLA_HEADS, MLA_NOPE_DIM + MLA_ROPE_DIM)
        wqt = jnp.concatenate([wq[:, :, :MLA_NOPE_DIM].reshape(MLA_Q_RANK, -1),
                               wq[:, :, MLA_NOPE_DIM:MLA_NOPE_DIM + half_m].reshape(MLA_Q_RANK, -1),
                               wq[:, :, MLA_NOPE_DIM + half_m:].reshape(MLA_Q_RANK, -1)], axis=1).T.astype(BF16)
        wkv = w_kv_up[l].reshape(MLA_KV_RANK, MLA_HEADS, MLA_NOPE_DIM + MLA_V_DIM)
        wk = wkv[:, :, :MLA_NOPE_DIM].transpose(1, 0, 2).astype(BF16)
        wv = wkv[:, :, MLA_NOPE_DIM:].transpose(1, 2, 0).reshape(MLA_WIDTH, MLA_KV_RANK).astype(BF16)

        (qt, kaug, kwin, vts, vtw, kcmp, vcmp, gt, ztn, ztm, qtm, kmla, vtm) = _proj(
            x, mod3, norm_g[l].reshape(1, d), wtok, wtr, pos_b, pos_row, inv_l, sign_l, inv_nb, inv_mb,
            q_norm_g[l].reshape(1, -1), kv_norm_g[l].reshape(1, -1), wqt, wk, wv, tm=tm, tk=tk)

        pos_l = cmp_pos[l]
        ptop = jnp.broadcast_to(pos_l[:CMP_STRIDE, None, :], (CMP_STRIDE, NSA_KV_GROUPS, NSA_HEAD_DIM)).reshape(1, -1)
        pbot = jnp.broadcast_to(pos_l[CMP_STRIDE:, None, :], (CMP_STRIDE, NSA_KV_GROUPS, NSA_HEAD_DIM)).reshape(1, -1)
        wk2 = jnp.pad(cmp_k_w2[l], ((0, 0), (0, LANES - NSA_HEAD_DIM))).astype(BF16)
        kc, vct = _compress(kcmp.reshape(b, ncp, -1), vcmp.reshape(b, ncp, -1), ptop, pbot,
                            _layout_w1(cmp_k_w1[l]), _layout_w1(cmp_v_w1[l]), wk2,
                            cmp_v_w2[l].T.astype(BF16), pos_c, inv_l, sign_l)

        mix_n = _nsa(qt, kc, vct, kaug, vts, kwin, vtw, gt, ztn, mt, tq=tq, tk=tk, n_sel=min(SLC_TOPK, nslc))
        mix_m = _mla(qtm, kmla, vtm, ztm, tq=tq, tk=tk)
        x = _out(x, mix_n, mix_m, w_out[l].astype(BF16), mod3, final_norm_g.reshape(1, d), tm=tm,
                 final=(l == depth - 1))
    return x
```

```python
import functools

import numpy as np
import jax
import jax.numpy as jnp
from jax import lax
from jax.experimental import pallas as pl
from jax.experimental.pallas import tpu as pltpu

F32 = jnp.float32
BF16 = jnp.bfloat16
I32 = jnp.int32

NSA_HEADS = 8
NSA_KV_GROUPS = 2
NSA_HPG = NSA_HEADS // NSA_KV_GROUPS
NSA_HEAD_DIM = 64
NSA_WIDTH = NSA_HEADS * NSA_HEAD_DIM
NSA_KV_WIDTH = NSA_KV_GROUPS * NSA_HEAD_DIM
CMP_BLOCK = 32
CMP_STRIDE = 16
CMP_HIDDEN = 128
SLC_BLOCK = 64
SLC_TOPK = 16
WINDOW = 512
N_BRANCH = 3
FORCED_SCORE = 1.0e4
MLA_HEADS = 8
MLA_NOPE_DIM = 64
MLA_ROPE_DIM = 32
MLA_V_DIM = 64
MLA_WIDTH = MLA_HEADS * MLA_V_DIM
MLA_Q_RANK = 256
MLA_KV_RANK = 128
MIX_WIDTH = NSA_WIDTH + MLA_WIDTH
ROPE_THETA = 10000.0
NORM_EPS = 1e-6
NEG_INF = -1e30
IN_SIZES = (NSA_WIDTH, NSA_KV_WIDTH, NSA_KV_WIDTH, NSA_KV_WIDTH, NSA_KV_WIDTH, NSA_KV_WIDTH, NSA_KV_WIDTH,
            NSA_HEADS * N_BRANCH, NSA_WIDTH, MLA_Q_RANK, MLA_KV_RANK, MLA_ROPE_DIM, MLA_WIDTH)
IN_OFFSETS = tuple(int(o) for o in np.cumsum(IN_SIZES)[:-1])

LANES = 128
SUBLANES = 8
VMEM_LIMIT_BYTES = 56 * 1024 * 1024

PROJ_TILE = 512
ATT_TILE = 256
SEL_BIAS = -30000.0
GATE_ROWS = 16
ONES_ROWS = 16
MLA_QK = MLA_KV_RANK + MLA_ROPE_DIM
LOG2E = float(np.log2(np.e))
SLC_SHIFT = SLC_BLOCK.bit_length() - 1
CMP_SHIFT = CMP_STRIDE.bit_length() - 1
assert 1 << SLC_SHIFT == SLC_BLOCK and 1 << CMP_SHIFT == CMP_STRIDE

NT = (((1,), (1,)), ((), ()))
TN = (((0,), (0,)), ((), ()))


def _silu(v):
    return v * jax.nn.sigmoid(v)


def _rms(v, g):
    ms = jnp.mean(v * v, axis=-1, keepdims=True)
    return v * lax.rsqrt(ms + NORM_EPS) * g


def _rope_tok(v, c, s_signed, half, lane, base=0):
    up = pltpu.roll(v, LANES - half, axis=1)
    dn = pltpu.roll(v, half, axis=1)
    return v * c + jnp.where(lane < base + half, up, dn) * s_signed


def _rope_lanes(pos_b, inv_ref, sign_ref):
    ang = pos_b * inv_ref[...]
    return jnp.cos(ang), jnp.sin(ang) * sign_ref[...]


def _chain_update(s_t, v_t, m_ref, acc_ref, ch, keep=None):
    if keep is not None:
        s_t = jnp.where(keep, s_t, NEG_INF)
    m_prev = m_ref[ch]
    m_new = jnp.maximum(m_prev, jnp.max(s_t, axis=0, keepdims=True))
    alpha = jnp.exp2(m_prev - m_new)
    p = jnp.exp2(s_t - m_new)
    acc_ref[ch] = alpha * acc_ref[ch] + jnp.dot(v_t, p.astype(BF16), preferred_element_type=F32)
    m_ref[ch] = m_new


def _causal_sweep(k_tile, v_tile, k_group, v_group, q_ref, sa_ref, sb_ref, m_ref, acc_ref, last, keep_last,
                  lookahead=2):
    n_chains = len(k_group)

    def loader(tile_fn, j):
        cache = {}
        return lambda g: cache.setdefault(g, tile_fn(j, g))

    def phase(j_next, s_next_ref, s_cur_ref, j_cur, keep=None):
        k_next = loader(k_tile, j_next) if j_next is not None else None
        v_cur = loader(v_tile, j_cur)
        for n in range(n_chains + lookahead):
            if k_next is not None and n < n_chains:
                s_next_ref[n] = jnp.dot(k_next(k_group[n]), q_ref[n], preferred_element_type=F32)
            if n >= lookahead:
                ch = n - lookahead
                _chain_update(s_cur_ref[ch], v_cur(v_group[ch]), m_ref, acc_ref, ch, keep=keep)

    k_0 = loader(k_tile, 0)
    for ch in range(n_chains):
        sa_ref[ch] = jnp.dot(k_0(k_group[ch]), q_ref[ch], preferred_element_type=F32)

    def body(jj, carry):
        j = 2 * jj
        phase(j + 1, sb_ref, sa_ref, j)
        phase(j + 2, sa_ref, sb_ref, j + 1)
        return carry

    lax.fori_loop(0, last >> 1, body, 0)

    @pl.when((last & 1) == 0)
    def _():
        phase(None, None, sa_ref, last, keep=keep_last)

    @pl.when((last & 1) == 1)
    def _():
        phase(last, sb_ref, sa_ref, last - 1)
        phase(None, None, sb_ref, last, keep=keep_last)


def _pipeline(stages, lookahead):
    pending = {}
    for n in range(len(stages) + lookahead):
        if n < len(stages) and stages[n][0] is not None:
            pending[n] = stages[n][0]()
        if n >= lookahead:
            stages[n - lookahead][1](pending.pop(n - lookahead, None))


def _adaln_kernel(c_ref, w_ref, b_ref, o_ref):
    sc = _silu(c_ref[...])
    o_ref[...] = jnp.dot(sc.astype(BF16), w_ref[...].astype(BF16), preferred_element_type=F32) + b_ref[...]


def _adaln(c_pad, w, b):
    bp, d = c_pad.shape
    n = w.shape[1] // d
    return pl.pallas_call(
        _adaln_kernel,
        grid=(n,),
        in_specs=[pl.BlockSpec((bp, d), lambda j: (0, 0)),
                  pl.BlockSpec((d, d), lambda j: (0, j)),
                  pl.BlockSpec((1, d), lambda j: (0, j))],
        out_specs=pl.BlockSpec((bp, d), lambda j: (0, j)),
        out_shape=jax.ShapeDtypeStruct((bp, n * d), F32),
        name="adaln",
    )(c_pad, w, b)


TOK_KS, TOK_KW, TOK_KC, TOK_VC, TOK_CQ, TOK_CKV, TOK_KR, TOK_COLS = 0, 256, 512, 640, 768, 1024, 1152, 1280
KR_LANE = NSA_HEAD_DIM
TR_Q, TR_VS, TR_VW, TR_G, TR_ZN, TR_ZM, TR_ROWS = 0, 512, 640, 768, 800, 1312, 1824


def _proj_kernel(x_ref, mod_ref, ng_ref, wtok_ref, wtr_ref, posb_ref, posr_ref, invl_ref, signl_ref,
                 invn_ref, invm_ref, qng_ref, kvng_ref, wqt_ref, wk_ref, wv_ref,
                 qt_ref, kaug_ref, kwin_ref, vts_ref, vtw_ref, kcmp_ref, vcmp_ref, gt_ref,
                 ztn_ref, ztm_ref, qtm_ref, kmla_ref, vtm_ref, *, tm, tk, scale_nsa, scale_mla):
    i = pl.program_id(1)
    mod = mod_ref[0]
    h = _rms(x_ref[0], ng_ref[...]) * (1.0 + mod[1:2]) + mod[0:1]
    hb = h.astype(BF16)
    tok = jnp.dot(hb, wtok_ref[...], preferred_element_type=F32)
    tr = lax.dot_general(wtr_ref[...], hb, NT, preferred_element_type=F32)

    lane = lax.broadcasted_iota(I32, (tm, LANES), 1)
    row = lax.broadcasted_iota(I32, (tm, LANES), 0)
    blk = (i * tm + row) >> SLC_SHIFT
    onehot = (lane - NSA_HEAD_DIM == blk).astype(F32)
    ct, st = _rope_lanes(posb_ref[...], invl_ref, signl_ref)
    half_n = NSA_HEAD_DIM // 2
    for g in range(NSA_KV_GROUPS):
        ks = _rope_tok(tok[:, TOK_KS + LANES * g:TOK_KS + LANES * (g + 1)], ct, st, half_n, lane)
        kaug_ref[0, g] = jnp.where(lane >= NSA_HEAD_DIM, onehot, ks).astype(BF16)
        kw = _rope_tok(tok[:, TOK_KW + LANES * g:TOK_KW + LANES * (g + 1)], ct, st, half_n, lane)
        kwin_ref[0, g] = kw.astype(BF16)
    kcmp_ref[0] = tok[:, TOK_KC:TOK_KC + LANES]
    vcmp_ref[0] = tok[:, TOK_VC:TOK_VC + LANES]

    ckvn = _rms(tok[:, TOK_CKV:TOK_CKV + MLA_KV_RANK], kvng_ref[...])
    krr = _rope_tok(tok[:, TOK_KR:TOK_KR + LANES], ct, st, MLA_ROPE_DIM // 2, lane, base=KR_LANE)
    kmla_ref[0, :, MLA_KV_RANK:MLA_QK] = krr[:, KR_LANE:KR_LANE + MLA_ROPE_DIM].astype(BF16)
    ckvb = ckvn.astype(BF16)
    kmla_ref[0, :, 0:MLA_KV_RANK] = ckvb
    vtm = lax.dot_general(wv_ref[...], ckvb, NT, preferred_element_type=F32).astype(BF16)
    ones = jnp.ones((ONES_ROWS, tk), BF16)
    vr = MLA_V_DIM + ONES_ROWS
    for ii in range(tm // tk):
        for hd in range(MLA_HEADS):
            vtm_ref[0, ii, hd * vr:hd * vr + MLA_V_DIM, :] = vtm[hd * MLA_V_DIM:(hd + 1) * MLA_V_DIM,
                                                                 ii * tk:(ii + 1) * tk]
            vtm_ref[0, ii, hd * vr + MLA_V_DIM:(hd + 1) * vr, :] = ones

    cqn = _rms(tok[:, TOK_CQ:TOK_CQ + MLA_Q_RANK], qng_ref[...]).astype(BF16)
    qm = lax.dot_general(wqt_ref[...], cqn, NT, preferred_element_type=F32)
    nq = MLA_HEADS * MLA_NOPE_DIM
    hr = MLA_ROPE_DIM // 2
    x1 = qm[nq:nq + MLA_HEADS * hr].reshape(MLA_HEADS, hr, tm)
    x2 = qm[nq + MLA_HEADS * hr:nq + 2 * MLA_HEADS * hr].reshape(MLA_HEADS, hr, tm)
    ang_m = invm_ref[...] * posr_ref[0]
    cm_t, sm_t = jnp.cos(ang_m)[None], jnp.sin(ang_m)[None]
    o1 = (x1 * cm_t - x2 * sm_t) * scale_mla
    o2 = (x2 * cm_t + x1 * sm_t) * scale_mla
    for hd in range(MLA_HEADS):
        qn_h = qm[hd * MLA_NOPE_DIM:(hd + 1) * MLA_NOPE_DIM].astype(BF16)
        qabs = jnp.dot(wk_ref[hd], qn_h, preferred_element_type=F32)
        qtm_ref[0, hd, 0:MLA_KV_RANK, :] = (qabs * scale_mla).astype(BF16)
        qtm_ref[0, hd, MLA_KV_RANK:MLA_KV_RANK + hr, :] = o1[hd].astype(BF16)
        qtm_ref[0, hd, MLA_KV_RANK + hr:MLA_QK, :] = o2[hd].astype(BF16)

    hq = NSA_HEADS * half_n
    q1 = tr[TR_Q:TR_Q + hq].reshape(NSA_HEADS, half_n, tm)
    q2 = tr[TR_Q + hq:TR_Q + 2 * hq].reshape(NSA_HEADS, half_n, tm)
    ang_n = invn_ref[...] * posr_ref[0]
    cn_t, sn_t = jnp.cos(ang_n)[None], jnp.sin(ang_n)[None]
    qt_ref[0, 0:hq, :] = ((q1 * cn_t - q2 * sn_t) * scale_nsa).reshape(hq, tm).astype(BF16)
    qt_ref[0, hq:2 * hq, :] = ((q2 * cn_t + q1 * sn_t) * scale_nsa).reshape(hq, tm).astype(BF16)

    vts = tr[TR_VS:TR_VS + NSA_KV_WIDTH].astype(BF16)
    vtw = tr[TR_VW:TR_VW + NSA_KV_WIDTH].astype(BF16)
    dk = NSA_HEAD_DIM
    gr = dk + ONES_ROWS
    for ii in range(tm // tk):
        for g in range(NSA_KV_GROUPS):
            vts_ref[0, ii, g * gr:g * gr + dk, :] = vts[g * dk:(g + 1) * dk, ii * tk:(ii + 1) * tk]
            vtw_ref[0, ii, g * gr:g * gr + dk, :] = vtw[g * dk:(g + 1) * dk, ii * tk:(ii + 1) * tk]
            vts_ref[0, ii, g * gr + dk:(g + 1) * gr, :] = ones
            vtw_ref[0, ii, g * gr + dk:(g + 1) * gr, :] = ones
    gt_ref[0] = jax.nn.sigmoid(tr[TR_G:TR_G + NSA_KV_GROUPS * GATE_ROWS])
    ztn_ref[0] = _silu(tr[TR_ZN:TR_ZN + NSA_WIDTH]).astype(BF16)
    ztm_ref[0] = _silu(tr[TR_ZM:TR_ZM + MLA_WIDTH]).astype(BF16)


def _proj(x, mod3, ng, wtok, wtr, pos_b, pos_row, inv_l, sign_l, inv_nb, inv_mb, qng, kvng, wqt, wk, wv, *, tm, tk):
    b, s, d = x.shape
    nt = s // tk
    v_rows_n = NSA_KV_GROUPS * (NSA_HEAD_DIM + ONES_ROWS)
    v_rows_m = MLA_HEADS * (MLA_V_DIM + ONES_ROWS)
    tile_tok = pl.BlockSpec((tm, LANES), lambda bi, i: (bi * (s // tm) + i, 0))

    def full(a):
        return pl.BlockSpec(a.shape, lambda bi, i, _n=a.ndim: (0,) * _n)

    in_specs = [pl.BlockSpec((1, tm, d), lambda bi, i: (bi, i, 0)),
                pl.BlockSpec((1, 3, d), lambda bi, i: (bi, 0, 0)),
                full(ng), full(wtok), full(wtr),
                tile_tok, pl.BlockSpec((1, 1, tm), lambda bi, i: (bi, 0, i)),
                full(inv_l), full(sign_l), full(inv_nb), full(inv_mb),
                full(qng), full(kvng), full(wqt), full(wk), full(wv)]
    out_shape = [
        jax.ShapeDtypeStruct((b, NSA_WIDTH, s), BF16),
        jax.ShapeDtypeStruct((b, NSA_KV_GROUPS, s, LANES), BF16),
        jax.ShapeDtypeStruct((b, NSA_KV_GROUPS, s, LANES), BF16),
        jax.ShapeDtypeStruct((b, nt, v_rows_n, tk), BF16),
        jax.ShapeDtypeStruct((b, nt, v_rows_n, tk), BF16),
        jax.ShapeDtypeStruct((b, s, NSA_KV_WIDTH), F32),
        jax.ShapeDtypeStruct((b, s, NSA_KV_WIDTH), F32),
        jax.ShapeDtypeStruct((b, NSA_KV_GROUPS * GATE_ROWS, s), F32),
        jax.ShapeDtypeStruct((b, NSA_WIDTH, s), BF16),
        jax.ShapeDtypeStruct((b, MLA_WIDTH, s), BF16),
        jax.ShapeDtypeStruct((b, MLA_HEADS, MLA_QK, s), BF16),
        jax.ShapeDtypeStruct((b, s, MLA_QK), BF16),
        jax.ShapeDtypeStruct((b, nt, v_rows_m, tk), BF16),
    ]
    out_specs = [
        pl.BlockSpec((1, NSA_WIDTH, tm), lambda bi, i: (bi, 0, i)),
        pl.BlockSpec((1, NSA_KV_GROUPS, tm, LANES), lambda bi, i: (bi, 0, i, 0)),
        pl.BlockSpec((1, NSA_KV_GROUPS, tm, LANES), lambda bi, i: (bi, 0, i, 0)),
        pl.BlockSpec((1, tm // tk, v_rows_n, tk), lambda bi, i: (bi, i, 0, 0)),
        pl.BlockSpec((1, tm // tk, v_rows_n, tk), lambda bi, i: (bi, i, 0, 0)),
        pl.BlockSpec((1, tm, NSA_KV_WIDTH), lambda bi, i: (bi, i, 0)),
        pl.BlockSpec((1, tm, NSA_KV_WIDTH), lambda bi, i: (bi, i, 0)),
        pl.BlockSpec((1, NSA_KV_GROUPS * GATE_ROWS, tm), lambda bi, i: (bi, 0, i)),
        pl.BlockSpec((1, NSA_WIDTH, tm), lambda bi, i: (bi, 0, i)),
        pl.BlockSpec((1, MLA_WIDTH, tm), lambda bi, i: (bi, 0, i)),
        pl.BlockSpec((1, MLA_HEADS, MLA_QK, tm), lambda bi, i: (bi, 0, 0, i)),
        pl.BlockSpec((1, tm, MLA_QK), lambda bi, i: (bi, i, 0)),
        pl.BlockSpec((1, tm // tk, v_rows_m, tk), lambda bi, i: (bi, i, 0, 0)),
    ]
    kern = functools.partial(_proj_kernel, tm=tm, tk=tk, scale_nsa=NSA_HEAD_DIM ** -0.5 * LOG2E,
                             scale_mla=(MLA_NOPE_DIM + MLA_ROPE_DIM) ** -0.5 * LOG2E)
    return pl.pallas_call(
        kern, grid=(b, s // tm), in_specs=in_specs, out_specs=out_specs, out_shape=out_shape,
        compiler_params=pltpu.CompilerParams(vmem_limit_bytes=VMEM_LIMIT_BYTES),
        name="proj",
    )(x, mod3, ng, wtok, wtr, pos_b, pos_row, inv_l, sign_l, inv_nb, inv_mb, qng, kvng, wqt, wk, wv)


def _compress_kernel(k_ref, v_ref, ptop_ref, pbot_ref, wk1_ref, wv1_ref, wk2_ref, wv2t_ref, posc_ref, invl_ref,
                     signl_ref, kc_ref, vct_ref):
    ncp = k_ref.shape[1] // CMP_STRIDE
    gw = NSA_KV_GROUPS * CMP_HIDDEN
    lane = lax.broadcasted_iota(I32, (ncp, LANES), 1)

    def hidden(r_ref, w1_ref):
        r = jnp.concatenate([r_ref[0, pl.ds(t, ncp, stride=CMP_STRIDE), :] for t in range(CMP_STRIDE)], axis=1)
        a = jnp.dot((r + ptop_ref[...]).astype(BF16), w1_ref[:, 0:gw], preferred_element_type=F32)
        bt = jnp.dot((r + pbot_ref[...]).astype(BF16), w1_ref[:, gw:2 * gw], preferred_element_type=F32)
        return _silu(a + pltpu.roll(bt, ncp - 1, axis=0))

    cc, sc = _rope_lanes(posc_ref[...], invl_ref, signl_ref)
    hk = hidden(k_ref, wk1_ref)
    hv = hidden(v_ref, wv1_ref)
    for g in range(NSA_KV_GROUPS):
        hkg = hk[:, g * CMP_HIDDEN:(g + 1) * CMP_HIDDEN].astype(BF16)
        kc = jnp.dot(hkg, wk2_ref[...], preferred_element_type=F32)
        kc_ref[0, g] = _rope_tok(kc, cc, sc, NSA_HEAD_DIM // 2, lane).astype(BF16)
        hvg = hv[:, g * CMP_HIDDEN:(g + 1) * CMP_HIDDEN].astype(BF16)
        vct_ref[0, g] = lax.dot_general(wv2t_ref[...], hvg, NT, preferred_element_type=F32).astype(BF16)


def _compress(kcmp, vcmp, ptop, pbot, wk1, wv1, wk2, wv2t, pos_c, inv_l, sign_l):
    b, s, width = kcmp.shape
    ncp = s // CMP_STRIDE

    def full(a):
        return pl.BlockSpec(a.shape, lambda bi, _n=a.ndim: (0,) * _n)

    blk = pl.BlockSpec((1, s, width), lambda bi: (bi, 0, 0))
    tab = pl.BlockSpec((ncp, LANES), lambda bi: (bi, 0))
    return pl.pallas_call(
        _compress_kernel, grid=(b,),
        in_specs=[blk, blk, full(ptop), full(pbot), full(wk1), full(wv1), full(wk2), full(wv2t), tab,
                  full(inv_l), full(sign_l)],
        out_specs=[pl.BlockSpec((1, NSA_KV_GROUPS, ncp, LANES), lambda bi: (bi, 0, 0, 0)),
                   pl.BlockSpec((1, NSA_KV_GROUPS, NSA_HEAD_DIM, ncp), lambda bi: (bi, 0, 0, 0))],
        out_shape=[jax.ShapeDtypeStruct((b, NSA_KV_GROUPS, ncp, LANES), BF16),
                   jax.ShapeDtypeStruct((b, NSA_KV_GROUPS, NSA_HEAD_DIM, ncp), BF16)],
        compiler_params=pltpu.CompilerParams(vmem_limit_bytes=VMEM_LIMIT_BYTES),
        name="compress",
    )(kcmp, vcmp, ptop, pbot, wk1, wv1, wk2, wv2t, pos_c, inv_l, sign_l)


def _nsa_kernel(q1_ref, q2_ref, kc_ref, vct_ref, kaug_ref, vts_ref, kwin_ref, vtw_ref, g_ref, z_ref, mt_ref,
                o_ref, qaug_ref, sa_ref, sb_ref, m_ref, acc_ref, tot_ref, imp_ref, rank_ref, *, tq, tk, n_sel):
    i = pl.program_id(1)
    nh, ng, hpg, dk, half = NSA_HEADS, NSA_KV_GROUPS, NSA_HPG, NSA_HEAD_DIM, NSA_HEAD_DIM // 2
    vr = dk + ONES_ROWS
    group = [h // hpg for h in range(nh)]
    for h in range(nh):
        qaug_ref[h, 0:half, :] = q1_ref[0, h * half:(h + 1) * half, :]
        qaug_ref[h, half:dk, :] = q2_ref[0, h * half:(h + 1) * half, :]

    def gate(h, branch):
        row = group[h] * GATE_ROWS + (h % hpg) * N_BRANCH + branch
        return g_ref[0, row:row + 1, :]

    row_k = lax.broadcasted_iota(I32, (tk, tq), 0)
    col_q = lax.broadcasted_iota(I32, (tk, tq), 1)
    causal = row_k <= col_q

    def reset():
        m_ref[...] = jnp.full_like(m_ref, NEG_INF)
        acc_ref[...] = jnp.zeros_like(acc_ref)

    def add_branch(branch):
        for h in range(nh):
            inv_l = 1.0 / acc_ref[h, dk:dk + 1, :]
            tot_ref[h] = tot_ref[h] + (gate(h, branch) * inv_l) * acc_ref[h, 0:dk, :]

    def q_cols(h):
        return qaug_ref[h, 0:dk, :]

    ncp = kc_ref.shape[2]
    t_row = i * tq + lax.broadcasted_iota(I32, (1, tq), 1)
    last_n = (t_row - (CMP_BLOCK - 1)) >> CMP_SHIFT
    valid = lax.broadcasted_iota(I32, (ncp, tq), 0) <= last_n
    col_ok = last_n >= 0
    p_heads = {}

    def cmp_update(h, s_c):
        s_c = jnp.where(valid, s_c, NEG_INF)
        e = jnp.exp2(s_c - jnp.max(s_c, axis=0, keepdims=True))
        l_c = jnp.sum(e, axis=0, keepdims=True)
        p_c = e * jnp.where(col_ok, 1.0 / l_c, 0.0)
        o_c = jnp.dot(vct_ref[0, group[h]], p_c.astype(BF16), preferred_element_type=F32)
        tot_ref[h] = gate(h, 0) * o_c
        p_heads[h] = p_c

    nb = mt_ref.shape[0]

    def importance(g, _):
        psum = functools.reduce(lambda a, b: a + b, [p_heads[h] for h in range(nh) if group[h] == g])
        hi = psum.astype(BF16)
        lo = (psum - hi.astype(F32)).astype(BF16)
        mt = mt_ref[...]
        imp = jnp.dot(mt, hi, preferred_element_type=F32) + jnp.dot(mt, lo, preferred_element_type=F32)
        j_idx = lax.broadcasted_iota(I32, (nb, tq), 0)
        cur = (i * tq + lax.broadcasted_iota(I32, (nb, tq), 1)) >> SLC_SHIFT
        forced = (j_idx == 0) | (j_idx == cur) | (j_idx == cur - 1)
        imp_ref[g] = jnp.where(forced, FORCED_SCORE, jnp.where(j_idx > cur, -FORCED_SCORE, imp))

    stages = []
    for g in range(ng):
        kc = kc_ref[0, g, :, 0:dk]
        stages += [(functools.partial(jnp.dot, kc, q_cols(h), preferred_element_type=F32),
                    functools.partial(cmp_update, h)) for h in range(nh) if group[h] == g]
        stages.append((None, functools.partial(importance, g)))

    reset()
    n_back = WINDOW // tk
    for back in range(n_back + 1):
        jb = jnp.maximum(i - back, 0)
        if back == 0:
            keep = causal
        elif back == n_back:
            keep = (row_k > col_q) & (i >= back)
        else:
            keep = jnp.broadcast_to(i >= back, (tk, tq))
        for g in range(ng):
            kt_b = kwin_ref[0, g, pl.ds(pl.multiple_of(jb * tk, tk), tk), 0:dk]
            vt_b = vtw_ref[0, jb, g * vr:(g + 1) * vr, :]
            for h in range(nh):
                if group[h] == g:
                    stages.append((functools.partial(jnp.dot, kt_b, q_cols(h), preferred_element_type=F32),
                                   functools.partial(_chain_update, v_t=vt_b, m_ref=m_ref, acc_ref=acc_ref,
                                                     ch=h, keep=keep)))
    _pipeline(stages, lookahead=3)
    add_branch(2)

    rank_ref[...] = jnp.zeros_like(rank_ref)
    sub = lax.broadcasted_iota(I32, (SUBLANES, tq), 0)
    last_group = ((i + 1) * (tq // SLC_BLOCK) - 1) // SUBLANES

    def count(g, c, v):
        blk = imp_ref[g, v * SUBLANES:(v + 1) * SUBLANES, :]
        cnt = rank_ref[g, v * SUBLANES:(v + 1) * SUBLANES, :]
        for rr in range(SUBLANES):
            row = imp_ref[g, c * SUBLANES + rr:c * SUBLANES + rr + 1, :]
            if c < v:
                beats = row >= blk
            elif c > v:
                beats = row > blk
            else:
                beats = (row > blk) | ((row == blk) & (sub > rr))
            cnt = cnt + beats.astype(I32)
        rank_ref[g, v * SUBLANES:(v + 1) * SUBLANES, :] = cnt

    for lvl in range(nb // SUBLANES):
        @pl.when(lvl <= last_group)
        def _(lvl=lvl):
            for g in range(ng):
                for v in range(lvl + 1):
                    count(g, lvl, v)
                for c in range(lvl):
                    count(g, c, lvl)

    for g in range(ng):
        bias = jnp.where(rank_ref[g] < n_sel, 0.0, SEL_BIAS).astype(BF16)
        for h in range(nh):
            if group[h] == g:
                qaug_ref[h, dk:dk + nb, :] = bias

    reset()
    _causal_sweep(lambda j, g: kaug_ref[0, g, pl.ds(pl.multiple_of(j * tk, tk), tk), :],
                  lambda j, g: vts_ref[0, j, g * vr:(g + 1) * vr, :],
                  group, group, qaug_ref, sa_ref, sb_ref, m_ref, acc_ref, i, causal)
    add_branch(1)

    for h in range(nh):
        zz = z_ref[0, h * dk:(h + 1) * dk, :].astype(F32)
        o_ref[0, h * dk:(h + 1) * dk, :] = (tot_ref[h] * zz).astype(BF16)


def _nsa(qt, kc, vct, kaug, vts, kwin, vtw, gt, ztn, mt, *, tq, tk, n_sel):
    b, _, s = qt.shape
    nt = s // tk
    ncp = kc.shape[2]
    ng, nh = NSA_KV_GROUPS, NSA_HEADS
    hq = nh * (NSA_HEAD_DIM // 2)
    nb = mt.shape[0]
    vr = NSA_HEAD_DIM + ONES_ROWS
    in_specs = [
        pl.BlockSpec((1, hq, tq), lambda bi, i: (bi, 0, i)),
        pl.BlockSpec((1, hq, tq), lambda bi, i: (bi, 1, i)),
        pl.BlockSpec((1, ng, ncp, LANES), lambda bi, i: (bi, 0, 0, 0)),
        pl.BlockSpec((1, ng, NSA_HEAD_DIM, ncp), lambda bi, i: (bi, 0, 0, 0)),
        pl.BlockSpec((1, ng, s, LANES), lambda bi, i: (bi, 0, 0, 0)),
        pl.BlockSpec((1, nt, ng * vr, tk), lambda bi, i: (bi, 0, 0, 0)),
        pl.BlockSpec((1, ng, s, LANES), lambda bi, i: (bi, 0, 0, 0)),
        pl.BlockSpec((1, nt, ng * vr, tk), lambda bi, i: (bi, 0, 0, 0)),
        pl.BlockSpec((1, ng * GATE_ROWS, tq), lambda bi, i: (bi, 0, i)),
        pl.BlockSpec((1, NSA_WIDTH, tq), lambda bi, i: (bi, 0, i)),
        pl.BlockSpec(mt.shape, lambda bi, i: (0, 0)),
    ]
    kern = functools.partial(_nsa_kernel, tq=tq, tk=tk, n_sel=n_sel)
    return pl.pallas_call(
        kern, grid=(b, s // tq), in_specs=in_specs,
        out_specs=pl.BlockSpec((1, NSA_WIDTH, tq), lambda bi, i: (bi, 0, i)),
        out_shape=jax.ShapeDtypeStruct((b, NSA_WIDTH, s), BF16),
        scratch_shapes=[pltpu.VMEM((nh, NSA_HEAD_DIM + nb, tq), BF16),
                        pltpu.VMEM((nh, tk, tq), F32), pltpu.VMEM((nh, tk, tq), F32),
                        pltpu.VMEM((nh, 1, tq), F32), pltpu.VMEM((nh, vr, tq), F32),
                        pltpu.VMEM((nh, NSA_HEAD_DIM, tq), F32),
                        pltpu.VMEM((ng, nb, tq), F32), pltpu.VMEM((ng, nb, tq), I32)],
        compiler_params=pltpu.CompilerParams(vmem_limit_bytes=VMEM_LIMIT_BYTES),
        name="nsa",
    )(qt, qt, kc, vct, kaug, vts, kwin, vtw, gt, ztn, mt)


def _mla_kernel(q_ref, k_ref, vt_ref, z_ref, o_ref, sa_ref, sb_ref, m_ref, acc_ref, *, tq, tk):
    i = pl.program_id(1)
    dv, vr = MLA_V_DIM, MLA_V_DIM + ONES_ROWS
    m_ref[...] = jnp.full_like(m_ref, NEG_INF)
    acc_ref[...] = jnp.zeros_like(acc_ref)

    causal = lax.broadcasted_iota(I32, (tk, tq), 0) <= lax.broadcasted_iota(I32, (tk, tq), 1)
    _causal_sweep(lambda j, g: k_ref[0, pl.ds(pl.multiple_of(j * tk, tk), tk), :],
                  lambda j, hd: vt_ref[0, j, hd * vr:(hd + 1) * vr, :],
                  [0] * MLA_HEADS, list(range(MLA_HEADS)), q_ref.at[0], sa_ref, sb_ref, m_ref, acc_ref, i, causal)
    for hd in range(MLA_HEADS):
        o_h = acc_ref[hd, 0:dv, :] * (1.0 / acc_ref[hd, dv:dv + 1, :])
        zz = z_ref[0, hd * dv:(hd + 1) * dv, :].astype(F32)
        o_ref[0, hd * dv:(hd + 1) * dv, :] = (o_h * zz).astype(BF16)


def _mla(qtm, kmla, vtm, ztm, *, tq, tk):
    b, s, _ = kmla.shape
    nt = s // tk
    vrows = MLA_HEADS * (MLA_V_DIM + ONES_ROWS)
    kern = functools.partial(_mla_kernel, tq=tq, tk=tk)
    return pl.pallas_call(
        kern, grid=(b, s // tq),
        in_specs=[pl.BlockSpec((1, MLA_HEADS, MLA_QK, tq), lambda bi, i: (bi, 0, 0, i)),
                  pl.BlockSpec((1, s, MLA_QK), lambda bi, i: (bi, 0, 0)),
                  pl.BlockSpec((1, nt, vrows, tk), lambda bi, i: (bi, 0, 0, 0)),
                  pl.BlockSpec((1, MLA_WIDTH, tq), lambda bi, i: (bi, 0, i))],
        out_specs=pl.BlockSpec((1, MLA_WIDTH, tq), lambda bi, i: (bi, 0, i)),
        out_shape=jax.ShapeDtypeStruct((b, MLA_WIDTH, s), BF16),
        scratch_shapes=[pltpu.VMEM((MLA_HEADS, tk, tq), F32), pltpu.VMEM((MLA_HEADS, tk, tq), F32),
                        pltpu.VMEM((MLA_HEADS, 1, tq), F32), pltpu.VMEM((MLA_HEADS, MLA_V_DIM + ONES_ROWS, tq), F32)],
        compiler_params=pltpu.CompilerParams(vmem_limit_bytes=VMEM_LIMIT_BYTES),
        name="mla",
    )(qtm, kmla, vtm, ztm)


def _out_kernel(x_ref, mn_ref, mm_ref, w_ref, mod_ref, fg_ref, o_ref, *, final):
    y = lax.dot_general(mn_ref[0], w_ref[0:NSA_WIDTH, :], TN, preferred_element_type=F32)
    y = y + lax.dot_general(mm_ref[0], w_ref[NSA_WIDTH:MIX_WIDTH, :], TN, preferred_element_type=F32)
    x2 = x_ref[0] + mod_ref[0][2:3] * y
    o_ref[0] = _rms(x2, fg_ref[...]) if final else x2


def _out(x, mn, mm, w_out, mod3, fg, *, tm, final):
    b, s, d = x.shape
    return pl.pallas_call(
        functools.partial(_out_kernel, final=final), grid=(b, s // tm),
        in_specs=[pl.BlockSpec((1, tm, d), lambda bi, i: (bi, i, 0)),
                  pl.BlockSpec((1, NSA_WIDTH, tm), lambda bi, i: (bi, 0, i)),
                  pl.BlockSpec((1, MLA_WIDTH, tm), lambda bi, i: (bi, 0, i)),
                  pl.BlockSpec(w_out.shape, lambda bi, i: (0, 0)),
                  pl.BlockSpec((1, 3, d), lambda bi, i: (bi, 0, 0)),
                  pl.BlockSpec((1, d), lambda bi, i: (0, 0))],
        out_specs=pl.BlockSpec((1, tm, d), lambda bi, i: (bi, i, 0)),
        out_shape=jax.ShapeDtypeStruct((b, s, d), F32),
        compiler_params=pltpu.CompilerParams(vmem_limit_bytes=VMEM_LIMIT_BYTES),
        name="out_proj",
    )(x, mn, mm, w_out, mod3, fg)


def _cmp_to_slc_t(ncp, nc, nslc, nb):
    start = np.arange(nc)[:, None] * CMP_STRIDE
    bstart = np.arange(nslc)[None, :] * SLC_BLOCK
    ov = np.minimum(start + CMP_BLOCK, bstart + SLC_BLOCK) - np.maximum(start, bstart)
    m = (np.clip(ov, 0, None) / CMP_BLOCK).astype(np.float32)
    out = np.zeros((nb, ncp), np.float32)
    out[:nslc, :nc] = m.T
    return out


def _layout_w_in(w):
    d = w.shape[0]
    (q_n, kc_n, vc_n, ks_n, vs_n, kw_n, vw_n, gl_n, z_n, cq_m, ckv_m, kr_m, z_m) = jnp.split(w, IN_OFFSETS, axis=-1)
    dk = NSA_HEAD_DIM
    z64 = jnp.zeros((d, LANES - dk), w.dtype)
    wtok = jnp.concatenate(
        [ks_n[:, :dk], z64, ks_n[:, dk:], z64, kw_n[:, :dk], z64, kw_n[:, dk:], z64,
         kc_n, vc_n, cq_m, ckv_m, jnp.zeros((d, KR_LANE), w.dtype), kr_m,
         jnp.zeros((d, LANES - KR_LANE - MLA_ROPE_DIM), w.dtype)], axis=1)
    qr = q_n.reshape(d, NSA_HEADS, 2, dk // 2)
    q_perm = jnp.concatenate([qr[:, :, 0, :].reshape(d, -1), qr[:, :, 1, :].reshape(d, -1)], axis=1)
    gl = gl_n.reshape(d, NSA_KV_GROUPS, NSA_HPG * N_BRANCH)
    gl = jnp.pad(gl, ((0, 0), (0, 0), (0, GATE_ROWS - NSA_HPG * N_BRANCH))).reshape(d, -1)
    wtr = jnp.concatenate([q_perm, vs_n, vw_n, gl, z_n, z_m], axis=1).T
    assert wtok.shape[1] == TOK_COLS and wtr.shape[0] == TR_ROWS
    return wtok.astype(BF16), wtr.astype(BF16)


def _layout_w1(w1):
    hid = w1.shape[1]
    w1r = w1.reshape(2, CMP_STRIDE, NSA_HEAD_DIM, hid)
    eye = jnp.eye(NSA_KV_GROUPS, dtype=w1.dtype)
    halves = [jnp.einsum('ldh,pg->lpdgh', w1r[k], eye).reshape(CMP_STRIDE * NSA_KV_WIDTH, NSA_KV_GROUPS * hid)
              for k in range(2)]
    return jnp.concatenate(halves, axis=1).astype(BF16)


def kernel(x, c, positions, ada_w, ada_b, norm_g, w_in, cmp_pos, cmp_k_w1, cmp_k_w2, cmp_v_w1, cmp_v_w2,
           q_norm_g, w_q_up, kv_norm_g, w_kv_up, w_out, final_norm_g):
    b, s, d = x.shape
    depth = ada_w.shape[0]
    tm, tq = PROJ_TILE, ATT_TILE
    tk = tq
    assert s % tm == 0 and tm % tk == 0 and WINDOW % tk == 0 and (tq & (tq - 1)) == 0
    assert CMP_BLOCK == 2 * CMP_STRIDE and s % SLC_BLOCK == 0
    nslc = s // SLC_BLOCK
    nb = LANES - NSA_HEAD_DIM
    assert nslc <= nb
    ncp = s // CMP_STRIDE
    nc = ncp - 1

    half_n, half_m = NSA_HEAD_DIM // 2, MLA_ROPE_DIM // 2
    inv_n = ROPE_THETA ** (-jnp.arange(half_n, dtype=F32) / half_n)
    inv_m = ROPE_THETA ** (-jnp.arange(half_m, dtype=F32) / half_m)
    ones_n, ones_m = jnp.ones((half_n,), F32), jnp.ones((half_m,), F32)
    pad = jnp.zeros((LANES - NSA_HEAD_DIM - MLA_ROPE_DIM,), F32)
    inv_l = jnp.concatenate([inv_n, inv_n, inv_m, inv_m, pad])[None]
    sign_l = jnp.concatenate([-ones_n, ones_n, -ones_m, ones_m, pad])[None]
    pos_f = positions.astype(F32)
    pos_b = jnp.broadcast_to(pos_f.reshape(b * s, 1), (b * s, LANES))
    pos_row = pos_f.reshape(b, 1, s)
    cmp_end = np.minimum(np.arange(ncp) * CMP_STRIDE + CMP_BLOCK - 1, s - 1)
    pos_c = jnp.broadcast_to(pos_f[:, cmp_end].reshape(b * ncp, 1), (b * ncp, LANES))
    inv_nb = jnp.broadcast_to(inv_n[:, None], (half_n, tm))
    inv_mb = jnp.broadcast_to(inv_m[:, None], (half_m, tm))

    mt = jnp.asarray(_cmp_to_slc_t(ncp, nc, nslc, nb), dtype=BF16)
    bp = -(-b // SUBLANES) * SUBLANES
    c_pad = jnp.pad(c, ((0, bp - b), (0, 0)))

    for l in range(depth):
        mod = _adaln(c_pad, ada_w[l], ada_b[l].reshape(1, -1))
        mod3 = mod[:b].reshape(b, 3, d)
        wtok, wtr = _layout_w_in(w_in[l])
        wq = w_q_up[l].reshape(MLA_Q_RANK, MLA_HEADS, MLA_NOPE_DIM + MLA_ROPE_DIM)
        wqt = jnp.concatenate([wq[:, :, :MLA_NOPE_DIM].reshape(MLA_Q_RANK, -1),
                               wq[:, :, MLA_NOPE_DIM:MLA_NOPE_DIM + half_m].reshape(MLA_Q_RANK, -1),
                               wq[:, :, MLA_NOPE_DIM + half_m:].reshape(MLA_Q_RANK, -1)], axis=1).T.astype(BF16)
        wkv = w_kv_up[l].reshape(MLA_KV_RANK, MLA_HEADS, MLA_NOPE_DIM + MLA_V_DIM)
        wk = wkv[:, :, :MLA_NOPE_DIM].transpose(1, 0, 2).astype(BF16)
        wv = wkv[:, :, MLA_NOPE_DIM:].transpose(1, 2, 0).reshape(MLA_WIDTH, MLA_KV_RANK).astype(BF16)

        (qt, kaug, kwin, vts, vtw, kcmp, vcmp, gt, ztn, ztm, qtm, kmla, vtm) = _proj(
            x, mod3, norm_g[l].reshape(1, d), wtok, wtr, pos_b, pos_row, inv_l, sign_l, inv_nb, inv_mb,
            q_norm_g[l].reshape(1, -1), kv_norm_g[l].reshape(1, -1), wqt, wk, wv, tm=tm, tk=tk)

        pos_l = cmp_pos[l]
        ptop = jnp.broadcast_to(pos_l[:CMP_STRIDE, None, :], (CMP_STRIDE, NSA_KV_GROUPS, NSA_HEAD_DIM)).reshape(1, -1)
        pbot = jnp.broadcast_to(pos_l[CMP_STRIDE:, None, :], (CMP_STRIDE, NSA_KV_GROUPS, NSA_HEAD_DIM)).reshape(1, -1)
        wk2 = jnp.pad(cmp_k_w2[l], ((0, 0), (0, LANES - NSA_HEAD_DIM))).astype(BF16)
        kc, vct = _compress(kcmp, vcmp, ptop, pbot,
                            _layout_w1(cmp_k_w1[l]), _layout_w1(cmp_v_w1[l]), wk2,
                            cmp_v_w2[l].T.astype(BF16), pos_c, inv_l, sign_l)

        mix_n = _nsa(qt, kc, vct, kaug, vts, kwin, vtw, gt, ztn, mt, tq=tq, tk=tk, n_sel=min(SLC_TOPK, nslc))
        mix_m = _mla(qtm, kmla, vtm, ztm, tq=tq, tk=tk)
        x = _out(x, mix_n, mix_m, w_out[l].astype(BF16), mod3, final_norm_g.reshape(1, d), tm=tm,
                 final=(l == depth - 1))
    return x
```

```python
import functools

import numpy as np
import jax
import jax.numpy as jnp
from jax import lax
from jax.experimental import pallas as pl
from jax.experimental.pallas import tpu as pltpu

F32 = jnp.float32
BF16 = jnp.bfloat16
I32 = jnp.int32

NSA_HEADS = 8
NSA_KV_GROUPS = 2
NSA_HPG = NSA_HEADS // NSA_KV_GROUPS
NSA_HEAD_DIM = 64
NSA_WIDTH = NSA_HEADS * NSA_HEAD_DIM
NSA_KV_WIDTH = NSA_KV_GROUPS * NSA_HEAD_DIM
CMP_BLOCK = 32
CMP_STRIDE = 16
CMP_HIDDEN = 128
SLC_BLOCK = 64
SLC_TOPK = 16
WINDOW = 512
N_BRANCH = 3
FORCED_SCORE = 1.0e4
MLA_HEADS = 8
MLA_NOPE_DIM = 64
MLA_ROPE_DIM = 32
MLA_V_DIM = 64
MLA_WIDTH = MLA_HEADS * MLA_V_DIM
MLA_Q_RANK = 256
MLA_KV_RANK = 128
MIX_WIDTH = NSA_WIDTH + MLA_WIDTH
ROPE_THETA = 10000.0
NORM_EPS = 1e-6
NEG_INF = -1e30
IN_SIZES = (NSA_WIDTH, NSA_KV_WIDTH, NSA_KV_WIDTH, NSA_KV_WIDTH, NSA_KV_WIDTH, NSA_KV_WIDTH, NSA_KV_WIDTH,
            NSA_HEADS * N_BRANCH, NSA_WIDTH, MLA_Q_RANK, MLA_KV_RANK, MLA_ROPE_DIM, MLA_WIDTH)
IN_OFFSETS = tuple(int(o) for o in np.cumsum(IN_SIZES)[:-1])

LANES = 128
SUBLANES = 8
VMEM_LIMIT_BYTES = 56 * 1024 * 1024

PROJ_TILE = 512
ATT_TILE = 256
SEL_BIAS = NEG_INF
GATE_ROWS = 16
ONES_ROWS = 16
MLA_QK = MLA_KV_RANK + MLA_ROPE_DIM
LOG2E = float(np.log2(np.e))
SWEEP_UNROLL = 4
SLC_SHIFT = SLC_BLOCK.bit_length() - 1
CMP_SHIFT = CMP_STRIDE.bit_length() - 1
assert 1 << SLC_SHIFT == SLC_BLOCK and 1 << CMP_SHIFT == CMP_STRIDE

NT = (((1,), (1,)), ((), ()))
TN = (((0,), (0,)), ((), ()))


def _silu(v):
    return v * jax.nn.sigmoid(v)


def _rms(v, g):
    ms = jnp.mean(v * v, axis=-1, keepdims=True)
    return v * lax.rsqrt(ms + NORM_EPS) * g


def _rope_tok(v, c, s_signed, half, lane, base=0):
    up = pltpu.roll(v, LANES - half, axis=1)
    dn = pltpu.roll(v, half, axis=1)
    return v * c + jnp.where(lane < base + half, up, dn) * s_signed


def _rope_lanes(pos_b, inv_ref, sign_ref):
    ang = pos_b * inv_ref[...]
    return jnp.cos(ang), jnp.sin(ang) * sign_ref[...]


def _chain_update(s_t, v_t, m_ref, acc_ref, ch, keep=None):
    if keep is not None:
        s_t = jnp.where(keep, s_t, NEG_INF)
    m_prev = m_ref[ch]
    m_new = jnp.maximum(m_prev, jnp.max(s_t, axis=0, keepdims=True))
    alpha = jnp.exp2(m_prev - m_new)
    p = jnp.exp2(s_t - m_new)
    acc_ref[ch] = alpha * acc_ref[ch] + jnp.dot(v_t, p.astype(BF16), preferred_element_type=F32)
    m_ref[ch] = m_new


def _causal_sweep(k_tile, v_tile, k_group, v_group, q_ref, sa_ref, sb_ref, m_ref, acc_ref, last, keep_last,
                  lookahead=2):
    n_chains = len(k_group)

    def loader(tile_fn, j):
        cache = {}
        return lambda g: cache.setdefault(g, tile_fn(j, g))

    def phase(j_next, s_next_ref, s_cur_ref, j_cur, keep=None):
        k_next = loader(k_tile, j_next) if j_next is not None else None
        v_cur = loader(v_tile, j_cur)
        for n in range(n_chains + lookahead):
            if k_next is not None and n < n_chains:
                s_next_ref[n] = jnp.dot(k_next(k_group[n]), q_ref[n], preferred_element_type=F32)
            if n >= lookahead:
                ch = n - lookahead
                _chain_update(s_cur_ref[ch], v_cur(v_group[ch]), m_ref, acc_ref, ch, keep=keep)

    bufs = (sa_ref, sb_ref)
    k_0 = loader(k_tile, 0)
    for ch in range(n_chains):
        sa_ref[ch] = jnp.dot(k_0(k_group[ch]), q_ref[ch], preferred_element_type=F32)

    def run(j0, count):
        for u in range(count):
            phase(j0 + u + 1, bufs[(u + 1) % 2], bufs[u % 2], j0 + u)

    def body(jj, carry):
        run(SWEEP_UNROLL * jj, SWEEP_UNROLL)
        return carry

    lax.fori_loop(0, last // SWEEP_UNROLL, body, 0)
    rem = last % SWEEP_UNROLL
    for r in range(SWEEP_UNROLL):
        @pl.when(rem == r)
        def _(r=r):
            run(last - r, r)
            phase(None, None, bufs[r % 2], last, keep=keep_last)


def _pipeline(stages, lookahead):
    pending = {}
    for n in range(len(stages) + lookahead):
        if n < len(stages) and stages[n][0] is not None:
            pending[n] = stages[n][0]()
        if n >= lookahead:
            stages[n - lookahead][1](pending.pop(n - lookahead, None))


def _adaln_kernel(c_ref, w_ref, b_ref, o_ref):
    sc = _silu(c_ref[...])
    o_ref[...] = jnp.dot(sc.astype(BF16), w_ref[...].astype(BF16), preferred_element_type=F32) + b_ref[...]


def _adaln(c_pad, w, b):
    bp, d = c_pad.shape
    n = w.shape[1] // d
    return pl.pallas_call(
        _adaln_kernel,
        grid=(n,),
        in_specs=[pl.BlockSpec((bp, d), lambda j: (0, 0)),
                  pl.BlockSpec((d, d), lambda j: (0, j)),
                  pl.BlockSpec((1, d), lambda j: (0, j))],
        out_specs=pl.BlockSpec((bp, d), lambda j: (0, j)),
        out_shape=jax.ShapeDtypeStruct((bp, n * d), F32),
        name="adaln",
    )(c_pad, w, b)


TOK_KS, TOK_KW, TOK_KC, TOK_VC, TOK_CQ, TOK_CKV, TOK_KR, TOK_COLS = 0, 256, 512, 640, 768, 1024, 1152, 1280
KR_LANE = NSA_HEAD_DIM
TR_Q, TR_VS, TR_VW, TR_G, TR_ZN, TR_ZM, TR_ROWS = 0, 512, 640, 768, 800, 1312, 1824


def _proj_kernel(x_ref, mod_ref, ng_ref, wtok_ref, wtr_ref, posb_ref, posr_ref, invl_ref, signl_ref,
                 invn_ref, invm_ref, qng_ref, kvng_ref, wqt_ref, wk_ref, wv_ref,
                 qt_ref, kaug_ref, kwin_ref, vts_ref, vtw_ref, kcmp_ref, vcmp_ref, gt_ref,
                 ztn_ref, ztm_ref, qtm_ref, kmla_ref, vtm_ref, *, tm, tk, scale_nsa, scale_mla):
    i = pl.program_id(1)
    mod = mod_ref[0]
    h = _rms(x_ref[0], ng_ref[...]) * (1.0 + mod[1:2]) + mod[0:1]
    hb = h.astype(BF16)
    tok = jnp.dot(hb, wtok_ref[...], preferred_element_type=F32)
    tr = lax.dot_general(wtr_ref[...], hb, NT, preferred_element_type=F32)

    lane = lax.broadcasted_iota(I32, (tm, LANES), 1)
    row = lax.broadcasted_iota(I32, (tm, LANES), 0)
    blk = (i * tm + row) >> SLC_SHIFT
    onehot = (lane - NSA_HEAD_DIM == blk).astype(F32)
    ct, st = _rope_lanes(posb_ref[...], invl_ref, signl_ref)
    half_n = NSA_HEAD_DIM // 2
    for g in range(NSA_KV_GROUPS):
        ks = _rope_tok(tok[:, TOK_KS + LANES * g:TOK_KS + LANES * (g + 1)], ct, st, half_n, lane)
        kaug_ref[0, g] = jnp.where(lane >= NSA_HEAD_DIM, onehot, ks).astype(BF16)
        kw = _rope_tok(tok[:, TOK_KW + LANES * g:TOK_KW + LANES * (g + 1)], ct, st, half_n, lane)
        kwin_ref[0, g] = kw.astype(BF16)
    kcmp_ref[0] = tok[:, TOK_KC:TOK_KC + LANES]
    vcmp_ref[0] = tok[:, TOK_VC:TOK_VC + LANES]

    ckvn = _rms(tok[:, TOK_CKV:TOK_CKV + MLA_KV_RANK], kvng_ref[...])
    krr = _rope_tok(tok[:, TOK_KR:TOK_KR + LANES], ct, st, MLA_ROPE_DIM // 2, lane, base=KR_LANE)
    kmla_ref[0, :, MLA_KV_RANK:MLA_QK] = krr[:, KR_LANE:KR_LANE + MLA_ROPE_DIM].astype(BF16)
    ckvb = ckvn.astype(BF16)
    kmla_ref[0, :, 0:MLA_KV_RANK] = ckvb
    vtm = lax.dot_general(wv_ref[...], ckvb, NT, preferred_element_type=F32).astype(BF16)
    ones = jnp.ones((ONES_ROWS, tk), BF16)
    vr = MLA_V_DIM + ONES_ROWS
    for ii in range(tm // tk):
        for hd in range(MLA_HEADS):
            vtm_ref[0, ii, hd * vr:hd * vr + MLA_V_DIM, :] = vtm[hd * MLA_V_DIM:(hd + 1) * MLA_V_DIM,
                                                                 ii * tk:(ii + 1) * tk]
            vtm_ref[0, ii, hd * vr + MLA_V_DIM:(hd + 1) * vr, :] = ones

    cqn = _rms(tok[:, TOK_CQ:TOK_CQ + MLA_Q_RANK], qng_ref[...]).astype(BF16)
    qm = lax.dot_general(wqt_ref[...], cqn, NT, preferred_element_type=F32)
    nq = MLA_HEADS * MLA_NOPE_DIM
    hr = MLA_ROPE_DIM // 2
    x1 = qm[nq:nq + MLA_HEADS * hr].reshape(MLA_HEADS, hr, tm)
    x2 = qm[nq + MLA_HEADS * hr:nq + 2 * MLA_HEADS * hr].reshape(MLA_HEADS, hr, tm)
    ang_m = invm_ref[...] * posr_ref[0]
    cm_t, sm_t = jnp.cos(ang_m)[None], jnp.sin(ang_m)[None]
    o1 = (x1 * cm_t - x2 * sm_t) * scale_mla
    o2 = (x2 * cm_t + x1 * sm_t) * scale_mla
    for hd in range(MLA_HEADS):
        qn_h = qm[hd * MLA_NOPE_DIM:(hd + 1) * MLA_NOPE_DIM].astype(BF16)
        qabs = jnp.dot(wk_ref[hd], qn_h, preferred_element_type=F32)
        qtm_ref[0, hd, 0:MLA_KV_RANK, :] = (qabs * scale_mla).astype(BF16)
        qtm_ref[0, hd, MLA_KV_RANK:MLA_KV_RANK + hr, :] = o1[hd].astype(BF16)
        qtm_ref[0, hd, MLA_KV_RANK + hr:MLA_QK, :] = o2[hd].astype(BF16)

    hq = NSA_HEADS * half_n
    q1 = tr[TR_Q:TR_Q + hq].reshape(NSA_HEADS, half_n, tm)
    q2 = tr[TR_Q + hq:TR_Q + 2 * hq].reshape(NSA_HEADS, half_n, tm)
    ang_n = invn_ref[...] * posr_ref[0]
    cn_t, sn_t = jnp.cos(ang_n)[None], jnp.sin(ang_n)[None]
    qt_ref[0, 0:hq, :] = ((q1 * cn_t - q2 * sn_t) * scale_nsa).reshape(hq, tm).astype(BF16)
    qt_ref[0, hq:2 * hq, :] = ((q2 * cn_t + q1 * sn_t) * scale_nsa).reshape(hq, tm).astype(BF16)

    vts = tr[TR_VS:TR_VS + NSA_KV_WIDTH].astype(BF16)
    vtw = tr[TR_VW:TR_VW + NSA_KV_WIDTH].astype(BF16)
    dk = NSA_HEAD_DIM
    gr = dk + ONES_ROWS
    for ii in range(tm // tk):
        for g in range(NSA_KV_GROUPS):
            vts_ref[0, ii, g * gr:g * gr + dk, :] = vts[g * dk:(g + 1) * dk, ii * tk:(ii + 1) * tk]
            vtw_ref[0, ii, g * gr:g * gr + dk, :] = vtw[g * dk:(g + 1) * dk, ii * tk:(ii + 1) * tk]
            vts_ref[0, ii, g * gr + dk:(g + 1) * gr, :] = ones
            vtw_ref[0, ii, g * gr + dk:(g + 1) * gr, :] = ones
    gt_ref[0] = jax.nn.sigmoid(tr[TR_G:TR_G + NSA_KV_GROUPS * GATE_ROWS])
    ztn_ref[0] = _silu(tr[TR_ZN:TR_ZN + NSA_WIDTH]).astype(BF16)
    ztm_ref[0] = _silu(tr[TR_ZM:TR_ZM + MLA_WIDTH]).astype(BF16)


def _proj(x, mod3, ng, wtok, wtr, pos_b, pos_row, inv_l, sign_l, inv_nb, inv_mb, qng, kvng, wqt, wk, wv, *, tm, tk):
    b, s, d = x.shape
    nt = s // tk
    v_rows_n = NSA_KV_GROUPS * (NSA_HEAD_DIM + ONES_ROWS)
    v_rows_m = MLA_HEADS * (MLA_V_DIM + ONES_ROWS)
    tile_tok = pl.BlockSpec((tm, LANES), lambda bi, i: (bi * (s // tm) + i, 0))

    def full(a):
        return pl.BlockSpec(a.shape, lambda bi, i, _n=a.ndim: (0,) * _n)

    in_specs = [pl.BlockSpec((1, tm, d), lambda bi, i: (bi, i, 0)),
                pl.BlockSpec((1, 3, d), lambda bi, i: (bi, 0, 0)),
                full(ng), full(wtok), full(wtr),
                tile_tok, pl.BlockSpec((1, 1, tm), lambda bi, i: (bi, 0, i)),
                full(inv_l), full(sign_l), full(inv_nb), full(inv_mb),
                full(qng), full(kvng), full(wqt), full(wk), full(wv)]
    out_shape = [
        jax.ShapeDtypeStruct((b, NSA_WIDTH, s), BF16),
        jax.ShapeDtypeStruct((b, NSA_KV_GROUPS, s, LANES), BF16),
        jax.ShapeDtypeStruct((b, NSA_KV_GROUPS, s, LANES), BF16),
        jax.ShapeDtypeStruct((b, nt, v_rows_n, tk), BF16),
        jax.ShapeDtypeStruct((b, nt, v_rows_n, tk), BF16),
        jax.ShapeDtypeStruct((b, s, NSA_KV_WIDTH), F32),
        jax.ShapeDtypeStruct((b, s, NSA_KV_WIDTH), F32),
        jax.ShapeDtypeStruct((b, NSA_KV_GROUPS * GATE_ROWS, s), F32),
        jax.ShapeDtypeStruct((b, NSA_WIDTH, s), BF16),
        jax.ShapeDtypeStruct((b, MLA_WIDTH, s), BF16),
        jax.ShapeDtypeStruct((b, MLA_HEADS, MLA_QK, s), BF16),
        jax.ShapeDtypeStruct((b, s, MLA_QK), BF16),
        jax.ShapeDtypeStruct((b, nt, v_rows_m, tk), BF16),
    ]
    out_specs = [
        pl.BlockSpec((1, NSA_WIDTH, tm), lambda bi, i: (bi, 0, i)),
        pl.BlockSpec((1, NSA_KV_GROUPS, tm, LANES), lambda bi, i: (bi, 0, i, 0)),
        pl.BlockSpec((1, NSA_KV_GROUPS, tm, LANES), lambda bi, i: (bi, 0, i, 0)),
        pl.BlockSpec((1, tm // tk, v_rows_n, tk), lambda bi, i: (bi, i, 0, 0)),
        pl.BlockSpec((1, tm // tk, v_rows_n, tk), lambda bi, i: (bi, i, 0, 0)),
        pl.BlockSpec((1, tm, NSA_KV_WIDTH), lambda bi, i: (bi, i, 0)),
        pl.BlockSpec((1, tm, NSA_KV_WIDTH), lambda bi, i: (bi, i, 0)),
        pl.BlockSpec((1, NSA_KV_GROUPS * GATE_ROWS, tm), lambda bi, i: (bi, 0, i)),
        pl.BlockSpec((1, NSA_WIDTH, tm), lambda bi, i: (bi, 0, i)),
        pl.BlockSpec((1, MLA_WIDTH, tm), lambda bi, i: (bi, 0, i)),
        pl.BlockSpec((1, MLA_HEADS, MLA_QK, tm), lambda bi, i: (bi, 0, 0, i)),
        pl.BlockSpec((1, tm, MLA_QK), lambda bi, i: (bi, i, 0)),
        pl.BlockSpec((1, tm // tk, v_rows_m, tk), lambda bi, i: (bi, i, 0, 0)),
    ]
    kern = functools.partial(_proj_kernel, tm=tm, tk=tk, scale_nsa=NSA_HEAD_DIM ** -0.5 * LOG2E,
                             scale_mla=(MLA_NOPE_DIM + MLA_ROPE_DIM) ** -0.5 * LOG2E)
    return pl.pallas_call(
        kern, grid=(b, s // tm), in_specs=in_specs, out_specs=out_specs, out_shape=out_shape,
        compiler_params=pltpu.CompilerParams(vmem_limit_bytes=VMEM_LIMIT_BYTES),
        name="proj",
    )(x, mod3, ng, wtok, wtr, pos_b, pos_row, inv_l, sign_l, inv_nb, inv_mb, qng, kvng, wqt, wk, wv)


def _compress_kernel(k_ref, v_ref, ptop_ref, pbot_ref, wk1_ref, wv1_ref, wk2_ref, wv2t_ref, posc_ref, invl_ref,
                     signl_ref, kc_ref, vct_ref):
    ncp = k_ref.shape[1] // CMP_STRIDE
    gw = NSA_KV_GROUPS * CMP_HIDDEN
    lane = lax.broadcasted_iota(I32, (ncp, LANES), 1)

    def hidden(r_ref, w1_ref):
        r = jnp.concatenate([r_ref[0, pl.ds(t, ncp, stride=CMP_STRIDE), :] for t in range(CMP_STRIDE)], axis=1)
        a = jnp.dot((r + ptop_ref[...]).astype(BF16), w1_ref[:, 0:gw], preferred_element_type=F32)
        bt = jnp.dot((r + pbot_ref[...]).astype(BF16), w1_ref[:, gw:2 * gw], preferred_element_type=F32)
        return _silu(a + pltpu.roll(bt, ncp - 1, axis=0))

    cc, sc = _rope_lanes(posc_ref[...], invl_ref, signl_ref)
    hk = hidden(k_ref, wk1_ref)
    hv = hidden(v_ref, wv1_ref)
    for g in range(NSA_KV_GROUPS):
        hkg = hk[:, g * CMP_HIDDEN:(g + 1) * CMP_HIDDEN].astype(BF16)
        kc = jnp.dot(hkg, wk2_ref[...], preferred_element_type=F32)
        kc_ref[0, g] = _rope_tok(kc, cc, sc, NSA_HEAD_DIM // 2, lane).astype(BF16)
        hvg = hv[:, g * CMP_HIDDEN:(g + 1) * CMP_HIDDEN].astype(BF16)
        vct_ref[0, g] = lax.dot_general(wv2t_ref[...], hvg, NT, preferred_element_type=F32).astype(BF16)


def _compress(kcmp, vcmp, ptop, pbot, wk1, wv1, wk2, wv2t, pos_c, inv_l, sign_l):
    b, s, width = kcmp.shape
    ncp = s // CMP_STRIDE

    def full(a):
        return pl.BlockSpec(a.shape, lambda bi, _n=a.ndim: (0,) * _n)

    blk = pl.BlockSpec((1, s, width), lambda bi: (bi, 0, 0))
    tab = pl.BlockSpec((ncp, LANES), lambda bi: (bi, 0))
    return pl.pallas_call(
        _compress_kernel, grid=(b,),
        in_specs=[blk, blk, full(ptop), full(pbot), full(wk1), full(wv1), full(wk2), full(wv2t), tab,
                  full(inv_l), full(sign_l)],
        out_specs=[pl.BlockSpec((1, NSA_KV_GROUPS, ncp, LANES), lambda bi: (bi, 0, 0, 0)),
                   pl.BlockSpec((1, NSA_KV_GROUPS, NSA_HEAD_DIM, ncp), lambda bi: (bi, 0, 0, 0))],
        out_shape=[jax.ShapeDtypeStruct((b, NSA_KV_GROUPS, ncp, LANES), BF16),
                   jax.ShapeDtypeStruct((b, NSA_KV_GROUPS, NSA_HEAD_DIM, ncp), BF16)],
        compiler_params=pltpu.CompilerParams(vmem_limit_bytes=VMEM_LIMIT_BYTES),
        name="compress",
    )(kcmp, vcmp, ptop, pbot, wk1, wv1, wk2, wv2t, pos_c, inv_l, sign_l)


def _nsa_kernel(q1_ref, q2_ref, kc_ref, vct_ref, kaug_ref, vts_ref, kwin_ref, vtw_ref, g_ref, z_ref, mt_ref,
                o_ref, qaug_ref, sa_ref, sb_ref, m_ref, acc_ref, tot_ref, imp_ref, rank_ref, *, tq, tk, n_sel):
    i = pl.program_id(1)
    nh, ng, hpg, dk, half = NSA_HEADS, NSA_KV_GROUPS, NSA_HPG, NSA_HEAD_DIM, NSA_HEAD_DIM // 2
    vr = dk + ONES_ROWS
    group = [h // hpg for h in range(nh)]
    for h in range(nh):
        qaug_ref[h, 0:half, :] = q1_ref[0, h * half:(h + 1) * half, :]
        qaug_ref[h, half:dk, :] = q2_ref[0, h * half:(h + 1) * half, :]

    def gate(h, branch):
        row = group[h] * GATE_ROWS + (h % hpg) * N_BRANCH + branch
        return g_ref[0, row:row + 1, :]

    row_k = lax.broadcasted_iota(I32, (tk, tq), 0)
    col_q = lax.broadcasted_iota(I32, (tk, tq), 1)
    causal = row_k <= col_q

    def reset():
        m_ref[...] = jnp.full_like(m_ref, NEG_INF)
        acc_ref[...] = jnp.zeros_like(acc_ref)

    def add_branch(branch):
        for h in range(nh):
            inv_l = 1.0 / acc_ref[h, dk:dk + 1, :]
            tot_ref[h] = tot_ref[h] + (gate(h, branch) * inv_l) * acc_ref[h, 0:dk, :]

    def q_cols(h):
        return qaug_ref[h, 0:dk, :]

    ncp = kc_ref.shape[2]
    t_row = i * tq + lax.broadcasted_iota(I32, (1, tq), 1)
    last_n = (t_row - (CMP_BLOCK - 1)) >> CMP_SHIFT
    valid = lax.broadcasted_iota(I32, (ncp, tq), 0) <= last_n
    col_ok = last_n >= 0
    p_heads = {}

    def cmp_update(h, s_c):
        s_c = jnp.where(valid, s_c, NEG_INF)
        e = jnp.exp2(s_c - jnp.max(s_c, axis=0, keepdims=True))
        l_c = jnp.sum(e, axis=0, keepdims=True)
        p_c = e * jnp.where(col_ok, 1.0 / l_c, 0.0)
        o_c = jnp.dot(vct_ref[0, group[h]], p_c.astype(BF16), preferred_element_type=F32)
        tot_ref[h] = gate(h, 0) * o_c
        p_heads[h] = p_c

    nb = mt_ref.shape[0]

    def importance(g, _):
        psum = functools.reduce(lambda a, b: a + b, [p_heads[h] for h in range(nh) if group[h] == g])
        hi = psum.astype(BF16)
        lo = (psum - hi.astype(F32)).astype(BF16)
        mt = mt_ref[...]
        imp = jnp.dot(mt, hi, preferred_element_type=F32) + jnp.dot(mt, lo, preferred_element_type=F32)
        j_idx = lax.broadcasted_iota(I32, (nb, tq), 0)
        cur = (i * tq + lax.broadcasted_iota(I32, (nb, tq), 1)) >> SLC_SHIFT
        forced = (j_idx == 0) | (j_idx == cur) | (j_idx == cur - 1)
        imp_ref[g] = jnp.where(forced, FORCED_SCORE, jnp.where(j_idx > cur, -FORCED_SCORE, imp))

    stages = []
    for g in range(ng):
        kc = kc_ref[0, g, :, 0:dk]
        stages += [(functools.partial(jnp.dot, kc, q_cols(h), preferred_element_type=F32),
                    functools.partial(cmp_update, h)) for h in range(nh) if group[h] == g]
        stages.append((None, functools.partial(importance, g)))

    reset()
    n_back = WINDOW // tk
    for back in range(n_back + 1):
        jb = jnp.maximum(i - back, 0)
        if back == 0:
            keep = causal
        elif back == n_back:
            keep = (row_k > col_q) & (i >= back)
        else:
            keep = jnp.broadcast_to(i >= back, (tk, tq))
        for g in range(ng):
            kt_b = kwin_ref[0, g, pl.ds(pl.multiple_of(jb * tk, tk), tk), 0:dk]
            vt_b = vtw_ref[0, jb, g * vr:(g + 1) * vr, :]
            for h in range(nh):
                if group[h] == g:
                    stages.append((functools.partial(jnp.dot, kt_b, q_cols(h), preferred_element_type=F32),
                                   functools.partial(_chain_update, v_t=vt_b, m_ref=m_ref, acc_ref=acc_ref,
                                                     ch=h, keep=keep)))
    _pipeline(stages, lookahead=5)
    add_branch(2)

    rank_ref[...] = jnp.zeros_like(rank_ref)
    sub = lax.broadcasted_iota(I32, (SUBLANES, tq), 0)
    last_group = ((i + 1) * (tq // SLC_BLOCK) - 1) // SUBLANES

    def count(g, c, v):
        blk = imp_ref[g, v * SUBLANES:(v + 1) * SUBLANES, :]
        cnt = rank_ref[g, v * SUBLANES:(v + 1) * SUBLANES, :]
        for rr in range(SUBLANES):
            row = imp_ref[g, c * SUBLANES + rr:c * SUBLANES + rr + 1, :]
            if c < v:
                beats = row >= blk
            elif c > v:
                beats = row > blk
            else:
                beats = (row > blk) | ((row == blk) & (sub > rr))
            cnt = cnt + beats.astype(I32)
        rank_ref[g, v * SUBLANES:(v + 1) * SUBLANES, :] = cnt

    for lvl in range(nb // SUBLANES):
        @pl.when(lvl <= last_group)
        def _(lvl=lvl):
            for g in range(ng):
                for v in range(lvl + 1):
                    count(g, lvl, v)
                for c in range(lvl):
                    count(g, c, lvl)

    for g in range(ng):
        bias = jnp.where(rank_ref[g] < n_sel, 0.0, SEL_BIAS).astype(BF16)
        for h in range(nh):
            if group[h] == g:
                qaug_ref[h, dk:dk + nb, :] = bias

    reset()
    _causal_sweep(lambda j, g: kaug_ref[0, g, pl.ds(pl.multiple_of(j * tk, tk), tk), :],
                  lambda j, g: vts_ref[0, j, g * vr:(g + 1) * vr, :],
                  group, group, qaug_ref, sa_ref, sb_ref, m_ref, acc_ref, i, causal)
    add_branch(1)

    for h in range(nh):
        zz = z_ref[0, h * dk:(h + 1) * dk, :].astype(F32)
        o_ref[0, h * dk:(h + 1) * dk, :] = (tot_ref[h] * zz).astype(BF16)


def _nsa(qt, kc, vct, kaug, vts, kwin, vtw, gt, ztn, mt, *, tq, tk, n_sel):
    b, _, s = qt.shape
    nt = s // tk
    ncp = kc.shape[2]
    ng, nh = NSA_KV_GROUPS, NSA_HEADS
    hq = nh * (NSA_HEAD_DIM // 2)
    nb = mt.shape[0]
    vr = NSA_HEAD_DIM + ONES_ROWS
    in_specs = [
        pl.BlockSpec((1, hq, tq), lambda bi, i: (bi, 0, i)),
        pl.BlockSpec((1, hq, tq), lambda bi, i: (bi, 1, i)),
        pl.BlockSpec((1, ng, ncp, LANES), lambda bi, i: (bi, 0, 0, 0)),
        pl.BlockSpec((1, ng, NSA_HEAD_DIM, ncp), lambda bi, i: (bi, 0, 0, 0)),
        pl.BlockSpec((1, ng, s, LANES), lambda bi, i: (bi, 0, 0, 0)),
        pl.BlockSpec((1, nt, ng * vr, tk), lambda bi, i: (bi, 0, 0, 0)),
        pl.BlockSpec((1, ng, s, LANES), lambda bi, i: (bi, 0, 0, 0)),
        pl.BlockSpec((1, nt, ng * vr, tk), lambda bi, i: (bi, 0, 0, 0)),
        pl.BlockSpec((1, ng * GATE_ROWS, tq), lambda bi, i: (bi, 0, i)),
        pl.BlockSpec((1, NSA_WIDTH, tq), lambda bi, i: (bi, 0, i)),
        pl.BlockSpec(mt.shape, lambda bi, i: (0, 0)),
    ]
    kern = functools.partial(_nsa_kernel, tq=tq, tk=tk, n_sel=n_sel)
    return pl.pallas_call(
        kern, grid=(b, s // tq), in_specs=in_specs,
        out_specs=pl.BlockSpec((1, NSA_WIDTH, tq), lambda bi, i: (bi, 0, i)),
        out_shape=jax.ShapeDtypeStruct((b, NSA_WIDTH, s), BF16),
        scratch_shapes=[pltpu.VMEM((nh, NSA_HEAD_DIM + nb, tq), BF16),
                        pltpu.VMEM((nh, tk, tq), F32), pltpu.VMEM((nh, tk, tq), F32),
                        pltpu.VMEM((nh, 1, tq), F32), pltpu.VMEM((nh, vr, tq), F32),
                        pltpu.VMEM((nh, NSA_HEAD_DIM, tq), F32),
                        pltpu.VMEM((ng, nb, tq), F32), pltpu.VMEM((ng, nb, tq), I32)],
        compiler_params=pltpu.CompilerParams(vmem_limit_bytes=VMEM_LIMIT_BYTES),
        name="nsa",
    )(qt, qt, kc, vct, kaug, vts, kwin, vtw, gt, ztn, mt)


def _mla_kernel(q_ref, k_ref, vt_ref, z_ref, o_ref, sa_ref, sb_ref, m_ref, acc_ref, *, tq, tk):
    i = pl.program_id(1)
    dv, vr = MLA_V_DIM, MLA_V_DIM + ONES_ROWS
    m_ref[...] = jnp.full_like(m_ref, NEG_INF)
    acc_ref[...] = jnp.zeros_like(acc_ref)

    causal = lax.broadcasted_iota(I32, (tk, tq), 0) <= lax.broadcasted_iota(I32, (tk, tq), 1)
    _causal_sweep(lambda j, g: k_ref[0, pl.ds(pl.multiple_of(j * tk, tk), tk), :],
                  lambda j, hd: vt_ref[0, j, hd * vr:(hd + 1) * vr, :],
                  [0] * MLA_HEADS, list(range(MLA_HEADS)), q_ref.at[0], sa_ref, sb_ref, m_ref, acc_ref, i, causal)
    for hd in range(MLA_HEADS):
        o_h = acc_ref[hd, 0:dv, :] * (1.0 / acc_ref[hd, dv:dv + 1, :])
        zz = z_ref[0, hd * dv:(hd + 1) * dv, :].astype(F32)
        o_ref[0, hd * dv:(hd + 1) * dv, :] = (o_h * zz).astype(BF16)


def _mla(qtm, kmla, vtm, ztm, *, tq, tk):
    b, s, _ = kmla.shape
    nt = s // tk
    vrows = MLA_HEADS * (MLA_V_DIM + ONES_ROWS)
    kern = functools.partial(_mla_kernel, tq=tq, tk=tk)
    return pl.pallas_call(
        kern, grid=(b, s // tq),
        in_specs=[pl.BlockSpec((1, MLA_HEADS, MLA_QK, tq), lambda bi, i: (bi, 0, 0, i)),
                  pl.BlockSpec((1, s, MLA_QK), lambda bi, i: (bi, 0, 0)),
                  pl.BlockSpec((1, nt, vrows, tk), lambda bi, i: (bi, 0, 0, 0)),
                  pl.BlockSpec((1, MLA_WIDTH, tq), lambda bi, i: (bi, 0, i))],
        out_specs=pl.BlockSpec((1, MLA_WIDTH, tq), lambda bi, i: (bi, 0, i)),
        out_shape=jax.ShapeDtypeStruct((b, MLA_WIDTH, s), BF16),
        scratch_shapes=[pltpu.VMEM((MLA_HEADS, tk, tq), F32), pltpu.VMEM((MLA_HEADS, tk, tq), F32),
                        pltpu.VMEM((MLA_HEADS, 1, tq), F32), pltpu.VMEM((MLA_HEADS, MLA_V_DIM + ONES_ROWS, tq), F32)],
        compiler_params=pltpu.CompilerParams(vmem_limit_bytes=VMEM_LIMIT_BYTES),
        name="mla",
    )(qtm, kmla, vtm, ztm)


def _out_kernel(x_ref, mn_ref, mm_ref, w_ref, mod_ref, fg_ref, o_ref, *, final):
    y = lax.dot_general(mn_ref[0], w_ref[0:NSA_WIDTH, :], TN, preferred_element_type=F32)
    y = y + lax.dot_general(mm_ref[0], w_ref[NSA_WIDTH:MIX_WIDTH, :], TN, preferred_element_type=F32)
    x2 = x_ref[0] + mod_ref[0][2:3] * y
    o_ref[0] = _rms(x2, fg_ref[...]) if final else x2


def _out(x, mn, mm, w_out, mod3, fg, *, tm, final):
    b, s, d = x.shape
    return pl.pallas_call(
        functools.partial(_out_kernel, final=final), grid=(b, s // tm),
        in_specs=[pl.BlockSpec((1, tm, d), lambda bi, i: (bi, i, 0)),
                  pl.BlockSpec((1, NSA_WIDTH, tm), lambda bi, i: (bi, 0, i)),
                  pl.BlockSpec((1, MLA_WIDTH, tm), lambda bi, i: (bi, 0, i)),
                  pl.BlockSpec(w_out.shape, lambda bi, i: (0, 0)),
                  pl.BlockSpec((1, 3, d), lambda bi, i: (bi, 0, 0)),
                  pl.BlockSpec((1, d), lambda bi, i: (0, 0))],
        out_specs=pl.BlockSpec((1, tm, d), lambda bi, i: (bi, i, 0)),
        out_shape=jax.ShapeDtypeStruct((b, s, d), F32),
        compiler_params=pltpu.CompilerParams(vmem_limit_bytes=VMEM_LIMIT_BYTES),
        name="out_proj",
    )(x, mn, mm, w_out, mod3, fg)


def _cmp_to_slc_t(ncp, nc, nslc, nb):
    start = np.arange(nc)[:, None] * CMP_STRIDE
    bstart = np.arange(nslc)[None, :] * SLC_BLOCK
    ov = np.minimum(start + CMP_BLOCK, bstart + SLC_BLOCK) - np.maximum(start, bstart)
    m = (np.clip(ov, 0, None) / CMP_BLOCK).astype(np.float32)
    out = np.zeros((nb, ncp), np.float32)
    out[:nslc, :nc] = m.T
    return out


def _layout_w_in(w):
    d = w.shape[0]
    (q_n, kc_n, vc_n, ks_n, vs_n, kw_n, vw_n, gl_n, z_n, cq_m, ckv_m, kr_m, z_m) = jnp.split(w, IN_OFFSETS, axis=-1)
    dk = NSA_HEAD_DIM
    z64 = jnp.zeros((d, LANES - dk), w.dtype)
    wtok = jnp.concatenate(
        [ks_n[:, :dk], z64, ks_n[:, dk:], z64, kw_n[:, :dk], z64, kw_n[:, dk:], z64,
         kc_n, vc_n, cq_m, ckv_m, jnp.zeros((d, KR_LANE), w.dtype), kr_m,
         jnp.zeros((d, LANES - KR_LANE - MLA_ROPE_DIM), w.dtype)], axis=1)
    qr = q_n.reshape(d, NSA_HEADS, 2, dk // 2)
    q_perm = jnp.concatenate([qr[:, :, 0, :].reshape(d, -1), qr[:, :, 1, :].reshape(d, -1)], axis=1)
    gl = gl_n.reshape(d, NSA_KV_GROUPS, NSA_HPG * N_BRANCH)
    gl = jnp.pad(gl, ((0, 0), (0, 0), (0, GATE_ROWS - NSA_HPG * N_BRANCH))).reshape(d, -1)
    wtr = jnp.concatenate([q_perm, vs_n, vw_n, gl, z_n, z_m], axis=1).T
    assert wtok.shape[1] == TOK_COLS and wtr.shape[0] == TR_ROWS
    return wtok.astype(BF16), wtr.astype(BF16)


def _layout_w1(w1):
    hid = w1.shape[1]
    w1r = w1.reshape(2, CMP_STRIDE, NSA_HEAD_DIM, hid)
    eye = jnp.eye(NSA_KV_GROUPS, dtype=w1.dtype)
    halves = [jnp.einsum('ldh,pg->lpdgh', w1r[k], eye).reshape(CMP_STRIDE * NSA_KV_WIDTH, NSA_KV_GROUPS * hid)
              for k in range(2)]
    return jnp.concatenate(halves, axis=1).astype(BF16)


def kernel(x, c, positions, ada_w, ada_b, norm_g, w_in, cmp_pos, cmp_k_w1, cmp_k_w2, cmp_v_w1, cmp_v_w2,
           q_norm_g, w_q_up, kv_norm_g, w_kv_up, w_out, final_norm_g):
    b, s, d = x.shape
    depth = ada_w.shape[0]
    tm, tq = PROJ_TILE, ATT_TILE
    tk = tq
    assert s % tm == 0 and tm % tk == 0 and WINDOW % tk == 0 and (tq & (tq - 1)) == 0
    assert CMP_BLOCK == 2 * CMP_STRIDE and s % SLC_BLOCK == 0
    nslc = s // SLC_BLOCK
    nb = LANES - NSA_HEAD_DIM
    assert nslc <= nb
    ncp = s // CMP_STRIDE
    nc = ncp - 1

    half_n, half_m = NSA_HEAD_DIM // 2, MLA_ROPE_DIM // 2
    inv_n = ROPE_THETA ** (-jnp.arange(half_n, dtype=F32) / half_n)
    inv_m = ROPE_THETA ** (-jnp.arange(half_m, dtype=F32) / half_m)
    ones_n, ones_m = jnp.ones((half_n,), F32), jnp.ones((half_m,), F32)
    pad = jnp.zeros((LANES - NSA_HEAD_DIM - MLA_ROPE_DIM,), F32)
    inv_l = jnp.concatenate([inv_n, inv_n, inv_m, inv_m, pad])[None]
    sign_l = jnp.concatenate([-ones_n, ones_n, -ones_m, ones_m, pad])[None]
    pos_f = positions.astype(F32)
    pos_b = jnp.broadcast_to(pos_f.reshape(b * s, 1), (b * s, LANES))
    pos_row = pos_f.reshape(b, 1, s)
    cmp_end = np.minimum(np.arange(ncp) * CMP_STRIDE + CMP_BLOCK - 1, s - 1)
    pos_c = jnp.broadcast_to(pos_f[:, cmp_end].reshape(b * ncp, 1), (b * ncp, LANES))
    inv_nb = jnp.broadcast_to(inv_n[:, None], (half_n, tm))
    inv_mb = jnp.broadcast_to(inv_m[:, None], (half_m, tm))

    mt = jnp.asarray(_cmp_to_slc_t(ncp, nc, nslc, nb), dtype=BF16)
    bp = -(-b // SUBLANES) * SUBLANES
    c_pad = jnp.pad(c, ((0, bp - b), (0, 0)))

    for l in range(depth):
        mod = _adaln(c_pad, ada_w[l], ada_b[l].reshape(1, -1))
        mod3 = mod[:b].reshape(b, 3, d)
        wtok, wtr = _layout_w_in(w_in[l])
        wq = w_q_up[l].reshape(MLA_Q_RANK, MLA_HEADS, MLA_NOPE_DIM + MLA_ROPE_DIM)
        wqt = jnp.concatenate([wq[:, :, :MLA_NOPE_DIM].reshape(MLA_Q_RANK, -1),
                               wq[:, :, MLA_NOPE_DIM:MLA_NOPE_DIM + half_m].reshape(MLA_Q_RANK, -1),
                               wq[:, :, MLA_NOPE_DIM + half_m:].reshape(MLA_Q_RANK, -1)], axis=1).T.astype(BF16)
        wkv = w_kv_up[l].reshape(MLA_KV_RANK, MLA_HEADS, MLA_NOPE_DIM + MLA_V_DIM)
        wk = wkv[:, :, :MLA_NOPE_DIM].transpose(1, 0, 2).astype(BF16)
        wv = wkv[:, :, MLA_NOPE_DIM:].transpose(1, 2, 0).reshape(MLA_WIDTH, MLA_KV_RANK).astype(BF16)

        (qt, kaug, kwin, vts, vtw, kcmp, vcmp, gt, ztn, ztm, qtm, kmla, vtm) = _proj(
            x, mod3, norm_g[l].reshape(1, d), wtok, wtr, pos_b, pos_row, inv_l, sign_l, inv_nb, inv_mb,
            q_norm_g[l].reshape(1, -1), kv_norm_g[l].reshape(1, -1), wqt, wk, wv, tm=tm, tk=tk)

        pos_l = cmp_pos[l]
        ptop = jnp.broadcast_to(pos_l[:CMP_STRIDE, None, :], (CMP_STRIDE, NSA_KV_GROUPS, NSA_HEAD_DIM)).reshape(1, -1)
        pbot = jnp.broadcast_to(pos_l[CMP_STRIDE:, None, :], (CMP_STRIDE, NSA_KV_GROUPS, NSA_HEAD_DIM)).reshape(1, -1)
        wk2 = jnp.pad(cmp_k_w2[l], ((0, 0), (0, LANES - NSA_HEAD_DIM))).astype(BF16)
        kc, vct = _compress(kcmp, vcmp, ptop, pbot,
                            _layout_w1(cmp_k_w1[l]), _layout_w1(cmp_v_w1[l]), wk2,
                            cmp_v_w2[l].T.astype(BF16), pos_c, inv_l, sign_l)

        mix_n = _nsa(qt, kc, vct, kaug, vts, kwin, vtw, gt, ztn, mt, tq=tq, tk=tk, n_sel=min(SLC_TOPK, nslc))
        mix_m = _mla(qtm, kmla, vtm, ztm, tq=tq, tk=tk)
        x = _out(x, mix_n, mix_m, w_out[l].astype(BF16), mod3, final_norm_g.reshape(1, d), tm=tm,
                 final=(l == depth - 1))
    return x
```

```python
import functools

import numpy as np
import jax
import jax.numpy as jnp
from jax import lax
from jax.experimental import pallas as pl
from jax.experimental.pallas import tpu as pltpu

F32 = jnp.float32
BF16 = jnp.bfloat16
I32 = jnp.int32

NSA_HEADS = 8
NSA_KV_GROUPS = 2
NSA_HPG = NSA_HEADS // NSA_KV_GROUPS
NSA_HEAD_DIM = 64
NSA_WIDTH = NSA_HEADS * NSA_HEAD_DIM
NSA_KV_WIDTH = NSA_KV_GROUPS * NSA_HEAD_DIM
CMP_BLOCK = 32
CMP_STRIDE = 16
CMP_HIDDEN = 128
SLC_BLOCK = 64
SLC_TOPK = 16
WINDOW = 512
N_BRANCH = 3
FORCED_SCORE = 1.0e4
MLA_HEADS = 8
MLA_NOPE_DIM = 64
MLA_ROPE_DIM = 32
MLA_V_DIM = 64
MLA_WIDTH = MLA_HEADS * MLA_V_DIM
MLA_Q_RANK = 256
MLA_KV_RANK = 128
MIX_WIDTH = NSA_WIDTH + MLA_WIDTH
ROPE_THETA = 10000.0
NORM_EPS = 1e-6
NEG_INF = -1e30
IN_SIZES = (NSA_WIDTH, NSA_KV_WIDTH, NSA_KV_WIDTH, NSA_KV_WIDTH, NSA_KV_WIDTH, NSA_KV_WIDTH, NSA_KV_WIDTH,
            NSA_HEADS * N_BRANCH, NSA_WIDTH, MLA_Q_RANK, MLA_KV_RANK, MLA_ROPE_DIM, MLA_WIDTH)
IN_OFFSETS = tuple(int(o) for o in np.cumsum(IN_SIZES)[:-1])

LANES = 128
SUBLANES = 8
VMEM_LIMIT_BYTES = 56 * 1024 * 1024

PROJ_TILE = 512
ATT_TILE = 256
SEL_BIAS = NEG_INF
GATE_ROWS = 16
ONES_ROWS = 16
MLA_QK = MLA_KV_RANK + MLA_ROPE_DIM
LOG2E = float(np.log2(np.e))
SWEEP_UNROLL = 4
SLC_SHIFT = SLC_BLOCK.bit_length() - 1
CMP_SHIFT = CMP_STRIDE.bit_length() - 1
assert 1 << SLC_SHIFT == SLC_BLOCK and 1 << CMP_SHIFT == CMP_STRIDE

NT = (((1,), (1,)), ((), ()))
TN = (((0,), (0,)), ((), ()))


def _silu(v):
    return v * jax.nn.sigmoid(v)


def _rms(v, g):
    ms = jnp.mean(v * v, axis=-1, keepdims=True)
    return v * lax.rsqrt(ms + NORM_EPS) * g


def _rope_tok(v, c, s_signed, half, lane, base=0):
    up = pltpu.roll(v, LANES - half, axis=1)
    dn = pltpu.roll(v, half, axis=1)
    return v * c + jnp.where(lane < base + half, up, dn) * s_signed


def _rope_lanes(pos_b, inv_ref, sign_ref):
    ang = pos_b * inv_ref[...]
    return jnp.cos(ang), jnp.sin(ang) * sign_ref[...]


def _chain_update(s_t, v_t, m_ref, acc_ref, ch, keep=None):
    if keep is not None:
        s_t = jnp.where(keep, s_t, NEG_INF)
    m_prev = m_ref[ch]
    m_new = jnp.maximum(m_prev, jnp.max(s_t, axis=0, keepdims=True))
    alpha = jnp.exp2(m_prev - m_new)
    p = jnp.exp2(s_t - m_new)
    acc_ref[ch] = alpha * acc_ref[ch] + jnp.dot(v_t, p.astype(BF16), preferred_element_type=F32)
    m_ref[ch] = m_new


def _causal_sweep(k_tile, v_tile, k_group, v_group, q_ref, sa_ref, sb_ref, m_ref, acc_ref, last, keep_last,
                  lookahead=2, q_next_ref=None, first=None):
    n_chains = len(k_group)

    def loader(tile_fn, j):
        cache = {}
        return lambda g: cache.setdefault(g, tile_fn(j, g))

    def phase(j_next, s_next_ref, s_cur_ref, j_cur, keep=None, ahead=False):
        k_next = loader(k_tile, j_next) if j_next is not None else None
        k_zero = loader(k_tile, 0) if ahead else None
        v_cur = loader(v_tile, j_cur)
        for n in range(n_chains + lookahead):
            if k_next is not None and n < n_chains:
                s_next_ref[n] = jnp.dot(k_next(k_group[n]), q_ref[n], preferred_element_type=F32)
            if n >= lookahead:
                ch = n - lookahead
                _chain_update(s_cur_ref[ch], v_cur(v_group[ch]), m_ref, acc_ref, ch, keep=keep)
                if ahead:
                    sa_ref[ch] = jnp.dot(k_zero(k_group[ch]), q_next_ref[ch], preferred_element_type=F32)

    bufs = (sa_ref, sb_ref)

    def prologue():
        k_0 = loader(k_tile, 0)
        for ch in range(n_chains):
            sa_ref[ch] = jnp.dot(k_0(k_group[ch]), q_ref[ch], preferred_element_type=F32)

    if q_next_ref is None:
        prologue()
    else:
        pl.when(first)(prologue)

    def run(j0, count):
        for u in range(count):
            phase(j0 + u + 1, bufs[(u + 1) % 2], bufs[u % 2], j0 + u)

    def body(jj, carry):
        run(SWEEP_UNROLL * jj, SWEEP_UNROLL)
        return carry

    lax.fori_loop(0, last // SWEEP_UNROLL, body, 0)
    rem = last % SWEEP_UNROLL
    for r in range(SWEEP_UNROLL):
        @pl.when(rem == r)
        def _(r=r):
            run(last - r, r)
            phase(None, None, bufs[r % 2], last, keep=keep_last, ahead=q_next_ref is not None)


def _pipeline(stages, lookahead):
    pending = {}
    for n in range(len(stages) + lookahead):
        if n < len(stages) and stages[n][0] is not None:
            pending[n] = stages[n][0]()
        if n >= lookahead:
            stages[n - lookahead][1](pending.pop(n - lookahead, None))


def _adaln_kernel(c_ref, w_ref, b_ref, o_ref):
    sc = _silu(c_ref[...])
    o_ref[...] = jnp.dot(sc.astype(BF16), w_ref[...].astype(BF16), preferred_element_type=F32) + b_ref[...]


def _adaln(c_pad, w, b):
    bp, d = c_pad.shape
    n = w.shape[1] // d
    return pl.pallas_call(
        _adaln_kernel,
        grid=(n,),
        in_specs=[pl.BlockSpec((bp, d), lambda j: (0, 0)),
                  pl.BlockSpec((d, d), lambda j: (0, j)),
                  pl.BlockSpec((1, d), lambda j: (0, j))],
        out_specs=pl.BlockSpec((bp, d), lambda j: (0, j)),
        out_shape=jax.ShapeDtypeStruct((bp, n * d), F32),
        name="adaln",
    )(c_pad, w, b)


TOK_KS, TOK_KW, TOK_KC, TOK_VC, TOK_CQ, TOK_CKV, TOK_KR, TOK_COLS = 0, 256, 512, 640, 768, 1024, 1152, 1280
KR_LANE = NSA_HEAD_DIM
TR_Q, TR_VS, TR_VW, TR_G, TR_ZN, TR_ZM, TR_ROWS = 0, 512, 640, 768, 800, 1312, 1824


def _proj_kernel(x_ref, mod_ref, ng_ref, wtok_ref, wtr_ref, posb_ref, posr_ref, invl_ref, signl_ref,
                 invn_ref, invm_ref, qng_ref, kvng_ref, wqt_ref, wk_ref, wv_ref,
                 qt_ref, kaug_ref, kwin_ref, vts_ref, vtw_ref, kcmp_ref, vcmp_ref, gt_ref,
                 ztn_ref, ztm_ref, qtm_ref, kmla_ref, vtm_ref, *, tm, tk, scale_nsa, scale_mla):
    i = pl.program_id(1)
    mod = mod_ref[0]
    h = _rms(x_ref[0], ng_ref[...]) * (1.0 + mod[1:2]) + mod[0:1]
    hb = h.astype(BF16)
    tok = jnp.dot(hb, wtok_ref[...], preferred_element_type=F32)
    tr = lax.dot_general(wtr_ref[...], hb, NT, preferred_element_type=F32)

    lane = lax.broadcasted_iota(I32, (tm, LANES), 1)
    row = lax.broadcasted_iota(I32, (tm, LANES), 0)
    blk = (i * tm + row) >> SLC_SHIFT
    onehot = (lane - NSA_HEAD_DIM == blk).astype(F32)
    ct, st = _rope_lanes(posb_ref[...], invl_ref, signl_ref)
    half_n = NSA_HEAD_DIM // 2
    for g in range(NSA_KV_GROUPS):
        ks = _rope_tok(tok[:, TOK_KS + LANES * g:TOK_KS + LANES * (g + 1)], ct, st, half_n, lane)
        kaug_ref[0, g] = jnp.where(lane >= NSA_HEAD_DIM, onehot, ks).astype(BF16)
        kw = _rope_tok(tok[:, TOK_KW + LANES * g:TOK_KW + LANES * (g + 1)], ct, st, half_n, lane)
        kwin_ref[0, g] = kw.astype(BF16)
    kcmp_ref[0] = tok[:, TOK_KC:TOK_KC + LANES]
    vcmp_ref[0] = tok[:, TOK_VC:TOK_VC + LANES]

    ckvn = _rms(tok[:, TOK_CKV:TOK_CKV + MLA_KV_RANK], kvng_ref[...])
    krr = _rope_tok(tok[:, TOK_KR:TOK_KR + LANES], ct, st, MLA_ROPE_DIM // 2, lane, base=KR_LANE)
    kmla_ref[0, :, MLA_KV_RANK:MLA_QK] = krr[:, KR_LANE:KR_LANE + MLA_ROPE_DIM].astype(BF16)
    ckvb = ckvn.astype(BF16)
    kmla_ref[0, :, 0:MLA_KV_RANK] = ckvb
    vtm = lax.dot_general(wv_ref[...], ckvb, NT, preferred_element_type=F32).astype(BF16)
    ones = jnp.ones((ONES_ROWS, tk), BF16)
    vr = MLA_V_DIM + ONES_ROWS
    for ii in range(tm // tk):
        for hd in range(MLA_HEADS):
            vtm_ref[0, ii, hd * vr:hd * vr + MLA_V_DIM, :] = vtm[hd * MLA_V_DIM:(hd + 1) * MLA_V_DIM,
                                                                 ii * tk:(ii + 1) * tk]
            vtm_ref[0, ii, hd * vr + MLA_V_DIM:(hd + 1) * vr, :] = ones

    cqn = _rms(tok[:, TOK_CQ:TOK_CQ + MLA_Q_RANK], qng_ref[...]).astype(BF16)
    qm = lax.dot_general(wqt_ref[...], cqn, NT, preferred_element_type=F32)
    nq = MLA_HEADS * MLA_NOPE_DIM
    hr = MLA_ROPE_DIM // 2
    x1 = qm[nq:nq + MLA_HEADS * hr].reshape(MLA_HEADS, hr, tm)
    x2 = qm[nq + MLA_HEADS * hr:nq + 2 * MLA_HEADS * hr].reshape(MLA_HEADS, hr, tm)
    ang_m = invm_ref[...] * posr_ref[0]
    cm_t, sm_t = jnp.cos(ang_m)[None], jnp.sin(ang_m)[None]
    o1 = (x1 * cm_t - x2 * sm_t) * scale_mla
    o2 = (x2 * cm_t + x1 * sm_t) * scale_mla
    for hd in range(MLA_HEADS):
        qn_h = qm[hd * MLA_NOPE_DIM:(hd + 1) * MLA_NOPE_DIM].astype(BF16)
        qabs = jnp.dot(wk_ref[hd], qn_h, preferred_element_type=F32)
        qtm_ref[0, hd, 0:MLA_KV_RANK, :] = (qabs * scale_mla).astype(BF16)
        qtm_ref[0, hd, MLA_KV_RANK:MLA_KV_RANK + hr, :] = o1[hd].astype(BF16)
        qtm_ref[0, hd, MLA_KV_RANK + hr:MLA_QK, :] = o2[hd].astype(BF16)

    hq = NSA_HEADS * half_n
    q1 = tr[TR_Q:TR_Q + hq].reshape(NSA_HEADS, half_n, tm)
    q2 = tr[TR_Q + hq:TR_Q + 2 * hq].reshape(NSA_HEADS, half_n, tm)
    ang_n = invn_ref[...] * posr_ref[0]
    cn_t, sn_t = jnp.cos(ang_n)[None], jnp.sin(ang_n)[None]
    qt_ref[0, 0:hq, :] = ((q1 * cn_t - q2 * sn_t) * scale_nsa).reshape(hq, tm).astype(BF16)
    qt_ref[0, hq:2 * hq, :] = ((q2 * cn_t + q1 * sn_t) * scale_nsa).reshape(hq, tm).astype(BF16)

    vts = tr[TR_VS:TR_VS + NSA_KV_WIDTH].astype(BF16)
    vtw = tr[TR_VW:TR_VW + NSA_KV_WIDTH].astype(BF16)
    dk = NSA_HEAD_DIM
    gr = dk + ONES_ROWS
    for ii in range(tm // tk):
        for g in range(NSA_KV_GROUPS):
            vts_ref[0, ii, g * gr:g * gr + dk, :] = vts[g * dk:(g + 1) * dk, ii * tk:(ii + 1) * tk]
            vtw_ref[0, ii, g * gr:g * gr + dk, :] = vtw[g * dk:(g + 1) * dk, ii * tk:(ii + 1) * tk]
            vts_ref[0, ii, g * gr + dk:(g + 1) * gr, :] = ones
            vtw_ref[0, ii, g * gr + dk:(g + 1) * gr, :] = ones
    gt_ref[0] = jax.nn.sigmoid(tr[TR_G:TR_G + NSA_KV_GROUPS * GATE_ROWS])
    ztn_ref[0] = _silu(tr[TR_ZN:TR_ZN + NSA_WIDTH]).astype(BF16)
    ztm_ref[0] = _silu(tr[TR_ZM:TR_ZM + MLA_WIDTH]).astype(BF16)


def _proj(x, mod3, ng, wtok, wtr, pos_b, pos_row, inv_l, sign_l, inv_nb, inv_mb, qng, kvng, wqt, wk, wv, *, tm, tk):
    b, s, d = x.shape
    nt = s // tk
    v_rows_n = NSA_KV_GROUPS * (NSA_HEAD_DIM + ONES_ROWS)
    v_rows_m = MLA_HEADS * (MLA_V_DIM + ONES_ROWS)
    tile_tok = pl.BlockSpec((tm, LANES), lambda bi, i: (bi * (s // tm) + i, 0))

    def full(a):
        return pl.BlockSpec(a.shape, lambda bi, i, _n=a.ndim: (0,) * _n)

    in_specs = [pl.BlockSpec((1, tm, d), lambda bi, i: (bi, i, 0)),
                pl.BlockSpec((1, 3, d), lambda bi, i: (bi, 0, 0)),
                full(ng), full(wtok), full(wtr),
                tile_tok, pl.BlockSpec((1, 1, tm), lambda bi, i: (bi, 0, i)),
                full(inv_l), full(sign_l), full(inv_nb), full(inv_mb),
                full(qng), full(kvng), full(wqt), full(wk), full(wv)]
    out_shape = [
        jax.ShapeDtypeStruct((b, NSA_WIDTH, s), BF16),
        jax.ShapeDtypeStruct((b, NSA_KV_GROUPS, s, LANES), BF16),
        jax.ShapeDtypeStruct((b, NSA_KV_GROUPS, s, LANES), BF16),
        jax.ShapeDtypeStruct((b, nt, v_rows_n, tk), BF16),
        jax.ShapeDtypeStruct((b, nt, v_rows_n, tk), BF16),
        jax.ShapeDtypeStruct((b, s, NSA_KV_WIDTH), F32),
        jax.ShapeDtypeStruct((b, s, NSA_KV_WIDTH), F32),
        jax.ShapeDtypeStruct((b, NSA_KV_GROUPS * GATE_ROWS, s), F32),
        jax.ShapeDtypeStruct((b, NSA_WIDTH, s), BF16),
        jax.ShapeDtypeStruct((b, MLA_WIDTH, s), BF16),
        jax.ShapeDtypeStruct((b, MLA_HEADS, MLA_QK, s), BF16),
        jax.ShapeDtypeStruct((b, s, MLA_QK), BF16),
        jax.ShapeDtypeStruct((b, nt, v_rows_m, tk), BF16),
    ]
    out_specs = [
        pl.BlockSpec((1, NSA_WIDTH, tm), lambda bi, i: (bi, 0, i)),
        pl.BlockSpec((1, NSA_KV_GROUPS, tm, LANES), lambda bi, i: (bi, 0, i, 0)),
        pl.BlockSpec((1, NSA_KV_GROUPS, tm, LANES), lambda bi, i: (bi, 0, i, 0)),
        pl.BlockSpec((1, tm // tk, v_rows_n, tk), lambda bi, i: (bi, i, 0, 0)),
        pl.BlockSpec((1, tm // tk, v_rows_n, tk), lambda bi, i: (bi, i, 0, 0)),
        pl.BlockSpec((1, tm, NSA_KV_WIDTH), lambda bi, i: (bi, i, 0)),
        pl.BlockSpec((1, tm, NSA_KV_WIDTH), lambda bi, i: (bi, i, 0)),
        pl.BlockSpec((1, NSA_KV_GROUPS * GATE_ROWS, tm), lambda bi, i: (bi, 0, i)),
        pl.BlockSpec((1, NSA_WIDTH, tm), lambda bi, i: (bi, 0, i)),
        pl.BlockSpec((1, MLA_WIDTH, tm), lambda bi, i: (bi, 0, i)),
        pl.BlockSpec((1, MLA_HEADS, MLA_QK, tm), lambda bi, i: (bi, 0, 0, i)),
        pl.BlockSpec((1, tm, MLA_QK), lambda bi, i: (bi, i, 0)),
        pl.BlockSpec((1, tm // tk, v_rows_m, tk), lambda bi, i: (bi, i, 0, 0)),
    ]
    kern = functools.partial(_proj_kernel, tm=tm, tk=tk, scale_nsa=NSA_HEAD_DIM ** -0.5 * LOG2E,
                             scale_mla=(MLA_NOPE_DIM + MLA_ROPE_DIM) ** -0.5 * LOG2E)
    return pl.pallas_call(
        kern, grid=(b, s // tm), in_specs=in_specs, out_specs=out_specs, out_shape=out_shape,
        compiler_params=pltpu.CompilerParams(vmem_limit_bytes=VMEM_LIMIT_BYTES),
        name="proj",
    )(x, mod3, ng, wtok, wtr, pos_b, pos_row, inv_l, sign_l, inv_nb, inv_mb, qng, kvng, wqt, wk, wv)


def _compress_kernel(k_ref, v_ref, ptop_ref, pbot_ref, wk1_ref, wv1_ref, wk2_ref, wv2t_ref, posc_ref, invl_ref,
                     signl_ref, kc_ref, vct_ref):
    ncp = k_ref.shape[1] // CMP_STRIDE
    gw = NSA_KV_GROUPS * CMP_HIDDEN
    lane = lax.broadcasted_iota(I32, (ncp, LANES), 1)

    def hidden(r_ref, w1_ref):
        r = jnp.concatenate([r_ref[0, pl.ds(t, ncp, stride=CMP_STRIDE), :] for t in range(CMP_STRIDE)], axis=1)
        a = jnp.dot((r + ptop_ref[...]).astype(BF16), w1_ref[:, 0:gw], preferred_element_type=F32)
        bt = jnp.dot((r + pbot_ref[...]).astype(BF16), w1_ref[:, gw:2 * gw], preferred_element_type=F32)
        return _silu(a + pltpu.roll(bt, ncp - 1, axis=0))

    cc, sc = _rope_lanes(posc_ref[...], invl_ref, signl_ref)
    hk = hidden(k_ref, wk1_ref)
    hv = hidden(v_ref, wv1_ref)
    for g in range(NSA_KV_GROUPS):
        hkg = hk[:, g * CMP_HIDDEN:(g + 1) * CMP_HIDDEN].astype(BF16)
        kc = jnp.dot(hkg, wk2_ref[...], preferred_element_type=F32)
        kc_ref[0, g] = _rope_tok(kc, cc, sc, NSA_HEAD_DIM // 2, lane).astype(BF16)
        hvg = hv[:, g * CMP_HIDDEN:(g + 1) * CMP_HIDDEN].astype(BF16)
        vct_ref[0, g] = lax.dot_general(wv2t_ref[...], hvg, NT, preferred_element_type=F32).astype(BF16)


def _compress(kcmp, vcmp, ptop, pbot, wk1, wv1, wk2, wv2t, pos_c, inv_l, sign_l):
    b, s, width = kcmp.shape
    ncp = s // CMP_STRIDE

    def full(a):
        return pl.BlockSpec(a.shape, lambda bi, _n=a.ndim: (0,) * _n)

    blk = pl.BlockSpec((1, s, width), lambda bi: (bi, 0, 0))
    tab = pl.BlockSpec((ncp, LANES), lambda bi: (bi, 0))
    return pl.pallas_call(
        _compress_kernel, grid=(b,),
        in_specs=[blk, blk, full(ptop), full(pbot), full(wk1), full(wv1), full(wk2), full(wv2t), tab,
                  full(inv_l), full(sign_l)],
        out_specs=[pl.BlockSpec((1, NSA_KV_GROUPS, ncp, LANES), lambda bi: (bi, 0, 0, 0)),
                   pl.BlockSpec((1, NSA_KV_GROUPS, NSA_HEAD_DIM, ncp), lambda bi: (bi, 0, 0, 0))],
        out_shape=[jax.ShapeDtypeStruct((b, NSA_KV_GROUPS, ncp, LANES), BF16),
                   jax.ShapeDtypeStruct((b, NSA_KV_GROUPS, NSA_HEAD_DIM, ncp), BF16)],
        compiler_params=pltpu.CompilerParams(vmem_limit_bytes=VMEM_LIMIT_BYTES),
        name="compress",
    )(kcmp, vcmp, ptop, pbot, wk1, wv1, wk2, wv2t, pos_c, inv_l, sign_l)


def _nsa_kernel(q1_ref, q2_ref, kc_ref, vct_ref, kaug_ref, vts_ref, kwin_ref, vtw_ref, g_ref, z_ref, mt_ref,
                o_ref, qaug_ref, sa_ref, sb_ref, m_ref, acc_ref, tot_ref, imp_ref, rank_ref, *, tq, tk, n_sel):
    i = pl.program_id(1)
    nh, ng, hpg, dk, half = NSA_HEADS, NSA_KV_GROUPS, NSA_HPG, NSA_HEAD_DIM, NSA_HEAD_DIM // 2
    vr = dk + ONES_ROWS
    group = [h // hpg for h in range(nh)]
    for h in range(nh):
        qaug_ref[h, 0:half, :] = q1_ref[0, h * half:(h + 1) * half, :]
        qaug_ref[h, half:dk, :] = q2_ref[0, h * half:(h + 1) * half, :]

    def gate(h, branch):
        row = group[h] * GATE_ROWS + (h % hpg) * N_BRANCH + branch
        return g_ref[0, row:row + 1, :]

    row_k = lax.broadcasted_iota(I32, (tk, tq), 0)
    col_q = lax.broadcasted_iota(I32, (tk, tq), 1)
    causal = row_k <= col_q

    def reset():
        m_ref[...] = jnp.full_like(m_ref, NEG_INF)
        acc_ref[...] = jnp.zeros_like(acc_ref)

    def add_branch(branch):
        for h in range(nh):
            inv_l = 1.0 / acc_ref[h, dk:dk + 1, :]
            tot_ref[h] = tot_ref[h] + (gate(h, branch) * inv_l) * acc_ref[h, 0:dk, :]

    def q_cols(h):
        return qaug_ref[h, 0:dk, :]

    ncp = kc_ref.shape[2]
    t_row = i * tq + lax.broadcasted_iota(I32, (1, tq), 1)
    last_n = (t_row - (CMP_BLOCK - 1)) >> CMP_SHIFT
    valid = lax.broadcasted_iota(I32, (ncp, tq), 0) <= last_n
    col_ok = last_n >= 0
    p_heads = {}

    def cmp_update(h, s_c):
        s_c = jnp.where(valid, s_c, NEG_INF)
        e = jnp.exp2(s_c - jnp.max(s_c, axis=0, keepdims=True))
        l_c = jnp.sum(e, axis=0, keepdims=True)
        p_c = e * jnp.where(col_ok, 1.0 / l_c, 0.0)
        o_c = jnp.dot(vct_ref[0, group[h]], p_c.astype(BF16), preferred_element_type=F32)
        tot_ref[h] = gate(h, 0) * o_c
        p_heads[h] = p_c

    nb = mt_ref.shape[0]

    def importance(g, _):
        psum = functools.reduce(lambda a, b: a + b, [p_heads[h] for h in range(nh) if group[h] == g])
        hi = psum.astype(BF16)
        lo = (psum - hi.astype(F32)).astype(BF16)
        mt = mt_ref[...]
        imp = jnp.dot(mt, hi, preferred_element_type=F32) + jnp.dot(mt, lo, preferred_element_type=F32)
        j_idx = lax.broadcasted_iota(I32, (nb, tq), 0)
        cur = (i * tq + lax.broadcasted_iota(I32, (nb, tq), 1)) >> SLC_SHIFT
        forced = (j_idx == 0) | (j_idx == cur) | (j_idx == cur - 1)
        imp_ref[g] = jnp.where(forced, FORCED_SCORE, jnp.where(j_idx > cur, -FORCED_SCORE, imp))

    stages = []
    for g in range(ng):
        kc = kc_ref[0, g, :, 0:dk]
        stages += [(functools.partial(jnp.dot, kc, q_cols(h), preferred_element_type=F32),
                    functools.partial(cmp_update, h)) for h in range(nh) if group[h] == g]
        stages.append((None, functools.partial(importance, g)))

    reset()
    n_back = WINDOW // tk
    for back in range(n_back + 1):
        jb = jnp.maximum(i - back, 0)
        if back == 0:
            keep = causal
        elif back == n_back:
            keep = (row_k > col_q) & (i >= back)
        else:
            keep = jnp.broadcast_to(i >= back, (tk, tq))
        for g in range(ng):
            kt_b = kwin_ref[0, g, pl.ds(pl.multiple_of(jb * tk, tk), tk), 0:dk]
            vt_b = vtw_ref[0, jb, g * vr:(g + 1) * vr, :]
            for h in range(nh):
                if group[h] == g:
                    stages.append((functools.partial(jnp.dot, kt_b, q_cols(h), preferred_element_type=F32),
                                   functools.partial(_chain_update, v_t=vt_b, m_ref=m_ref, acc_ref=acc_ref,
                                                     ch=h, keep=keep)))
    _pipeline(stages, lookahead=5)
    add_branch(2)

    rank_ref[...] = jnp.zeros_like(rank_ref)
    sub = lax.broadcasted_iota(I32, (SUBLANES, tq), 0)
    last_group = ((i + 1) * (tq // SLC_BLOCK) - 1) // SUBLANES

    def count(g, c, v):
        blk = imp_ref[g, v * SUBLANES:(v + 1) * SUBLANES, :]
        cnt = rank_ref[g, v * SUBLANES:(v + 1) * SUBLANES, :]
        for rr in range(SUBLANES):
            row = imp_ref[g, c * SUBLANES + rr:c * SUBLANES + rr + 1, :]
            if c < v:
                beats = row >= blk
            elif c > v:
                beats = row > blk
            else:
                beats = (row > blk) | ((row == blk) & (sub > rr))
            cnt = cnt + beats.astype(I32)
        rank_ref[g, v * SUBLANES:(v + 1) * SUBLANES, :] = cnt

    for lvl in range(nb // SUBLANES):
        @pl.when(lvl <= last_group)
        def _(lvl=lvl):
            for g in range(ng):
                for v in range(lvl + 1):
                    count(g, lvl, v)
                for c in range(lvl):
                    count(g, c, lvl)

    for g in range(ng):
        bias = jnp.where(rank_ref[g] < n_sel, 0.0, SEL_BIAS).astype(BF16)
        for h in range(nh):
            if group[h] == g:
                qaug_ref[h, dk:dk + nb, :] = bias

    reset()
    _causal_sweep(lambda j, g: kaug_ref[0, g, pl.ds(pl.multiple_of(j * tk, tk), tk), :],
                  lambda j, g: vts_ref[0, j, g * vr:(g + 1) * vr, :],
                  group, group, qaug_ref, sa_ref, sb_ref, m_ref, acc_ref, i, causal)
    add_branch(1)

    for h in range(nh):
        zz = z_ref[0, h * dk:(h + 1) * dk, :].astype(F32)
        o_ref[0, h * dk:(h + 1) * dk, :] = (tot_ref[h] * zz).astype(BF16)


def _nsa(qt, kc, vct, kaug, vts, kwin, vtw, gt, ztn, mt, *, tq, tk, n_sel):
    b, _, s = qt.shape
    nt = s // tk
    ncp = kc.shape[2]
    ng, nh = NSA_KV_GROUPS, NSA_HEADS
    hq = nh * (NSA_HEAD_DIM // 2)
    nb = mt.shape[0]
    vr = NSA_HEAD_DIM + ONES_ROWS
    in_specs = [
        pl.BlockSpec((1, hq, tq), lambda bi, i: (bi, 0, i)),
        pl.BlockSpec((1, hq, tq), lambda bi, i: (bi, 1, i)),
        pl.BlockSpec((1, ng, ncp, LANES), lambda bi, i: (bi, 0, 0, 0)),
        pl.BlockSpec((1, ng, NSA_HEAD_DIM, ncp), lambda bi, i: (bi, 0, 0, 0)),
        pl.BlockSpec((1, ng, s, LANES), lambda bi, i: (bi, 0, 0, 0)),
        pl.BlockSpec((1, nt, ng * vr, tk), lambda bi, i: (bi, 0, 0, 0)),
        pl.BlockSpec((1, ng, s, LANES), lambda bi, i: (bi, 0, 0, 0)),
        pl.BlockSpec((1, nt, ng * vr, tk), lambda bi, i: (bi, 0, 0, 0)),
        pl.BlockSpec((1, ng * GATE_ROWS, tq), lambda bi, i: (bi, 0, i)),
        pl.BlockSpec((1, NSA_WIDTH, tq), lambda bi, i: (bi, 0, i)),
        pl.BlockSpec(mt.shape, lambda bi, i: (0, 0)),
    ]
    kern = functools.partial(_nsa_kernel, tq=tq, tk=tk, n_sel=n_sel)
    return pl.pallas_call(
        kern, grid=(b, s // tq), in_specs=in_specs,
        out_specs=pl.BlockSpec((1, NSA_WIDTH, tq), lambda bi, i: (bi, 0, i)),
        out_shape=jax.ShapeDtypeStruct((b, NSA_WIDTH, s), BF16),
        scratch_shapes=[pltpu.VMEM((nh, NSA_HEAD_DIM + nb, tq), BF16),
                        pltpu.VMEM((nh, tk, tq), F32), pltpu.VMEM((nh, tk, tq), F32),
                        pltpu.VMEM((nh, 1, tq), F32), pltpu.VMEM((nh, vr, tq), F32),
                        pltpu.VMEM((nh, NSA_HEAD_DIM, tq), F32),
                        pltpu.VMEM((ng, nb, tq), F32), pltpu.VMEM((ng, nb, tq), I32)],
        compiler_params=pltpu.CompilerParams(vmem_limit_bytes=VMEM_LIMIT_BYTES),
        name="nsa",
    )(qt, qt, kc, vct, kaug, vts, kwin, vtw, gt, ztn, mt)


def _mla_kernel(q_ref, qn_ref, k_ref, vt_ref, z_ref, o_ref, sa_ref, sb_ref, m_ref, acc_ref, *, tq, tk):
    i = pl.program_id(1)
    dv, vr = MLA_V_DIM, MLA_V_DIM + ONES_ROWS
    m_ref[...] = jnp.full_like(m_ref, NEG_INF)
    acc_ref[...] = jnp.zeros_like(acc_ref)

    causal = lax.broadcasted_iota(I32, (tk, tq), 0) <= lax.broadcasted_iota(I32, (tk, tq), 1)
    _causal_sweep(lambda j, g: k_ref[0, pl.ds(pl.multiple_of(j * tk, tk), tk), :],
                  lambda j, hd: vt_ref[0, j, hd * vr:(hd + 1) * vr, :],
                  [0] * MLA_HEADS, list(range(MLA_HEADS)), q_ref.at[0], sa_ref, sb_ref, m_ref, acc_ref, i, causal,
                  q_next_ref=qn_ref.at[0], first=i == 0)
    for hd in range(MLA_HEADS):
        o_h = acc_ref[hd, 0:dv, :] * (1.0 / acc_ref[hd, dv:dv + 1, :])
        zz = z_ref[0, hd * dv:(hd + 1) * dv, :].astype(F32)
        o_ref[0, hd * dv:(hd + 1) * dv, :] = (o_h * zz).astype(BF16)


def _mla(qtm, kmla, vtm, ztm, *, tq, tk):
    b, s, _ = kmla.shape
    nt = s // tk
    vrows = MLA_HEADS * (MLA_V_DIM + ONES_ROWS)
    kern = functools.partial(_mla_kernel, tq=tq, tk=tk)
    return pl.pallas_call(
        kern, grid=(b, s // tq),
        in_specs=[pl.BlockSpec((1, MLA_HEADS, MLA_QK, tq), lambda bi, i: (bi, 0, 0, i)),
                  pl.BlockSpec((1, MLA_HEADS, MLA_QK, tq), lambda bi, i: (bi, 0, 0, jnp.minimum(i + 1, s // tq - 1))),
                  pl.BlockSpec((1, s, MLA_QK), lambda bi, i: (bi, 0, 0)),
                  pl.BlockSpec((1, nt, vrows, tk), lambda bi, i: (bi, 0, 0, 0)),
                  pl.BlockSpec((1, MLA_WIDTH, tq), lambda bi, i: (bi, 0, i))],
        out_specs=pl.BlockSpec((1, MLA_WIDTH, tq), lambda bi, i: (bi, 0, i)),
        out_shape=jax.ShapeDtypeStruct((b, MLA_WIDTH, s), BF16),
        scratch_shapes=[pltpu.VMEM((MLA_HEADS, tk, tq), F32), pltpu.VMEM((MLA_HEADS, tk, tq), F32),
                        pltpu.VMEM((MLA_HEADS, 1, tq), F32), pltpu.VMEM((MLA_HEADS, MLA_V_DIM + ONES_ROWS, tq), F32)],
        compiler_params=pltpu.CompilerParams(vmem_limit_bytes=VMEM_LIMIT_BYTES),
        name="mla",
    )(qtm, qtm, kmla, vtm, ztm)


def _out_kernel(x_ref, mn_ref, mm_ref, w_ref, mod_ref, fg_ref, o_ref, *, final):
    y = lax.dot_general(mn_ref[0], w_ref[0:NSA_WIDTH, :], TN, preferred_element_type=F32)
    y = y + lax.dot_general(mm_ref[0], w_ref[NSA_WIDTH:MIX_WIDTH, :], TN, preferred_element_type=F32)
    x2 = x_ref[0] + mod_ref[0][2:3] * y
    o_ref[0] = _rms(x2, fg_ref[...]) if final else x2


def _out(x, mn, mm, w_out, mod3, fg, *, tm, final):
    b, s, d = x.shape
    return pl.pallas_call(
        functools.partial(_out_kernel, final=final), grid=(b, s // tm),
        in_specs=[pl.BlockSpec((1, tm, d), lambda bi, i: (bi, i, 0)),
                  pl.BlockSpec((1, NSA_WIDTH, tm), lambda bi, i: (bi, 0, i)),
                  pl.BlockSpec((1, MLA_WIDTH, tm), lambda bi, i: (bi, 0, i)),
                  pl.BlockSpec(w_out.shape, lambda bi, i: (0, 0)),
                  pl.BlockSpec((1, 3, d), lambda bi, i: (bi, 0, 0)),
                  pl.BlockSpec((1, d), lambda bi, i: (0, 0))],
        out_specs=pl.BlockSpec((1, tm, d), lambda bi, i: (bi, i, 0)),
        out_shape=jax.ShapeDtypeStruct((b, s, d), F32),
        compiler_params=pltpu.CompilerParams(vmem_limit_bytes=VMEM_LIMIT_BYTES),
        name="out_proj",
    )(x, mn, mm, w_out, mod3, fg)


def _cmp_to_slc_t(ncp, nc, nslc, nb):
    start = np.arange(nc)[:, None] * CMP_STRIDE
    bstart = np.arange(nslc)[None, :] * SLC_BLOCK
    ov = np.minimum(start + CMP_BLOCK, bstart + SLC_BLOCK) - np.maximum(start, bstart)
    m = (np.clip(ov, 0, None) / CMP_BLOCK).astype(np.float32)
    out = np.zeros((nb, ncp), np.float32)
    out[:nslc, :nc] = m.T
    return out


def _layout_w_in(w):
    d = w.shape[0]
    (q_n, kc_n, vc_n, ks_n, vs_n, kw_n, vw_n, gl_n, z_n, cq_m, ckv_m, kr_m, z_m) = jnp.split(w, IN_OFFSETS, axis=-1)
    dk = NSA_HEAD_DIM
    z64 = jnp.zeros((d, LANES - dk), w.dtype)
    wtok = jnp.concatenate(
        [ks_n[:, :dk], z64, ks_n[:, dk:], z64, kw_n[:, :dk], z64, kw_n[:, dk:], z64,
         kc_n, vc_n, cq_m, ckv_m, jnp.zeros((d, KR_LANE), w.dtype), kr_m,
         jnp.zeros((d, LANES - KR_LANE - MLA_ROPE_DIM), w.dtype)], axis=1)
    qr = q_n.reshape(d, NSA_HEADS, 2, dk // 2)
    q_perm = jnp.concatenate([qr[:, :, 0, :].reshape(d, -1), qr[:, :, 1, :].reshape(d, -1)], axis=1)
    gl = gl_n.reshape(d, NSA_KV_GROUPS, NSA_HPG * N_BRANCH)
    gl = jnp.pad(gl, ((0, 0), (0, 0), (0, GATE_ROWS - NSA_HPG * N_BRANCH))).reshape(d, -1)
    wtr = jnp.concatenate([q_perm, vs_n, vw_n, gl, z_n, z_m], axis=1).T
    assert wtok.shape[1] == TOK_COLS and wtr.shape[0] == TR_ROWS
    return wtok.astype(BF16), wtr.astype(BF16)


def _layout_w1(w1):
    hid = w1.shape[1]
    w1r = w1.reshape(2, CMP_STRIDE, NSA_HEAD_DIM, hid)
    eye = jnp.eye(NSA_KV_GROUPS, dtype=w1.dtype)
    halves = [jnp.einsum('ldh,pg->lpdgh', w1r[k], eye).reshape(CMP_STRIDE * NSA_KV_WIDTH, NSA_KV_GROUPS * hid)
              for k in range(2)]
    return jnp.concatenate(halves, axis=1).astype(BF16)


def kernel(x, c, positions, ada_w, ada_b, norm_g, w_in, cmp_pos, cmp_k_w1, cmp_k_w2, cmp_v_w1, cmp_v_w2,
           q_norm_g, w_q_up, kv_norm_g, w_kv_up, w_out, final_norm_g):
    b, s, d = x.shape
    depth = ada_w.shape[0]
    tm, tq = PROJ_TILE, ATT_TILE
    tk = tq
    assert s % tm == 0 and tm % tk == 0 and WINDOW % tk == 0 and (tq & (tq - 1)) == 0
    assert CMP_BLOCK == 2 * CMP_STRIDE and s % SLC_BLOCK == 0
    nslc = s // SLC_BLOCK
    nb = LANES - NSA_HEAD_DIM
    assert nslc <= nb
    ncp = s // CMP_STRIDE
    nc = ncp - 1

    half_n, half_m = NSA_HEAD_DIM // 2, MLA_ROPE_DIM // 2
    inv_n = ROPE_THETA ** (-jnp.arange(half_n, dtype=F32) / half_n)
    inv_m = ROPE_THETA ** (-jnp.arange(half_m, dtype=F32) / half_m)
    ones_n, ones_m = jnp.ones((half_n,), F32), jnp.ones((half_m,), F32)
    pad = jnp.zeros((LANES - NSA_HEAD_DIM - MLA_ROPE_DIM,), F32)
    inv_l = jnp.concatenate([inv_n, inv_n, inv_m, inv_m, pad])[None]
    sign_l = jnp.concatenate([-ones_n, ones_n, -ones_m, ones_m, pad])[None]
    pos_f = positions.astype(F32)
    pos_b = jnp.broadcast_to(pos_f.reshape(b * s, 1), (b * s, LANES))
    pos_row = pos_f.reshape(b, 1, s)
    cmp_end = np.minimum(np.arange(ncp) * CMP_STRIDE + CMP_BLOCK - 1, s - 1)
    pos_c = jnp.broadcast_to(pos_f[:, cmp_end].reshape(b * ncp, 1), (b * ncp, LANES))
    inv_nb = jnp.broadcast_to(inv_n[:, None], (half_n, tm))
    inv_mb = jnp.broadcast_to(inv_m[:, None], (half_m, tm))

    mt = jnp.asarray(_cmp_to_slc_t(ncp, nc, nslc, nb), dtype=BF16)
    bp = -(-b // SUBLANES) * SUBLANES
    c_pad = jnp.pad(c, ((0, bp - b), (0, 0)))

    for l in range(depth):
        mod = _adaln(c_pad, ada_w[l], ada_b[l].reshape(1, -1))
        mod3 = mod[:b].reshape(b, 3, d)
        wtok, wtr = _layout_w_in(w_in[l])
        wq = w_q_up[l].reshape(MLA_Q_RANK, MLA_HEADS, MLA_NOPE_DIM + MLA_ROPE_DIM)
        wqt = jnp.concatenate([wq[:, :, :MLA_NOPE_DIM].reshape(MLA_Q_RANK, -1),
                               wq[:, :, MLA_NOPE_DIM:MLA_NOPE_DIM + half_m].reshape(MLA_Q_RANK, -1),
                               wq[:, :, MLA_NOPE_DIM + half_m:].reshape(MLA_Q_RANK, -1)], axis=1).T.astype(BF16)
        wkv = w_kv_up[l].reshape(MLA_KV_RANK, MLA_HEADS, MLA_NOPE_DIM + MLA_V_DIM)
        wk = wkv[:, :, :MLA_NOPE_DIM].transpose(1, 0, 2).astype(BF16)
        wv = wkv[:, :, MLA_NOPE_DIM:].transpose(1, 2, 0).reshape(MLA_WIDTH, MLA_KV_RANK).astype(BF16)

        (qt, kaug, kwin, vts, vtw, kcmp, vcmp, gt, ztn, ztm, qtm, kmla, vtm) = _proj(
            x, mod3, norm_g[l].reshape(1, d), wtok, wtr, pos_b, pos_row, inv_l, sign_l, inv_nb, inv_mb,
            q_norm_g[l].reshape(1, -1), kv_norm_g[l].reshape(1, -1), wqt, wk, wv, tm=tm, tk=tk)

        pos_l = cmp_pos[l]
        ptop = jnp.broadcast_to(pos_l[:CMP_STRIDE, None, :], (CMP_STRIDE, NSA_KV_GROUPS, NSA_HEAD_DIM)).reshape(1, -1)
        pbot = jnp.broadcast_to(pos_l[CMP_STRIDE:, None, :], (CMP_STRIDE, NSA_KV_GROUPS, NSA_HEAD_DIM)).reshape(1, -1)
        wk2 = jnp.pad(cmp_k_w2[l], ((0, 0), (0, LANES - NSA_HEAD_DIM))).astype(BF16)
        kc, vct = _compress(kcmp, vcmp, ptop, pbot,
                            _layout_w1(cmp_k_w1[l]), _layout_w1(cmp_v_w1[l]), wk2,
                            cmp_v_w2[l].T.astype(BF16), pos_c, inv_l, sign_l)

        mix_n = _nsa(qt, kc, vct, kaug, vts, kwin, vtw, gt, ztn, mt, tq=tq, tk=tk, n_sel=min(SLC_TOPK, nslc))
        mix_m = _mla(qtm, kmla, vtm, ztm, tq=tq, tk=tk)
        x = _out(x, mix_n, mix_m, w_out[l].astype(BF16), mod3, final_norm_g.reshape(1, d), tm=tm,
                 final=(l == depth - 1))
    return x
```

```python
import functools

import numpy as np
import jax
import jax.numpy as jnp
from jax import lax
from jax.experimental import pallas as pl
from jax.experimental.pallas import tpu as pltpu

F32 = jnp.float32
BF16 = jnp.bfloat16
I32 = jnp.int32

NSA_HEADS = 8
NSA_KV_GROUPS = 2
NSA_HPG = NSA_HEADS // NSA_KV_GROUPS
NSA_HEAD_DIM = 64
NSA_WIDTH = NSA_HEADS * NSA_HEAD_DIM
NSA_KV_WIDTH = NSA_KV_GROUPS * NSA_HEAD_DIM
CMP_BLOCK = 32
CMP_STRIDE = 16
CMP_HIDDEN = 128
SLC_BLOCK = 64
SLC_TOPK = 16
WINDOW = 512
N_BRANCH = 3
FORCED_SCORE = 1.0e4
MLA_HEADS = 8
MLA_NOPE_DIM = 64
MLA_ROPE_DIM = 32
MLA_V_DIM = 64
MLA_WIDTH = MLA_HEADS * MLA_V_DIM
MLA_Q_RANK = 256
MLA_KV_RANK = 128
MIX_WIDTH = NSA_WIDTH + MLA_WIDTH
ROPE_THETA = 10000.0
NORM_EPS = 1e-6
NEG_INF = -1e30
IN_SIZES = (NSA_WIDTH, NSA_KV_WIDTH, NSA_KV_WIDTH, NSA_KV_WIDTH, NSA_KV_WIDTH, NSA_KV_WIDTH, NSA_KV_WIDTH,
            NSA_HEADS * N_BRANCH, NSA_WIDTH, MLA_Q_RANK, MLA_KV_RANK, MLA_ROPE_DIM, MLA_WIDTH)
IN_OFFSETS = tuple(int(o) for o in np.cumsum(IN_SIZES)[:-1])

LANES = 128
SUBLANES = 8
VMEM_LIMIT_BYTES = 56 * 1024 * 1024

PROJ_TILE = 512
OUT_TILE = 1024
ATT_TILE = 256
SEL_BIAS = NEG_INF
GATE_ROWS = 16
ONES_ROWS = 16
MLA_QK = MLA_KV_RANK + MLA_ROPE_DIM
LOG2E = float(np.log2(np.e))
SWEEP_UNROLL = 4
SLC_SHIFT = SLC_BLOCK.bit_length() - 1
CMP_SHIFT = CMP_STRIDE.bit_length() - 1
assert 1 << SLC_SHIFT == SLC_BLOCK and 1 << CMP_SHIFT == CMP_STRIDE

NT = (((1,), (1,)), ((), ()))
TN = (((0,), (0,)), ((), ()))


def _silu(v):
    return v * jax.nn.sigmoid(v)


def _rms(v, g):
    ms = jnp.mean(v * v, axis=-1, keepdims=True)
    return v * lax.rsqrt(ms + NORM_EPS) * g


def _rope_tok(v, c, s_signed, half, lane, base=0):
    up = pltpu.roll(v, LANES - half, axis=1)
    dn = pltpu.roll(v, half, axis=1)
    return v * c + jnp.where(lane < base + half, up, dn) * s_signed


def _rope_lanes(pos_b, inv_ref, sign_ref):
    ang = pos_b * inv_ref[...]
    return jnp.cos(ang), jnp.sin(ang) * sign_ref[...]


def _chain_update(s_t, v_t, m_ref, acc_ref, ch, keep=None):
    if keep is not None:
        s_t = jnp.where(keep, s_t, NEG_INF)
    m_prev = m_ref[ch]
    m_new = jnp.maximum(m_prev, jnp.max(s_t, axis=0, keepdims=True))
    alpha = jnp.exp2(m_prev - m_new)
    p = jnp.exp2(s_t - m_new)
    acc_ref[ch] = alpha * acc_ref[ch] + jnp.dot(v_t, p.astype(BF16), preferred_element_type=F32)
    m_ref[ch] = m_new


def _causal_sweep(k_tile, v_tile, k_group, v_group, q_ref, sa_ref, sb_ref, m_ref, acc_ref, last, keep_last,
                  lookahead=2, q_next_ref=None, first=None):
    n_chains = len(k_group)

    def loader(tile_fn, j):
        cache = {}
        return lambda g: cache.setdefault(g, tile_fn(j, g))

    def phase(j_next, s_next_ref, s_cur_ref, j_cur, keep=None, ahead=False):
        k_next = loader(k_tile, j_next) if j_next is not None else None
        k_zero = loader(k_tile, 0) if ahead else None
        v_cur = loader(v_tile, j_cur)
        for n in range(n_chains + lookahead):
            if k_next is not None and n < n_chains:
                s_next_ref[n] = jnp.dot(k_next(k_group[n]), q_ref[n], preferred_element_type=F32)
            if n >= lookahead:
                ch = n - lookahead
                _chain_update(s_cur_ref[ch], v_cur(v_group[ch]), m_ref, acc_ref, ch, keep=keep)
                if ahead:
                    sa_ref[ch] = jnp.dot(k_zero(k_group[ch]), q_next_ref[ch], preferred_element_type=F32)

    bufs = (sa_ref, sb_ref)

    def prologue():
        k_0 = loader(k_tile, 0)
        for ch in range(n_chains):
            sa_ref[ch] = jnp.dot(k_0(k_group[ch]), q_ref[ch], preferred_element_type=F32)

    if q_next_ref is None:
        prologue()
    else:
        pl.when(first)(prologue)

    def run(j0, count):
        for u in range(count):
            phase(j0 + u + 1, bufs[(u + 1) % 2], bufs[u % 2], j0 + u)

    def body(jj, carry):
        run(SWEEP_UNROLL * jj, SWEEP_UNROLL)
        return carry

    lax.fori_loop(0, last // SWEEP_UNROLL, body, 0)
    rem = last % SWEEP_UNROLL
    for r in range(SWEEP_UNROLL):
        @pl.when(rem == r)
        def _(r=r):
            run(last - r, r)
            phase(None, None, bufs[r % 2], last, keep=keep_last, ahead=q_next_ref is not None)


def _pipeline(stages, lookahead):
    pending = {}
    for n in range(len(stages) + lookahead):
        if n < len(stages) and stages[n][0] is not None:
            pending[n] = stages[n][0]()
        if n >= lookahead:
            stages[n - lookahead][1](pending.pop(n - lookahead, None))


def _adaln_kernel(c_ref, w_ref, b_ref, o_ref):
    sc = _silu(c_ref[...])
    o_ref[...] = jnp.dot(sc.astype(BF16), w_ref[...].astype(BF16), preferred_element_type=F32) + b_ref[...]


def _adaln(c_pad, w, b):
    bp, d = c_pad.shape
    n = w.shape[1] // d
    return pl.pallas_call(
        _adaln_kernel,
        grid=(n,),
        in_specs=[pl.BlockSpec((bp, d), lambda j: (0, 0)),
                  pl.BlockSpec((d, d), lambda j: (0, j)),
                  pl.BlockSpec((1, d), lambda j: (0, j))],
        out_specs=pl.BlockSpec((bp, d), lambda j: (0, j)),
        out_shape=jax.ShapeDtypeStruct((bp, n * d), F32),
        name="adaln",
    )(c_pad, w, b)


TOK_KS, TOK_KW, TOK_KC, TOK_VC, TOK_CQ, TOK_CKV, TOK_KR, TOK_COLS = 0, 256, 512, 640, 768, 1024, 1152, 1280
KR_LANE = NSA_HEAD_DIM
TR_Q, TR_VS, TR_VW, TR_G, TR_ZN, TR_ZM, TR_ROWS = 0, 512, 640, 768, 800, 1312, 1824


def _proj_kernel(x_ref, mod_ref, ng_ref, wtok_ref, wtr_ref, posb_ref, posr_ref, invl_ref, signl_ref,
                 invn_ref, invm_ref, qng_ref, kvng_ref, wqt_ref, wk_ref, wv_ref,
                 qt_ref, kaug_ref, kwin_ref, vts_ref, vtw_ref, kcmp_ref, vcmp_ref, gt_ref,
                 ztn_ref, ztm_ref, qtm_ref, kmla_ref, vtm_ref, *, tm, tk, scale_nsa, scale_mla):
    i = pl.program_id(1)
    mod = mod_ref[0]
    h = _rms(x_ref[0], ng_ref[...]) * (1.0 + mod[1:2]) + mod[0:1]
    hb = h.astype(BF16)
    tok = jnp.dot(hb, wtok_ref[...], preferred_element_type=F32)
    tr = lax.dot_general(wtr_ref[...], hb, NT, preferred_element_type=F32)

    lane = lax.broadcasted_iota(I32, (tm, LANES), 1)
    row = lax.broadcasted_iota(I32, (tm, LANES), 0)
    blk = (i * tm + row) >> SLC_SHIFT
    onehot = (lane - NSA_HEAD_DIM == blk).astype(F32)
    ct, st = _rope_lanes(posb_ref[...], invl_ref, signl_ref)
    half_n = NSA_HEAD_DIM // 2
    for g in range(NSA_KV_GROUPS):
        ks = _rope_tok(tok[:, TOK_KS + LANES * g:TOK_KS + LANES * (g + 1)], ct, st, half_n, lane)
        kaug_ref[0, g] = jnp.where(lane >= NSA_HEAD_DIM, onehot, ks).astype(BF16)
        kw = _rope_tok(tok[:, TOK_KW + LANES * g:TOK_KW + LANES * (g + 1)], ct, st, half_n, lane)
        kwin_ref[0, g] = kw.astype(BF16)
    kcmp_ref[0] = tok[:, TOK_KC:TOK_KC + LANES]
    vcmp_ref[0] = tok[:, TOK_VC:TOK_VC + LANES]

    ckvn = _rms(tok[:, TOK_CKV:TOK_CKV + MLA_KV_RANK], kvng_ref[...])
    krr = _rope_tok(tok[:, TOK_KR:TOK_KR + LANES], ct, st, MLA_ROPE_DIM // 2, lane, base=KR_LANE)
    kmla_ref[0, :, MLA_KV_RANK:MLA_QK] = krr[:, KR_LANE:KR_LANE + MLA_ROPE_DIM].astype(BF16)
    ckvb = ckvn.astype(BF16)
    kmla_ref[0, :, 0:MLA_KV_RANK] = ckvb
    vtm = lax.dot_general(wv_ref[...], ckvb, NT, preferred_element_type=F32).astype(BF16)
    ones = jnp.ones((ONES_ROWS, tk), BF16)
    vr = MLA_V_DIM + ONES_ROWS
    for ii in range(tm // tk):
        for hd in range(MLA_HEADS):
            vtm_ref[0, ii, hd * vr:hd * vr + MLA_V_DIM, :] = vtm[hd * MLA_V_DIM:(hd + 1) * MLA_V_DIM,
                                                                 ii * tk:(ii + 1) * tk]
            vtm_ref[0, ii, hd * vr + MLA_V_DIM:(hd + 1) * vr, :] = ones

    cqn = _rms(tok[:, TOK_CQ:TOK_CQ + MLA_Q_RANK], qng_ref[...]).astype(BF16)
    qm = lax.dot_general(wqt_ref[...], cqn, NT, preferred_element_type=F32)
    nq = MLA_HEADS * MLA_NOPE_DIM
    hr = MLA_ROPE_DIM // 2
    x1 = qm[nq:nq + MLA_HEADS * hr].reshape(MLA_HEADS, hr, tm)
    x2 = qm[nq + MLA_HEADS * hr:nq + 2 * MLA_HEADS * hr].reshape(MLA_HEADS, hr, tm)
    ang_m = invm_ref[...] * posr_ref[0]
    cm_t, sm_t = jnp.cos(ang_m)[None], jnp.sin(ang_m)[None]
    o1 = (x1 * cm_t - x2 * sm_t) * scale_mla
    o2 = (x2 * cm_t + x1 * sm_t) * scale_mla
    for hd in range(MLA_HEADS):
        qn_h = qm[hd * MLA_NOPE_DIM:(hd + 1) * MLA_NOPE_DIM].astype(BF16)
        qabs = jnp.dot(wk_ref[hd], qn_h, preferred_element_type=F32)
        qtm_ref[0, hd, 0:MLA_KV_RANK, :] = (qabs * scale_mla).astype(BF16)
        qtm_ref[0, hd, MLA_KV_RANK:MLA_KV_RANK + hr, :] = o1[hd].astype(BF16)
        qtm_ref[0, hd, MLA_KV_RANK + hr:MLA_QK, :] = o2[hd].astype(BF16)

    hq = NSA_HEADS * half_n
    q1 = tr[TR_Q:TR_Q + hq].reshape(NSA_HEADS, half_n, tm)
    q2 = tr[TR_Q + hq:TR_Q + 2 * hq].reshape(NSA_HEADS, half_n, tm)
    ang_n = invn_ref[...] * posr_ref[0]
    cn_t, sn_t = jnp.cos(ang_n)[None], jnp.sin(ang_n)[None]
    qt_ref[0, 0:hq, :] = ((q1 * cn_t - q2 * sn_t) * scale_nsa).reshape(hq, tm).astype(BF16)
    qt_ref[0, hq:2 * hq, :] = ((q2 * cn_t + q1 * sn_t) * scale_nsa).reshape(hq, tm).astype(BF16)

    vts = tr[TR_VS:TR_VS + NSA_KV_WIDTH].astype(BF16)
    vtw = tr[TR_VW:TR_VW + NSA_KV_WIDTH].astype(BF16)
    dk = NSA_HEAD_DIM
    gr = dk + ONES_ROWS
    for ii in range(tm // tk):
        for g in range(NSA_KV_GROUPS):
            vts_ref[0, ii, g * gr:g * gr + dk, :] = vts[g * dk:(g + 1) * dk, ii * tk:(ii + 1) * tk]
            vtw_ref[0, ii, g * gr:g * gr + dk, :] = vtw[g * dk:(g + 1) * dk, ii * tk:(ii + 1) * tk]
            vts_ref[0, ii, g * gr + dk:(g + 1) * gr, :] = ones
            vtw_ref[0, ii, g * gr + dk:(g + 1) * gr, :] = ones
    gt_ref[0] = jax.nn.sigmoid(tr[TR_G:TR_G + NSA_KV_GROUPS * GATE_ROWS])
    ztn_ref[0] = _silu(tr[TR_ZN:TR_ZN + NSA_WIDTH]).astype(BF16)
    ztm_ref[0] = _silu(tr[TR_ZM:TR_ZM + MLA_WIDTH]).astype(BF16)


def _proj(x, mod3, ng, wtok, wtr, pos_b, pos_row, inv_l, sign_l, inv_nb, inv_mb, qng, kvng, wqt, wk, wv, *, tm, tk):
    b, s, d = x.shape
    nt = s // tk
    v_rows_n = NSA_KV_GROUPS * (NSA_HEAD_DIM + ONES_ROWS)
    v_rows_m = MLA_HEADS * (MLA_V_DIM + ONES_ROWS)
    tile_tok = pl.BlockSpec((tm, LANES), lambda bi, i: (bi * (s // tm) + i, 0))

    def full(a):
        return pl.BlockSpec(a.shape, lambda bi, i, _n=a.ndim: (0,) * _n)

    in_specs = [pl.BlockSpec((1, tm, d), lambda bi, i: (bi, i, 0)),
                pl.BlockSpec((1, 3, d), lambda bi, i: (bi, 0, 0)),
                full(ng), full(wtok), full(wtr),
                tile_tok, pl.BlockSpec((1, 1, tm), lambda bi, i: (bi, 0, i)),
                full(inv_l), full(sign_l), full(inv_nb), full(inv_mb),
                full(qng), full(kvng), full(wqt), full(wk), full(wv)]
    out_shape = [
        jax.ShapeDtypeStruct((b, NSA_WIDTH, s), BF16),
        jax.ShapeDtypeStruct((b, NSA_KV_GROUPS, s, LANES), BF16),
        jax.ShapeDtypeStruct((b, NSA_KV_GROUPS, s, LANES), BF16),
        jax.ShapeDtypeStruct((b, nt, v_rows_n, tk), BF16),
        jax.ShapeDtypeStruct((b, nt, v_rows_n, tk), BF16),
        jax.ShapeDtypeStruct((b, s, NSA_KV_WIDTH), F32),
        jax.ShapeDtypeStruct((b, s, NSA_KV_WIDTH), F32),
        jax.ShapeDtypeStruct((b, NSA_KV_GROUPS * GATE_ROWS, s), F32),
        jax.ShapeDtypeStruct((b, NSA_WIDTH, s), BF16),
        jax.ShapeDtypeStruct((b, MLA_WIDTH, s), BF16),
        jax.ShapeDtypeStruct((b, MLA_HEADS, MLA_QK, s), BF16),
        jax.ShapeDtypeStruct((b, s, MLA_QK), BF16),
        jax.ShapeDtypeStruct((b, nt, v_rows_m, tk), BF16),
    ]
    out_specs = [
        pl.BlockSpec((1, NSA_WIDTH, tm), lambda bi, i: (bi, 0, i)),
        pl.BlockSpec((1, NSA_KV_GROUPS, tm, LANES), lambda bi, i: (bi, 0, i, 0)),
        pl.BlockSpec((1, NSA_KV_GROUPS, tm, LANES), lambda bi, i: (bi, 0, i, 0)),
        pl.BlockSpec((1, tm // tk, v_rows_n, tk), lambda bi, i: (bi, i, 0, 0)),
        pl.BlockSpec((1, tm // tk, v_rows_n, tk), lambda bi, i: (bi, i, 0, 0)),
        pl.BlockSpec((1, tm, NSA_KV_WIDTH), lambda bi, i: (bi, i, 0)),
        pl.BlockSpec((1, tm, NSA_KV_WIDTH), lambda bi, i: (bi, i, 0)),
        pl.BlockSpec((1, NSA_KV_GROUPS * GATE_ROWS, tm), lambda bi, i: (bi, 0, i)),
        pl.BlockSpec((1, NSA_WIDTH, tm), lambda bi, i: (bi, 0, i)),
        pl.BlockSpec((1, MLA_WIDTH, tm), lambda bi, i: (bi, 0, i)),
        pl.BlockSpec((1, MLA_HEADS, MLA_QK, tm), lambda bi, i: (bi, 0, 0, i)),
        pl.BlockSpec((1, tm, MLA_QK), lambda bi, i: (bi, i, 0)),
        pl.BlockSpec((1, tm // tk, v_rows_m, tk), lambda bi, i: (bi, i, 0, 0)),
    ]
    kern = functools.partial(_proj_kernel, tm=tm, tk=tk, scale_nsa=NSA_HEAD_DIM ** -0.5 * LOG2E,
                             scale_mla=(MLA_NOPE_DIM + MLA_ROPE_DIM) ** -0.5 * LOG2E)
    return pl.pallas_call(
        kern, grid=(b, s // tm), in_specs=in_specs, out_specs=out_specs, out_shape=out_shape,
        compiler_params=pltpu.CompilerParams(vmem_limit_bytes=VMEM_LIMIT_BYTES),
        name="proj",
    )(x, mod3, ng, wtok, wtr, pos_b, pos_row, inv_l, sign_l, inv_nb, inv_mb, qng, kvng, wqt, wk, wv)


def _compress_kernel(k_ref, v_ref, ptop_ref, pbot_ref, wk1_ref, wv1_ref, wk2_ref, wv2t_ref, posc_ref, invl_ref,
                     signl_ref, kc_ref, vct_ref):
    ncp = k_ref.shape[1] // CMP_STRIDE
    gw = NSA_KV_GROUPS * CMP_HIDDEN
    lane = lax.broadcasted_iota(I32, (ncp, LANES), 1)

    def hidden(r_ref, w1_ref):
        r = jnp.concatenate([r_ref[0, pl.ds(t, ncp, stride=CMP_STRIDE), :] for t in range(CMP_STRIDE)], axis=1)
        a = jnp.dot((r + ptop_ref[...]).astype(BF16), w1_ref[:, 0:gw], preferred_element_type=F32)
        bt = jnp.dot((r + pbot_ref[...]).astype(BF16), w1_ref[:, gw:2 * gw], preferred_element_type=F32)
        return _silu(a + pltpu.roll(bt, ncp - 1, axis=0))

    cc, sc = _rope_lanes(posc_ref[...], invl_ref, signl_ref)
    hk = hidden(k_ref, wk1_ref)
    hv = hidden(v_ref, wv1_ref)
    for g in range(NSA_KV_GROUPS):
        hkg = hk[:, g * CMP_HIDDEN:(g + 1) * CMP_HIDDEN].astype(BF16)
        kc = jnp.dot(hkg, wk2_ref[...], preferred_element_type=F32)
        kc_ref[0, g] = _rope_tok(kc, cc, sc, NSA_HEAD_DIM // 2, lane).astype(BF16)
        hvg = hv[:, g * CMP_HIDDEN:(g + 1) * CMP_HIDDEN].astype(BF16)
        vct_ref[0, g] = lax.dot_general(wv2t_ref[...], hvg, NT, preferred_element_type=F32).astype(BF16)


def _compress(kcmp, vcmp, ptop, pbot, wk1, wv1, wk2, wv2t, pos_c, inv_l, sign_l):
    b, s, width = kcmp.shape
    ncp = s // CMP_STRIDE

    def full(a):
        return pl.BlockSpec(a.shape, lambda bi, _n=a.ndim: (0,) * _n)

    blk = pl.BlockSpec((1, s, width), lambda bi: (bi, 0, 0))
    tab = pl.BlockSpec((ncp, LANES), lambda bi: (bi, 0))
    return pl.pallas_call(
        _compress_kernel, grid=(b,),
        in_specs=[blk, blk, full(ptop), full(pbot), full(wk1), full(wv1), full(wk2), full(wv2t), tab,
                  full(inv_l), full(sign_l)],
        out_specs=[pl.BlockSpec((1, NSA_KV_GROUPS, ncp, LANES), lambda bi: (bi, 0, 0, 0)),
                   pl.BlockSpec((1, NSA_KV_GROUPS, NSA_HEAD_DIM, ncp), lambda bi: (bi, 0, 0, 0))],
        out_shape=[jax.ShapeDtypeStruct((b, NSA_KV_GROUPS, ncp, LANES), BF16),
                   jax.ShapeDtypeStruct((b, NSA_KV_GROUPS, NSA_HEAD_DIM, ncp), BF16)],
        compiler_params=pltpu.CompilerParams(vmem_limit_bytes=VMEM_LIMIT_BYTES),
        name="compress",
    )(kcmp, vcmp, ptop, pbot, wk1, wv1, wk2, wv2t, pos_c, inv_l, sign_l)


def _nsa_kernel(q1_ref, q2_ref, kc_ref, vct_ref, kaug_ref, vts_ref, kwin_ref, vtw_ref, g_ref, z_ref, mt_ref,
                o_ref, qaug_ref, sa_ref, sb_ref, m_ref, acc_ref, tot_ref, imp_ref, rank_ref, *, tq, tk, n_sel):
    i = pl.program_id(1)
    nh, ng, hpg, dk, half = NSA_HEADS, NSA_KV_GROUPS, NSA_HPG, NSA_HEAD_DIM, NSA_HEAD_DIM // 2
    vr = dk + ONES_ROWS
    group = [h // hpg for h in range(nh)]
    for h in range(nh):
        qaug_ref[h, 0:half, :] = q1_ref[0, h * half:(h + 1) * half, :]
        qaug_ref[h, half:dk, :] = q2_ref[0, h * half:(h + 1) * half, :]

    def gate(h, branch):
        row = group[h] * GATE_ROWS + (h % hpg) * N_BRANCH + branch
        return g_ref[0, row:row + 1, :]

    row_k = lax.broadcasted_iota(I32, (tk, tq), 0)
    col_q = lax.broadcasted_iota(I32, (tk, tq), 1)
    causal = row_k <= col_q

    def reset():
        m_ref[...] = jnp.full_like(m_ref, NEG_INF)
        acc_ref[...] = jnp.zeros_like(acc_ref)

    def add_branch(branch):
        for h in range(nh):
            inv_l = 1.0 / acc_ref[h, dk:dk + 1, :]
            tot_ref[h] = tot_ref[h] + (gate(h, branch) * inv_l) * acc_ref[h, 0:dk, :]

    def q_cols(h):
        return qaug_ref[h, 0:dk, :]

    ncp = kc_ref.shape[2]
    t_row = i * tq + lax.broadcasted_iota(I32, (1, tq), 1)
    last_n = (t_row - (CMP_BLOCK - 1)) >> CMP_SHIFT
    valid = lax.broadcasted_iota(I32, (ncp, tq), 0) <= last_n
    col_ok = last_n >= 0
    p_heads = {}

    def cmp_update(h, s_c):
        s_c = jnp.where(valid, s_c, NEG_INF)
        e = jnp.exp2(s_c - jnp.max(s_c, axis=0, keepdims=True))
        l_c = jnp.sum(e, axis=0, keepdims=True)
        p_c = e * jnp.where(col_ok, 1.0 / l_c, 0.0)
        o_c = jnp.dot(vct_ref[0, group[h]], p_c.astype(BF16), preferred_element_type=F32)
        tot_ref[h] = gate(h, 0) * o_c
        p_heads[h] = p_c

    nb = mt_ref.shape[0]

    def importance(g, _):
        psum = functools.reduce(lambda a, b: a + b, [p_heads[h] for h in range(nh) if group[h] == g])
        hi = psum.astype(BF16)
        lo = (psum - hi.astype(F32)).astype(BF16)
        mt = mt_ref[...]
        imp = jnp.dot(mt, hi, preferred_element_type=F32) + jnp.dot(mt, lo, preferred_element_type=F32)
        j_idx = lax.broadcasted_iota(I32, (nb, tq), 0)
        cur = (i * tq + lax.broadcasted_iota(I32, (nb, tq), 1)) >> SLC_SHIFT
        forced = (j_idx == 0) | (j_idx == cur) | (j_idx == cur - 1)
        imp_ref[g] = jnp.where(forced, FORCED_SCORE, jnp.where(j_idx > cur, -FORCED_SCORE, imp))

    stages = []
    for g in range(ng):
        kc = kc_ref[0, g, :, 0:dk]
        stages += [(functools.partial(jnp.dot, kc, q_cols(h), preferred_element_type=F32),
                    functools.partial(cmp_update, h)) for h in range(nh) if group[h] == g]
        stages.append((None, functools.partial(importance, g)))

    reset()
    n_back = WINDOW // tk
    for back in range(n_back + 1):
        jb = jnp.maximum(i - back, 0)
        if back == 0:
            keep = causal
        elif back == n_back:
            keep = (row_k > col_q) & (i >= back)
        else:
            keep = jnp.broadcast_to(i >= back, (tk, tq))
        for g in range(ng):
            kt_b = kwin_ref[0, g, pl.ds(pl.multiple_of(jb * tk, tk), tk), 0:dk]
            vt_b = vtw_ref[0, jb, g * vr:(g + 1) * vr, :]
            for h in range(nh):
                if group[h] == g:
                    stages.append((functools.partial(jnp.dot, kt_b, q_cols(h), preferred_element_type=F32),
                                   functools.partial(_chain_update, v_t=vt_b, m_ref=m_ref, acc_ref=acc_ref,
                                                     ch=h, keep=keep)))
    _pipeline(stages, lookahead=5)
    add_branch(2)

    rank_ref[...] = jnp.zeros_like(rank_ref)
    sub = lax.broadcasted_iota(I32, (SUBLANES, tq), 0)
    last_group = ((i + 1) * (tq // SLC_BLOCK) - 1) // SUBLANES

    def count(g, c, v):
        blk = imp_ref[g, v * SUBLANES:(v + 1) * SUBLANES, :]
        cnt = rank_ref[g, v * SUBLANES:(v + 1) * SUBLANES, :]
        for rr in range(SUBLANES):
            row = imp_ref[g, c * SUBLANES + rr:c * SUBLANES + rr + 1, :]
            if c < v:
                beats = row >= blk
            elif c > v:
                beats = row > blk
            else:
                beats = (row > blk) | ((row == blk) & (sub > rr))
            cnt = cnt + beats.astype(I32)
        rank_ref[g, v * SUBLANES:(v + 1) * SUBLANES, :] = cnt

    for lvl in range(nb // SUBLANES):
        @pl.when(lvl <= last_group)
        def _(lvl=lvl):
            for g in range(ng):
                for v in range(lvl + 1):
                    count(g, lvl, v)
                for c in range(lvl):
                    count(g, c, lvl)

    for g in range(ng):
        bias = jnp.where(rank_ref[g] < n_sel, 0.0, SEL_BIAS).astype(BF16)
        for h in range(nh):
            if group[h] == g:
                qaug_ref[h, dk:dk + nb, :] = bias

    reset()
    _causal_sweep(lambda j, g: kaug_ref[0, g, pl.ds(pl.multiple_of(j * tk, tk), tk), :],
                  lambda j, g: vts_ref[0, j, g * vr:(g + 1) * vr, :],
                  group, group, qaug_ref, sa_ref, sb_ref, m_ref, acc_ref, i, causal)
    add_branch(1)

    for h in range(nh):
        zz = z_ref[0, h * dk:(h + 1) * dk, :].astype(F32)
        o_ref[0, h * dk:(h + 1) * dk, :] = (tot_ref[h] * zz).astype(BF16)


def _nsa(qt, kc, vct, kaug, vts, kwin, vtw, gt, ztn, mt, *, tq, tk, n_sel):
    b, _, s = qt.shape
    nt = s // tk
    ncp = kc.shape[2]
    ng, nh = NSA_KV_GROUPS, NSA_HEADS
    hq = nh * (NSA_HEAD_DIM // 2)
    nb = mt.shape[0]
    vr = NSA_HEAD_DIM + ONES_ROWS
    in_specs = [
        pl.BlockSpec((1, hq, tq), lambda bi, i: (bi, 0, i)),
        pl.BlockSpec((1, hq, tq), lambda bi, i: (bi, 1, i)),
        pl.BlockSpec((1, ng, ncp, LANES), lambda bi, i: (bi, 0, 0, 0)),
        pl.BlockSpec((1, ng, NSA_HEAD_DIM, ncp), lambda bi, i: (bi, 0, 0, 0)),
        pl.BlockSpec((1, ng, s, LANES), lambda bi, i: (bi, 0, 0, 0)),
        pl.BlockSpec((1, nt, ng * vr, tk), lambda bi, i: (bi, 0, 0, 0)),
        pl.BlockSpec((1, ng, s, LANES), lambda bi, i: (bi, 0, 0, 0)),
        pl.BlockSpec((1, nt, ng * vr, tk), lambda bi, i: (bi, 0, 0, 0)),
        pl.BlockSpec((1, ng * GATE_ROWS, tq), lambda bi, i: (bi, 0, i)),
        pl.BlockSpec((1, NSA_WIDTH, tq), lambda bi, i: (bi, 0, i)),
        pl.BlockSpec(mt.shape, lambda bi, i: (0, 0)),
    ]
    kern = functools.partial(_nsa_kernel, tq=tq, tk=tk, n_sel=n_sel)
    return pl.pallas_call(
        kern, grid=(b, s // tq), in_specs=in_specs,
        out_specs=pl.BlockSpec((1, NSA_WIDTH, tq), lambda bi, i: (bi, 0, i)),
        out_shape=jax.ShapeDtypeStruct((b, NSA_WIDTH, s), BF16),
        scratch_shapes=[pltpu.VMEM((nh, NSA_HEAD_DIM + nb, tq), BF16),
                        pltpu.VMEM((nh, tk, tq), F32), pltpu.VMEM((nh, tk, tq), F32),
                        pltpu.VMEM((nh, 1, tq), F32), pltpu.VMEM((nh, vr, tq), F32),
                        pltpu.VMEM((nh, NSA_HEAD_DIM, tq), F32),
                        pltpu.VMEM((ng, nb, tq), F32), pltpu.VMEM((ng, nb, tq), I32)],
        compiler_params=pltpu.CompilerParams(dimension_semantics=("arbitrary", "arbitrary"),
                                             vmem_limit_bytes=VMEM_LIMIT_BYTES),
        name="nsa",
    )(qt, qt, kc, vct, kaug, vts, kwin, vtw, gt, ztn, mt)


def _mla_kernel(q_ref, qn_ref, k_ref, vt_ref, z_ref, o_ref, sa_ref, sb_ref, m_ref, acc_ref, *, tq, tk):
    i = pl.program_id(1)
    dv, vr = MLA_V_DIM, MLA_V_DIM + ONES_ROWS
    m_ref[...] = jnp.full_like(m_ref, NEG_INF)
    acc_ref[...] = jnp.zeros_like(acc_ref)

    causal = lax.broadcasted_iota(I32, (tk, tq), 0) <= lax.broadcasted_iota(I32, (tk, tq), 1)
    _causal_sweep(lambda j, g: k_ref[0, pl.ds(pl.multiple_of(j * tk, tk), tk), :],
                  lambda j, hd: vt_ref[0, j, hd * vr:(hd + 1) * vr, :],
                  [0] * MLA_HEADS, list(range(MLA_HEADS)), q_ref.at[0], sa_ref, sb_ref, m_ref, acc_ref, i, causal,
                  q_next_ref=qn_ref.at[0], first=i == 0)
    for hd in range(MLA_HEADS):
        o_h = acc_ref[hd, 0:dv, :] * (1.0 / acc_ref[hd, dv:dv + 1, :])
        zz = z_ref[0, hd * dv:(hd + 1) * dv, :].astype(F32)
        o_ref[0, hd * dv:(hd + 1) * dv, :] = (o_h * zz).astype(BF16)


def _mla(qtm, kmla, vtm, ztm, *, tq, tk):
    b, s, _ = kmla.shape
    nt = s // tk
    vrows = MLA_HEADS * (MLA_V_DIM + ONES_ROWS)
    kern = functools.partial(_mla_kernel, tq=tq, tk=tk)
    return pl.pallas_call(
        kern, grid=(b, s // tq),
        in_specs=[pl.BlockSpec((1, MLA_HEADS, MLA_QK, tq), lambda bi, i: (bi, 0, 0, i)),
                  pl.BlockSpec((1, MLA_HEADS, MLA_QK, tq), lambda bi, i: (bi, 0, 0, jnp.minimum(i + 1, s // tq - 1))),
                  pl.BlockSpec((1, s, MLA_QK), lambda bi, i: (bi, 0, 0)),
                  pl.BlockSpec((1, nt, vrows, tk), lambda bi, i: (bi, 0, 0, 0)),
                  pl.BlockSpec((1, MLA_WIDTH, tq), lambda bi, i: (bi, 0, i))],
        out_specs=pl.BlockSpec((1, MLA_WIDTH, tq), lambda bi, i: (bi, 0, i)),
        out_shape=jax.ShapeDtypeStruct((b, MLA_WIDTH, s), BF16),
        scratch_shapes=[pltpu.VMEM((MLA_HEADS, tk, tq), F32), pltpu.VMEM((MLA_HEADS, tk, tq), F32),
                        pltpu.VMEM((MLA_HEADS, 1, tq), F32), pltpu.VMEM((MLA_HEADS, MLA_V_DIM + ONES_ROWS, tq), F32)],
        compiler_params=pltpu.CompilerParams(dimension_semantics=("arbitrary", "arbitrary"),
                                             vmem_limit_bytes=VMEM_LIMIT_BYTES),
        name="mla",
    )(qtm, qtm, kmla, vtm, ztm)


def _out_kernel(x_ref, mn_ref, mm_ref, w_ref, mod_ref, fg_ref, o_ref, *, final):
    y = lax.dot_general(mn_ref[0], w_ref[0:NSA_WIDTH, :], TN, preferred_element_type=F32)
    y = y + lax.dot_general(mm_ref[0], w_ref[NSA_WIDTH:MIX_WIDTH, :], TN, preferred_element_type=F32)
    x2 = x_ref[0] + mod_ref[0][2:3] * y
    o_ref[0] = _rms(x2, fg_ref[...]) if final else x2


def _out(x, mn, mm, w_out, mod3, fg, *, tm, final):
    b, s, d = x.shape
    return pl.pallas_call(
        functools.partial(_out_kernel, final=final), grid=(b, s // tm),
        in_specs=[pl.BlockSpec((1, tm, d), lambda bi, i: (bi, i, 0)),
                  pl.BlockSpec((1, NSA_WIDTH, tm), lambda bi, i: (bi, 0, i)),
                  pl.BlockSpec((1, MLA_WIDTH, tm), lambda bi, i: (bi, 0, i)),
                  pl.BlockSpec(w_out.shape, lambda bi, i: (0, 0)),
                  pl.BlockSpec((1, 3, d), lambda bi, i: (bi, 0, 0)),
                  pl.BlockSpec((1, d), lambda bi, i: (0, 0))],
        out_specs=pl.BlockSpec((1, tm, d), lambda bi, i: (bi, i, 0)),
        out_shape=jax.ShapeDtypeStruct((b, s, d), F32),
        compiler_params=pltpu.CompilerParams(vmem_limit_bytes=VMEM_LIMIT_BYTES),
        name="out_proj",
    )(x, mn, mm, w_out, mod3, fg)


def _cmp_to_slc_t(ncp, nc, nslc, nb):
    start = np.arange(nc)[:, None] * CMP_STRIDE
    bstart = np.arange(nslc)[None, :] * SLC_BLOCK
    ov = np.minimum(start + CMP_BLOCK, bstart + SLC_BLOCK) - np.maximum(start, bstart)
    m = (np.clip(ov, 0, None) / CMP_BLOCK).astype(np.float32)
    out = np.zeros((nb, ncp), np.float32)
    out[:nslc, :nc] = m.T
    return out


def _layout_w_in(w):
    d = w.shape[0]
    (q_n, kc_n, vc_n, ks_n, vs_n, kw_n, vw_n, gl_n, z_n, cq_m, ckv_m, kr_m, z_m) = jnp.split(w, IN_OFFSETS, axis=-1)
    dk = NSA_HEAD_DIM
    z64 = jnp.zeros((d, LANES - dk), w.dtype)
    wtok = jnp.concatenate(
        [ks_n[:, :dk], z64, ks_n[:, dk:], z64, kw_n[:, :dk], z64, kw_n[:, dk:], z64,
         kc_n, vc_n, cq_m, ckv_m, jnp.zeros((d, KR_LANE), w.dtype), kr_m,
         jnp.zeros((d, LANES - KR_LANE - MLA_ROPE_DIM), w.dtype)], axis=1)
    qr = q_n.reshape(d, NSA_HEADS, 2, dk // 2)
    q_perm = jnp.concatenate([qr[:, :, 0, :].reshape(d, -1), qr[:, :, 1, :].reshape(d, -1)], axis=1)
    gl = gl_n.reshape(d, NSA_KV_GROUPS, NSA_HPG * N_BRANCH)
    gl = jnp.pad(gl, ((0, 0), (0, 0), (0, GATE_ROWS - NSA_HPG * N_BRANCH))).reshape(d, -1)
    wtr = jnp.concatenate([q_perm, vs_n, vw_n, gl, z_n, z_m], axis=1).T
    assert wtok.shape[1] == TOK_COLS and wtr.shape[0] == TR_ROWS
    return wtok.astype(BF16), wtr.astype(BF16)


def _layout_w1(w1):
    hid = w1.shape[1]
    w1r = w1.reshape(2, CMP_STRIDE, NSA_HEAD_DIM, hid)
    eye = jnp.eye(NSA_KV_GROUPS, dtype=w1.dtype)
    halves = [jnp.einsum('ldh,pg->lpdgh', w1r[k], eye).reshape(CMP_STRIDE * NSA_KV_WIDTH, NSA_KV_GROUPS * hid)
              for k in range(2)]
    return jnp.concatenate(halves, axis=1).astype(BF16)


def kernel(x, c, positions, ada_w, ada_b, norm_g, w_in, cmp_pos, cmp_k_w1, cmp_k_w2, cmp_v_w1, cmp_v_w2,
           q_norm_g, w_q_up, kv_norm_g, w_kv_up, w_out, final_norm_g):
    b, s, d = x.shape
    depth = ada_w.shape[0]
    tm, tq = PROJ_TILE, ATT_TILE
    tk = tq
    assert s % tm == 0 and tm % tk == 0 and WINDOW % tk == 0 and (tq & (tq - 1)) == 0 and s % OUT_TILE == 0
    assert CMP_BLOCK == 2 * CMP_STRIDE and s % SLC_BLOCK == 0
    nslc = s // SLC_BLOCK
    nb = LANES - NSA_HEAD_DIM
    assert nslc <= nb
    ncp = s // CMP_STRIDE
    nc = ncp - 1

    half_n, half_m = NSA_HEAD_DIM // 2, MLA_ROPE_DIM // 2
    inv_n = ROPE_THETA ** (-jnp.arange(half_n, dtype=F32) / half_n)
    inv_m = ROPE_THETA ** (-jnp.arange(half_m, dtype=F32) / half_m)
    ones_n, ones_m = jnp.ones((half_n,), F32), jnp.ones((half_m,), F32)
    pad = jnp.zeros((LANES - NSA_HEAD_DIM - MLA_ROPE_DIM,), F32)
    inv_l = jnp.concatenate([inv_n, inv_n, inv_m, inv_m, pad])[None]
    sign_l = jnp.concatenate([-ones_n, ones_n, -ones_m, ones_m, pad])[None]
    pos_f = positions.astype(F32)
    pos_b = jnp.broadcast_to(pos_f.reshape(b * s, 1), (b * s, LANES))
    pos_row = pos_f.reshape(b, 1, s)
    cmp_end = np.minimum(np.arange(ncp) * CMP_STRIDE + CMP_BLOCK - 1, s - 1)
    pos_c = jnp.broadcast_to(pos_f[:, cmp_end].reshape(b * ncp, 1), (b * ncp, LANES))
    inv_nb = jnp.broadcast_to(inv_n[:, None], (half_n, tm))
    inv_mb = jnp.broadcast_to(inv_m[:, None], (half_m, tm))

    mt = jnp.asarray(_cmp_to_slc_t(ncp, nc, nslc, nb), dtype=BF16)
    bp = -(-b // SUBLANES) * SUBLANES
    c_pad = jnp.pad(c, ((0, bp - b), (0, 0)))

    for l in range(depth):
        mod = _adaln(c_pad, ada_w[l], ada_b[l].reshape(1, -1))
        mod3 = mod[:b].reshape(b, 3, d)
        wtok, wtr = _layout_w_in(w_in[l])
        wq = w_q_up[l].reshape(MLA_Q_RANK, MLA_HEADS, MLA_NOPE_DIM + MLA_ROPE_DIM)
        wqt = jnp.concatenate([wq[:, :, :MLA_NOPE_DIM].reshape(MLA_Q_RANK, -1),
                               wq[:, :, MLA_NOPE_DIM:MLA_NOPE_DIM + half_m].reshape(MLA_Q_RANK, -1),
                               wq[:, :, MLA_NOPE_DIM + half_m:].reshape(MLA_Q_RANK, -1)], axis=1).T.astype(BF16)
        wkv = w_kv_up[l].reshape(MLA_KV_RANK, MLA_HEADS, MLA_NOPE_DIM + MLA_V_DIM)
        wk = wkv[:, :, :MLA_NOPE_DIM].transpose(1, 0, 2).astype(BF16)
        wv = wkv[:, :, MLA_NOPE_DIM:].transpose(1, 2, 0).reshape(MLA_WIDTH, MLA_KV_RANK).astype(BF16)

        (qt, kaug, kwin, vts, vtw, kcmp, vcmp, gt, ztn, ztm, qtm, kmla, vtm) = _proj(
            x, mod3, norm_g[l].reshape(1, d), wtok, wtr, pos_b, pos_row, inv_l, sign_l, inv_nb, inv_mb,
            q_norm_g[l].reshape(1, -1), kv_norm_g[l].reshape(1, -1), wqt, wk, wv, tm=tm, tk=tk)

        pos_l = cmp_pos[l]
        ptop = jnp.broadcast_to(pos_l[:CMP_STRIDE, None, :], (CMP_STRIDE, NSA_KV_GROUPS, NSA_HEAD_DIM)).reshape(1, -1)
        pbot = jnp.broadcast_to(pos_l[CMP_STRIDE:, None, :], (CMP_STRIDE, NSA_KV_GROUPS, NSA_HEAD_DIM)).reshape(1, -1)
        wk2 = jnp.pad(cmp_k_w2[l], ((0, 0), (0, LANES - NSA_HEAD_DIM))).astype(BF16)
        kc, vct = _compress(kcmp, vcmp, ptop, pbot,
                            _layout_w1(cmp_k_w1[l]), _layout_w1(cmp_v_w1[l]), wk2,
                            cmp_v_w2[l].T.astype(BF16), pos_c, inv_l, sign_l)

        mix_n = _nsa(qt, kc, vct, kaug, vts, kwin, vtw, gt, ztn, mt, tq=tq, tk=tk, n_sel=min(SLC_TOPK, nslc))
        mix_m = _mla(qtm, kmla, vtm, ztm, tq=tq, tk=tk)
        x = _out(x, mix_n, mix_m, w_out[l].astype(BF16), mod3, final_norm_g.reshape(1, d), tm=OUT_TILE,
                 final=(l == depth - 1))
    return x
```

```python
import functools

import numpy as np
import jax
import jax.numpy as jnp
from jax import lax
from jax.experimental import pallas as pl
from jax.experimental.pallas import tpu as pltpu

F32 = jnp.float32
BF16 = jnp.bfloat16
I32 = jnp.int32

NSA_HEADS = 8
NSA_KV_GROUPS = 2
NSA_HPG = NSA_HEADS // NSA_KV_GROUPS
NSA_HEAD_DIM = 64
NSA_WIDTH = NSA_HEADS * NSA_HEAD_DIM
NSA_KV_WIDTH = NSA_KV_GROUPS * NSA_HEAD_DIM
CMP_BLOCK = 32
CMP_STRIDE = 16
CMP_HIDDEN = 128
SLC_BLOCK = 64
SLC_TOPK = 16
WINDOW = 512
N_BRANCH = 3
FORCED_SCORE = 1.0e4
MLA_HEADS = 8
MLA_NOPE_DIM = 64
MLA_ROPE_DIM = 32
MLA_V_DIM = 64
MLA_WIDTH = MLA_HEADS * MLA_V_DIM
MLA_Q_RANK = 256
MLA_KV_RANK = 128
MIX_WIDTH = NSA_WIDTH + MLA_WIDTH
ROPE_THETA = 10000.0
NORM_EPS = 1e-6
NEG_INF = -1e30
IN_SIZES = (NSA_WIDTH, NSA_KV_WIDTH, NSA_KV_WIDTH, NSA_KV_WIDTH, NSA_KV_WIDTH, NSA_KV_WIDTH, NSA_KV_WIDTH,
            NSA_HEADS * N_BRANCH, NSA_WIDTH, MLA_Q_RANK, MLA_KV_RANK, MLA_ROPE_DIM, MLA_WIDTH)
IN_OFFSETS = tuple(int(o) for o in np.cumsum(IN_SIZES)[:-1])

LANES = 128
SUBLANES = 8
VMEM_LIMIT_BYTES = 56 * 1024 * 1024

PROJ_TILE = 1024
OUT_TILE = 1024
ATT_TILE = 256
SEL_BIAS = NEG_INF
GATE_ROWS = 16
ONES_ROWS = 16
MLA_QK = MLA_KV_RANK + MLA_ROPE_DIM
LOG2E = float(np.log2(np.e))
SWEEP_UNROLL = 4
SLC_SHIFT = SLC_BLOCK.bit_length() - 1
CMP_SHIFT = CMP_STRIDE.bit_length() - 1
assert 1 << SLC_SHIFT == SLC_BLOCK and 1 << CMP_SHIFT == CMP_STRIDE

NT = (((1,), (1,)), ((), ()))
TN = (((0,), (0,)), ((), ()))


def _silu(v):
    return v * jax.nn.sigmoid(v)


def _rms(v, g):
    ms = jnp.mean(v * v, axis=-1, keepdims=True)
    return v * lax.rsqrt(ms + NORM_EPS) * g


def _rope_tok(v, c, s_signed, half, lane, base=0):
    up = pltpu.roll(v, LANES - half, axis=1)
    dn = pltpu.roll(v, half, axis=1)
    return v * c + jnp.where(lane < base + half, up, dn) * s_signed


def _rope_lanes(pos_b, inv_ref, sign_ref):
    ang = pos_b * inv_ref[...]
    return jnp.cos(ang), jnp.sin(ang) * sign_ref[...]


def _chain_update(s_t, v_t, m_ref, acc_ref, ch, keep=None):
    if keep is not None:
        s_t = jnp.where(keep, s_t, NEG_INF)
    m_prev = m_ref[ch]
    m_new = jnp.maximum(m_prev, jnp.max(s_t, axis=0, keepdims=True))
    alpha = jnp.exp2(m_prev - m_new)
    p = jnp.exp2(s_t - m_new)
    acc_ref[ch] = alpha * acc_ref[ch] + jnp.dot(v_t, p.astype(BF16), preferred_element_type=F32)
    m_ref[ch] = m_new


def _causal_sweep(k_tile, v_tile, k_group, v_group, q_ref, sa_ref, sb_ref, m_ref, acc_ref, last, keep_last,
                  lookahead=2, q_next_ref=None, first=None):
    n_chains = len(k_group)

    def loader(tile_fn, j):
        cache = {}
        return lambda g: cache.setdefault(g, tile_fn(j, g))

    def phase(j_next, s_next_ref, s_cur_ref, j_cur, keep=None, ahead=False):
        k_next = loader(k_tile, j_next) if j_next is not None else None
        k_zero = loader(k_tile, 0) if ahead else None
        v_cur = loader(v_tile, j_cur)
        for n in range(n_chains + lookahead):
            if k_next is not None and n < n_chains:
                s_next_ref[n] = jnp.dot(k_next(k_group[n]), q_ref[n], preferred_element_type=F32)
            if n >= lookahead:
                ch = n - lookahead
                _chain_update(s_cur_ref[ch], v_cur(v_group[ch]), m_ref, acc_ref, ch, keep=keep)
                if ahead:
                    sa_ref[ch] = jnp.dot(k_zero(k_group[ch]), q_next_ref[ch], preferred_element_type=F32)

    bufs = (sa_ref, sb_ref)

    def prologue():
        k_0 = loader(k_tile, 0)
        for ch in range(n_chains):
            sa_ref[ch] = jnp.dot(k_0(k_group[ch]), q_ref[ch], preferred_element_type=F32)

    if q_next_ref is None:
        prologue()
    else:
        pl.when(first)(prologue)

    def run(j0, count):
        for u in range(count):
            phase(j0 + u + 1, bufs[(u + 1) % 2], bufs[u % 2], j0 + u)

    def body(jj, carry):
        run(SWEEP_UNROLL * jj, SWEEP_UNROLL)
        return carry

    lax.fori_loop(0, last // SWEEP_UNROLL, body, 0)
    rem = last % SWEEP_UNROLL
    for r in range(SWEEP_UNROLL):
        @pl.when(rem == r)
        def _(r=r):
            run(last - r, r)
            phase(None, None, bufs[r % 2], last, keep=keep_last, ahead=q_next_ref is not None)


def _pipeline(stages, lookahead):
    pending = {}
    for n in range(len(stages) + lookahead):
        if n < len(stages) and stages[n][0] is not None:
            pending[n] = stages[n][0]()
        if n >= lookahead:
            stages[n - lookahead][1](pending.pop(n - lookahead, None))


def _adaln_kernel(c_ref, w_ref, b_ref, o_ref):
    sc = _silu(c_ref[...])
    o_ref[...] = jnp.dot(sc.astype(BF16), w_ref[...].astype(BF16), preferred_element_type=F32) + b_ref[...]


def _adaln(c_pad, w, b):
    bp, d = c_pad.shape
    n = w.shape[1] // d
    return pl.pallas_call(
        _adaln_kernel,
        grid=(n,),
        in_specs=[pl.BlockSpec((bp, d), lambda j: (0, 0)),
                  pl.BlockSpec((d, d), lambda j: (0, j)),
                  pl.BlockSpec((1, d), lambda j: (0, j))],
        out_specs=pl.BlockSpec((bp, d), lambda j: (0, j)),
        out_shape=jax.ShapeDtypeStruct((bp, n * d), F32),
        name="adaln",
    )(c_pad, w, b)


TOK_KS, TOK_KW, TOK_KC, TOK_VC, TOK_CQ, TOK_CKV, TOK_KR, TOK_COLS = 0, 256, 512, 640, 768, 1024, 1152, 1280
KR_LANE = NSA_HEAD_DIM
TR_Q, TR_VS, TR_VW, TR_G, TR_ZN, TR_ZM, TR_ROWS = 0, 512, 640, 768, 800, 1312, 1824


def _proj_kernel(x_ref, mod_ref, ng_ref, wtok_ref, wtr_ref, posb_ref, posr_ref, invl_ref, signl_ref,
                 invn_ref, invm_ref, qng_ref, kvng_ref, wqt_ref, wk_ref, wv_ref,
                 qt_ref, kaug_ref, kwin_ref, vts_ref, vtw_ref, kcmp_ref, vcmp_ref, gt_ref,
                 ztn_ref, ztm_ref, qtm_ref, kmla_ref, vtm_ref, *, tm, tk, scale_nsa, scale_mla):
    i = pl.program_id(1)
    mod = mod_ref[0]
    h = _rms(x_ref[0], ng_ref[...]) * (1.0 + mod[1:2]) + mod[0:1]
    hb = h.astype(BF16)
    tok = jnp.dot(hb, wtok_ref[...], preferred_element_type=F32)
    tr = lax.dot_general(wtr_ref[...], hb, NT, preferred_element_type=F32)

    lane = lax.broadcasted_iota(I32, (tm, LANES), 1)
    row = lax.broadcasted_iota(I32, (tm, LANES), 0)
    blk = (i * tm + row) >> SLC_SHIFT
    onehot = (lane - NSA_HEAD_DIM == blk).astype(F32)
    ct, st = _rope_lanes(posb_ref[...], invl_ref, signl_ref)
    half_n = NSA_HEAD_DIM // 2
    for g in range(NSA_KV_GROUPS):
        ks = _rope_tok(tok[:, TOK_KS + LANES * g:TOK_KS + LANES * (g + 1)], ct, st, half_n, lane)
        kaug_ref[0, g] = jnp.where(lane >= NSA_HEAD_DIM, onehot, ks).astype(BF16)
        kw = _rope_tok(tok[:, TOK_KW + LANES * g:TOK_KW + LANES * (g + 1)], ct, st, half_n, lane)
        kwin_ref[0, g] = kw.astype(BF16)
    kcmp_ref[0] = tok[:, TOK_KC:TOK_KC + LANES]
    vcmp_ref[0] = tok[:, TOK_VC:TOK_VC + LANES]

    ckvn = _rms(tok[:, TOK_CKV:TOK_CKV + MLA_KV_RANK], kvng_ref[...])
    krr = _rope_tok(tok[:, TOK_KR:TOK_KR + LANES], ct, st, MLA_ROPE_DIM // 2, lane, base=KR_LANE)
    kmla_ref[0, :, MLA_KV_RANK:MLA_QK] = krr[:, KR_LANE:KR_LANE + MLA_ROPE_DIM].astype(BF16)
    ckvb = ckvn.astype(BF16)
    kmla_ref[0, :, 0:MLA_KV_RANK] = ckvb
    vtm = lax.dot_general(wv_ref[...], ckvb, NT, preferred_element_type=F32).astype(BF16)
    ones = jnp.ones((ONES_ROWS, tk), BF16)
    vr = MLA_V_DIM + ONES_ROWS
    for ii in range(tm // tk):
        for hd in range(MLA_HEADS):
            vtm_ref[0, ii, hd * vr:hd * vr + MLA_V_DIM, :] = vtm[hd * MLA_V_DIM:(hd + 1) * MLA_V_DIM,
                                                                 ii * tk:(ii + 1) * tk]
            vtm_ref[0, ii, hd * vr + MLA_V_DIM:(hd + 1) * vr, :] = ones

    cqn = _rms(tok[:, TOK_CQ:TOK_CQ + MLA_Q_RANK], qng_ref[...]).astype(BF16)
    qm = lax.dot_general(wqt_ref[...], cqn, NT, preferred_element_type=F32)
    nq = MLA_HEADS * MLA_NOPE_DIM
    hr = MLA_ROPE_DIM // 2
    x1 = qm[nq:nq + MLA_HEADS * hr].reshape(MLA_HEADS, hr, tm)
    x2 = qm[nq + MLA_HEADS * hr:nq + 2 * MLA_HEADS * hr].reshape(MLA_HEADS, hr, tm)
    ang_m = invm_ref[...] * posr_ref[0]
    cm_t, sm_t = jnp.cos(ang_m)[None], jnp.sin(ang_m)[None]
    o1 = (x1 * cm_t - x2 * sm_t) * scale_mla
    o2 = (x2 * cm_t + x1 * sm_t) * scale_mla
    for hd in range(MLA_HEADS):
        qn_h = qm[hd * MLA_NOPE_DIM:(hd + 1) * MLA_NOPE_DIM].astype(BF16)
        qabs = jnp.dot(wk_ref[hd], qn_h, preferred_element_type=F32)
        qtm_ref[0, hd, 0:MLA_KV_RANK, :] = (qabs * scale_mla).astype(BF16)
        qtm_ref[0, hd, MLA_KV_RANK:MLA_KV_RANK + hr, :] = o1[hd].astype(BF16)
        qtm_ref[0, hd, MLA_KV_RANK + hr:MLA_QK, :] = o2[hd].astype(BF16)

    hq = NSA_HEADS * half_n
    q1 = tr[TR_Q:TR_Q + hq].reshape(NSA_HEADS, half_n, tm)
    q2 = tr[TR_Q + hq:TR_Q + 2 * hq].reshape(NSA_HEADS, half_n, tm)
    ang_n = invn_ref[...] * posr_ref[0]
    cn_t, sn_t = jnp.cos(ang_n)[None], jnp.sin(ang_n)[None]
    qt_ref[0, 0:hq, :] = ((q1 * cn_t - q2 * sn_t) * scale_nsa).reshape(hq, tm).astype(BF16)
    qt_ref[0, hq:2 * hq, :] = ((q2 * cn_t + q1 * sn_t) * scale_nsa).reshape(hq, tm).astype(BF16)

    vts = tr[TR_VS:TR_VS + NSA_KV_WIDTH].astype(BF16)
    vtw = tr[TR_VW:TR_VW + NSA_KV_WIDTH].astype(BF16)
    dk = NSA_HEAD_DIM
    gr = dk + ONES_ROWS
    for ii in range(tm // tk):
        for g in range(NSA_KV_GROUPS):
            vts_ref[0, ii, g * gr:g * gr + dk, :] = vts[g * dk:(g + 1) * dk, ii * tk:(ii + 1) * tk]
            vtw_ref[0, ii, g * gr:g * gr + dk, :] = vtw[g * dk:(g + 1) * dk, ii * tk:(ii + 1) * tk]
            vts_ref[0, ii, g * gr + dk:(g + 1) * gr, :] = ones
            vtw_ref[0, ii, g * gr + dk:(g + 1) * gr, :] = ones
    gt_ref[0] = jax.nn.sigmoid(tr[TR_G:TR_G + NSA_KV_GROUPS * GATE_ROWS])
    ztn_ref[0] = _silu(tr[TR_ZN:TR_ZN + NSA_WIDTH]).astype(BF16)
    ztm_ref[0] = _silu(tr[TR_ZM:TR_ZM + MLA_WIDTH]).astype(BF16)


def _proj(x, mod3, ng, wtok, wtr, pos_b, pos_row, inv_l, sign_l, inv_nb, inv_mb, qng, kvng, wqt, wk, wv, *, tm, tk):
    b, s, d = x.shape
    nt = s // tk
    v_rows_n = NSA_KV_GROUPS * (NSA_HEAD_DIM + ONES_ROWS)
    v_rows_m = MLA_HEADS * (MLA_V_DIM + ONES_ROWS)
    tile_tok = pl.BlockSpec((tm, LANES), lambda bi, i: (bi * (s // tm) + i, 0))

    def full(a):
        return pl.BlockSpec(a.shape, lambda bi, i, _n=a.ndim: (0,) * _n)

    in_specs = [pl.BlockSpec((1, tm, d), lambda bi, i: (bi, i, 0)),
                pl.BlockSpec((1, 3, d), lambda bi, i: (bi, 0, 0)),
                full(ng), full(wtok), full(wtr),
                tile_tok, pl.BlockSpec((1, 1, tm), lambda bi, i: (bi, 0, i)),
                full(inv_l), full(sign_l), full(inv_nb), full(inv_mb),
                full(qng), full(kvng), full(wqt), full(wk), full(wv)]
    out_shape = [
        jax.ShapeDtypeStruct((b, NSA_WIDTH, s), BF16),
        jax.ShapeDtypeStruct((b, NSA_KV_GROUPS, s, LANES), BF16),
        jax.ShapeDtypeStruct((b, NSA_KV_GROUPS, s, LANES), BF16),
        jax.ShapeDtypeStruct((b, nt, v_rows_n, tk), BF16),
        jax.ShapeDtypeStruct((b, nt, v_rows_n, tk), BF16),
        jax.ShapeDtypeStruct((b, s, NSA_KV_WIDTH), F32),
        jax.ShapeDtypeStruct((b, s, NSA_KV_WIDTH), F32),
        jax.ShapeDtypeStruct((b, NSA_KV_GROUPS * GATE_ROWS, s), F32),
        jax.ShapeDtypeStruct((b, NSA_WIDTH, s), BF16),
        jax.ShapeDtypeStruct((b, MLA_WIDTH, s), BF16),
        jax.ShapeDtypeStruct((b, MLA_HEADS, MLA_QK, s), BF16),
        jax.ShapeDtypeStruct((b, s, MLA_QK), BF16),
        jax.ShapeDtypeStruct((b, nt, v_rows_m, tk), BF16),
    ]
    out_specs = [
        pl.BlockSpec((1, NSA_WIDTH, tm), lambda bi, i: (bi, 0, i)),
        pl.BlockSpec((1, NSA_KV_GROUPS, tm, LANES), lambda bi, i: (bi, 0, i, 0)),
        pl.BlockSpec((1, NSA_KV_GROUPS, tm, LANES), lambda bi, i: (bi, 0, i, 0)),
        pl.BlockSpec((1, tm // tk, v_rows_n, tk), lambda bi, i: (bi, i, 0, 0)),
        pl.BlockSpec((1, tm // tk, v_rows_n, tk), lambda bi, i: (bi, i, 0, 0)),
        pl.BlockSpec((1, tm, NSA_KV_WIDTH), lambda bi, i: (bi, i, 0)),
        pl.BlockSpec((1, tm, NSA_KV_WIDTH), lambda bi, i: (bi, i, 0)),
        pl.BlockSpec((1, NSA_KV_GROUPS * GATE_ROWS, tm), lambda bi, i: (bi, 0, i)),
        pl.BlockSpec((1, NSA_WIDTH, tm), lambda bi, i: (bi, 0, i)),
        pl.BlockSpec((1, MLA_WIDTH, tm), lambda bi, i: (bi, 0, i)),
        pl.BlockSpec((1, MLA_HEADS, MLA_QK, tm), lambda bi, i: (bi, 0, 0, i)),
        pl.BlockSpec((1, tm, MLA_QK), lambda bi, i: (bi, i, 0)),
        pl.BlockSpec((1, tm // tk, v_rows_m, tk), lambda bi, i: (bi, i, 0, 0)),
    ]
    kern = functools.partial(_proj_kernel, tm=tm, tk=tk, scale_nsa=NSA_HEAD_DIM ** -0.5 * LOG2E,
                             scale_mla=(MLA_NOPE_DIM + MLA_ROPE_DIM) ** -0.5 * LOG2E)
    return pl.pallas_call(
        kern, grid=(b, s // tm), in_specs=in_specs, out_specs=out_specs, out_shape=out_shape,
        compiler_params=pltpu.CompilerParams(vmem_limit_bytes=VMEM_LIMIT_BYTES),
        name="proj",
    )(x, mod3, ng, wtok, wtr, pos_b, pos_row, inv_l, sign_l, inv_nb, inv_mb, qng, kvng, wqt, wk, wv)


def _compress_kernel(k_ref, v_ref, ptop_ref, pbot_ref, wk1_ref, wv1_ref, wk2_ref, wv2t_ref, posc_ref, invl_ref,
                     signl_ref, kc_ref, vct_ref):
    ncp = k_ref.shape[1] // CMP_STRIDE
    gw = NSA_KV_GROUPS * CMP_HIDDEN
    lane = lax.broadcasted_iota(I32, (ncp, LANES), 1)

    def hidden(r_ref, w1_ref):
        r = jnp.concatenate([r_ref[0, pl.ds(t, ncp, stride=CMP_STRIDE), :] for t in range(CMP_STRIDE)], axis=1)
        a = jnp.dot((r + ptop_ref[...]).astype(BF16), w1_ref[:, 0:gw], preferred_element_type=F32)
        bt = jnp.dot((r + pbot_ref[...]).astype(BF16), w1_ref[:, gw:2 * gw], preferred_element_type=F32)
        return _silu(a + pltpu.roll(bt, ncp - 1, axis=0))

    cc, sc = _rope_lanes(posc_ref[...], invl_ref, signl_ref)
    hk = hidden(k_ref, wk1_ref)
    hv = hidden(v_ref, wv1_ref)
    for g in range(NSA_KV_GROUPS):
        hkg = hk[:, g * CMP_HIDDEN:(g + 1) * CMP_HIDDEN].astype(BF16)
        kc = jnp.dot(hkg, wk2_ref[...], preferred_element_type=F32)
        kc_ref[0, g] = _rope_tok(kc, cc, sc, NSA_HEAD_DIM // 2, lane).astype(BF16)
        hvg = hv[:, g * CMP_HIDDEN:(g + 1) * CMP_HIDDEN].astype(BF16)
        vct_ref[0, g] = lax.dot_general(wv2t_ref[...], hvg, NT, preferred_element_type=F32).astype(BF16)


def _compress(kcmp, vcmp, ptop, pbot, wk1, wv1, wk2, wv2t, pos_c, inv_l, sign_l):
    b, s, width = kcmp.shape
    ncp = s // CMP_STRIDE

    def full(a):
        return pl.BlockSpec(a.shape, lambda bi, _n=a.ndim: (0,) * _n)

    blk = pl.BlockSpec((1, s, width), lambda bi: (bi, 0, 0))
    tab = pl.BlockSpec((ncp, LANES), lambda bi: (bi, 0))
    return pl.pallas_call(
        _compress_kernel, grid=(b,),
        in_specs=[blk, blk, full(ptop), full(pbot), full(wk1), full(wv1), full(wk2), full(wv2t), tab,
                  full(inv_l), full(sign_l)],
        out_specs=[pl.BlockSpec((1, NSA_KV_GROUPS, ncp, LANES), lambda bi: (bi, 0, 0, 0)),
                   pl.BlockSpec((1, NSA_KV_GROUPS, NSA_HEAD_DIM, ncp), lambda bi: (bi, 0, 0, 0))],
        out_shape=[jax.ShapeDtypeStruct((b, NSA_KV_GROUPS, ncp, LANES), BF16),
                   jax.ShapeDtypeStruct((b, NSA_KV_GROUPS, NSA_HEAD_DIM, ncp), BF16)],
        compiler_params=pltpu.CompilerParams(vmem_limit_bytes=VMEM_LIMIT_BYTES),
        name="compress",
    )(kcmp, vcmp, ptop, pbot, wk1, wv1, wk2, wv2t, pos_c, inv_l, sign_l)


def _nsa_kernel(q1_ref, q2_ref, kc_ref, vct_ref, kaug_ref, vts_ref, kwin_ref, vtw_ref, g_ref, z_ref, mt_ref,
                o_ref, qaug_ref, sa_ref, sb_ref, m_ref, acc_ref, tot_ref, imp_ref, rank_ref, *, tq, tk, n_sel):
    i = pl.program_id(1)
    nh, ng, hpg, dk, half = NSA_HEADS, NSA_KV_GROUPS, NSA_HPG, NSA_HEAD_DIM, NSA_HEAD_DIM // 2
    vr = dk + ONES_ROWS
    group = [h // hpg for h in range(nh)]
    for h in range(nh):
        qaug_ref[h, 0:half, :] = q1_ref[0, h * half:(h + 1) * half, :]
        qaug_ref[h, half:dk, :] = q2_ref[0, h * half:(h + 1) * half, :]

    def gate(h, branch):
        row = group[h] * GATE_ROWS + (h % hpg) * N_BRANCH + branch
        return g_ref[0, row:row + 1, :]

    row_k = lax.broadcasted_iota(I32, (tk, tq), 0)
    col_q = lax.broadcasted_iota(I32, (tk, tq), 1)
    causal = row_k <= col_q

    def reset():
        m_ref[...] = jnp.full_like(m_ref, NEG_INF)
        acc_ref[...] = jnp.zeros_like(acc_ref)

    def add_branch(branch):
        for h in range(nh):
            inv_l = 1.0 / acc_ref[h, dk:dk + 1, :]
            tot_ref[h] = tot_ref[h] + (gate(h, branch) * inv_l) * acc_ref[h, 0:dk, :]

    def q_cols(h):
        return qaug_ref[h, 0:dk, :]

    ncp = kc_ref.shape[2]
    t_row = i * tq + lax.broadcasted_iota(I32, (1, tq), 1)
    last_n = (t_row - (CMP_BLOCK - 1)) >> CMP_SHIFT
    valid = lax.broadcasted_iota(I32, (ncp, tq), 0) <= last_n
    col_ok = last_n >= 0
    p_heads = {}

    def cmp_update(h, s_c):
        s_c = jnp.where(valid, s_c, NEG_INF)
        e = jnp.exp2(s_c - jnp.max(s_c, axis=0, keepdims=True))
        l_c = jnp.sum(e, axis=0, keepdims=True)
        p_c = e * jnp.where(col_ok, 1.0 / l_c, 0.0)
        o_c = jnp.dot(vct_ref[0, group[h]], p_c.astype(BF16), preferred_element_type=F32)
        tot_ref[h] = gate(h, 0) * o_c
        p_heads[h] = p_c

    nb = mt_ref.shape[0]

    def importance(g, _):
        psum = functools.reduce(lambda a, b: a + b, [p_heads[h] for h in range(nh) if group[h] == g])
        hi = psum.astype(BF16)
        lo = (psum - hi.astype(F32)).astype(BF16)
        mt = mt_ref[...]
        imp = jnp.dot(mt, hi, preferred_element_type=F32) + jnp.dot(mt, lo, preferred_element_type=F32)
        j_idx = lax.broadcasted_iota(I32, (nb, tq), 0)
        cur = (i * tq + lax.broadcasted_iota(I32, (nb, tq), 1)) >> SLC_SHIFT
        forced = (j_idx == 0) | (j_idx == cur) | (j_idx == cur - 1)
        imp_ref[g] = jnp.where(forced, FORCED_SCORE, jnp.where(j_idx > cur, -FORCED_SCORE, imp))

    stages = []
    for g in range(ng):
        kc = kc_ref[0, g, :, 0:dk]
        stages += [(functools.partial(jnp.dot, kc, q_cols(h), preferred_element_type=F32),
                    functools.partial(cmp_update, h)) for h in range(nh) if group[h] == g]
        stages.append((None, functools.partial(importance, g)))

    reset()
    n_back = WINDOW // tk
    for back in range(n_back + 1):
        jb = jnp.maximum(i - back, 0)
        if back == 0:
            keep = causal
        elif back == n_back:
            keep = (row_k > col_q) & (i >= back)
        else:
            keep = jnp.broadcast_to(i >= back, (tk, tq))
        for g in range(ng):
            kt_b = kwin_ref[0, g, pl.ds(pl.multiple_of(jb * tk, tk), tk), 0:dk]
            vt_b = vtw_ref[0, jb, g * vr:(g + 1) * vr, :]
            for h in range(nh):
                if group[h] == g:
                    stages.append((functools.partial(jnp.dot, kt_b, q_cols(h), preferred_element_type=F32),
                                   functools.partial(_chain_update, v_t=vt_b, m_ref=m_ref, acc_ref=acc_ref,
                                                     ch=h, keep=keep)))
    _pipeline(stages, lookahead=5)
    add_branch(2)

    rank_ref[...] = jnp.zeros_like(rank_ref)
    sub = lax.broadcasted_iota(I32, (SUBLANES, tq), 0)
    last_group = ((i + 1) * (tq // SLC_BLOCK) - 1) // SUBLANES

    def count(g, c, v):
        blk = imp_ref[g, v * SUBLANES:(v + 1) * SUBLANES, :]
        cnt = rank_ref[g, v * SUBLANES:(v + 1) * SUBLANES, :]
        for rr in range(SUBLANES):
            row = imp_ref[g, c * SUBLANES + rr:c * SUBLANES + rr + 1, :]
            if c < v:
                beats = row >= blk
            elif c > v:
                beats = row > blk
            else:
                beats = (row > blk) | ((row == blk) & (sub > rr))
            cnt = cnt + beats.astype(I32)
        rank_ref[g, v * SUBLANES:(v + 1) * SUBLANES, :] = cnt

    for lvl in range(nb // SUBLANES):
        @pl.when(lvl <= last_group)
        def _(lvl=lvl):
            for g in range(ng):
                for v in range(lvl + 1):
                    count(g, lvl, v)
                for c in range(lvl):
                    count(g, c, lvl)

    for g in range(ng):
        bias = jnp.where(rank_ref[g] < n_sel, 0.0, SEL_BIAS).astype(BF16)
        for h in range(nh):
            if group[h] == g:
                qaug_ref[h, dk:dk + nb, :] = bias

    reset()
    _causal_sweep(lambda j, g: kaug_ref[0, g, pl.ds(pl.multiple_of(j * tk, tk), tk), :],
                  lambda j, g: vts_ref[0, j, g * vr:(g + 1) * vr, :],
                  group, group, qaug_ref, sa_ref, sb_ref, m_ref, acc_ref, i, causal)
    add_branch(1)

    for h in range(nh):
        zz = z_ref[0, h * dk:(h + 1) * dk, :].astype(F32)
        o_ref[0, h * dk:(h + 1) * dk, :] = (tot_ref[h] * zz).astype(BF16)


def _nsa(qt, kc, vct, kaug, vts, kwin, vtw, gt, ztn, mt, *, tq, tk, n_sel):
    b, _, s = qt.shape
    nt = s // tk
    ncp = kc.shape[2]
    ng, nh = NSA_KV_GROUPS, NSA_HEADS
    hq = nh * (NSA_HEAD_DIM // 2)
    nb = mt.shape[0]
    vr = NSA_HEAD_DIM + ONES_ROWS
    in_specs = [
        pl.BlockSpec((1, hq, tq), lambda bi, i: (bi, 0, i)),
        pl.BlockSpec((1, hq, tq), lambda bi, i: (bi, 1, i)),
        pl.BlockSpec((1, ng, ncp, LANES), lambda bi, i: (bi, 0, 0, 0)),
        pl.BlockSpec((1, ng, NSA_HEAD_DIM, ncp), lambda bi, i: (bi, 0, 0, 0)),
        pl.BlockSpec((1, ng, s, LANES), lambda bi, i: (bi, 0, 0, 0)),
        pl.BlockSpec((1, nt, ng * vr, tk), lambda bi, i: (bi, 0, 0, 0)),
        pl.BlockSpec((1, ng, s, LANES), lambda bi, i: (bi, 0, 0, 0)),
        pl.BlockSpec((1, nt, ng * vr, tk), lambda bi, i: (bi, 0, 0, 0)),
        pl.BlockSpec((1, ng * GATE_ROWS, tq), lambda bi, i: (bi, 0, i)),
        pl.BlockSpec((1, NSA_WIDTH, tq), lambda bi, i: (bi, 0, i)),
        pl.BlockSpec(mt.shape, lambda bi, i: (0, 0)),
    ]
    kern = functools.partial(_nsa_kernel, tq=tq, tk=tk, n_sel=n_sel)
    return pl.pallas_call(
        kern, grid=(b, s // tq), in_specs=in_specs,
        out_specs=pl.BlockSpec((1, NSA_WIDTH, tq), lambda bi, i: (bi, 0, i)),
        out_shape=jax.ShapeDtypeStruct((b, NSA_WIDTH, s), BF16),
        scratch_shapes=[pltpu.VMEM((nh, NSA_HEAD_DIM + nb, tq), BF16),
                        pltpu.VMEM((nh, tk, tq), F32), pltpu.VMEM((nh, tk, tq), F32),
                        pltpu.VMEM((nh, 1, tq), F32), pltpu.VMEM((nh, vr, tq), F32),
                        pltpu.VMEM((nh, NSA_HEAD_DIM, tq), F32),
                        pltpu.VMEM((ng, nb, tq), F32), pltpu.VMEM((ng, nb, tq), I32)],
        compiler_params=pltpu.CompilerParams(dimension_semantics=("arbitrary", "arbitrary"),
                                             vmem_limit_bytes=VMEM_LIMIT_BYTES),
        name="nsa",
    )(qt, qt, kc, vct, kaug, vts, kwin, vtw, gt, ztn, mt)


def _mla_kernel(q_ref, qn_ref, k_ref, vt_ref, z_ref, o_ref, sa_ref, sb_ref, m_ref, acc_ref, *, tq, tk):
    i = pl.program_id(1)
    dv, vr = MLA_V_DIM, MLA_V_DIM + ONES_ROWS
    m_ref[...] = jnp.full_like(m_ref, NEG_INF)
    acc_ref[...] = jnp.zeros_like(acc_ref)

    causal = lax.broadcasted_iota(I32, (tk, tq), 0) <= lax.broadcasted_iota(I32, (tk, tq), 1)
    _causal_sweep(lambda j, g: k_ref[0, pl.ds(pl.multiple_of(j * tk, tk), tk), :],
                  lambda j, hd: vt_ref[0, j, hd * vr:(hd + 1) * vr, :],
                  [0] * MLA_HEADS, list(range(MLA_HEADS)), q_ref.at[0], sa_ref, sb_ref, m_ref, acc_ref, i, causal,
                  q_next_ref=qn_ref.at[0], first=i == 0)
    for hd in range(MLA_HEADS):
        o_h = acc_ref[hd, 0:dv, :] * (1.0 / acc_ref[hd, dv:dv + 1, :])
        zz = z_ref[0, hd * dv:(hd + 1) * dv, :].astype(F32)
        o_ref[0, hd * dv:(hd + 1) * dv, :] = (o_h * zz).astype(BF16)


def _mla(qtm, kmla, vtm, ztm, *, tq, tk):
    b, s, _ = kmla.shape
    nt = s // tk
    vrows = MLA_HEADS * (MLA_V_DIM + ONES_ROWS)
    kern = functools.partial(_mla_kernel, tq=tq, tk=tk)
    return pl.pallas_call(
        kern, grid=(b, s // tq),
        in_specs=[pl.BlockSpec((1, MLA_HEADS, MLA_QK, tq), lambda bi, i: (bi, 0, 0, i)),
                  pl.BlockSpec((1, MLA_HEADS, MLA_QK, tq), lambda bi, i: (bi, 0, 0, jnp.minimum(i + 1, s // tq - 1))),
                  pl.BlockSpec((1, s, MLA_QK), lambda bi, i: (bi, 0, 0)),
                  pl.BlockSpec((1, nt, vrows, tk), lambda bi, i: (bi, 0, 0, 0)),
                  pl.BlockSpec((1, MLA_WIDTH, tq), lambda bi, i: (bi, 0, i))],
        out_specs=pl.BlockSpec((1, MLA_WIDTH, tq), lambda bi, i: (bi, 0, i)),
        out_shape=jax.ShapeDtypeStruct((b, MLA_WIDTH, s), BF16),
        scratch_shapes=[pltpu.VMEM((MLA_HEADS, tk, tq), F32), pltpu.VMEM((MLA_HEADS, tk, tq), F32),
                        pltpu.VMEM((MLA_HEADS, 1, tq), F32), pltpu.VMEM((MLA_HEADS, MLA_V_DIM + ONES_ROWS, tq), F32)],
        compiler_params=pltpu.CompilerParams(dimension_semantics=("arbitrary", "arbitrary"),
                                             vmem_limit_bytes=VMEM_LIMIT_BYTES),
        name="mla",
    )(qtm, qtm, kmla, vtm, ztm)


def _out_kernel(x_ref, mn_ref, mm_ref, w_ref, mod_ref, fg_ref, o_ref, *, final):
    y = lax.dot_general(mn_ref[0], w_ref[0:NSA_WIDTH, :], TN, preferred_element_type=F32)
    y = y + lax.dot_general(mm_ref[0], w_ref[NSA_WIDTH:MIX_WIDTH, :], TN, preferred_element_type=F32)
    x2 = x_ref[0] + mod_ref[0][2:3] * y
    o_ref[0] = _rms(x2, fg_ref[...]) if final else x2


def _out(x, mn, mm, w_out, mod3, fg, *, tm, final):
    b, s, d = x.shape
    return pl.pallas_call(
        functools.partial(_out_kernel, final=final), grid=(b, s // tm),
        in_specs=[pl.BlockSpec((1, tm, d), lambda bi, i: (bi, i, 0)),
                  pl.BlockSpec((1, NSA_WIDTH, tm), lambda bi, i: (bi, 0, i)),
                  pl.BlockSpec((1, MLA_WIDTH, tm), lambda bi, i: (bi, 0, i)),
                  pl.BlockSpec(w_out.shape, lambda bi, i: (0, 0)),
                  pl.BlockSpec((1, 3, d), lambda bi, i: (bi, 0, 0)),
                  pl.BlockSpec((1, d), lambda bi, i: (0, 0))],
        out_specs=pl.BlockSpec((1, tm, d), lambda bi, i: (bi, i, 0)),
        out_shape=jax.ShapeDtypeStruct((b, s, d), F32),
        compiler_params=pltpu.CompilerParams(vmem_limit_bytes=VMEM_LIMIT_BYTES),
        name="out_proj",
    )(x, mn, mm, w_out, mod3, fg)


def _cmp_to_slc_t(ncp, nc, nslc, nb):
    start = np.arange(nc)[:, None] * CMP_STRIDE
    bstart = np.arange(nslc)[None, :] * SLC_BLOCK
    ov = np.minimum(start + CMP_BLOCK, bstart + SLC_BLOCK) - np.maximum(start, bstart)
    m = (np.clip(ov, 0, None) / CMP_BLOCK).astype(np.float32)
    out = np.zeros((nb, ncp), np.float32)
    out[:nslc, :nc] = m.T
    return out


def _layout_w_in(w):
    d = w.shape[0]
    (q_n, kc_n, vc_n, ks_n, vs_n, kw_n, vw_n, gl_n, z_n, cq_m, ckv_m, kr_m, z_m) = jnp.split(w, IN_OFFSETS, axis=-1)
    dk = NSA_HEAD_DIM
    z64 = jnp.zeros((d, LANES - dk), w.dtype)
    wtok = jnp.concatenate(
        [ks_n[:, :dk], z64, ks_n[:, dk:], z64, kw_n[:, :dk], z64, kw_n[:, dk:], z64,
         kc_n, vc_n, cq_m, ckv_m, jnp.zeros((d, KR_LANE), w.dtype), kr_m,
         jnp.zeros((d, LANES - KR_LANE - MLA_ROPE_DIM), w.dtype)], axis=1)
    qr = q_n.reshape(d, NSA_HEADS, 2, dk // 2)
    q_perm = jnp.concatenate([qr[:, :, 0, :].reshape(d, -1), qr[:, :, 1, :].reshape(d, -1)], axis=1)
    gl = gl_n.reshape(d, NSA_KV_GROUPS, NSA_HPG * N_BRANCH)
    gl = jnp.pad(gl, ((0, 0), (0, 0), (0, GATE_ROWS - NSA_HPG * N_BRANCH))).reshape(d, -1)
    wtr = jnp.concatenate([q_perm, vs_n, vw_n, gl, z_n, z_m], axis=1).T
    assert wtok.shape[1] == TOK_COLS and wtr.shape[0] == TR_ROWS
    return wtok.astype(BF16), wtr.astype(BF16)


def _layout_w1(w1):
    hid = w1.shape[1]
    w1r = w1.reshape(2, CMP_STRIDE, NSA_HEAD_DIM, hid)
    eye = jnp.eye(NSA_KV_GROUPS, dtype=w1.dtype)
    halves = [jnp.einsum('ldh,pg->lpdgh', w1r[k], eye).reshape(CMP_STRIDE * NSA_KV_WIDTH, NSA_KV_GROUPS * hid)
              for k in range(2)]
    return jnp.concatenate(halves, axis=1).astype(BF16)


def kernel(x, c, positions, ada_w, ada_b, norm_g, w_in, cmp_pos, cmp_k_w1, cmp_k_w2, cmp_v_w1, cmp_v_w2,
           q_norm_g, w_q_up, kv_norm_g, w_kv_up, w_out, final_norm_g):
    b, s, d = x.shape
    depth = ada_w.shape[0]
    tm, tq = PROJ_TILE, ATT_TILE
    tk = tq
    assert s % tm == 0 and tm % tk == 0 and WINDOW % tk == 0 and (tq & (tq - 1)) == 0 and s % OUT_TILE == 0
    assert CMP_BLOCK == 2 * CMP_STRIDE and s % SLC_BLOCK == 0
    nslc = s // SLC_BLOCK
    nb = LANES - NSA_HEAD_DIM
    assert nslc <= nb
    ncp = s // CMP_STRIDE
    nc = ncp - 1

    half_n, half_m = NSA_HEAD_DIM // 2, MLA_ROPE_DIM // 2
    inv_n = ROPE_THETA ** (-jnp.arange(half_n, dtype=F32) / half_n)
    inv_m = ROPE_THETA ** (-jnp.arange(half_m, dtype=F32) / half_m)
    ones_n, ones_m = jnp.ones((half_n,), F32), jnp.ones((half_m,), F32)
    pad = jnp.zeros((LANES - NSA_HEAD_DIM - MLA_ROPE_DIM,), F32)
    inv_l = jnp.concatenate([inv_n, inv_n, inv_m, inv_m, pad])[None]
    sign_l = jnp.concatenate([-ones_n, ones_n, -ones_m, ones_m, pad])[None]
    pos_f = positions.astype(F32)
    pos_b = jnp.broadcast_to(pos_f.reshape(b * s, 1), (b * s, LANES))
    pos_row = pos_f.reshape(b, 1, s)
    cmp_end = np.minimum(np.arange(ncp) * CMP_STRIDE + CMP_BLOCK - 1, s - 1)
    pos_c = jnp.broadcast_to(pos_f[:, cmp_end].reshape(b * ncp, 1), (b * ncp, LANES))
    inv_nb = jnp.broadcast_to(inv_n[:, None], (half_n, tm))
    inv_mb = jnp.broadcast_to(inv_m[:, None], (half_m, tm))

    mt = jnp.asarray(_cmp_to_slc_t(ncp, nc, nslc, nb), dtype=BF16)
    bp = -(-b // SUBLANES) * SUBLANES
    c_pad = jnp.pad(c, ((0, bp - b), (0, 0)))

    for l in range(depth):
        mod = _adaln(c_pad, ada_w[l], ada_b[l].reshape(1, -1))
        mod3 = mod[:b].reshape(b, 3, d)
        wtok, wtr = _layout_w_in(w_in[l])
        wq = w_q_up[l].reshape(MLA_Q_RANK, MLA_HEADS, MLA_NOPE_DIM + MLA_ROPE_DIM)
        wqt = jnp.concatenate([wq[:, :, :MLA_NOPE_DIM].reshape(MLA_Q_RANK, -1),
                               wq[:, :, MLA_NOPE_DIM:MLA_NOPE_DIM + half_m].reshape(MLA_Q_RANK, -1),
                               wq[:, :, MLA_NOPE_DIM + half_m:].reshape(MLA_Q_RANK, -1)], axis=1).T.astype(BF16)
        wkv = w_kv_up[l].reshape(MLA_KV_RANK, MLA_HEADS, MLA_NOPE_DIM + MLA_V_DIM)
        wk = wkv[:, :, :MLA_NOPE_DIM].transpose(1, 0, 2).astype(BF16)
        wv = wkv[:, :, MLA_NOPE_DIM:].transpose(1, 2, 0).reshape(MLA_WIDTH, MLA_KV_RANK).astype(BF16)

        (qt, kaug, kwin, vts, vtw, kcmp, vcmp, gt, ztn, ztm, qtm, kmla, vtm) = _proj(
            x, mod3, norm_g[l].reshape(1, d), wtok, wtr, pos_b, pos_row, inv_l, sign_l, inv_nb, inv_mb,
            q_norm_g[l].reshape(1, -1), kv_norm_g[l].reshape(1, -1), wqt, wk, wv, tm=tm, tk=tk)

        pos_l = cmp_pos[l]
        ptop = jnp.broadcast_to(pos_l[:CMP_STRIDE, None, :], (CMP_STRIDE, NSA_KV_GROUPS, NSA_HEAD_DIM)).reshape(1, -1)
        pbot = jnp.broadcast_to(pos_l[CMP_STRIDE:, None, :], (CMP_STRIDE, NSA_KV_GROUPS, NSA_HEAD_DIM)).reshape(1, -1)
        wk2 = jnp.pad(cmp_k_w2[l], ((0, 0), (0, LANES - NSA_HEAD_DIM))).astype(BF16)
        kc, vct = _compress(kcmp, vcmp, ptop, pbot,
                            _layout_w1(cmp_k_w1[l]), _layout_w1(cmp_v_w1[l]), wk2,
                            cmp_v_w2[l].T.astype(BF16), pos_c, inv_l, sign_l)

        mix_n = _nsa(qt, kc, vct, kaug, vts, kwin, vtw, gt, ztn, mt, tq=tq, tk=tk, n_sel=min(SLC_TOPK, nslc))
        mix_m = _mla(qtm, kmla, vtm, ztm, tq=tq, tk=tk)
        x = _out(x, mix_n, mix_m, w_out[l].astype(BF16), mod3, final_norm_g.reshape(1, d), tm=OUT_TILE,
                 final=(l == depth - 1))
    return x
```

```python
import functools

import numpy as np
import jax
import jax.numpy as jnp
from jax import lax
from jax.experimental import pallas as pl
from jax.experimental.pallas import tpu as pltpu

F32 = jnp.float32
BF16 = jnp.bfloat16
I32 = jnp.int32

NSA_HEADS = 8
NSA_KV_GROUPS = 2
NSA_HPG = NSA_HEADS // NSA_KV_GROUPS
NSA_HEAD_DIM = 64
NSA_WIDTH = NSA_HEADS * NSA_HEAD_DIM
NSA_KV_WIDTH = NSA_KV_GROUPS * NSA_HEAD_DIM
CMP_BLOCK = 32
CMP_STRIDE = 16
CMP_HIDDEN = 128
SLC_BLOCK = 64
SLC_TOPK = 16
WINDOW = 512
N_BRANCH = 3
FORCED_SCORE = 1.0e4
MLA_HEADS = 8
MLA_NOPE_DIM = 64
MLA_ROPE_DIM = 32
MLA_V_DIM = 64
MLA_WIDTH = MLA_HEADS * MLA_V_DIM
MLA_Q_RANK = 256
MLA_KV_RANK = 128
MIX_WIDTH = NSA_WIDTH + MLA_WIDTH
ROPE_THETA = 10000.0
NORM_EPS = 1e-6
NEG_INF = -1e30
IN_SIZES = (NSA_WIDTH, NSA_KV_WIDTH, NSA_KV_WIDTH, NSA_KV_WIDTH, NSA_KV_WIDTH, NSA_KV_WIDTH, NSA_KV_WIDTH,
            NSA_HEADS * N_BRANCH, NSA_WIDTH, MLA_Q_RANK, MLA_KV_RANK, MLA_ROPE_DIM, MLA_WIDTH)
IN_OFFSETS = tuple(int(o) for o in np.cumsum(IN_SIZES)[:-1])

LANES = 128
SUBLANES = 8
VMEM_LIMIT_BYTES = 56 * 1024 * 1024

PROJ_TILE = 1024
OUT_TILE = 1024
ATT_TILE = 256
SEL_BIAS = NEG_INF
GATE_ROWS = 16
ONES_ROWS = 16
MLA_QK = LANES
LOG2E = float(np.log2(np.e))
SWEEP_UNROLL = 4
SLC_SHIFT = SLC_BLOCK.bit_length() - 1
CMP_SHIFT = CMP_STRIDE.bit_length() - 1
assert 1 << SLC_SHIFT == SLC_BLOCK and 1 << CMP_SHIFT == CMP_STRIDE

NT = (((1,), (1,)), ((), ()))
TN = (((0,), (0,)), ((), ()))


def _silu(v):
    return v * jax.nn.sigmoid(v)


def _rms(v, g):
    ms = jnp.mean(v * v, axis=-1, keepdims=True)
    return v * lax.rsqrt(ms + NORM_EPS) * g


def _rope_tok(v, c, s_signed, half, lane, base=0):
    up = pltpu.roll(v, LANES - half, axis=1)
    dn = pltpu.roll(v, half, axis=1)
    return v * c + jnp.where(lane < base + half, up, dn) * s_signed


def _rope_lanes(pos_b, inv_ref, sign_ref):
    ang = pos_b * inv_ref[...]
    return jnp.cos(ang), jnp.sin(ang) * sign_ref[...]


def _chain_update(s_t, v_t, m_ref, acc_ref, ch, keep=None):
    if keep is not None:
        s_t = jnp.where(keep, s_t, NEG_INF)
    m_prev = m_ref[ch]
    m_new = jnp.maximum(m_prev, jnp.max(s_t, axis=0, keepdims=True))
    alpha = jnp.exp2(m_prev - m_new)
    p = jnp.exp2(s_t - m_new)
    acc_ref[ch] = alpha * acc_ref[ch] + jnp.dot(v_t, p.astype(BF16), preferred_element_type=F32)
    m_ref[ch] = m_new


def _causal_sweep(k_tile, v_tile, k_group, v_group, q_ref, sa_ref, sb_ref, m_ref, acc_ref, last, keep_last,
                  lookahead=2, q_next_ref=None, first=None):
    n_chains = len(k_group)

    def loader(tile_fn, j):
        cache = {}
        return lambda g: cache.setdefault(g, tile_fn(j, g))

    def phase(j_next, s_next_ref, s_cur_ref, j_cur, keep=None, ahead=False):
        k_next = loader(k_tile, j_next) if j_next is not None else None
        k_zero = loader(k_tile, 0) if ahead else None
        v_cur = loader(v_tile, j_cur)
        for n in range(n_chains + lookahead):
            if k_next is not None and n < n_chains:
                s_next_ref[n] = jnp.dot(k_next(k_group[n]), q_ref[n], preferred_element_type=F32)
            if n >= lookahead:
                ch = n - lookahead
                _chain_update(s_cur_ref[ch], v_cur(v_group[ch]), m_ref, acc_ref, ch, keep=keep)
                if ahead:
                    sa_ref[ch] = jnp.dot(k_zero(k_group[ch]), q_next_ref[ch], preferred_element_type=F32)

    bufs = (sa_ref, sb_ref)

    def prologue():
        k_0 = loader(k_tile, 0)
        for ch in range(n_chains):
            sa_ref[ch] = jnp.dot(k_0(k_group[ch]), q_ref[ch], preferred_element_type=F32)

    if q_next_ref is None:
        prologue()
    else:
        pl.when(first)(prologue)

    def run(j0, count):
        for u in range(count):
            phase(j0 + u + 1, bufs[(u + 1) % 2], bufs[u % 2], j0 + u)

    def body(jj, carry):
        run(SWEEP_UNROLL * jj, SWEEP_UNROLL)
        return carry

    lax.fori_loop(0, last // SWEEP_UNROLL, body, 0)
    rem = last % SWEEP_UNROLL
    for r in range(SWEEP_UNROLL):
        @pl.when(rem == r)
        def _(r=r):
            run(last - r, r)
            phase(None, None, bufs[r % 2], last, keep=keep_last, ahead=q_next_ref is not None)


def _pipeline(stages, lookahead):
    pending = {}
    for n in range(len(stages) + lookahead):
        if n < len(stages) and stages[n][0] is not None:
            pending[n] = stages[n][0]()
        if n >= lookahead:
            stages[n - lookahead][1](pending.pop(n - lookahead, None))


def _adaln_kernel(c_ref, w_ref, b_ref, o_ref):
    sc = _silu(c_ref[...])
    o_ref[...] = jnp.dot(sc.astype(BF16), w_ref[...].astype(BF16), preferred_element_type=F32) + b_ref[...]


def _adaln(c_pad, w, b):
    bp, d = c_pad.shape
    n = w.shape[1] // d
    return pl.pallas_call(
        _adaln_kernel,
        grid=(n,),
        in_specs=[pl.BlockSpec((bp, d), lambda j: (0, 0)),
                  pl.BlockSpec((d, d), lambda j: (0, j)),
                  pl.BlockSpec((1, d), lambda j: (0, j))],
        out_specs=pl.BlockSpec((bp, d), lambda j: (0, j)),
        out_shape=jax.ShapeDtypeStruct((bp, n * d), F32),
        name="adaln",
    )(c_pad, w, b)


TOK_KS, TOK_KW, TOK_KC, TOK_VC, TOK_CQ, TOK_CKV, TOK_KR, TOK_COLS = 0, 256, 512, 640, 768, 1024, 1152, 1280
KR_LANE = NSA_HEAD_DIM
TR_Q, TR_VS, TR_VW, TR_G, TR_ZN, TR_ZM, TR_ROWS = 0, 512, 640, 768, 800, 1312, 1824


def _proj_kernel(x_ref, mod_ref, ng_ref, wtok_ref, wtr_ref, posb_ref, posr_ref, invl_ref, signl_ref,
                 invn_ref, invm_ref, qng_ref, kvng_ref, wqt_ref, wkn_ref, wv_ref,
                 qt_ref, kaug_ref, kwin_ref, vts_ref, vtw_ref, kcmp_ref, vcmp_ref, gt_ref,
                 ztn_ref, ztm_ref, qtm_ref, kmla_ref, vtm_ref, *, tm, tk, scale_nsa, scale_mla):
    i = pl.program_id(1)
    mod = mod_ref[0]
    h = _rms(x_ref[0], ng_ref[...]) * (1.0 + mod[1:2]) + mod[0:1]
    hb = h.astype(BF16)
    tok = jnp.dot(hb, wtok_ref[...], preferred_element_type=F32)
    tr = lax.dot_general(wtr_ref[...], hb, NT, preferred_element_type=F32)

    lane = lax.broadcasted_iota(I32, (tm, LANES), 1)
    row = lax.broadcasted_iota(I32, (tm, LANES), 0)
    blk = (i * tm + row) >> SLC_SHIFT
    onehot = (lane - NSA_HEAD_DIM == blk).astype(F32)
    ct, st = _rope_lanes(posb_ref[...], invl_ref, signl_ref)
    half_n = NSA_HEAD_DIM // 2
    for g in range(NSA_KV_GROUPS):
        ks = _rope_tok(tok[:, TOK_KS + LANES * g:TOK_KS + LANES * (g + 1)], ct, st, half_n, lane)
        kaug_ref[0, g] = jnp.where(lane >= NSA_HEAD_DIM, onehot, ks).astype(BF16)
        kw = _rope_tok(tok[:, TOK_KW + LANES * g:TOK_KW + LANES * (g + 1)], ct, st, half_n, lane)
        kwin_ref[0, g] = kw.astype(BF16)
    kcmp_ref[0] = tok[:, TOK_KC:TOK_KC + LANES]
    vcmp_ref[0] = tok[:, TOK_VC:TOK_VC + LANES]

    ckvn = _rms(tok[:, TOK_CKV:TOK_CKV + MLA_KV_RANK], kvng_ref[...])
    krr = _rope_tok(tok[:, TOK_KR:TOK_KR + LANES], ct, st, MLA_ROPE_DIM // 2, lane, base=KR_LANE)
    ckvb = ckvn.astype(BF16)
    kn = jnp.dot(ckvb, wkn_ref[...], preferred_element_type=F32)
    is_rot = (lane >= KR_LANE) & (lane < KR_LANE + MLA_ROPE_DIM)
    for hd in range(MLA_HEADS):
        kmla_ref[0, hd] = jnp.where(is_rot, krr, kn[:, hd * LANES:(hd + 1) * LANES]).astype(BF16)
    vtm = lax.dot_general(wv_ref[...], ckvb, NT, preferred_element_type=F32).astype(BF16)
    ones = jnp.ones((ONES_ROWS, tk), BF16)
    vr = MLA_V_DIM + ONES_ROWS
    for ii in range(tm // tk):
        for hd in range(MLA_HEADS):
            vtm_ref[0, ii, hd * vr:hd * vr + MLA_V_DIM, :] = vtm[hd * MLA_V_DIM:(hd + 1) * MLA_V_DIM,
                                                                 ii * tk:(ii + 1) * tk]
            vtm_ref[0, ii, hd * vr + MLA_V_DIM:(hd + 1) * vr, :] = ones

    cqn = _rms(tok[:, TOK_CQ:TOK_CQ + MLA_Q_RANK], qng_ref[...]).astype(BF16)
    qm = lax.dot_general(wqt_ref[...], cqn, NT, preferred_element_type=F32)
    nq = MLA_HEADS * MLA_NOPE_DIM
    hr = MLA_ROPE_DIM // 2
    x1 = qm[nq:nq + MLA_HEADS * hr].reshape(MLA_HEADS, hr, tm)
    x2 = qm[nq + MLA_HEADS * hr:nq + 2 * MLA_HEADS * hr].reshape(MLA_HEADS, hr, tm)
    ang_m = invm_ref[...] * posr_ref[0]
    cm_t, sm_t = jnp.cos(ang_m)[None], jnp.sin(ang_m)[None]
    o1 = (x1 * cm_t - x2 * sm_t) * scale_mla
    o2 = (x2 * cm_t + x1 * sm_t) * scale_mla
    nd = MLA_NOPE_DIM
    q_pad = jnp.zeros((MLA_QK - nd - MLA_ROPE_DIM, tm), BF16)
    for hd in range(MLA_HEADS):
        qtm_ref[0, hd, 0:nd, :] = (qm[hd * nd:(hd + 1) * nd] * scale_mla).astype(BF16)
        qtm_ref[0, hd, nd:nd + hr, :] = o1[hd].astype(BF16)
        qtm_ref[0, hd, nd + hr:nd + 2 * hr, :] = o2[hd].astype(BF16)
        qtm_ref[0, hd, nd + 2 * hr:MLA_QK, :] = q_pad

    hq = NSA_HEADS * half_n
    q1 = tr[TR_Q:TR_Q + hq].reshape(NSA_HEADS, half_n, tm)
    q2 = tr[TR_Q + hq:TR_Q + 2 * hq].reshape(NSA_HEADS, half_n, tm)
    ang_n = invn_ref[...] * posr_ref[0]
    cn_t, sn_t = jnp.cos(ang_n)[None], jnp.sin(ang_n)[None]
    qt_ref[0, 0:hq, :] = ((q1 * cn_t - q2 * sn_t) * scale_nsa).reshape(hq, tm).astype(BF16)
    qt_ref[0, hq:2 * hq, :] = ((q2 * cn_t + q1 * sn_t) * scale_nsa).reshape(hq, tm).astype(BF16)

    vts = tr[TR_VS:TR_VS + NSA_KV_WIDTH].astype(BF16)
    vtw = tr[TR_VW:TR_VW + NSA_KV_WIDTH].astype(BF16)
    dk = NSA_HEAD_DIM
    gr = dk + ONES_ROWS
    for ii in range(tm // tk):
        for g in range(NSA_KV_GROUPS):
            vts_ref[0, ii, g * gr:g * gr + dk, :] = vts[g * dk:(g + 1) * dk, ii * tk:(ii + 1) * tk]
            vtw_ref[0, ii, g * gr:g * gr + dk, :] = vtw[g * dk:(g + 1) * dk, ii * tk:(ii + 1) * tk]
            vts_ref[0, ii, g * gr + dk:(g + 1) * gr, :] = ones
            vtw_ref[0, ii, g * gr + dk:(g + 1) * gr, :] = ones
    gt_ref[0] = jax.nn.sigmoid(tr[TR_G:TR_G + NSA_KV_GROUPS * GATE_ROWS])
    ztn_ref[0] = _silu(tr[TR_ZN:TR_ZN + NSA_WIDTH]).astype(BF16)
    ztm_ref[0] = _silu(tr[TR_ZM:TR_ZM + MLA_WIDTH]).astype(BF16)


def _proj(x, mod3, ng, wtok, wtr, pos_b, pos_row, inv_l, sign_l, inv_nb, inv_mb, qng, kvng, wqt, wkn, wv, *, tm, tk):
    b, s, d = x.shape
    nt = s // tk
    v_rows_n = NSA_KV_GROUPS * (NSA_HEAD_DIM + ONES_ROWS)
    v_rows_m = MLA_HEADS * (MLA_V_DIM + ONES_ROWS)
    tile_tok = pl.BlockSpec((tm, LANES), lambda bi, i: (bi * (s // tm) + i, 0))

    def full(a):
        return pl.BlockSpec(a.shape, lambda bi, i, _n=a.ndim: (0,) * _n)

    in_specs = [pl.BlockSpec((1, tm, d), lambda bi, i: (bi, i, 0)),
                pl.BlockSpec((1, 3, d), lambda bi, i: (bi, 0, 0)),
                full(ng), full(wtok), full(wtr),
                tile_tok, pl.BlockSpec((1, 1, tm), lambda bi, i: (bi, 0, i)),
                full(inv_l), full(sign_l), full(inv_nb), full(inv_mb),
                full(qng), full(kvng), full(wqt), full(wkn), full(wv)]
    out_shape = [
        jax.ShapeDtypeStruct((b, NSA_WIDTH, s), BF16),
        jax.ShapeDtypeStruct((b, NSA_KV_GROUPS, s, LANES), BF16),
        jax.ShapeDtypeStruct((b, NSA_KV_GROUPS, s, LANES), BF16),
        jax.ShapeDtypeStruct((b, nt, v_rows_n, tk), BF16),
        jax.ShapeDtypeStruct((b, nt, v_rows_n, tk), BF16),
        jax.ShapeDtypeStruct((b, s, NSA_KV_WIDTH), F32),
        jax.ShapeDtypeStruct((b, s, NSA_KV_WIDTH), F32),
        jax.ShapeDtypeStruct((b, NSA_KV_GROUPS * GATE_ROWS, s), F32),
        jax.ShapeDtypeStruct((b, NSA_WIDTH, s), BF16),
        jax.ShapeDtypeStruct((b, MLA_WIDTH, s), BF16),
        jax.ShapeDtypeStruct((b, MLA_HEADS, MLA_QK, s), BF16),
        jax.ShapeDtypeStruct((b, MLA_HEADS, s, MLA_QK), BF16),
        jax.ShapeDtypeStruct((b, nt, v_rows_m, tk), BF16),
    ]
    out_specs = [
        pl.BlockSpec((1, NSA_WIDTH, tm), lambda bi, i: (bi, 0, i)),
        pl.BlockSpec((1, NSA_KV_GROUPS, tm, LANES), lambda bi, i: (bi, 0, i, 0)),
        pl.BlockSpec((1, NSA_KV_GROUPS, tm, LANES), lambda bi, i: (bi, 0, i, 0)),
        pl.BlockSpec((1, tm // tk, v_rows_n, tk), lambda bi, i: (bi, i, 0, 0)),
        pl.BlockSpec((1, tm // tk, v_rows_n, tk), lambda bi, i: (bi, i, 0, 0)),
        pl.BlockSpec((1, tm, NSA_KV_WIDTH), lambda bi, i: (bi, i, 0)),
        pl.BlockSpec((1, tm, NSA_KV_WIDTH), lambda bi, i: (bi, i, 0)),
        pl.BlockSpec((1, NSA_KV_GROUPS * GATE_ROWS, tm), lambda bi, i: (bi, 0, i)),
        pl.BlockSpec((1, NSA_WIDTH, tm), lambda bi, i: (bi, 0, i)),
        pl.BlockSpec((1, MLA_WIDTH, tm), lambda bi, i: (bi, 0, i)),
        pl.BlockSpec((1, MLA_HEADS, MLA_QK, tm), lambda bi, i: (bi, 0, 0, i)),
        pl.BlockSpec((1, MLA_HEADS, tm, MLA_QK), lambda bi, i: (bi, 0, i, 0)),
        pl.BlockSpec((1, tm // tk, v_rows_m, tk), lambda bi, i: (bi, i, 0, 0)),
    ]
    kern = functools.partial(_proj_kernel, tm=tm, tk=tk, scale_nsa=NSA_HEAD_DIM ** -0.5 * LOG2E,
                             scale_mla=(MLA_NOPE_DIM + MLA_ROPE_DIM) ** -0.5 * LOG2E)
    return pl.pallas_call(
        kern, grid=(b, s // tm), in_specs=in_specs, out_specs=out_specs, out_shape=out_shape,
        compiler_params=pltpu.CompilerParams(vmem_limit_bytes=VMEM_LIMIT_BYTES),
        name="proj",
    )(x, mod3, ng, wtok, wtr, pos_b, pos_row, inv_l, sign_l, inv_nb, inv_mb, qng, kvng, wqt, wkn, wv)


def _compress_kernel(k_ref, v_ref, ptop_ref, pbot_ref, wk1_ref, wv1_ref, wk2_ref, wv2t_ref, posc_ref, invl_ref,
                     signl_ref, kc_ref, vct_ref):
    ncp = k_ref.shape[1] // CMP_STRIDE
    gw = NSA_KV_GROUPS * CMP_HIDDEN
    lane = lax.broadcasted_iota(I32, (ncp, LANES), 1)

    def hidden(r_ref, w1_ref):
        r = jnp.concatenate([r_ref[0, pl.ds(t, ncp, stride=CMP_STRIDE), :] for t in range(CMP_STRIDE)], axis=1)
        a = jnp.dot((r + ptop_ref[...]).astype(BF16), w1_ref[:, 0:gw], preferred_element_type=F32)
        bt = jnp.dot((r + pbot_ref[...]).astype(BF16), w1_ref[:, gw:2 * gw], preferred_element_type=F32)
        return _silu(a + pltpu.roll(bt, ncp - 1, axis=0))

    cc, sc = _rope_lanes(posc_ref[...], invl_ref, signl_ref)
    hk = hidden(k_ref, wk1_ref)
    hv = hidden(v_ref, wv1_ref)
    for g in range(NSA_KV_GROUPS):
        hkg = hk[:, g * CMP_HIDDEN:(g + 1) * CMP_HIDDEN].astype(BF16)
        kc = jnp.dot(hkg, wk2_ref[...], preferred_element_type=F32)
        kc_ref[0, g] = _rope_tok(kc, cc, sc, NSA_HEAD_DIM // 2, lane).astype(BF16)
        hvg = hv[:, g * CMP_HIDDEN:(g + 1) * CMP_HIDDEN].astype(BF16)
        vct_ref[0, g] = lax.dot_general(wv2t_ref[...], hvg, NT, preferred_element_type=F32).astype(BF16)


def _compress(kcmp, vcmp, ptop, pbot, wk1, wv1, wk2, wv2t, pos_c, inv_l, sign_l):
    b, s, width = kcmp.shape
    ncp = s // CMP_STRIDE

    def full(a):
        return pl.BlockSpec(a.shape, lambda bi, _n=a.ndim: (0,) * _n)

    blk = pl.BlockSpec((1, s, width), lambda bi: (bi, 0, 0))
    tab = pl.BlockSpec((ncp, LANES), lambda bi: (bi, 0))
    return pl.pallas_call(
        _compress_kernel, grid=(b,),
        in_specs=[blk, blk, full(ptop), full(pbot), full(wk1), full(wv1), full(wk2), full(wv2t), tab,
                  full(inv_l), full(sign_l)],
        out_specs=[pl.BlockSpec((1, NSA_KV_GROUPS, ncp, LANES), lambda bi: (bi, 0, 0, 0)),
                   pl.BlockSpec((1, NSA_KV_GROUPS, NSA_HEAD_DIM, ncp), lambda bi: (bi, 0, 0, 0))],
        out_shape=[jax.ShapeDtypeStruct((b, NSA_KV_GROUPS, ncp, LANES), BF16),
                   jax.ShapeDtypeStruct((b, NSA_KV_GROUPS, NSA_HEAD_DIM, ncp), BF16)],
        compiler_params=pltpu.CompilerParams(vmem_limit_bytes=VMEM_LIMIT_BYTES),
        name="compress",
    )(kcmp, vcmp, ptop, pbot, wk1, wv1, wk2, wv2t, pos_c, inv_l, sign_l)


def _nsa_kernel(q1_ref, q2_ref, kc_ref, vct_ref, kaug_ref, vts_ref, kwin_ref, vtw_ref, g_ref, z_ref, mt_ref,
                o_ref, qaug_ref, sa_ref, sb_ref, m_ref, acc_ref, tot_ref, imp_ref, rank_ref, *, tq, tk, n_sel):
    i = pl.program_id(1)
    nh, ng, hpg, dk, half = NSA_HEADS, NSA_KV_GROUPS, NSA_HPG, NSA_HEAD_DIM, NSA_HEAD_DIM // 2
    vr = dk + ONES_ROWS
    group = [h // hpg for h in range(nh)]
    for h in range(nh):
        qaug_ref[h, 0:half, :] = q1_ref[0, h * half:(h + 1) * half, :]
        qaug_ref[h, half:dk, :] = q2_ref[0, h * half:(h + 1) * half, :]

    def gate(h, branch):
        row = group[h] * GATE_ROWS + (h % hpg) * N_BRANCH + branch
        return g_ref[0, row:row + 1, :]

    row_k = lax.broadcasted_iota(I32, (tk, tq), 0)
    col_q = lax.broadcasted_iota(I32, (tk, tq), 1)
    causal = row_k <= col_q

    def reset():
        m_ref[...] = jnp.full_like(m_ref, NEG_INF)
        acc_ref[...] = jnp.zeros_like(acc_ref)

    def add_branch(branch):
        for h in range(nh):
            inv_l = 1.0 / acc_ref[h, dk:dk + 1, :]
            tot_ref[h] = tot_ref[h] + (gate(h, branch) * inv_l) * acc_ref[h, 0:dk, :]

    def q_cols(h):
        return qaug_ref[h, 0:dk, :]

    ncp = kc_ref.shape[2]
    t_row = i * tq + lax.broadcasted_iota(I32, (1, tq), 1)
    last_n = (t_row - (CMP_BLOCK - 1)) >> CMP_SHIFT
    valid = lax.broadcasted_iota(I32, (ncp, tq), 0) <= last_n
    col_ok = last_n >= 0
    p_heads = {}

    def cmp_update(h, s_c):
        s_c = jnp.where(valid, s_c, NEG_INF)
        e = jnp.exp2(s_c - jnp.max(s_c, axis=0, keepdims=True))
        l_c = jnp.sum(e, axis=0, keepdims=True)
        p_c = e * jnp.where(col_ok, 1.0 / l_c, 0.0)
        o_c = jnp.dot(vct_ref[0, group[h]], p_c.astype(BF16), preferred_element_type=F32)
        tot_ref[h] = gate(h, 0) * o_c
        p_heads[h] = p_c

    nb = mt_ref.shape[0]

    def importance(g, _):
        psum = functools.reduce(lambda a, b: a + b, [p_heads[h] for h in range(nh) if group[h] == g])
        hi = psum.astype(BF16)
        lo = (psum - hi.astype(F32)).astype(BF16)
        mt = mt_ref[...]
        imp = jnp.dot(mt, hi, preferred_element_type=F32) + jnp.dot(mt, lo, preferred_element_type=F32)
        j_idx = lax.broadcasted_iota(I32, (nb, tq), 0)
        cur = (i * tq + lax.broadcasted_iota(I32, (nb, tq), 1)) >> SLC_SHIFT
        forced = (j_idx == 0) | (j_idx == cur) | (j_idx == cur - 1)
        imp_ref[g] = jnp.where(forced, FORCED_SCORE, jnp.where(j_idx > cur, -FORCED_SCORE, imp))

    stages = []
    for g in range(ng):
        kc = kc_ref[0, g, :, 0:dk]
        stages += [(functools.partial(jnp.dot, kc, q_cols(h), preferred_element_type=F32),
                    functools.partial(cmp_update, h)) for h in range(nh) if group[h] == g]
        stages.append((None, functools.partial(importance, g)))

    reset()
    n_back = WINDOW // tk
    for back in range(n_back + 1):
        jb = jnp.maximum(i - back, 0)
        if back == 0:
            keep = causal
        elif back == n_back:
            keep = (row_k > col_q) & (i >= back)
        else:
            keep = jnp.broadcast_to(i >= back, (tk, tq))
        for g in range(ng):
            kt_b = kwin_ref[0, g, pl.ds(pl.multiple_of(jb * tk, tk), tk), 0:dk]
            vt_b = vtw_ref[0, jb, g * vr:(g + 1) * vr, :]
            for h in range(nh):
                if group[h] == g:
                    stages.append((functools.partial(jnp.dot, kt_b, q_cols(h), preferred_element_type=F32),
                                   functools.partial(_chain_update, v_t=vt_b, m_ref=m_ref, acc_ref=acc_ref,
                                                     ch=h, keep=keep)))
    _pipeline(stages, lookahead=5)
    add_branch(2)

    rank_ref[...] = jnp.zeros_like(rank_ref)
    sub = lax.broadcasted_iota(I32, (SUBLANES, tq), 0)
    last_group = ((i + 1) * (tq // SLC_BLOCK) - 1) // SUBLANES

    def count(g, c, v):
        blk = imp_ref[g, v * SUBLANES:(v + 1) * SUBLANES, :]
        cnt = rank_ref[g, v * SUBLANES:(v + 1) * SUBLANES, :]
        for rr in range(SUBLANES):
            row = imp_ref[g, c * SUBLANES + rr:c * SUBLANES + rr + 1, :]
            if c < v:
                beats = row >= blk
            elif c > v:
                beats = row > blk
            else:
                beats = (row > blk) | ((row == blk) & (sub > rr))
            cnt = cnt + beats.astype(I32)
        rank_ref[g, v * SUBLANES:(v + 1) * SUBLANES, :] = cnt

    for lvl in range(nb // SUBLANES):
        @pl.when(lvl <= last_group)
        def _(lvl=lvl):
            for g in range(ng):
                for v in range(lvl + 1):
                    count(g, lvl, v)
                for c in range(lvl):
                    count(g, c, lvl)

    for g in range(ng):
        bias = jnp.where(rank_ref[g] < n_sel, 0.0, SEL_BIAS).astype(BF16)
        for h in range(nh):
            if group[h] == g:
                qaug_ref[h, dk:dk + nb, :] = bias

    reset()
    _causal_sweep(lambda j, g: kaug_ref[0, g, pl.ds(pl.multiple_of(j * tk, tk), tk), :],
                  lambda j, g: vts_ref[0, j, g * vr:(g + 1) * vr, :],
                  group, group, qaug_ref, sa_ref, sb_ref, m_ref, acc_ref, i, causal)
    add_branch(1)

    for h in range(nh):
        zz = z_ref[0, h * dk:(h + 1) * dk, :].astype(F32)
        o_ref[0, h * dk:(h + 1) * dk, :] = (tot_ref[h] * zz).astype(BF16)


def _nsa(qt, kc, vct, kaug, vts, kwin, vtw, gt, ztn, mt, *, tq, tk, n_sel):
    b, _, s = qt.shape
    nt = s // tk
    ncp = kc.shape[2]
    ng, nh = NSA_KV_GROUPS, NSA_HEADS
    hq = nh * (NSA_HEAD_DIM // 2)
    nb = mt.shape[0]
    vr = NSA_HEAD_DIM + ONES_ROWS
    in_specs = [
        pl.BlockSpec((1, hq, tq), lambda bi, i: (bi, 0, i)),
        pl.BlockSpec((1, hq, tq), lambda bi, i: (bi, 1, i)),
        pl.BlockSpec((1, ng, ncp, LANES), lambda bi, i: (bi, 0, 0, 0)),
        pl.BlockSpec((1, ng, NSA_HEAD_DIM, ncp), lambda bi, i: (bi, 0, 0, 0)),
        pl.BlockSpec((1, ng, s, LANES), lambda bi, i: (bi, 0, 0, 0)),
        pl.BlockSpec((1, nt, ng * vr, tk), lambda bi, i: (bi, 0, 0, 0)),
        pl.BlockSpec((1, ng, s, LANES), lambda bi, i: (bi, 0, 0, 0)),
        pl.BlockSpec((1, nt, ng * vr, tk), lambda bi, i: (bi, 0, 0, 0)),
        pl.BlockSpec((1, ng * GATE_ROWS, tq), lambda bi, i: (bi, 0, i)),
        pl.BlockSpec((1, NSA_WIDTH, tq), lambda bi, i: (bi, 0, i)),
        pl.BlockSpec(mt.shape, lambda bi, i: (0, 0)),
    ]
    kern = functools.partial(_nsa_kernel, tq=tq, tk=tk, n_sel=n_sel)
    return pl.pallas_call(
        kern, grid=(b, s // tq), in_specs=in_specs,
        out_specs=pl.BlockSpec((1, NSA_WIDTH, tq), lambda bi, i: (bi, 0, i)),
        out_shape=jax.ShapeDtypeStruct((b, NSA_WIDTH, s), BF16),
        scratch_shapes=[pltpu.VMEM((nh, NSA_HEAD_DIM + nb, tq), BF16),
                        pltpu.VMEM((nh, tk, tq), F32), pltpu.VMEM((nh, tk, tq), F32),
                        pltpu.VMEM((nh, 1, tq), F32), pltpu.VMEM((nh, vr, tq), F32),
                        pltpu.VMEM((nh, NSA_HEAD_DIM, tq), F32),
                        pltpu.VMEM((ng, nb, tq), F32), pltpu.VMEM((ng, nb, tq), I32)],
        compiler_params=pltpu.CompilerParams(dimension_semantics=("arbitrary", "arbitrary"),
                                             vmem_limit_bytes=VMEM_LIMIT_BYTES),
        name="nsa",
    )(qt, qt, kc, vct, kaug, vts, kwin, vtw, gt, ztn, mt)


def _mla_kernel(q_ref, qn_ref, k_ref, vt_ref, z_ref, o_ref, sa_ref, sb_ref, m_ref, acc_ref, *, tq, tk):
    i = pl.program_id(1)
    dv, vr = MLA_V_DIM, MLA_V_DIM + ONES_ROWS
    m_ref[...] = jnp.full_like(m_ref, NEG_INF)
    acc_ref[...] = jnp.zeros_like(acc_ref)

    causal = lax.broadcasted_iota(I32, (tk, tq), 0) <= lax.broadcasted_iota(I32, (tk, tq), 1)
    heads = list(range(MLA_HEADS))
    _causal_sweep(lambda j, hd: k_ref[0, hd, pl.ds(pl.multiple_of(j * tk, tk), tk), :],
                  lambda j, hd: vt_ref[0, j, hd * vr:(hd + 1) * vr, :],
                  heads, heads, q_ref.at[0], sa_ref, sb_ref, m_ref, acc_ref, i, causal,
                  q_next_ref=qn_ref.at[0], first=i == 0)
    for hd in range(MLA_HEADS):
        o_h = acc_ref[hd, 0:dv, :] * (1.0 / acc_ref[hd, dv:dv + 1, :])
        zz = z_ref[0, hd * dv:(hd + 1) * dv, :].astype(F32)
        o_ref[0, hd * dv:(hd + 1) * dv, :] = (o_h * zz).astype(BF16)


def _mla(qtm, kmla, vtm, ztm, *, tq, tk):
    b, _, s, _ = kmla.shape
    nt = s // tk
    vrows = MLA_HEADS * (MLA_V_DIM + ONES_ROWS)
    kern = functools.partial(_mla_kernel, tq=tq, tk=tk)
    return pl.pallas_call(
        kern, grid=(b, s // tq),
        in_specs=[pl.BlockSpec((1, MLA_HEADS, MLA_QK, tq), lambda bi, i: (bi, 0, 0, i)),
                  pl.BlockSpec((1, MLA_HEADS, MLA_QK, tq), lambda bi, i: (bi, 0, 0, jnp.minimum(i + 1, s // tq - 1))),
                  pl.BlockSpec((1, MLA_HEADS, s, MLA_QK), lambda bi, i: (bi, 0, 0, 0)),
                  pl.BlockSpec((1, nt, vrows, tk), lambda bi, i: (bi, 0, 0, 0)),
                  pl.BlockSpec((1, MLA_WIDTH, tq), lambda bi, i: (bi, 0, i))],
        out_specs=pl.BlockSpec((1, MLA_WIDTH, tq), lambda bi, i: (bi, 0, i)),
        out_shape=jax.ShapeDtypeStruct((b, MLA_WIDTH, s), BF16),
        scratch_shapes=[pltpu.VMEM((MLA_HEADS, tk, tq), F32), pltpu.VMEM((MLA_HEADS, tk, tq), F32),
                        pltpu.VMEM((MLA_HEADS, 1, tq), F32), pltpu.VMEM((MLA_HEADS, MLA_V_DIM + ONES_ROWS, tq), F32)],
        compiler_params=pltpu.CompilerParams(dimension_semantics=("arbitrary", "arbitrary"),
                                             vmem_limit_bytes=VMEM_LIMIT_BYTES),
        name="mla",
    )(qtm, qtm, kmla, vtm, ztm)


def _out_kernel(x_ref, mn_ref, mm_ref, w_ref, mod_ref, fg_ref, o_ref, *, final):
    y = lax.dot_general(mn_ref[0], w_ref[0:NSA_WIDTH, :], TN, preferred_element_type=F32)
    y = y + lax.dot_general(mm_ref[0], w_ref[NSA_WIDTH:MIX_WIDTH, :], TN, preferred_element_type=F32)
    x2 = x_ref[0] + mod_ref[0][2:3] * y
    o_ref[0] = _rms(x2, fg_ref[...]) if final else x2


def _out(x, mn, mm, w_out, mod3, fg, *, tm, final):
    b, s, d = x.shape
    return pl.pallas_call(
        functools.partial(_out_kernel, final=final), grid=(b, s // tm),
        in_specs=[pl.BlockSpec((1, tm, d), lambda bi, i: (bi, i, 0)),
                  pl.BlockSpec((1, NSA_WIDTH, tm), lambda bi, i: (bi, 0, i)),
                  pl.BlockSpec((1, MLA_WIDTH, tm), lambda bi, i: (bi, 0, i)),
                  pl.BlockSpec(w_out.shape, lambda bi, i: (0, 0)),
                  pl.BlockSpec((1, 3, d), lambda bi, i: (bi, 0, 0)),
                  pl.BlockSpec((1, d), lambda bi, i: (0, 0))],
        out_specs=pl.BlockSpec((1, tm, d), lambda bi, i: (bi, i, 0)),
        out_shape=jax.ShapeDtypeStruct((b, s, d), F32),
        compiler_params=pltpu.CompilerParams(vmem_limit_bytes=VMEM_LIMIT_BYTES),
        name="out_proj",
    )(x, mn, mm, w_out, mod3, fg)


def _cmp_to_slc_t(ncp, nc, nslc, nb):
    start = np.arange(nc)[:, None] * CMP_STRIDE
    bstart = np.arange(nslc)[None, :] * SLC_BLOCK
    ov = np.minimum(start + CMP_BLOCK, bstart + SLC_BLOCK) - np.maximum(start, bstart)
    m = (np.clip(ov, 0, None) / CMP_BLOCK).astype(np.float32)
    out = np.zeros((nb, ncp), np.float32)
    out[:nslc, :nc] = m.T
    return out


def _layout_w_in(w):
    d = w.shape[0]
    (q_n, kc_n, vc_n, ks_n, vs_n, kw_n, vw_n, gl_n, z_n, cq_m, ckv_m, kr_m, z_m) = jnp.split(w, IN_OFFSETS, axis=-1)
    dk = NSA_HEAD_DIM
    z64 = jnp.zeros((d, LANES - dk), w.dtype)
    wtok = jnp.concatenate(
        [ks_n[:, :dk], z64, ks_n[:, dk:], z64, kw_n[:, :dk], z64, kw_n[:, dk:], z64,
         kc_n, vc_n, cq_m, ckv_m, jnp.zeros((d, KR_LANE), w.dtype), kr_m,
         jnp.zeros((d, LANES - KR_LANE - MLA_ROPE_DIM), w.dtype)], axis=1)
    qr = q_n.reshape(d, NSA_HEADS, 2, dk // 2)
    q_perm = jnp.concatenate([qr[:, :, 0, :].reshape(d, -1), qr[:, :, 1, :].reshape(d, -1)], axis=1)
    gl = gl_n.reshape(d, NSA_KV_GROUPS, NSA_HPG * N_BRANCH)
    gl = jnp.pad(gl, ((0, 0), (0, 0), (0, GATE_ROWS - NSA_HPG * N_BRANCH))).reshape(d, -1)
    wtr = jnp.concatenate([q_perm, vs_n, vw_n, gl, z_n, z_m], axis=1).T
    assert wtok.shape[1] == TOK_COLS and wtr.shape[0] == TR_ROWS
    return wtok.astype(BF16), wtr.astype(BF16)


def _layout_w1(w1):
    hid = w1.shape[1]
    w1r = w1.reshape(2, CMP_STRIDE, NSA_HEAD_DIM, hid)
    eye = jnp.eye(NSA_KV_GROUPS, dtype=w1.dtype)
    halves = [jnp.einsum('ldh,pg->lpdgh', w1r[k], eye).reshape(CMP_STRIDE * NSA_KV_WIDTH, NSA_KV_GROUPS * hid)
              for k in range(2)]
    return jnp.concatenate(halves, axis=1).astype(BF16)


def kernel(x, c, positions, ada_w, ada_b, norm_g, w_in, cmp_pos, cmp_k_w1, cmp_k_w2, cmp_v_w1, cmp_v_w2,
           q_norm_g, w_q_up, kv_norm_g, w_kv_up, w_out, final_norm_g):
    b, s, d = x.shape
    depth = ada_w.shape[0]
    tm, tq = PROJ_TILE, ATT_TILE
    tk = tq
    assert s % tm == 0 and tm % tk == 0 and WINDOW % tk == 0 and (tq & (tq - 1)) == 0 and s % OUT_TILE == 0
    assert CMP_BLOCK == 2 * CMP_STRIDE and s % SLC_BLOCK == 0
    nslc = s // SLC_BLOCK
    nb = LANES - NSA_HEAD_DIM
    assert nslc <= nb
    ncp = s // CMP_STRIDE
    nc = ncp - 1

    half_n, half_m = NSA_HEAD_DIM // 2, MLA_ROPE_DIM // 2
    inv_n = ROPE_THETA ** (-jnp.arange(half_n, dtype=F32) / half_n)
    inv_m = ROPE_THETA ** (-jnp.arange(half_m, dtype=F32) / half_m)
    ones_n, ones_m = jnp.ones((half_n,), F32), jnp.ones((half_m,), F32)
    pad = jnp.zeros((LANES - NSA_HEAD_DIM - MLA_ROPE_DIM,), F32)
    inv_l = jnp.concatenate([inv_n, inv_n, inv_m, inv_m, pad])[None]
    sign_l = jnp.concatenate([-ones_n, ones_n, -ones_m, ones_m, pad])[None]
    pos_f = positions.astype(F32)
    pos_b = jnp.broadcast_to(pos_f.reshape(b * s, 1), (b * s, LANES))
    pos_row = pos_f.reshape(b, 1, s)
    cmp_end = np.minimum(np.arange(ncp) * CMP_STRIDE + CMP_BLOCK - 1, s - 1)
    pos_c = jnp.broadcast_to(pos_f[:, cmp_end].reshape(b * ncp, 1), (b * ncp, LANES))
    inv_nb = jnp.broadcast_to(inv_n[:, None], (half_n, tm))
    inv_mb = jnp.broadcast_to(inv_m[:, None], (half_m, tm))

    mt = jnp.asarray(_cmp_to_slc_t(ncp, nc, nslc, nb), dtype=BF16)
    bp = -(-b // SUBLANES) * SUBLANES
    c_pad = jnp.pad(c, ((0, bp - b), (0, 0)))

    for l in range(depth):
        mod = _adaln(c_pad, ada_w[l], ada_b[l].reshape(1, -1))
        mod3 = mod[:b].reshape(b, 3, d)
        wtok, wtr = _layout_w_in(w_in[l])
        wq = w_q_up[l].reshape(MLA_Q_RANK, MLA_HEADS, MLA_NOPE_DIM + MLA_ROPE_DIM)
        wqt = jnp.concatenate([wq[:, :, :MLA_NOPE_DIM].reshape(MLA_Q_RANK, -1),
                               wq[:, :, MLA_NOPE_DIM:MLA_NOPE_DIM + half_m].reshape(MLA_Q_RANK, -1),
                               wq[:, :, MLA_NOPE_DIM + half_m:].reshape(MLA_Q_RANK, -1)], axis=1).T.astype(BF16)
        wkv = w_kv_up[l].reshape(MLA_KV_RANK, MLA_HEADS, MLA_NOPE_DIM + MLA_V_DIM)
        wkn = jnp.pad(wkv[:, :, :MLA_NOPE_DIM], ((0, 0), (0, 0), (0, LANES - MLA_NOPE_DIM))).reshape(
            MLA_KV_RANK, MLA_HEADS * LANES).astype(BF16)
        wv = wkv[:, :, MLA_NOPE_DIM:].transpose(1, 2, 0).reshape(MLA_WIDTH, MLA_KV_RANK).astype(BF16)

        (qt, kaug, kwin, vts, vtw, kcmp, vcmp, gt, ztn, ztm, qtm, kmla, vtm) = _proj(
            x, mod3, norm_g[l].reshape(1, d), wtok, wtr, pos_b, pos_row, inv_l, sign_l, inv_nb, inv_mb,
            q_norm_g[l].reshape(1, -1), kv_norm_g[l].reshape(1, -1), wqt, wkn, wv, tm=tm, tk=tk)

        pos_l = cmp_pos[l]
        ptop = jnp.broadcast_to(pos_l[:CMP_STRIDE, None, :], (CMP_STRIDE, NSA_KV_GROUPS, NSA_HEAD_DIM)).reshape(1, -1)
        pbot = jnp.broadcast_to(pos_l[CMP_STRIDE:, None, :], (CMP_STRIDE, NSA_KV_GROUPS, NSA_HEAD_DIM)).reshape(1, -1)
        wk2 = jnp.pad(cmp_k_w2[l], ((0, 0), (0, LANES - NSA_HEAD_DIM))).astype(BF16)
        kc, vct = _compress(kcmp, vcmp, ptop, pbot,
                            _layout_w1(cmp_k_w1[l]), _layout_w1(cmp_v_w1[l]), wk2,
                            cmp_v_w2[l].T.astype(BF16), pos_c, inv_l, sign_l)

        mix_n = _nsa(qt, kc, vct, kaug, vts, kwin, vtw, gt, ztn, mt, tq=tq, tk=tk, n_sel=min(SLC_TOPK, nslc))
        mix_m = _mla(qtm, kmla, vtm, ztm, tq=tq, tk=tk)
        x = _out(x, mix_n, mix_m, w_out[l].astype(BF16), mod3, final_norm_g.reshape(1, d), tm=OUT_TILE,
                 final=(l == depth - 1))
    return x
```

```python
import functools

import numpy as np
import jax
import jax.numpy as jnp
from jax import lax
from jax.experimental import pallas as pl
from jax.experimental.pallas import tpu as pltpu

F32 = jnp.float32
BF16 = jnp.bfloat16
I32 = jnp.int32

NSA_HEADS = 8
NSA_KV_GROUPS = 2
NSA_HPG = NSA_HEADS // NSA_KV_GROUPS
NSA_HEAD_DIM = 64
NSA_WIDTH = NSA_HEADS * NSA_HEAD_DIM
NSA_KV_WIDTH = NSA_KV_GROUPS * NSA_HEAD_DIM
CMP_BLOCK = 32
CMP_STRIDE = 16
CMP_HIDDEN = 128
SLC_BLOCK = 64
SLC_TOPK = 16
WINDOW = 512
N_BRANCH = 3
FORCED_SCORE = 1.0e4
MLA_HEADS = 8
MLA_NOPE_DIM = 64
MLA_ROPE_DIM = 32
MLA_V_DIM = 64
MLA_WIDTH = MLA_HEADS * MLA_V_DIM
MLA_Q_RANK = 256
MLA_KV_RANK = 128
MIX_WIDTH = NSA_WIDTH + MLA_WIDTH
ROPE_THETA = 10000.0
NORM_EPS = 1e-6
NEG_INF = -1e30
IN_SIZES = (NSA_WIDTH, NSA_KV_WIDTH, NSA_KV_WIDTH, NSA_KV_WIDTH, NSA_KV_WIDTH, NSA_KV_WIDTH, NSA_KV_WIDTH,
            NSA_HEADS * N_BRANCH, NSA_WIDTH, MLA_Q_RANK, MLA_KV_RANK, MLA_ROPE_DIM, MLA_WIDTH)
IN_OFFSETS = tuple(int(o) for o in np.cumsum(IN_SIZES)[:-1])

LANES = 128
SUBLANES = 8
VMEM_LIMIT_BYTES = 56 * 1024 * 1024

PROJ_TILE = 1024
OUT_TILE = 1024
ATT_TILE = 256
SEL_BIAS = NEG_INF
GATE_ROWS = 16
ONES_ROWS = 16
MLA_QK = LANES
LOG2E = float(np.log2(np.e))
SWEEP_UNROLL = 4
SLC_SHIFT = SLC_BLOCK.bit_length() - 1
CMP_SHIFT = CMP_STRIDE.bit_length() - 1
assert 1 << SLC_SHIFT == SLC_BLOCK and 1 << CMP_SHIFT == CMP_STRIDE

NT = (((1,), (1,)), ((), ()))
TN = (((0,), (0,)), ((), ()))


def _silu(v):
    return v * jax.nn.sigmoid(v)


def _rms(v, g):
    ms = jnp.mean(v * v, axis=-1, keepdims=True)
    return v * lax.rsqrt(ms + NORM_EPS) * g


def _rope_tok(v, c, s_signed, half, lane, base=0):
    up = pltpu.roll(v, LANES - half, axis=1)
    dn = pltpu.roll(v, half, axis=1)
    return v * c + jnp.where(lane < base + half, up, dn) * s_signed


def _rope_lanes(pos_b, inv_ref, sign_ref):
    ang = pos_b * inv_ref[...]
    return jnp.cos(ang), jnp.sin(ang) * sign_ref[...]


def _chain_update(s_t, v_t, m_ref, acc_ref, ch, keep=None):
    if keep is not None:
        s_t = jnp.where(keep, s_t, NEG_INF)
    m_prev = m_ref[ch]
    m_new = jnp.maximum(m_prev, jnp.max(s_t, axis=0, keepdims=True))
    alpha = jnp.exp2(m_prev - m_new)
    p = jnp.exp2(s_t - m_new)
    acc_ref[ch] = alpha * acc_ref[ch] + jnp.dot(v_t, p.astype(BF16), preferred_element_type=F32)
    m_ref[ch] = m_new


def _causal_sweep(k_tile, v_tile, k_group, v_group, q_ref, sa_ref, sb_ref, m_ref, acc_ref, last, keep_last,
                  lookahead=2, q_next_ref=None, first=None):
    n_chains = len(k_group)

    def loader(tile_fn, j):
        cache = {}
        return lambda g: cache.setdefault(g, tile_fn(j, g))

    def phase(j_next, s_next_ref, s_cur_ref, j_cur, keep=None, ahead=False):
        k_next = loader(k_tile, j_next) if j_next is not None else None
        k_zero = loader(k_tile, 0) if ahead else None
        v_cur = loader(v_tile, j_cur)
        for n in range(n_chains + lookahead):
            if k_next is not None and n < n_chains:
                s_next_ref[n] = jnp.dot(k_next(k_group[n]), q_ref[n], preferred_element_type=F32)
            if n >= lookahead:
                ch = n - lookahead
                _chain_update(s_cur_ref[ch], v_cur(v_group[ch]), m_ref, acc_ref, ch, keep=keep)
                if ahead:
                    sa_ref[ch] = jnp.dot(k_zero(k_group[ch]), q_next_ref[ch], preferred_element_type=F32)

    bufs = (sa_ref, sb_ref)

    def prologue():
        k_0 = loader(k_tile, 0)
        for ch in range(n_chains):
            sa_ref[ch] = jnp.dot(k_0(k_group[ch]), q_ref[ch], preferred_element_type=F32)

    if q_next_ref is None:
        prologue()
    else:
        pl.when(first)(prologue)

    def run(j0, count):
        for u in range(count):
            phase(j0 + u + 1, bufs[(u + 1) % 2], bufs[u % 2], j0 + u)

    def body(jj, carry):
        run(SWEEP_UNROLL * jj, SWEEP_UNROLL)
        return carry

    lax.fori_loop(0, last // SWEEP_UNROLL, body, 0)
    rem = last % SWEEP_UNROLL
    for r in range(SWEEP_UNROLL):
        @pl.when(rem == r)
        def _(r=r):
            run(last - r, r)
            phase(None, None, bufs[r % 2], last, keep=keep_last, ahead=q_next_ref is not None)


def _pipeline(stages, lookahead):
    pending = {}
    for n in range(len(stages) + lookahead):
        if n < len(stages) and stages[n][0] is not None:
            pending[n] = stages[n][0]()
        if n >= lookahead:
            stages[n - lookahead][1](pending.pop(n - lookahead, None))


def _adaln_kernel(c_ref, w_ref, b_ref, o_ref):
    sc = _silu(c_ref[...])
    o_ref[...] = jnp.dot(sc.astype(BF16), w_ref[...].astype(BF16), preferred_element_type=F32) + b_ref[...]


def _adaln(c_pad, w, b):
    bp, d = c_pad.shape
    n = w.shape[1] // d
    return pl.pallas_call(
        _adaln_kernel,
        grid=(n,),
        in_specs=[pl.BlockSpec((bp, d), lambda j: (0, 0)),
                  pl.BlockSpec((d, d), lambda j: (0, j)),
                  pl.BlockSpec((1, d), lambda j: (0, j))],
        out_specs=pl.BlockSpec((bp, d), lambda j: (0, j)),
        out_shape=jax.ShapeDtypeStruct((bp, n * d), F32),
        name="adaln",
    )(c_pad, w, b)


TOK_KS, TOK_KW, TOK_KC, TOK_VC, TOK_CQ, TOK_CKV, TOK_KR, TOK_COLS = (int(o) for o in np.cumsum(
    [0, NSA_KV_GROUPS * LANES, NSA_KV_GROUPS * LANES, NSA_KV_WIDTH, NSA_KV_WIDTH, MLA_Q_RANK, MLA_KV_RANK, LANES]))
KR_LANE = NSA_HEAD_DIM
TR_Q, TR_VS, TR_VW, TR_G, TR_ZN, TR_ZM, TR_ROWS = (int(o) for o in np.cumsum(
    [0, NSA_WIDTH, NSA_KV_WIDTH, NSA_KV_WIDTH, NSA_KV_GROUPS * GATE_ROWS, NSA_WIDTH, MLA_WIDTH]))


def _proj_kernel(x_ref, mod_ref, ng_ref, wtok_ref, wtr_ref, posb_ref, posr_ref, invl_ref, signl_ref,
                 invn_ref, invm_ref, qng_ref, kvng_ref, wqt_ref, wkn_ref, wv_ref,
                 qt_ref, kaug_ref, kwin_ref, vts_ref, vtw_ref, kcmp_ref, vcmp_ref, gt_ref,
                 ztn_ref, ztm_ref, qtm_ref, kmla_ref, vtm_ref, *, tm, tk, scale_nsa, scale_mla):
    i = pl.program_id(1)
    mod = mod_ref[0]
    h = _rms(x_ref[0], ng_ref[...]) * (1.0 + mod[1:2]) + mod[0:1]
    hb = h.astype(BF16)
    tok = jnp.dot(hb, wtok_ref[...], preferred_element_type=F32)
    tr = lax.dot_general(wtr_ref[...], hb, NT, preferred_element_type=F32)

    lane = lax.broadcasted_iota(I32, (tm, LANES), 1)
    row = lax.broadcasted_iota(I32, (tm, LANES), 0)
    blk = (i * tm + row) >> SLC_SHIFT
    onehot = (lane - NSA_HEAD_DIM == blk).astype(F32)
    ct, st = _rope_lanes(posb_ref[...], invl_ref, signl_ref)
    half_n = NSA_HEAD_DIM // 2
    for g in range(NSA_KV_GROUPS):
        ks = _rope_tok(tok[:, TOK_KS + LANES * g:TOK_KS + LANES * (g + 1)], ct, st, half_n, lane)
        kaug_ref[0, g] = jnp.where(lane >= NSA_HEAD_DIM, onehot, ks).astype(BF16)
        kw = _rope_tok(tok[:, TOK_KW + LANES * g:TOK_KW + LANES * (g + 1)], ct, st, half_n, lane)
        kwin_ref[0, g] = kw.astype(BF16)
    kcmp_ref[0] = tok[:, TOK_KC:TOK_KC + LANES]
    vcmp_ref[0] = tok[:, TOK_VC:TOK_VC + LANES]

    ckvn = _rms(tok[:, TOK_CKV:TOK_CKV + MLA_KV_RANK], kvng_ref[...])
    krr = _rope_tok(tok[:, TOK_KR:TOK_KR + LANES], ct, st, MLA_ROPE_DIM // 2, lane, base=KR_LANE)
    ckvb = ckvn.astype(BF16)
    kn = jnp.dot(ckvb, wkn_ref[...], preferred_element_type=F32)
    is_rot = (lane >= KR_LANE) & (lane < KR_LANE + MLA_ROPE_DIM)
    for hd in range(MLA_HEADS):
        kmla_ref[0, hd] = jnp.where(is_rot, krr, kn[:, hd * LANES:(hd + 1) * LANES]).astype(BF16)
    vtm = lax.dot_general(wv_ref[...], ckvb, NT, preferred_element_type=F32).astype(BF16)
    ones = jnp.ones((ONES_ROWS, tk), BF16)
    vr = MLA_V_DIM + ONES_ROWS
    for ii in range(tm // tk):
        for hd in range(MLA_HEADS):
            vtm_ref[0, ii, hd * vr:hd * vr + MLA_V_DIM, :] = vtm[hd * MLA_V_DIM:(hd + 1) * MLA_V_DIM,
                                                                 ii * tk:(ii + 1) * tk]
            vtm_ref[0, ii, hd * vr + MLA_V_DIM:(hd + 1) * vr, :] = ones

    cqn = _rms(tok[:, TOK_CQ:TOK_CQ + MLA_Q_RANK], qng_ref[...]).astype(BF16)
    qm = lax.dot_general(wqt_ref[...], cqn, NT, preferred_element_type=F32)
    nq = MLA_HEADS * MLA_NOPE_DIM
    hr = MLA_ROPE_DIM // 2
    x1 = qm[nq:nq + MLA_HEADS * hr].reshape(MLA_HEADS, hr, tm)
    x2 = qm[nq + MLA_HEADS * hr:nq + 2 * MLA_HEADS * hr].reshape(MLA_HEADS, hr, tm)
    ang_m = invm_ref[...] * posr_ref[0]
    cm_t, sm_t = jnp.cos(ang_m)[None], jnp.sin(ang_m)[None]
    o1 = (x1 * cm_t - x2 * sm_t) * scale_mla
    o2 = (x2 * cm_t + x1 * sm_t) * scale_mla
    nd = MLA_NOPE_DIM
    q_pad = jnp.zeros((MLA_QK - nd - MLA_ROPE_DIM, tm), BF16)
    for hd in range(MLA_HEADS):
        qtm_ref[0, hd, 0:nd, :] = (qm[hd * nd:(hd + 1) * nd] * scale_mla).astype(BF16)
        qtm_ref[0, hd, nd:nd + hr, :] = o1[hd].astype(BF16)
        qtm_ref[0, hd, nd + hr:nd + 2 * hr, :] = o2[hd].astype(BF16)
        qtm_ref[0, hd, nd + 2 * hr:MLA_QK, :] = q_pad

    hq = NSA_HEADS * half_n
    q1 = tr[TR_Q:TR_Q + hq].reshape(NSA_HEADS, half_n, tm)
    q2 = tr[TR_Q + hq:TR_Q + 2 * hq].reshape(NSA_HEADS, half_n, tm)
    ang_n = invn_ref[...] * posr_ref[0]
    cn_t, sn_t = jnp.cos(ang_n)[None], jnp.sin(ang_n)[None]
    qt_ref[0, 0:hq, :] = ((q1 * cn_t - q2 * sn_t) * scale_nsa).reshape(hq, tm).astype(BF16)
    qt_ref[0, hq:2 * hq, :] = ((q2 * cn_t + q1 * sn_t) * scale_nsa).reshape(hq, tm).astype(BF16)

    vts = tr[TR_VS:TR_VS + NSA_KV_WIDTH].astype(BF16)
    vtw = tr[TR_VW:TR_VW + NSA_KV_WIDTH].astype(BF16)
    dk = NSA_HEAD_DIM
    gr = dk + ONES_ROWS
    for ii in range(tm // tk):
        for g in range(NSA_KV_GROUPS):
            vts_ref[0, ii, g * gr:g * gr + dk, :] = vts[g * dk:(g + 1) * dk, ii * tk:(ii + 1) * tk]
            vtw_ref[0, ii, g * gr:g * gr + dk, :] = vtw[g * dk:(g + 1) * dk, ii * tk:(ii + 1) * tk]
            vts_ref[0, ii, g * gr + dk:(g + 1) * gr, :] = ones
            vtw_ref[0, ii, g * gr + dk:(g + 1) * gr, :] = ones
    gt_ref[0] = jax.nn.sigmoid(tr[TR_G:TR_G + NSA_KV_GROUPS * GATE_ROWS])
    ztn_ref[0] = _silu(tr[TR_ZN:TR_ZN + NSA_WIDTH]).astype(BF16)
    ztm_ref[0] = _silu(tr[TR_ZM:TR_ZM + MLA_WIDTH]).astype(BF16)


def _proj(x, mod3, ng, wtok, wtr, pos_b, pos_row, inv_l, sign_l, inv_nb, inv_mb, qng, kvng, wqt, wkn, wv, *, tm, tk):
    b, s, d = x.shape
    nt = s // tk
    v_rows_n = NSA_KV_GROUPS * (NSA_HEAD_DIM + ONES_ROWS)
    v_rows_m = MLA_HEADS * (MLA_V_DIM + ONES_ROWS)
    tile_tok = pl.BlockSpec((tm, LANES), lambda bi, i: (bi * (s // tm) + i, 0))

    def full(a):
        return pl.BlockSpec(a.shape, lambda bi, i, _n=a.ndim: (0,) * _n)

    in_specs = [pl.BlockSpec((1, tm, d), lambda bi, i: (bi, i, 0)),
                pl.BlockSpec((1, 3, d), lambda bi, i: (bi, 0, 0)),
                full(ng), full(wtok), full(wtr),
                tile_tok, pl.BlockSpec((1, 1, tm), lambda bi, i: (bi, 0, i)),
                full(inv_l), full(sign_l), full(inv_nb), full(inv_mb),
                full(qng), full(kvng), full(wqt), full(wkn), full(wv)]
    out_shape = [
        jax.ShapeDtypeStruct((b, NSA_WIDTH, s), BF16),
        jax.ShapeDtypeStruct((b, NSA_KV_GROUPS, s, LANES), BF16),
        jax.ShapeDtypeStruct((b, NSA_KV_GROUPS, s, LANES), BF16),
        jax.ShapeDtypeStruct((b, nt, v_rows_n, tk), BF16),
        jax.ShapeDtypeStruct((b, nt, v_rows_n, tk), BF16),
        jax.ShapeDtypeStruct((b, s, NSA_KV_WIDTH), F32),
        jax.ShapeDtypeStruct((b, s, NSA_KV_WIDTH), F32),
        jax.ShapeDtypeStruct((b, NSA_KV_GROUPS * GATE_ROWS, s), F32),
        jax.ShapeDtypeStruct((b, NSA_WIDTH, s), BF16),
        jax.ShapeDtypeStruct((b, MLA_WIDTH, s), BF16),
        jax.ShapeDtypeStruct((b, MLA_HEADS, MLA_QK, s), BF16),
        jax.ShapeDtypeStruct((b, MLA_HEADS, s, MLA_QK), BF16),
        jax.ShapeDtypeStruct((b, nt, v_rows_m, tk), BF16),
    ]
    out_specs = [
        pl.BlockSpec((1, NSA_WIDTH, tm), lambda bi, i: (bi, 0, i)),
        pl.BlockSpec((1, NSA_KV_GROUPS, tm, LANES), lambda bi, i: (bi, 0, i, 0)),
        pl.BlockSpec((1, NSA_KV_GROUPS, tm, LANES), lambda bi, i: (bi, 0, i, 0)),
        pl.BlockSpec((1, tm // tk, v_rows_n, tk), lambda bi, i: (bi, i, 0, 0)),
        pl.BlockSpec((1, tm // tk, v_rows_n, tk), lambda bi, i: (bi, i, 0, 0)),
        pl.BlockSpec((1, tm, NSA_KV_WIDTH), lambda bi, i: (bi, i, 0)),
        pl.BlockSpec((1, tm, NSA_KV_WIDTH), lambda bi, i: (bi, i, 0)),
        pl.BlockSpec((1, NSA_KV_GROUPS * GATE_ROWS, tm), lambda bi, i: (bi, 0, i)),
        pl.BlockSpec((1, NSA_WIDTH, tm), lambda bi, i: (bi, 0, i)),
        pl.BlockSpec((1, MLA_WIDTH, tm), lambda bi, i: (bi, 0, i)),
        pl.BlockSpec((1, MLA_HEADS, MLA_QK, tm), lambda bi, i: (bi, 0, 0, i)),
        pl.BlockSpec((1, MLA_HEADS, tm, MLA_QK), lambda bi, i: (bi, 0, i, 0)),
        pl.BlockSpec((1, tm // tk, v_rows_m, tk), lambda bi, i: (bi, i, 0, 0)),
    ]
    kern = functools.partial(_proj_kernel, tm=tm, tk=tk, scale_nsa=NSA_HEAD_DIM ** -0.5 * LOG2E,
                             scale_mla=(MLA_NOPE_DIM + MLA_ROPE_DIM) ** -0.5 * LOG2E)
    return pl.pallas_call(
        kern, grid=(b, s // tm), in_specs=in_specs, out_specs=out_specs, out_shape=out_shape,
        compiler_params=pltpu.CompilerParams(vmem_limit_bytes=VMEM_LIMIT_BYTES),
        name="proj",
    )(x, mod3, ng, wtok, wtr, pos_b, pos_row, inv_l, sign_l, inv_nb, inv_mb, qng, kvng, wqt, wkn, wv)


def _compress_kernel(k_ref, v_ref, ptop_ref, pbot_ref, wk1_ref, wv1_ref, wk2_ref, wv2t_ref, posc_ref, invl_ref,
                     signl_ref, kc_ref, vct_ref):
    ncp = k_ref.shape[1] // CMP_STRIDE
    gw = NSA_KV_GROUPS * CMP_HIDDEN
    lane = lax.broadcasted_iota(I32, (ncp, LANES), 1)

    def hidden(r_ref, w1_ref):
        r = jnp.concatenate([r_ref[0, pl.ds(t, ncp, stride=CMP_STRIDE), :] for t in range(CMP_STRIDE)], axis=1)
        a = jnp.dot((r + ptop_ref[...]).astype(BF16), w1_ref[:, 0:gw], preferred_element_type=F32)
        bt = jnp.dot((r + pbot_ref[...]).astype(BF16), w1_ref[:, gw:2 * gw], preferred_element_type=F32)
        return _silu(a + pltpu.roll(bt, ncp - 1, axis=0))

    cc, sc = _rope_lanes(posc_ref[...], invl_ref, signl_ref)
    hk = hidden(k_ref, wk1_ref)
    hv = hidden(v_ref, wv1_ref)
    for g in range(NSA_KV_GROUPS):
        hkg = hk[:, g * CMP_HIDDEN:(g + 1) * CMP_HIDDEN].astype(BF16)
        kc = jnp.dot(hkg, wk2_ref[...], preferred_element_type=F32)
        kc_ref[0, g] = _rope_tok(kc, cc, sc, NSA_HEAD_DIM // 2, lane).astype(BF16)
        hvg = hv[:, g * CMP_HIDDEN:(g + 1) * CMP_HIDDEN].astype(BF16)
        vct_ref[0, g] = lax.dot_general(wv2t_ref[...], hvg, NT, preferred_element_type=F32).astype(BF16)


def _compress(kcmp, vcmp, ptop, pbot, wk1, wv1, wk2, wv2t, pos_c, inv_l, sign_l):
    b, s, width = kcmp.shape
    ncp = s // CMP_STRIDE

    def full(a):
        return pl.BlockSpec(a.shape, lambda bi, _n=a.ndim: (0,) * _n)

    blk = pl.BlockSpec((1, s, width), lambda bi: (bi, 0, 0))
    tab = pl.BlockSpec((ncp, LANES), lambda bi: (bi, 0))
    return pl.pallas_call(
        _compress_kernel, grid=(b,),
        in_specs=[blk, blk, full(ptop), full(pbot), full(wk1), full(wv1), full(wk2), full(wv2t), tab,
                  full(inv_l), full(sign_l)],
        out_specs=[pl.BlockSpec((1, NSA_KV_GROUPS, ncp, LANES), lambda bi: (bi, 0, 0, 0)),
                   pl.BlockSpec((1, NSA_KV_GROUPS, NSA_HEAD_DIM, ncp), lambda bi: (bi, 0, 0, 0))],
        out_shape=[jax.ShapeDtypeStruct((b, NSA_KV_GROUPS, ncp, LANES), BF16),
                   jax.ShapeDtypeStruct((b, NSA_KV_GROUPS, NSA_HEAD_DIM, ncp), BF16)],
        compiler_params=pltpu.CompilerParams(vmem_limit_bytes=VMEM_LIMIT_BYTES),
        name="compress",
    )(kcmp, vcmp, ptop, pbot, wk1, wv1, wk2, wv2t, pos_c, inv_l, sign_l)


def _nsa_kernel(q1_ref, q2_ref, kc_ref, vct_ref, kaug_ref, vts_ref, kwin_ref, vtw_ref, g_ref, z_ref, mt_ref,
                o_ref, qaug_ref, sa_ref, sb_ref, m_ref, acc_ref, tot_ref, imp_ref, rank_ref, *, tq, tk, n_sel):
    i = pl.program_id(1)
    nh, ng, hpg, dk, half = NSA_HEADS, NSA_KV_GROUPS, NSA_HPG, NSA_HEAD_DIM, NSA_HEAD_DIM // 2
    vr = dk + ONES_ROWS
    group = [h // hpg for h in range(nh)]
    for h in range(nh):
        qaug_ref[h, 0:half, :] = q1_ref[0, h * half:(h + 1) * half, :]
        qaug_ref[h, half:dk, :] = q2_ref[0, h * half:(h + 1) * half, :]

    def gate(h, branch):
        row = group[h] * GATE_ROWS + (h % hpg) * N_BRANCH + branch
        return g_ref[0, row:row + 1, :]

    row_k = lax.broadcasted_iota(I32, (tk, tq), 0)
    col_q = lax.broadcasted_iota(I32, (tk, tq), 1)
    causal = row_k <= col_q

    def reset():
        m_ref[...] = jnp.full_like(m_ref, NEG_INF)
        acc_ref[...] = jnp.zeros_like(acc_ref)

    def add_branch(branch):
        for h in range(nh):
            inv_l = 1.0 / acc_ref[h, dk:dk + 1, :]
            tot_ref[h] = tot_ref[h] + (gate(h, branch) * inv_l) * acc_ref[h, 0:dk, :]

    def q_cols(h):
        return qaug_ref[h, 0:dk, :]

    ncp = kc_ref.shape[2]
    t_row = i * tq + lax.broadcasted_iota(I32, (1, tq), 1)
    last_n = (t_row - (CMP_BLOCK - 1)) >> CMP_SHIFT
    valid = lax.broadcasted_iota(I32, (ncp, tq), 0) <= last_n
    col_ok = last_n >= 0
    p_heads = {}

    def cmp_update(h, s_c):
        s_c = jnp.where(valid, s_c, NEG_INF)
        e = jnp.exp2(s_c - jnp.max(s_c, axis=0, keepdims=True))
        l_c = jnp.sum(e, axis=0, keepdims=True)
        p_c = e * jnp.where(col_ok, 1.0 / l_c, 0.0)
        o_c = jnp.dot(vct_ref[0, group[h]], p_c.astype(BF16), preferred_element_type=F32)
        tot_ref[h] = gate(h, 0) * o_c
        p_heads[h] = p_c

    nb = mt_ref.shape[0]

    def importance(g, _):
        psum = functools.reduce(lambda a, b: a + b, [p_heads[h] for h in range(nh) if group[h] == g])
        hi = psum.astype(BF16)
        lo = (psum - hi.astype(F32)).astype(BF16)
        mt = mt_ref[...]
        imp = jnp.dot(mt, hi, preferred_element_type=F32) + jnp.dot(mt, lo, preferred_element_type=F32)
        j_idx = lax.broadcasted_iota(I32, (nb, tq), 0)
        cur = (i * tq + lax.broadcasted_iota(I32, (nb, tq), 1)) >> SLC_SHIFT
        forced = (j_idx == 0) | (j_idx == cur) | (j_idx == cur - 1)
        imp_ref[g] = jnp.where(forced, FORCED_SCORE, jnp.where(j_idx > cur, -FORCED_SCORE, imp))

    stages = []
    for g in range(ng):
        kc = kc_ref[0, g, :, 0:dk]
        stages += [(functools.partial(jnp.dot, kc, q_cols(h), preferred_element_type=F32),
                    functools.partial(cmp_update, h)) for h in range(nh) if group[h] == g]
        stages.append((None, functools.partial(importance, g)))

    reset()
    n_back = WINDOW // tk
    for back in range(n_back + 1):
        jb = jnp.maximum(i - back, 0)
        if back == 0:
            keep = causal
        elif back == n_back:
            keep = (row_k > col_q) & (i >= back)
        else:
            keep = jnp.broadcast_to(i >= back, (tk, tq))
        for g in range(ng):
            kt_b = kwin_ref[0, g, pl.ds(pl.multiple_of(jb * tk, tk), tk), 0:dk]
            vt_b = vtw_ref[0, jb, g * vr:(g + 1) * vr, :]
            for h in range(nh):
                if group[h] == g:
                    stages.append((functools.partial(jnp.dot, kt_b, q_cols(h), preferred_element_type=F32),
                                   functools.partial(_chain_update, v_t=vt_b, m_ref=m_ref, acc_ref=acc_ref,
                                                     ch=h, keep=keep)))
    _pipeline(stages, lookahead=5)
    add_branch(2)

    rank_ref[...] = jnp.zeros_like(rank_ref)
    sub = lax.broadcasted_iota(I32, (SUBLANES, tq), 0)
    last_group = ((i + 1) * (tq // SLC_BLOCK) - 1) // SUBLANES

    def count(g, c, v):
        blk = imp_ref[g, v * SUBLANES:(v + 1) * SUBLANES, :]
        cnt = rank_ref[g, v * SUBLANES:(v + 1) * SUBLANES, :]
        for rr in range(SUBLANES):
            row = imp_ref[g, c * SUBLANES + rr:c * SUBLANES + rr + 1, :]
            if c < v:
                beats = row >= blk
            elif c > v:
                beats = row > blk
            else:
                beats = (row > blk) | ((row == blk) & (sub > rr))
            cnt = cnt + beats.astype(I32)
        rank_ref[g, v * SUBLANES:(v + 1) * SUBLANES, :] = cnt

    for lvl in range(nb // SUBLANES):
        @pl.when(lvl <= last_group)
        def _(lvl=lvl):
            for g in range(ng):
                for v in range(lvl + 1):
                    count(g, lvl, v)
                for c in range(lvl):
                    count(g, c, lvl)

    for g in range(ng):
        bias = jnp.where(rank_ref[g] < n_sel, 0.0, SEL_BIAS).astype(BF16)
        for h in range(nh):
            if group[h] == g:
                qaug_ref[h, dk:dk + nb, :] = bias

    reset()
    _causal_sweep(lambda j, g: kaug_ref[0, g, pl.ds(pl.multiple_of(j * tk, tk), tk), :],
                  lambda j, g: vts_ref[0, j, g * vr:(g + 1) * vr, :],
                  group, group, qaug_ref, sa_ref, sb_ref, m_ref, acc_ref, i, causal)
    add_branch(1)

    for h in range(nh):
        zz = z_ref[0, h * dk:(h + 1) * dk, :].astype(F32)
        o_ref[0, h * dk:(h + 1) * dk, :] = (tot_ref[h] * zz).astype(BF16)


def _nsa(qt, kc, vct, kaug, vts, kwin, vtw, gt, ztn, mt, *, tq, tk, n_sel):
    b, _, s = qt.shape
    nt = s // tk
    ncp = kc.shape[2]
    ng, nh = NSA_KV_GROUPS, NSA_HEADS
    hq = nh * (NSA_HEAD_DIM // 2)
    nb = mt.shape[0]
    vr = NSA_HEAD_DIM + ONES_ROWS
    in_specs = [
        pl.BlockSpec((1, hq, tq), lambda bi, i: (bi, 0, i)),
        pl.BlockSpec((1, hq, tq), lambda bi, i: (bi, 1, i)),
        pl.BlockSpec((1, ng, ncp, LANES), lambda bi, i: (bi, 0, 0, 0)),
        pl.BlockSpec((1, ng, NSA_HEAD_DIM, ncp), lambda bi, i: (bi, 0, 0, 0)),
        pl.BlockSpec((1, ng, s, LANES), lambda bi, i: (bi, 0, 0, 0)),
        pl.BlockSpec((1, nt, ng * vr, tk), lambda bi, i: (bi, 0, 0, 0)),
        pl.BlockSpec((1, ng, s, LANES), lambda bi, i: (bi, 0, 0, 0)),
        pl.BlockSpec((1, nt, ng * vr, tk), lambda bi, i: (bi, 0, 0, 0)),
        pl.BlockSpec((1, ng * GATE_ROWS, tq), lambda bi, i: (bi, 0, i)),
        pl.BlockSpec((1, NSA_WIDTH, tq), lambda bi, i: (bi, 0, i)),
        pl.BlockSpec(mt.shape, lambda bi, i: (0, 0)),
    ]
    kern = functools.partial(_nsa_kernel, tq=tq, tk=tk, n_sel=n_sel)
    return pl.pallas_call(
        kern, grid=(b, s // tq), in_specs=in_specs,
        out_specs=pl.BlockSpec((1, NSA_WIDTH, tq), lambda bi, i: (bi, 0, i)),
        out_shape=jax.ShapeDtypeStruct((b, NSA_WIDTH, s), BF16),
        scratch_shapes=[pltpu.VMEM((nh, NSA_HEAD_DIM + nb, tq), BF16),
                        pltpu.VMEM((nh, tk, tq), F32), pltpu.VMEM((nh, tk, tq), F32),
                        pltpu.VMEM((nh, 1, tq), F32), pltpu.VMEM((nh, vr, tq), F32),
                        pltpu.VMEM((nh, NSA_HEAD_DIM, tq), F32),
                        pltpu.VMEM((ng, nb, tq), F32), pltpu.VMEM((ng, nb, tq), I32)],
        compiler_params=pltpu.CompilerParams(dimension_semantics=("arbitrary", "arbitrary"),
                                             vmem_limit_bytes=VMEM_LIMIT_BYTES),
        name="nsa",
    )(qt, qt, kc, vct, kaug, vts, kwin, vtw, gt, ztn, mt)


def _mla_kernel(q_ref, qn_ref, k_ref, vt_ref, z_ref, o_ref, sa_ref, sb_ref, m_ref, acc_ref, *, tq, tk):
    i = pl.program_id(1)
    dv, vr = MLA_V_DIM, MLA_V_DIM + ONES_ROWS
    m_ref[...] = jnp.full_like(m_ref, NEG_INF)
    acc_ref[...] = jnp.zeros_like(acc_ref)

    causal = lax.broadcasted_iota(I32, (tk, tq), 0) <= lax.broadcasted_iota(I32, (tk, tq), 1)
    heads = list(range(MLA_HEADS))
    _causal_sweep(lambda j, hd: k_ref[0, hd, pl.ds(pl.multiple_of(j * tk, tk), tk), :],
                  lambda j, hd: vt_ref[0, j, hd * vr:(hd + 1) * vr, :],
                  heads, heads, q_ref.at[0], sa_ref, sb_ref, m_ref, acc_ref, i, causal,
                  q_next_ref=qn_ref.at[0], first=i == 0)
    for hd in range(MLA_HEADS):
        o_h = acc_ref[hd, 0:dv, :] * (1.0 / acc_ref[hd, dv:dv + 1, :])
        zz = z_ref[0, hd * dv:(hd + 1) * dv, :].astype(F32)
        o_ref[0, hd * dv:(hd + 1) * dv, :] = (o_h * zz).astype(BF16)


def _mla(qtm, kmla, vtm, ztm, *, tq, tk):
    b, _, s, _ = kmla.shape
    nt = s // tk
    vrows = MLA_HEADS * (MLA_V_DIM + ONES_ROWS)
    kern = functools.partial(_mla_kernel, tq=tq, tk=tk)
    return pl.pallas_call(
        kern, grid=(b, s // tq),
        in_specs=[pl.BlockSpec((1, MLA_HEADS, MLA_QK, tq), lambda bi, i: (bi, 0, 0, i)),
                  pl.BlockSpec((1, MLA_HEADS, MLA_QK, tq), lambda bi, i: (bi, 0, 0, jnp.minimum(i + 1, s // tq - 1))),
                  pl.BlockSpec((1, MLA_HEADS, s, MLA_QK), lambda bi, i: (bi, 0, 0, 0)),
                  pl.BlockSpec((1, nt, vrows, tk), lambda bi, i: (bi, 0, 0, 0)),
                  pl.BlockSpec((1, MLA_WIDTH, tq), lambda bi, i: (bi, 0, i))],
        out_specs=pl.BlockSpec((1, MLA_WIDTH, tq), lambda bi, i: (bi, 0, i)),
        out_shape=jax.ShapeDtypeStruct((b, MLA_WIDTH, s), BF16),
        scratch_shapes=[pltpu.VMEM((MLA_HEADS, tk, tq), F32), pltpu.VMEM((MLA_HEADS, tk, tq), F32),
                        pltpu.VMEM((MLA_HEADS, 1, tq), F32), pltpu.VMEM((MLA_HEADS, MLA_V_DIM + ONES_ROWS, tq), F32)],
        compiler_params=pltpu.CompilerParams(dimension_semantics=("arbitrary", "arbitrary"),
                                             vmem_limit_bytes=VMEM_LIMIT_BYTES),
        name="mla",
    )(qtm, qtm, kmla, vtm, ztm)


def _out_kernel(x_ref, mn_ref, mm_ref, w_ref, mod_ref, fg_ref, o_ref, *, final):
    y = lax.dot_general(mn_ref[0], w_ref[0:NSA_WIDTH, :], TN, preferred_element_type=F32)
    y = y + lax.dot_general(mm_ref[0], w_ref[NSA_WIDTH:MIX_WIDTH, :], TN, preferred_element_type=F32)
    x2 = x_ref[0] + mod_ref[0][2:3] * y
    o_ref[0] = _rms(x2, fg_ref[...]) if final else x2


def _out(x, mn, mm, w_out, mod3, fg, *, tm, final):
    b, s, d = x.shape
    return pl.pallas_call(
        functools.partial(_out_kernel, final=final), grid=(b, s // tm),
        in_specs=[pl.BlockSpec((1, tm, d), lambda bi, i: (bi, i, 0)),
                  pl.BlockSpec((1, NSA_WIDTH, tm), lambda bi, i: (bi, 0, i)),
                  pl.BlockSpec((1, MLA_WIDTH, tm), lambda bi, i: (bi, 0, i)),
                  pl.BlockSpec(w_out.shape, lambda bi, i: (0, 0)),
                  pl.BlockSpec((1, 3, d), lambda bi, i: (bi, 0, 0)),
                  pl.BlockSpec((1, d), lambda bi, i: (0, 0))],
        out_specs=pl.BlockSpec((1, tm, d), lambda bi, i: (bi, i, 0)),
        out_shape=jax.ShapeDtypeStruct((b, s, d), F32),
        compiler_params=pltpu.CompilerParams(vmem_limit_bytes=VMEM_LIMIT_BYTES),
        name="out_proj",
    )(x, mn, mm, w_out, mod3, fg)


def _cmp_to_slc_t(ncp, nc, nslc, nb):
    start = np.arange(nc)[:, None] * CMP_STRIDE
    bstart = np.arange(nslc)[None, :] * SLC_BLOCK
    ov = np.minimum(start + CMP_BLOCK, bstart + SLC_BLOCK) - np.maximum(start, bstart)
    m = (np.clip(ov, 0, None) / CMP_BLOCK).astype(np.float32)
    out = np.zeros((nb, ncp), np.float32)
    out[:nslc, :nc] = m.T
    return out


def _layout_w_in(w):
    d = w.shape[0]
    (q_n, kc_n, vc_n, ks_n, vs_n, kw_n, vw_n, gl_n, z_n, cq_m, ckv_m, kr_m, z_m) = jnp.split(w, IN_OFFSETS, axis=-1)
    dk = NSA_HEAD_DIM
    z64 = jnp.zeros((d, LANES - dk), w.dtype)
    wtok = jnp.concatenate(
        [ks_n[:, :dk], z64, ks_n[:, dk:], z64, kw_n[:, :dk], z64, kw_n[:, dk:], z64,
         kc_n, vc_n, cq_m, ckv_m, jnp.zeros((d, KR_LANE), w.dtype), kr_m,
         jnp.zeros((d, LANES - KR_LANE - MLA_ROPE_DIM), w.dtype)], axis=1)
    qr = q_n.reshape(d, NSA_HEADS, 2, dk // 2)
    q_perm = jnp.concatenate([qr[:, :, 0, :].reshape(d, -1), qr[:, :, 1, :].reshape(d, -1)], axis=1)
    gl = gl_n.reshape(d, NSA_KV_GROUPS, NSA_HPG * N_BRANCH)
    gl = jnp.pad(gl, ((0, 0), (0, 0), (0, GATE_ROWS - NSA_HPG * N_BRANCH))).reshape(d, -1)
    wtr = jnp.concatenate([q_perm, vs_n, vw_n, gl, z_n, z_m], axis=1).T
    assert wtok.shape[1] == TOK_COLS and wtr.shape[0] == TR_ROWS
    return wtok.astype(BF16), wtr.astype(BF16)


def _layout_w1(w1):
    hid = w1.shape[1]
    w1r = w1.reshape(2, CMP_STRIDE, NSA_HEAD_DIM, hid)
    eye = jnp.eye(NSA_KV_GROUPS, dtype=w1.dtype)
    halves = [jnp.einsum('ldh,pg->lpdgh', w1r[k], eye).reshape(CMP_STRIDE * NSA_KV_WIDTH, NSA_KV_GROUPS * hid)
              for k in range(2)]
    return jnp.concatenate(halves, axis=1).astype(BF16)


def kernel(x, c, positions, ada_w, ada_b, norm_g, w_in, cmp_pos, cmp_k_w1, cmp_k_w2, cmp_v_w1, cmp_v_w2,
           q_norm_g, w_q_up, kv_norm_g, w_kv_up, w_out, final_norm_g):
    b, s, d = x.shape
    depth = ada_w.shape[0]
    tm, tq = PROJ_TILE, ATT_TILE
    tk = tq
    assert s % tm == 0 and tm % tk == 0 and WINDOW % tk == 0 and (tq & (tq - 1)) == 0 and s % OUT_TILE == 0
    assert CMP_BLOCK == 2 * CMP_STRIDE and s % SLC_BLOCK == 0
    nslc = s // SLC_BLOCK
    nb = LANES - NSA_HEAD_DIM
    assert nslc <= nb
    ncp = s // CMP_STRIDE
    nc = ncp - 1

    half_n, half_m = NSA_HEAD_DIM // 2, MLA_ROPE_DIM // 2
    inv_n = ROPE_THETA ** (-jnp.arange(half_n, dtype=F32) / half_n)
    inv_m = ROPE_THETA ** (-jnp.arange(half_m, dtype=F32) / half_m)
    ones_n, ones_m = jnp.ones((half_n,), F32), jnp.ones((half_m,), F32)
    pad = jnp.zeros((LANES - NSA_HEAD_DIM - MLA_ROPE_DIM,), F32)
    inv_l = jnp.concatenate([inv_n, inv_n, inv_m, inv_m, pad])[None]
    sign_l = jnp.concatenate([-ones_n, ones_n, -ones_m, ones_m, pad])[None]
    pos_f = positions.astype(F32)
    pos_b = jnp.broadcast_to(pos_f.reshape(b * s, 1), (b * s, LANES))
    pos_row = pos_f.reshape(b, 1, s)
    cmp_end = np.minimum(np.arange(ncp) * CMP_STRIDE + CMP_BLOCK - 1, s - 1)
    pos_c = jnp.broadcast_to(pos_f[:, cmp_end].reshape(b * ncp, 1), (b * ncp, LANES))
    inv_nb = jnp.broadcast_to(inv_n[:, None], (half_n, tm))
    inv_mb = jnp.broadcast_to(inv_m[:, None], (half_m, tm))

    mt = jnp.asarray(_cmp_to_slc_t(ncp, nc, nslc, nb), dtype=BF16)
    bp = -(-b // SUBLANES) * SUBLANES
    c_pad = jnp.pad(c, ((0, bp - b), (0, 0)))

    for l in range(depth):
        mod = _adaln(c_pad, ada_w[l], ada_b[l].reshape(1, -1))
        mod3 = mod[:b].reshape(b, 3, d)
        wtok, wtr = _layout_w_in(w_in[l])
        wq = w_q_up[l].reshape(MLA_Q_RANK, MLA_HEADS, MLA_NOPE_DIM + MLA_ROPE_DIM)
        wqt = jnp.concatenate([wq[:, :, :MLA_NOPE_DIM].reshape(MLA_Q_RANK, -1),
                               wq[:, :, MLA_NOPE_DIM:MLA_NOPE_DIM + half_m].reshape(MLA_Q_RANK, -1),
                               wq[:, :, MLA_NOPE_DIM + half_m:].reshape(MLA_Q_RANK, -1)], axis=1).T.astype(BF16)
        wkv = w_kv_up[l].reshape(MLA_KV_RANK, MLA_HEADS, MLA_NOPE_DIM + MLA_V_DIM)
        wkn = jnp.pad(wkv[:, :, :MLA_NOPE_DIM], ((0, 0), (0, 0), (0, LANES - MLA_NOPE_DIM))).reshape(
            MLA_KV_RANK, MLA_HEADS * LANES).astype(BF16)
        wv = wkv[:, :, MLA_NOPE_DIM:].transpose(1, 2, 0).reshape(MLA_WIDTH, MLA_KV_RANK).astype(BF16)

        (qt, kaug, kwin, vts, vtw, kcmp, vcmp, gt, ztn, ztm, qtm, kmla, vtm) = _proj(
            x, mod3, norm_g[l].reshape(1, d), wtok, wtr, pos_b, pos_row, inv_l, sign_l, inv_nb, inv_mb,
            q_norm_g[l].reshape(1, -1), kv_norm_g[l].reshape(1, -1), wqt, wkn, wv, tm=tm, tk=tk)

        pos_l = cmp_pos[l]
        ptop = jnp.broadcast_to(pos_l[:CMP_STRIDE, None, :], (CMP_STRIDE, NSA_KV_GROUPS, NSA_HEAD_DIM)).reshape(1, -1)
        pbot = jnp.broadcast_to(pos_l[CMP_STRIDE:, None, :], (CMP_STRIDE, NSA_KV_GROUPS, NSA_HEAD_DIM)).reshape(1, -1)
        wk2 = jnp.pad(cmp_k_w2[l], ((0, 0), (0, LANES - NSA_HEAD_DIM))).astype(BF16)
        kc, vct = _compress(kcmp, vcmp, ptop, pbot,
                            _layout_w1(cmp_k_w1[l]), _layout_w1(cmp_v_w1[l]), wk2,
                            cmp_v_w2[l].T.astype(BF16), pos_c, inv_l, sign_l)

        mix_n = _nsa(qt, kc, vct, kaug, vts, kwin, vtw, gt, ztn, mt, tq=tq, tk=tk, n_sel=min(SLC_TOPK, nslc))
        mix_m = _mla(qtm, kmla, vtm, ztm, tq=tq, tk=tk)
        x = _out(x, mix_n, mix_m, w_out[l].astype(BF16), mod3, final_norm_g.reshape(1, d), tm=OUT_TILE,
                 final=(l == depth - 1))
    return x
```

```python
import functools

import numpy as np
import jax
import jax.numpy as jnp
from jax import lax
from jax.experimental import pallas as pl
from jax.experimental.pallas import tpu as pltpu

F32 = jnp.float32
BF16 = jnp.bfloat16
I32 = jnp.int32

NSA_HEADS = 8
NSA_KV_GROUPS = 2
NSA_HPG = NSA_HEADS // NSA_KV_GROUPS
NSA_HEAD_DIM = 64
NSA_WIDTH = NSA_HEADS * NSA_HEAD_DIM
NSA_KV_WIDTH = NSA_KV_GROUPS * NSA_HEAD_DIM
CMP_BLOCK = 32
CMP_STRIDE = 16
CMP_HIDDEN = 128
SLC_BLOCK = 64
SLC_TOPK = 16
WINDOW = 512
N_BRANCH = 3
FORCED_SCORE = 1.0e4
MLA_HEADS = 8
MLA_NOPE_DIM = 64
MLA_ROPE_DIM = 32
MLA_V_DIM = 64
MLA_WIDTH = MLA_HEADS * MLA_V_DIM
MLA_Q_RANK = 256
MLA_KV_RANK = 128
MIX_WIDTH = NSA_WIDTH + MLA_WIDTH
ROPE_THETA = 10000.0
NORM_EPS = 1e-6
NEG_INF = -1e30
IN_SIZES = (NSA_WIDTH, NSA_KV_WIDTH, NSA_KV_WIDTH, NSA_KV_WIDTH, NSA_KV_WIDTH, NSA_KV_WIDTH, NSA_KV_WIDTH,
            NSA_HEADS * N_BRANCH, NSA_WIDTH, MLA_Q_RANK, MLA_KV_RANK, MLA_ROPE_DIM, MLA_WIDTH)
IN_OFFSETS = tuple(int(o) for o in np.cumsum(IN_SIZES)[:-1])

LANES = 128
SUBLANES = 8
VMEM_LIMIT_BYTES = 56 * 1024 * 1024

PROJ_TILE = 1024
OUT_TILE = 1024
ATT_TILE = 256
SEL_BIAS = NEG_INF
GATE_ROWS = 16
ONES_ROWS = 16
MLA_QK = LANES
LOG2E = float(np.log2(np.e))
SWEEP_UNROLL = 4
SLC_SHIFT = SLC_BLOCK.bit_length() - 1
CMP_SHIFT = CMP_STRIDE.bit_length() - 1
assert 1 << SLC_SHIFT == SLC_BLOCK and 1 << CMP_SHIFT == CMP_STRIDE

NT = (((1,), (1,)), ((), ()))
TN = (((0,), (0,)), ((), ()))


def _silu(v):
    return v * jax.nn.sigmoid(v)


def _rms(v, g):
    ms = jnp.mean(v * v, axis=-1, keepdims=True)
    return v * lax.rsqrt(ms + NORM_EPS) * g


def _rope_tok(v, c, s_signed, half, lane, base=0):
    up = pltpu.roll(v, LANES - half, axis=1)
    dn = pltpu.roll(v, half, axis=1)
    return v * c + jnp.where(lane < base + half, up, dn) * s_signed


def _rope_lanes(pos_b, inv_ref, sign_ref):
    ang = pos_b * inv_ref[...]
    return jnp.cos(ang), jnp.sin(ang) * sign_ref[...]


def _chain_update(s_t, v_t, m_ref, acc_ref, ch, keep=None):
    if keep is not None:
        s_t = jnp.where(keep, s_t, NEG_INF)
    m_prev = m_ref[ch]
    m_new = jnp.maximum(m_prev, jnp.max(s_t, axis=0, keepdims=True))
    alpha = jnp.exp2(m_prev - m_new)
    p = jnp.exp2(s_t - m_new)
    acc_ref[ch] = alpha * acc_ref[ch] + jnp.dot(v_t, p.astype(BF16), preferred_element_type=F32)
    m_ref[ch] = m_new


def _causal_sweep(k_tile, v_tile, k_group, v_group, q_ref, sa_ref, sb_ref, m_ref, acc_ref, last, keep_last,
                  lookahead=2, q_next_ref=None, first=None):
    n_chains = len(k_group)

    def loader(tile_fn, j):
        cache = {}
        return lambda g: cache.setdefault(g, tile_fn(j, g))

    def phase(j_next, s_next_ref, s_cur_ref, j_cur, keep=None, ahead=False):
        k_next = loader(k_tile, j_next) if j_next is not None else None
        k_zero = loader(k_tile, 0) if ahead else None
        v_cur = loader(v_tile, j_cur)
        for n in range(n_chains + lookahead):
            if k_next is not None and n < n_chains:
                s_next_ref[n] = jnp.dot(k_next(k_group[n]), q_ref[n], preferred_element_type=F32)
            if n >= lookahead:
                ch = n - lookahead
                _chain_update(s_cur_ref[ch], v_cur(v_group[ch]), m_ref, acc_ref, ch, keep=keep)
                if ahead:
                    sa_ref[ch] = jnp.dot(k_zero(k_group[ch]), q_next_ref[ch], preferred_element_type=F32)

    bufs = (sa_ref, sb_ref)

    def prologue():
        k_0 = loader(k_tile, 0)
        for ch in range(n_chains):
            sa_ref[ch] = jnp.dot(k_0(k_group[ch]), q_ref[ch], preferred_element_type=F32)

    if q_next_ref is None:
        prologue()
    else:
        pl.when(first)(prologue)

    def run(j0, count):
        for u in range(count):
            phase(j0 + u + 1, bufs[(u + 1) % 2], bufs[u % 2], j0 + u)

    def body(jj, carry):
        run(SWEEP_UNROLL * jj, SWEEP_UNROLL)
        return carry

    lax.fori_loop(0, last // SWEEP_UNROLL, body, 0)
    rem = last % SWEEP_UNROLL
    for r in range(SWEEP_UNROLL):
        @pl.when(rem == r)
        def _(r=r):
            run(last - r, r)
            phase(None, None, bufs[r % 2], last, keep=keep_last, ahead=q_next_ref is not None)


def _pipeline(stages, lookahead):
    pending = {}
    for n in range(len(stages) + lookahead):
        if n < len(stages) and stages[n][0] is not None:
            pending[n] = stages[n][0]()
        if n >= lookahead:
            stages[n - lookahead][1](pending.pop(n - lookahead, None))


def _adaln_kernel(c_ref, w_ref, b_ref, o_ref):
    sc = _silu(c_ref[...])
    o_ref[...] = jnp.dot(sc.astype(BF16), w_ref[...].astype(BF16), preferred_element_type=F32) + b_ref[...]


def _adaln(c_pad, w, b):
    bp, d = c_pad.shape
    n = w.shape[1] // d
    return pl.pallas_call(
        _adaln_kernel,
        grid=(n,),
        in_specs=[pl.BlockSpec((bp, d), lambda j: (0, 0)),
                  pl.BlockSpec((d, d), lambda j: (0, j)),
                  pl.BlockSpec((1, d), lambda j: (0, j))],
        out_specs=pl.BlockSpec((bp, d), lambda j: (0, j)),
        out_shape=jax.ShapeDtypeStruct((bp, n * d), F32),
        name="adaln",
    )(c_pad, w, b)


TOK_KS, TOK_KW, TOK_KC, TOK_VC, TOK_CQ, TOK_CKV, TOK_KR, TOK_COLS = (int(o) for o in np.cumsum(
    [0, NSA_KV_GROUPS * LANES, NSA_KV_GROUPS * LANES, NSA_KV_WIDTH, NSA_KV_WIDTH, MLA_Q_RANK, MLA_KV_RANK, LANES]))
KR_LANE = NSA_HEAD_DIM
TR_Q, TR_VS, TR_VW, TR_G, TR_ZN, TR_ZM, TR_ROWS = (int(o) for o in np.cumsum(
    [0, NSA_WIDTH, NSA_KV_WIDTH, NSA_KV_WIDTH, NSA_KV_GROUPS * GATE_ROWS, NSA_WIDTH, MLA_WIDTH]))


def _proj_kernel(x_ref, mod_ref, ng_ref, wtok_ref, wtr_ref, posb_ref, posr_ref, invl_ref, signl_ref,
                 invn_ref, invm_ref, qng_ref, kvng_ref, wqt_ref, wkn_ref, wv_ref,
                 qt_ref, kaug_ref, kwin_ref, vts_ref, vtw_ref, kcmp_ref, vcmp_ref, gt_ref,
                 ztn_ref, ztm_ref, qtm_ref, kmla_ref, vtm_ref, *, tm, tk, scale_nsa, scale_mla):
    i = pl.program_id(1)
    mod = mod_ref[0]
    h = _rms(x_ref[0], ng_ref[...]) * (1.0 + mod[1:2]) + mod[0:1]
    hb = h.astype(BF16)
    tok = jnp.dot(hb, wtok_ref[...], preferred_element_type=F32)
    tr = lax.dot_general(wtr_ref[...], hb, NT, preferred_element_type=F32)

    lane = lax.broadcasted_iota(I32, (tm, LANES), 1)
    row = lax.broadcasted_iota(I32, (tm, LANES), 0)
    blk = (i * tm + row) >> SLC_SHIFT
    onehot = (lane - NSA_HEAD_DIM == blk).astype(F32)
    ct, st = _rope_lanes(posb_ref[...], invl_ref, signl_ref)
    half_n = NSA_HEAD_DIM // 2
    for g in range(NSA_KV_GROUPS):
        ks = _rope_tok(tok[:, TOK_KS + LANES * g:TOK_KS + LANES * (g + 1)], ct, st, half_n, lane)
        kaug_ref[0, g] = jnp.where(lane >= NSA_HEAD_DIM, onehot, ks).astype(BF16)
        kw = _rope_tok(tok[:, TOK_KW + LANES * g:TOK_KW + LANES * (g + 1)], ct, st, half_n, lane)
        kwin_ref[0, g] = kw.astype(BF16)
    kcmp_ref[0] = tok[:, TOK_KC:TOK_KC + LANES]
    vcmp_ref[0] = tok[:, TOK_VC:TOK_VC + LANES]

    ckvn = _rms(tok[:, TOK_CKV:TOK_CKV + MLA_KV_RANK], kvng_ref[...])
    krr = _rope_tok(tok[:, TOK_KR:TOK_KR + LANES], ct, st, MLA_ROPE_DIM // 2, lane, base=KR_LANE)
    ckvb = ckvn.astype(BF16)
    kn = jnp.dot(ckvb, wkn_ref[...], preferred_element_type=F32)
    is_rot = (lane >= KR_LANE) & (lane < KR_LANE + MLA_ROPE_DIM)
    for hd in range(MLA_HEADS):
        kmla_ref[0, hd] = jnp.where(is_rot, krr, kn[:, hd * LANES:(hd + 1) * LANES]).astype(BF16)
    vtm = lax.dot_general(wv_ref[...], ckvb, NT, preferred_element_type=F32).astype(BF16)
    ones = jnp.ones((ONES_ROWS, tk), BF16)
    vr = MLA_V_DIM + ONES_ROWS
    for ii in range(tm // tk):
        for hd in range(MLA_HEADS):
            vtm_ref[0, ii, hd * vr:hd * vr + MLA_V_DIM, :] = vtm[hd * MLA_V_DIM:(hd + 1) * MLA_V_DIM,
                                                                 ii * tk:(ii + 1) * tk]
            vtm_ref[0, ii, hd * vr + MLA_V_DIM:(hd + 1) * vr, :] = ones

    cqn = _rms(tok[:, TOK_CQ:TOK_CQ + MLA_Q_RANK], qng_ref[...]).astype(BF16)
    qm = lax.dot_general(wqt_ref[...], cqn, NT, preferred_element_type=F32)
    nq = MLA_HEADS * MLA_NOPE_DIM
    hr = MLA_ROPE_DIM // 2
    x1 = qm[nq:nq + MLA_HEADS * hr].reshape(MLA_HEADS, hr, tm)
    x2 = qm[nq + MLA_HEADS * hr:nq + 2 * MLA_HEADS * hr].reshape(MLA_HEADS, hr, tm)
    ang_m = invm_ref[...] * posr_ref[0]
    cm_t, sm_t = jnp.cos(ang_m)[None], jnp.sin(ang_m)[None]
    o1 = (x1 * cm_t - x2 * sm_t) * scale_mla
    o2 = (x2 * cm_t + x1 * sm_t) * scale_mla
    nd = MLA_NOPE_DIM
    q_pad = jnp.zeros((MLA_QK - nd - MLA_ROPE_DIM, tm), BF16)
    for hd in range(MLA_HEADS):
        qtm_ref[0, hd, 0:nd, :] = (qm[hd * nd:(hd + 1) * nd] * scale_mla).astype(BF16)
        qtm_ref[0, hd, nd:nd + hr, :] = o1[hd].astype(BF16)
        qtm_ref[0, hd, nd + hr:nd + 2 * hr, :] = o2[hd].astype(BF16)
        qtm_ref[0, hd, nd + 2 * hr:MLA_QK, :] = q_pad

    hq = NSA_HEADS * half_n
    q1 = tr[TR_Q:TR_Q + hq].reshape(NSA_HEADS, half_n, tm)
    q2 = tr[TR_Q + hq:TR_Q + 2 * hq].reshape(NSA_HEADS, half_n, tm)
    ang_n = invn_ref[...] * posr_ref[0]
    cn_t, sn_t = jnp.cos(ang_n)[None], jnp.sin(ang_n)[None]
    qt_ref[0, 0:hq, :] = ((q1 * cn_t - q2 * sn_t) * scale_nsa).reshape(hq, tm).astype(BF16)
    qt_ref[0, hq:2 * hq, :] = ((q2 * cn_t + q1 * sn_t) * scale_nsa).reshape(hq, tm).astype(BF16)

    vts = tr[TR_VS:TR_VS + NSA_KV_WIDTH].astype(BF16)
    vtw = tr[TR_VW:TR_VW + NSA_KV_WIDTH].astype(BF16)
    dk = NSA_HEAD_DIM
    gr = dk + ONES_ROWS
    for ii in range(tm // tk):
        for g in range(NSA_KV_GROUPS):
            vts_ref[0, ii, g * gr:g * gr + dk, :] = vts[g * dk:(g + 1) * dk, ii * tk:(ii + 1) * tk]
            vtw_ref[0, ii, g * gr:g * gr + dk, :] = vtw[g * dk:(g + 1) * dk, ii * tk:(ii + 1) * tk]
            vts_ref[0, ii, g * gr + dk:(g + 1) * gr, :] = ones
            vtw_ref[0, ii, g * gr + dk:(g + 1) * gr, :] = ones
    gt_ref[0] = jax.nn.sigmoid(tr[TR_G:TR_G + NSA_KV_GROUPS * GATE_ROWS])
    ztn_ref[0] = _silu(tr[TR_ZN:TR_ZN + NSA_WIDTH]).astype(BF16)
    ztm_ref[0] = _silu(tr[TR_ZM:TR_ZM + MLA_WIDTH]).astype(BF16)


def _proj(x, mod3, ng, wtok, wtr, pos_b, pos_row, inv_l, sign_l, inv_nb, inv_mb, qng, kvng, wqt, wkn, wv, *, tm, tk):
    b, s, d = x.shape
    nt = s // tk
    v_rows_n = NSA_KV_GROUPS * (NSA_HEAD_DIM + ONES_ROWS)
    v_rows_m = MLA_HEADS * (MLA_V_DIM + ONES_ROWS)
    tile_tok = pl.BlockSpec((tm, LANES), lambda bi, i: (bi * (s // tm) + i, 0))

    def full(a):
        return pl.BlockSpec(a.shape, lambda bi, i, _n=a.ndim: (0,) * _n)

    in_specs = [pl.BlockSpec((1, tm, d), lambda bi, i: (bi, i, 0)),
                pl.BlockSpec((1, 3, d), lambda bi, i: (bi, 0, 0)),
                full(ng), full(wtok), full(wtr),
                tile_tok, pl.BlockSpec((1, 1, tm), lambda bi, i: (bi, 0, i)),
                full(inv_l), full(sign_l), full(inv_nb), full(inv_mb),
                full(qng), full(kvng), full(wqt), full(wkn), full(wv)]
    out_shape = [
        jax.ShapeDtypeStruct((b, NSA_WIDTH, s), BF16),
        jax.ShapeDtypeStruct((b, NSA_KV_GROUPS, s, LANES), BF16),
        jax.ShapeDtypeStruct((b, NSA_KV_GROUPS, s, LANES), BF16),
        jax.ShapeDtypeStruct((b, nt, v_rows_n, tk), BF16),
        jax.ShapeDtypeStruct((b, nt, v_rows_n, tk), BF16),
        jax.ShapeDtypeStruct((b, s, NSA_KV_WIDTH), F32),
        jax.ShapeDtypeStruct((b, s, NSA_KV_WIDTH), F32),
        jax.ShapeDtypeStruct((b, NSA_KV_GROUPS * GATE_ROWS, s), F32),
        jax.ShapeDtypeStruct((b, NSA_WIDTH, s), BF16),
        jax.ShapeDtypeStruct((b, MLA_WIDTH, s), BF16),
        jax.ShapeDtypeStruct((b, MLA_HEADS, MLA_QK, s), BF16),
        jax.ShapeDtypeStruct((b, MLA_HEADS, s, MLA_QK), BF16),
        jax.ShapeDtypeStruct((b, nt, v_rows_m, tk), BF16),
    ]
    out_specs = [
        pl.BlockSpec((1, NSA_WIDTH, tm), lambda bi, i: (bi, 0, i)),
        pl.BlockSpec((1, NSA_KV_GROUPS, tm, LANES), lambda bi, i: (bi, 0, i, 0)),
        pl.BlockSpec((1, NSA_KV_GROUPS, tm, LANES), lambda bi, i: (bi, 0, i, 0)),
        pl.BlockSpec((1, tm // tk, v_rows_n, tk), lambda bi, i: (bi, i, 0, 0)),
        pl.BlockSpec((1, tm // tk, v_rows_n, tk), lambda bi, i: (bi, i, 0, 0)),
        pl.BlockSpec((1, tm, NSA_KV_WIDTH), lambda bi, i: (bi, i, 0)),
        pl.BlockSpec((1, tm, NSA_KV_WIDTH), lambda bi, i: (bi, i, 0)),
        pl.BlockSpec((1, NSA_KV_GROUPS * GATE_ROWS, tm), lambda bi, i: (bi, 0, i)),
        pl.BlockSpec((1, NSA_WIDTH, tm), lambda bi, i: (bi, 0, i)),
        pl.BlockSpec((1, MLA_WIDTH, tm), lambda bi, i: (bi, 0, i)),
        pl.BlockSpec((1, MLA_HEADS, MLA_QK, tm), lambda bi, i: (bi, 0, 0, i)),
        pl.BlockSpec((1, MLA_HEADS, tm, MLA_QK), lambda bi, i: (bi, 0, i, 0)),
        pl.BlockSpec((1, tm // tk, v_rows_m, tk), lambda bi, i: (bi, i, 0, 0)),
    ]
    kern = functools.partial(_proj_kernel, tm=tm, tk=tk, scale_nsa=NSA_HEAD_DIM ** -0.5 * LOG2E,
                             scale_mla=(MLA_NOPE_DIM + MLA_ROPE_DIM) ** -0.5 * LOG2E)
    return pl.pallas_call(
        kern, grid=(b, s // tm), in_specs=in_specs, out_specs=out_specs, out_shape=out_shape,
        compiler_params=pltpu.CompilerParams(vmem_limit_bytes=VMEM_LIMIT_BYTES),
        name="proj",
    )(x, mod3, ng, wtok, wtr, pos_b, pos_row, inv_l, sign_l, inv_nb, inv_mb, qng, kvng, wqt, wkn, wv)


def _compress_kernel(k_ref, v_ref, ptop_ref, pbot_ref, wk1_ref, wv1_ref, wk2_ref, wv2t_ref, posc_ref, invl_ref,
                     signl_ref, kc_ref, vct_ref):
    ncp = k_ref.shape[1] // CMP_STRIDE
    gw = NSA_KV_GROUPS * CMP_HIDDEN
    lane = lax.broadcasted_iota(I32, (ncp, LANES), 1)

    def hidden(r_ref, w1_ref):
        r = jnp.concatenate([r_ref[0, pl.ds(t, ncp, stride=CMP_STRIDE), :] for t in range(CMP_STRIDE)], axis=1)
        a = jnp.dot((r + ptop_ref[...]).astype(BF16), w1_ref[:, 0:gw], preferred_element_type=F32)
        bt = jnp.dot((r + pbot_ref[...]).astype(BF16), w1_ref[:, gw:2 * gw], preferred_element_type=F32)
        return _silu(a + pltpu.roll(bt, ncp - 1, axis=0))

    cc, sc = _rope_lanes(posc_ref[...], invl_ref, signl_ref)
    hk = hidden(k_ref, wk1_ref)
    hv = hidden(v_ref, wv1_ref)
    for g in range(NSA_KV_GROUPS):
        hkg = hk[:, g * CMP_HIDDEN:(g + 1) * CMP_HIDDEN].astype(BF16)
        kc = jnp.dot(hkg, wk2_ref[...], preferred_element_type=F32)
        kc_ref[0, g] = _rope_tok(kc, cc, sc, NSA_HEAD_DIM // 2, lane).astype(BF16)
        hvg = hv[:, g * CMP_HIDDEN:(g + 1) * CMP_HIDDEN].astype(BF16)
        vct_ref[0, g] = lax.dot_general(wv2t_ref[...], hvg, NT, preferred_element_type=F32).astype(BF16)


def _compress(kcmp, vcmp, ptop, pbot, wk1, wv1, wk2, wv2t, pos_c, inv_l, sign_l):
    b, s, width = kcmp.shape
    ncp = s // CMP_STRIDE

    def full(a):
        return pl.BlockSpec(a.shape, lambda bi, _n=a.ndim: (0,) * _n)

    blk = pl.BlockSpec((1, s, width), lambda bi: (bi, 0, 0))
    tab = pl.BlockSpec((ncp, LANES), lambda bi: (bi, 0))
    return pl.pallas_call(
        _compress_kernel, grid=(b,),
        in_specs=[blk, blk, full(ptop), full(pbot), full(wk1), full(wv1), full(wk2), full(wv2t), tab,
                  full(inv_l), full(sign_l)],
        out_specs=[pl.BlockSpec((1, NSA_KV_GROUPS, ncp, LANES), lambda bi: (bi, 0, 0, 0)),
                   pl.BlockSpec((1, NSA_KV_GROUPS, NSA_HEAD_DIM, ncp), lambda bi: (bi, 0, 0, 0))],
        out_shape=[jax.ShapeDtypeStruct((b, NSA_KV_GROUPS, ncp, LANES), BF16),
                   jax.ShapeDtypeStruct((b, NSA_KV_GROUPS, NSA_HEAD_DIM, ncp), BF16)],
        compiler_params=pltpu.CompilerParams(vmem_limit_bytes=VMEM_LIMIT_BYTES),
        name="compress",
    )(kcmp, vcmp, ptop, pbot, wk1, wv1, wk2, wv2t, pos_c, inv_l, sign_l)


def _nsa_kernel(q1_ref, q2_ref, kc_ref, vct_ref, kaug_ref, vts_ref, kwin_ref, vtw_ref, g_ref, z_ref, mt_ref,
                o_ref, qaug_ref, sa_ref, sb_ref, m_ref, acc_ref, tot_ref, imp_ref, rank_ref, *, tq, tk, n_sel):
    i = pl.program_id(1)
    nh, ng, hpg, dk, half = NSA_HEADS, NSA_KV_GROUPS, NSA_HPG, NSA_HEAD_DIM, NSA_HEAD_DIM // 2
    vr = dk + ONES_ROWS
    group = [h // hpg for h in range(nh)]
    for h in range(nh):
        qaug_ref[h, 0:half, :] = q1_ref[0, h * half:(h + 1) * half, :]
        qaug_ref[h, half:dk, :] = q2_ref[0, h * half:(h + 1) * half, :]

    def gate(h, branch):
        row = group[h] * GATE_ROWS + (h % hpg) * N_BRANCH + branch
        return g_ref[0, row:row + 1, :]

    row_k = lax.broadcasted_iota(I32, (tk, tq), 0)
    col_q = lax.broadcasted_iota(I32, (tk, tq), 1)
    causal = row_k <= col_q

    def reset():
        m_ref[...] = jnp.full_like(m_ref, NEG_INF)
        acc_ref[...] = jnp.zeros_like(acc_ref)

    def add_branch(branch):
        for h in range(nh):
            inv_l = 1.0 / acc_ref[h, dk:dk + 1, :]
            tot_ref[h] = tot_ref[h] + (gate(h, branch) * inv_l) * acc_ref[h, 0:dk, :]

    def q_cols(h):
        return qaug_ref[h, 0:dk, :]

    ncp = kc_ref.shape[2]
    t_row = i * tq + lax.broadcasted_iota(I32, (1, tq), 1)
    last_n = (t_row - (CMP_BLOCK - 1)) >> CMP_SHIFT
    valid = lax.broadcasted_iota(I32, (ncp, tq), 0) <= last_n
    col_ok = last_n >= 0
    p_heads = {}

    def cmp_update(h, s_c):
        s_c = jnp.where(valid, s_c, NEG_INF)
        e = jnp.exp2(s_c - jnp.max(s_c, axis=0, keepdims=True))
        l_c = jnp.sum(e, axis=0, keepdims=True)
        p_c = e * jnp.where(col_ok, 1.0 / l_c, 0.0)
        o_c = jnp.dot(vct_ref[0, group[h]], p_c.astype(BF16), preferred_element_type=F32)
        tot_ref[h] = gate(h, 0) * o_c
        p_heads[h] = p_c

    nb = mt_ref.shape[0]

    def importance(g, _):
        psum = functools.reduce(lambda a, b: a + b, [p_heads[h] for h in range(nh) if group[h] == g])
        hi = psum.astype(BF16)
        lo = (psum - hi.astype(F32)).astype(BF16)
        mt = mt_ref[...]
        imp = jnp.dot(mt, hi, preferred_element_type=F32) + jnp.dot(mt, lo, preferred_element_type=F32)
        j_idx = lax.broadcasted_iota(I32, (nb, tq), 0)
        cur = (i * tq + lax.broadcasted_iota(I32, (nb, tq), 1)) >> SLC_SHIFT
        forced = (j_idx == 0) | (j_idx == cur) | (j_idx == cur - 1)
        imp_ref[g] = jnp.where(forced, FORCED_SCORE, jnp.where(j_idx > cur, -FORCED_SCORE, imp))

    stages = []
    for g in range(ng):
        kc = kc_ref[0, g, :, 0:dk]
        stages += [(functools.partial(jnp.dot, kc, q_cols(h), preferred_element_type=F32),
                    functools.partial(cmp_update, h)) for h in range(nh) if group[h] == g]
        stages.append((None, functools.partial(importance, g)))

    reset()
    n_back = WINDOW // tk
    for back in range(n_back + 1):
        jb = jnp.maximum(i - back, 0)
        if back == 0:
            keep = causal
        elif back == n_back:
            keep = (row_k > col_q) & (i >= back)
        else:
            keep = jnp.broadcast_to(i >= back, (tk, tq))
        for g in range(ng):
            kt_b = kwin_ref[0, g, pl.ds(pl.multiple_of(jb * tk, tk), tk), 0:dk]
            vt_b = vtw_ref[0, jb, g * vr:(g + 1) * vr, :]
            for h in range(nh):
                if group[h] == g:
                    stages.append((functools.partial(jnp.dot, kt_b, q_cols(h), preferred_element_type=F32),
                                   functools.partial(_chain_update, v_t=vt_b, m_ref=m_ref, acc_ref=acc_ref,
                                                     ch=h, keep=keep)))
    _pipeline(stages, lookahead=5)
    add_branch(2)

    rank_ref[...] = jnp.zeros_like(rank_ref)
    sub = lax.broadcasted_iota(I32, (SUBLANES, tq), 0)
    last_group = ((i + 1) * (tq // SLC_BLOCK) - 1) // SUBLANES

    def count(g, c, v):
        blk = imp_ref[g, v * SUBLANES:(v + 1) * SUBLANES, :]
        cnt = rank_ref[g, v * SUBLANES:(v + 1) * SUBLANES, :]
        for rr in range(SUBLANES):
            row = imp_ref[g, c * SUBLANES + rr:c * SUBLANES + rr + 1, :]
            if c < v:
                beats = row >= blk
            elif c > v:
                beats = row > blk
            else:
                beats = (row > blk) | ((row == blk) & (sub > rr))
            cnt = cnt + beats.astype(I32)
        rank_ref[g, v * SUBLANES:(v + 1) * SUBLANES, :] = cnt

    for lvl in range(nb // SUBLANES):
        @pl.when(lvl <= last_group)
        def _(lvl=lvl):
            for g in range(ng):
                for v in range(lvl + 1):
                    count(g, lvl, v)
                for c in range(lvl):
                    count(g, c, lvl)

    for g in range(ng):
        bias = jnp.where(rank_ref[g] < n_sel, 0.0, SEL_BIAS).astype(BF16)
        for h in range(nh):
            if group[h] == g:
                qaug_ref[h, dk:dk + nb, :] = bias

    reset()
    _causal_sweep(lambda j, g: kaug_ref[0, g, pl.ds(pl.multiple_of(j * tk, tk), tk), :],
                  lambda j, g: vts_ref[0, j, g * vr:(g + 1) * vr, :],
                  group, group, qaug_ref, sa_ref, sb_ref, m_ref, acc_ref, i, causal)
    add_branch(1)

    for h in range(nh):
        zz = z_ref[0, h * dk:(h + 1) * dk, :].astype(F32)
        o_ref[0, h * dk:(h + 1) * dk, :] = (tot_ref[h] * zz).astype(BF16)


def _nsa(qt, kc, vct, kaug, vts, kwin, vtw, gt, ztn, mt, *, tq, tk, n_sel):
    b, _, s = qt.shape
    nt = s // tk
    ncp = kc.shape[2]
    ng, nh = NSA_KV_GROUPS, NSA_HEADS
    hq = nh * (NSA_HEAD_DIM // 2)
    nb = mt.shape[0]
    vr = NSA_HEAD_DIM + ONES_ROWS
    in_specs = [
        pl.BlockSpec((1, hq, tq), lambda bi, i: (bi, 0, i)),
        pl.BlockSpec((1, hq, tq), lambda bi, i: (bi, 1, i)),
        pl.BlockSpec((1, ng, ncp, LANES), lambda bi, i: (bi, 0, 0, 0)),
        pl.BlockSpec((1, ng, NSA_HEAD_DIM, ncp), lambda bi, i: (bi, 0, 0, 0)),
        pl.BlockSpec((1, ng, s, LANES), lambda bi, i: (bi, 0, 0, 0)),
        pl.BlockSpec((1, nt, ng * vr, tk), lambda bi, i: (bi, 0, 0, 0)),
        pl.BlockSpec((1, ng, s, LANES), lambda bi, i: (bi, 0, 0, 0)),
        pl.BlockSpec((1, nt, ng * vr, tk), lambda bi, i: (bi, 0, 0, 0)),
        pl.BlockSpec((1, ng * GATE_ROWS, tq), lambda bi, i: (bi, 0, i)),
        pl.BlockSpec((1, NSA_WIDTH, tq), lambda bi, i: (bi, 0, i)),
        pl.BlockSpec(mt.shape, lambda bi, i: (0, 0)),
    ]
    kern = functools.partial(_nsa_kernel, tq=tq, tk=tk, n_sel=n_sel)
    return pl.pallas_call(
        kern, grid=(b, s // tq), in_specs=in_specs,
        out_specs=pl.BlockSpec((1, NSA_WIDTH, tq), lambda bi, i: (bi, 0, i)),
        out_shape=jax.ShapeDtypeStruct((b, NSA_WIDTH, s), BF16),
        scratch_shapes=[pltpu.VMEM((nh, NSA_HEAD_DIM + nb, tq), BF16),
                        pltpu.VMEM((nh, tk, tq), F32), pltpu.VMEM((nh, tk, tq), F32),
                        pltpu.VMEM((nh, 1, tq), F32), pltpu.VMEM((nh, vr, tq), F32),
                        pltpu.VMEM((nh, NSA_HEAD_DIM, tq), F32),
                        pltpu.VMEM((ng, nb, tq), F32), pltpu.VMEM((ng, nb, tq), I32)],
        compiler_params=pltpu.CompilerParams(dimension_semantics=("arbitrary", "arbitrary"),
                                             vmem_limit_bytes=VMEM_LIMIT_BYTES),
        name="nsa",
    )(qt, qt, kc, vct, kaug, vts, kwin, vtw, gt, ztn, mt)


def _mla_kernel(q_ref, qn_ref, k_ref, vt_ref, z_ref, o_ref, sa_ref, sb_ref, m_ref, acc_ref, *, tq, tk):
    i = pl.program_id(1)
    dv, vr = MLA_V_DIM, MLA_V_DIM + ONES_ROWS
    m_ref[...] = jnp.full_like(m_ref, NEG_INF)
    acc_ref[...] = jnp.zeros_like(acc_ref)

    causal = lax.broadcasted_iota(I32, (tk, tq), 0) <= lax.broadcasted_iota(I32, (tk, tq), 1)
    heads = list(range(MLA_HEADS))
    _causal_sweep(lambda j, hd: k_ref[0, hd, pl.ds(pl.multiple_of(j * tk, tk), tk), :],
                  lambda j, hd: vt_ref[0, j, hd * vr:(hd + 1) * vr, :],
                  heads, heads, q_ref.at[0], sa_ref, sb_ref, m_ref, acc_ref, i, causal,
                  q_next_ref=qn_ref.at[0], first=i == 0)
    for hd in range(MLA_HEADS):
        o_h = acc_ref[hd, 0:dv, :] * (1.0 / acc_ref[hd, dv:dv + 1, :])
        zz = z_ref[0, hd * dv:(hd + 1) * dv, :].astype(F32)
        o_ref[0, hd * dv:(hd + 1) * dv, :] = (o_h * zz).astype(BF16)


def _mla(qtm, kmla, vtm, ztm, *, tq, tk):
    b, _, s, _ = kmla.shape
    nt = s // tk
    vrows = MLA_HEADS * (MLA_V_DIM + ONES_ROWS)
    kern = functools.partial(_mla_kernel, tq=tq, tk=tk)
    return pl.pallas_call(
        kern, grid=(b, s // tq),
        in_specs=[pl.BlockSpec((1, MLA_HEADS, MLA_QK, tq), lambda bi, i: (bi, 0, 0, i)),
                  pl.BlockSpec((1, MLA_HEADS, MLA_QK, tq), lambda bi, i: (bi, 0, 0, jnp.minimum(i + 1, s // tq - 1))),
                  pl.BlockSpec((1, MLA_HEADS, s, MLA_QK), lambda bi, i: (bi, 0, 0, 0)),
                  pl.BlockSpec((1, nt, vrows, tk), lambda bi, i: (bi, 0, 0, 0)),
                  pl.BlockSpec((1, MLA_WIDTH, tq), lambda bi, i: (bi, 0, i))],
        out_specs=pl.BlockSpec((1, MLA_WIDTH, tq), lambda bi, i: (bi, 0, i)),
        out_shape=jax.ShapeDtypeStruct((b, MLA_WIDTH, s), BF16),
        scratch_shapes=[pltpu.VMEM((MLA_HEADS, tk, tq), F32), pltpu.VMEM((MLA_HEADS, tk, tq), F32),
                        pltpu.VMEM((MLA_HEADS, 1, tq), F32), pltpu.VMEM((MLA_HEADS, MLA_V_DIM + ONES_ROWS, tq), F32)],
        compiler_params=pltpu.CompilerParams(dimension_semantics=("arbitrary", "arbitrary"),
                                             vmem_limit_bytes=VMEM_LIMIT_BYTES),
        name="mla",
    )(qtm, qtm, kmla, vtm, ztm)


def _out_kernel(x_ref, mn_ref, mm_ref, w_ref, mod_ref, fg_ref, o_ref, *, final):
    y = lax.dot_general(mn_ref[0], w_ref[0:NSA_WIDTH, :], TN, preferred_element_type=F32)
    y = y + lax.dot_general(mm_ref[0], w_ref[NSA_WIDTH:MIX_WIDTH, :], TN, preferred_element_type=F32)
    x2 = x_ref[0] + mod_ref[0][2:3] * y
    o_ref[0] = _rms(x2, fg_ref[...]) if final else x2


def _out(x, mn, mm, w_out, mod3, fg, *, tm, final):
    b, s, d = x.shape
    return pl.pallas_call(
        functools.partial(_out_kernel, final=final), grid=(b, s // tm),
        in_specs=[pl.BlockSpec((1, tm, d), lambda bi, i: (bi, i, 0)),
                  pl.BlockSpec((1, NSA_WIDTH, tm), lambda bi, i: (bi, 0, i)),
                  pl.BlockSpec((1, MLA_WIDTH, tm), lambda bi, i: (bi, 0, i)),
                  pl.BlockSpec(w_out.shape, lambda bi, i: (0, 0)),
                  pl.BlockSpec((1, 3, d), lambda bi, i: (bi, 0, 0)),
                  pl.BlockSpec((1, d), lambda bi, i: (0, 0))],
        out_specs=pl.BlockSpec((1, tm, d), lambda bi, i: (bi, i, 0)),
        out_shape=jax.ShapeDtypeStruct((b, s, d), F32),
        compiler_params=pltpu.CompilerParams(vmem_limit_bytes=VMEM_LIMIT_BYTES),
        name="out_proj",
    )(x, mn, mm, w_out, mod3, fg)


def _cmp_to_slc_t(ncp, nc, nslc, nb):
    start = np.arange(nc)[:, None] * CMP_STRIDE
    bstart = np.arange(nslc)[None, :] * SLC_BLOCK
    ov = np.minimum(start + CMP_BLOCK, bstart + SLC_BLOCK) - np.maximum(start, bstart)
    m = (np.clip(ov, 0, None) / CMP_BLOCK).astype(np.float32)
    out = np.zeros((nb, ncp), np.float32)
    out[:nslc, :nc] = m.T
    return out


def _layout_w_in(w):
    d = w.shape[0]
    o = (0,) + IN_OFFSETS + (w.shape[1],)
    q_n, ks_n, vs_n, kw_n, vw_n, gl_n, kr_m = (w[:, o[k]:o[k + 1]] for k in (0, 3, 4, 5, 6, 7, 11))
    dk = NSA_HEAD_DIM

    def per_group(kx):
        return jnp.pad(kx.reshape(d, NSA_KV_GROUPS, dk), ((0, 0), (0, 0), (0, LANES - dk))).reshape(d, -1)

    wtok = jnp.concatenate(
        [per_group(ks_n), per_group(kw_n), w[:, o[1]:o[3]], w[:, o[9]:o[11]],
         jnp.pad(kr_m, ((0, 0), (KR_LANE, LANES - KR_LANE - MLA_ROPE_DIM)))], axis=1)
    qr = q_n.reshape(d, NSA_HEADS, 2, dk // 2)
    q_perm = jnp.swapaxes(qr, 1, 2).reshape(d, -1)
    gl = gl_n.reshape(d, NSA_KV_GROUPS, NSA_HPG * N_BRANCH)
    gl = jnp.pad(gl, ((0, 0), (0, 0), (0, GATE_ROWS - NSA_HPG * N_BRANCH))).reshape(d, -1)
    wtr = jnp.concatenate([q_perm, vs_n, vw_n, gl, w[:, o[8]:o[9]], w[:, o[12]:o[13]]], axis=1).T
    assert wtok.shape[1] == TOK_COLS and wtr.shape[0] == TR_ROWS
    return wtok.astype(BF16), wtr.astype(BF16)


def _layout_w1(w1):
    hid = w1.shape[1]
    ng = NSA_KV_GROUPS
    w1r = w1.reshape(2, CMP_STRIDE, 1, NSA_HEAD_DIM, 1, hid)
    blocks = [jnp.pad(w1r, ((0, 0), (0, 0), (0, 0), (0, 0), (g, ng - 1 - g), (0, 0))) for g in range(ng)]
    full = jnp.concatenate(blocks, axis=2)
    full = full.reshape(2, CMP_STRIDE * NSA_KV_WIDTH, ng * hid)
    return jnp.concatenate([full[0], full[1]], axis=1).astype(BF16)


def kernel(x, c, positions, ada_w, ada_b, norm_g, w_in, cmp_pos, cmp_k_w1, cmp_k_w2, cmp_v_w1, cmp_v_w2,
           q_norm_g, w_q_up, kv_norm_g, w_kv_up, w_out, final_norm_g):
    b, s, d = x.shape
    depth = ada_w.shape[0]
    tm, tq = PROJ_TILE, ATT_TILE
    tk = tq
    assert s % tm == 0 and tm % tk == 0 and WINDOW % tk == 0 and (tq & (tq - 1)) == 0 and s % OUT_TILE == 0
    assert CMP_BLOCK == 2 * CMP_STRIDE and s % SLC_BLOCK == 0
    nslc = s // SLC_BLOCK
    nb = LANES - NSA_HEAD_DIM
    assert nslc <= nb
    ncp = s // CMP_STRIDE
    nc = ncp - 1

    half_n, half_m = NSA_HEAD_DIM // 2, MLA_ROPE_DIM // 2
    inv_n = ROPE_THETA ** (-jnp.arange(half_n, dtype=F32) / half_n)
    inv_m = ROPE_THETA ** (-jnp.arange(half_m, dtype=F32) / half_m)
    ones_n, ones_m = jnp.ones((half_n,), F32), jnp.ones((half_m,), F32)
    pad = jnp.zeros((LANES - NSA_HEAD_DIM - MLA_ROPE_DIM,), F32)
    inv_l = jnp.concatenate([inv_n, inv_n, inv_m, inv_m, pad])[None]
    sign_l = jnp.concatenate([-ones_n, ones_n, -ones_m, ones_m, pad])[None]
    pos_f = positions.astype(F32)
    pos_b = jnp.broadcast_to(pos_f.reshape(b * s, 1), (b * s, LANES))
    pos_row = pos_f.reshape(b, 1, s)
    cmp_end = np.minimum(np.arange(ncp) * CMP_STRIDE + CMP_BLOCK - 1, s - 1)
    pos_c = jnp.broadcast_to(pos_f[:, cmp_end].reshape(b * ncp, 1), (b * ncp, LANES))
    inv_nb = jnp.broadcast_to(inv_n[:, None], (half_n, tm))
    inv_mb = jnp.broadcast_to(inv_m[:, None], (half_m, tm))

    mt = jnp.asarray(_cmp_to_slc_t(ncp, nc, nslc, nb), dtype=BF16)
    bp = -(-b // SUBLANES) * SUBLANES
    c_pad = jnp.pad(c, ((0, bp - b), (0, 0)))

    for l in range(depth):
        mod = _adaln(c_pad, ada_w[l], ada_b[l].reshape(1, -1))
        mod3 = mod[:b].reshape(b, 3, d)
        wtok, wtr = _layout_w_in(w_in[l])
        wq = w_q_up[l].reshape(MLA_Q_RANK, MLA_HEADS, MLA_NOPE_DIM + MLA_ROPE_DIM)
        wqt = jnp.concatenate([wq[:, :, :MLA_NOPE_DIM].reshape(MLA_Q_RANK, -1),
                               wq[:, :, MLA_NOPE_DIM:MLA_NOPE_DIM + half_m].reshape(MLA_Q_RANK, -1),
                               wq[:, :, MLA_NOPE_DIM + half_m:].reshape(MLA_Q_RANK, -1)], axis=1).T.astype(BF16)
        wkv = w_kv_up[l].reshape(MLA_KV_RANK, MLA_HEADS, MLA_NOPE_DIM + MLA_V_DIM)
        wkn = jnp.pad(wkv[:, :, :MLA_NOPE_DIM], ((0, 0), (0, 0), (0, LANES - MLA_NOPE_DIM))).reshape(
            MLA_KV_RANK, MLA_HEADS * LANES).astype(BF16)
        wv = wkv[:, :, MLA_NOPE_DIM:].transpose(1, 2, 0).reshape(MLA_WIDTH, MLA_KV_RANK).astype(BF16)

        (qt, kaug, kwin, vts, vtw, kcmp, vcmp, gt, ztn, ztm, qtm, kmla, vtm) = _proj(
            x, mod3, norm_g[l].reshape(1, d), wtok, wtr, pos_b, pos_row, inv_l, sign_l, inv_nb, inv_mb,
            q_norm_g[l].reshape(1, -1), kv_norm_g[l].reshape(1, -1), wqt, wkn, wv, tm=tm, tk=tk)

        pos_l = cmp_pos[l]
        ptop = jnp.broadcast_to(pos_l[:CMP_STRIDE, None, :], (CMP_STRIDE, NSA_KV_GROUPS, NSA_HEAD_DIM)).reshape(1, -1)
        pbot = jnp.broadcast_to(pos_l[CMP_STRIDE:, None, :], (CMP_STRIDE, NSA_KV_GROUPS, NSA_HEAD_DIM)).reshape(1, -1)
        wk2 = jnp.pad(cmp_k_w2[l], ((0, 0), (0, LANES - NSA_HEAD_DIM))).astype(BF16)
        kc, vct = _compress(kcmp, vcmp, ptop, pbot,
                            _layout_w1(cmp_k_w1[l]), _layout_w1(cmp_v_w1[l]), wk2,
                            cmp_v_w2[l].T.astype(BF16), pos_c, inv_l, sign_l)

        mix_n = _nsa(qt, kc, vct, kaug, vts, kwin, vtw, gt, ztn, mt, tq=tq, tk=tk, n_sel=min(SLC_TOPK, nslc))
        mix_m = _mla(qtm, kmla, vtm, ztm, tq=tq, tk=tk)
        x = _out(x, mix_n, mix_m, w_out[l].astype(BF16), mod3, final_norm_g.reshape(1, d), tm=OUT_TILE,
                 final=(l == depth - 1))
    return x
```

```python
import functools

import numpy as np
import jax
import jax.numpy as jnp
from jax import lax
from jax.experimental import pallas as pl
from jax.experimental.pallas import tpu as pltpu

F32 = jnp.float32
BF16 = jnp.bfloat16
I32 = jnp.int32

NSA_HEADS = 8
NSA_KV_GROUPS = 2
NSA_HPG = NSA_HEADS // NSA_KV_GROUPS
NSA_HEAD_DIM = 64
NSA_WIDTH = NSA_HEADS * NSA_HEAD_DIM
NSA_KV_WIDTH = NSA_KV_GROUPS * NSA_HEAD_DIM
CMP_BLOCK = 32
CMP_STRIDE = 16
CMP_HIDDEN = 128
SLC_BLOCK = 64
SLC_TOPK = 16
WINDOW = 512
N_BRANCH = 3
FORCED_SCORE = 1.0e4
MLA_HEADS = 8
MLA_NOPE_DIM = 64
MLA_ROPE_DIM = 32
MLA_V_DIM = 64
MLA_WIDTH = MLA_HEADS * MLA_V_DIM
MLA_Q_RANK = 256
MLA_KV_RANK = 128
MIX_WIDTH = NSA_WIDTH + MLA_WIDTH
ROPE_THETA = 10000.0
NORM_EPS = 1e-6
NEG_INF = -1e30
IN_SIZES = (NSA_WIDTH, NSA_KV_WIDTH, NSA_KV_WIDTH, NSA_KV_WIDTH, NSA_KV_WIDTH, NSA_KV_WIDTH, NSA_KV_WIDTH,
            NSA_HEADS * N_BRANCH, NSA_WIDTH, MLA_Q_RANK, MLA_KV_RANK, MLA_ROPE_DIM, MLA_WIDTH)
IN_OFFSETS = tuple(int(o) for o in np.cumsum(IN_SIZES)[:-1])

LANES = 128
SUBLANES = 8
VMEM_LIMIT_BYTES = 56 * 1024 * 1024

PROJ_TILE = 1024
OUT_TILE = 1024
ATT_TILE = 256
SEL_BIAS = NEG_INF
GATE_ROWS = 16
ONES_ROWS = 16
MLA_QK = LANES
LOG2E = float(np.log2(np.e))
SWEEP_UNROLL = 4
SLC_SHIFT = SLC_BLOCK.bit_length() - 1
CMP_SHIFT = CMP_STRIDE.bit_length() - 1
assert 1 << SLC_SHIFT == SLC_BLOCK and 1 << CMP_SHIFT == CMP_STRIDE

NT = (((1,), (1,)), ((), ()))
TN = (((0,), (0,)), ((), ()))


def _silu(v):
    return v * jax.nn.sigmoid(v)


def _rms(v, g):
    ms = jnp.mean(v * v, axis=-1, keepdims=True)
    return v * lax.rsqrt(ms + NORM_EPS) * g


def _rope_tok(v, c, s_signed, half, lane, base=0):
    up = pltpu.roll(v, LANES - half, axis=1)
    dn = pltpu.roll(v, half, axis=1)
    return v * c + jnp.where(lane < base + half, up, dn) * s_signed


def _rope_lanes(pos_b, inv_ref, sign_ref):
    ang = pos_b * inv_ref[...]
    return jnp.cos(ang), jnp.sin(ang) * sign_ref[...]


def _chain_update(s_t, v_t, m_ref, acc_ref, ch, keep=None):
    if keep is not None:
        s_t = jnp.where(keep, s_t, NEG_INF)
    m_prev = m_ref[ch]
    m_new = jnp.maximum(m_prev, jnp.max(s_t, axis=0, keepdims=True))
    alpha = jnp.exp2(m_prev - m_new)
    p = jnp.exp2(s_t - m_new)
    acc_ref[ch] = alpha * acc_ref[ch] + jnp.dot(v_t, p.astype(BF16), preferred_element_type=F32)
    m_ref[ch] = m_new


def _causal_sweep(k_tile, v_tile, k_group, v_group, q_ref, sa_ref, sb_ref, m_ref, acc_ref, last, keep_last,
                  lookahead=2, q_next_ref=None, first=None):
    n_chains = len(k_group)

    def loader(tile_fn, j):
        cache = {}
        return lambda g: cache.setdefault(g, tile_fn(j, g))

    def phase(j_next, s_next_ref, s_cur_ref, j_cur, keep=None, ahead=False):
        k_next = loader(k_tile, j_next) if j_next is not None else None
        k_zero = loader(k_tile, 0) if ahead else None
        v_cur = loader(v_tile, j_cur)
        for n in range(n_chains + lookahead):
            if k_next is not None and n < n_chains:
                s_next_ref[n] = jnp.dot(k_next(k_group[n]), q_ref[n], preferred_element_type=F32)
            if n >= lookahead:
                ch = n - lookahead
                _chain_update(s_cur_ref[ch], v_cur(v_group[ch]), m_ref, acc_ref, ch, keep=keep)
                if ahead:
                    sa_ref[ch] = jnp.dot(k_zero(k_group[ch]), q_next_ref[ch], preferred_element_type=F32)

    bufs = (sa_ref, sb_ref)

    def prologue():
        k_0 = loader(k_tile, 0)
        for ch in range(n_chains):
            sa_ref[ch] = jnp.dot(k_0(k_group[ch]), q_ref[ch], preferred_element_type=F32)

    if q_next_ref is None:
        prologue()
    else:
        pl.when(first)(prologue)

    def run(j0, count):
        for u in range(count):
            phase(j0 + u + 1, bufs[(u + 1) % 2], bufs[u % 2], j0 + u)

    def body(jj, carry):
        run(SWEEP_UNROLL * jj, SWEEP_UNROLL)
        return carry

    lax.fori_loop(0, last // SWEEP_UNROLL, body, 0)
    rem = last % SWEEP_UNROLL
    for r in range(SWEEP_UNROLL):
        @pl.when(rem == r)
        def _(r=r):
            run(last - r, r)
            phase(None, None, bufs[r % 2], last, keep=keep_last, ahead=q_next_ref is not None)


def _pipeline(stages, lookahead):
    pending = {}
    for n in range(len(stages) + lookahead):
        if n < len(stages) and stages[n][0] is not None:
            pending[n] = stages[n][0]()
        if n >= lookahead:
            stages[n - lookahead][1](pending.pop(n - lookahead, None))


def _adaln_kernel(c_ref, w_ref, b_ref, o_ref):
    sc = _silu(c_ref[...])
    o_ref[...] = jnp.dot(sc.astype(BF16), w_ref[...].astype(BF16), preferred_element_type=F32) + b_ref[...]


def _adaln(c_pad, w, b):
    bp, d = c_pad.shape
    n = w.shape[1] // d
    return pl.pallas_call(
        _adaln_kernel,
        grid=(n,),
        in_specs=[pl.BlockSpec((bp, d), lambda j: (0, 0)),
                  pl.BlockSpec((d, d), lambda j: (0, j)),
                  pl.BlockSpec((1, d), lambda j: (0, j))],
        out_specs=pl.BlockSpec((bp, d), lambda j: (0, j)),
        out_shape=jax.ShapeDtypeStruct((bp, n * d), F32),
        name="adaln",
    )(c_pad, w, b)


TOK_KS, TOK_KW, TOK_KC, TOK_VC, TOK_CQ, TOK_CKV, TOK_COLS = (int(o) for o in np.cumsum(
    [0, NSA_KV_GROUPS * LANES, NSA_KV_WIDTH, NSA_KV_WIDTH, NSA_KV_WIDTH, MLA_Q_RANK, MLA_KV_RANK]))
TOK_KR = TOK_KS
KR_LANE = NSA_HEAD_DIM
assert NSA_KV_WIDTH == LANES and KR_LANE + MLA_ROPE_DIM <= LANES
TR_Q, TR_VS, TR_VW, TR_G, TR_ZN, TR_ZM, TR_ROWS = (int(o) for o in np.cumsum(
    [0, NSA_WIDTH, NSA_KV_WIDTH, NSA_KV_WIDTH, NSA_KV_GROUPS * GATE_ROWS, NSA_WIDTH, MLA_WIDTH]))


def _proj_kernel(x_ref, mod_ref, ng_ref, wtok_ref, wtr_ref, posr_ref,
                 invn_ref, invm_ref, qng_ref, kvng_ref, wqt_ref, wkn_ref, wv_ref,
                 qt_ref, kaug_ref, kwin_ref, vts_ref, vtw_ref, kcmp_ref, vcmp_ref, gt_ref,
                 ztn_ref, ztm_ref, qtm_ref, kmla_ref, vtm_ref, *, tm, tk, scale_nsa, scale_mla):
    i = pl.program_id(1)
    mod = mod_ref[0]
    h = _rms(x_ref[0], ng_ref[...]) * (1.0 + mod[1:2]) + mod[0:1]
    hb = h.astype(BF16)
    tok = jnp.dot(hb, wtok_ref[...], preferred_element_type=F32)
    tr = lax.dot_general(wtr_ref[...], hb, NT, preferred_element_type=F32)

    lane = lax.broadcasted_iota(I32, (tm, LANES), 1)
    row = lax.broadcasted_iota(I32, (tm, LANES), 0)
    blk = (i * tm + row) >> SLC_SHIFT
    onehot = (lane - NSA_HEAD_DIM == blk).astype(F32)
    ang_n = invn_ref[...] * posr_ref[0]
    cn, sn = jnp.cos(ang_n), jnp.sin(ang_n)
    ang_m = invm_ref[...] * posr_ref[0]
    cm, sm = jnp.cos(ang_m), jnp.sin(ang_m)
    zpad = jnp.zeros((LANES - NSA_HEAD_DIM - MLA_ROPE_DIM, tm), F32)
    ct = jnp.concatenate([cn, cn, cm, cm, zpad], axis=0).T
    st = jnp.concatenate([-sn, sn, -sm, sm, zpad], axis=0).T
    half_n = NSA_HEAD_DIM // 2
    for g in range(NSA_KV_GROUPS):
        ks = _rope_tok(tok[:, TOK_KS + LANES * g:TOK_KS + LANES * (g + 1)], ct, st, half_n, lane)
        kaug_ref[0, g] = jnp.where(lane >= NSA_HEAD_DIM, onehot, ks).astype(BF16)
    ctw = jnp.concatenate([cn, cn] * NSA_KV_GROUPS, axis=0).T
    stw = jnp.concatenate([-sn, sn] * NSA_KV_GROUPS, axis=0).T
    kwp = tok[:, TOK_KW:TOK_KW + LANES]
    first_half = (lane & (NSA_HEAD_DIM - 1)) < half_n
    kw = kwp * ctw + jnp.where(first_half, pltpu.roll(kwp, LANES - half_n, axis=1),
                               pltpu.roll(kwp, half_n, axis=1)) * stw
    for g in range(NSA_KV_GROUPS):
        kwin_ref[0, g] = (kw if g == 0 else pltpu.roll(kw, LANES - g * NSA_HEAD_DIM, axis=1)).astype(BF16)
    kcmp_ref[0] = tok[:, TOK_KC:TOK_KC + LANES]
    vcmp_ref[0] = tok[:, TOK_VC:TOK_VC + LANES]

    ckvn = _rms(tok[:, TOK_CKV:TOK_CKV + MLA_KV_RANK], kvng_ref[...])
    krr = _rope_tok(tok[:, TOK_KR:TOK_KR + LANES], ct, st, MLA_ROPE_DIM // 2, lane, base=KR_LANE)
    ckvb = ckvn.astype(BF16)
    kn = jnp.dot(ckvb, wkn_ref[...], preferred_element_type=F32)
    is_rot = (lane >= KR_LANE) & (lane < KR_LANE + MLA_ROPE_DIM)
    for hd in range(MLA_HEADS):
        kmla_ref[0, hd] = jnp.where(is_rot, krr, kn[:, hd * LANES:(hd + 1) * LANES]).astype(BF16)
    vtm = lax.dot_general(wv_ref[...], ckvb, NT, preferred_element_type=F32).astype(BF16)
    ones = jnp.ones((ONES_ROWS, tk), BF16)
    vr = MLA_V_DIM + ONES_ROWS
    for ii in range(tm // tk):
        for hd in range(MLA_HEADS):
            vtm_ref[0, ii, hd * vr:hd * vr + MLA_V_DIM, :] = vtm[hd * MLA_V_DIM:(hd + 1) * MLA_V_DIM,
                                                                 ii * tk:(ii + 1) * tk]
            vtm_ref[0, ii, hd * vr + MLA_V_DIM:(hd + 1) * vr, :] = ones

    cqn = _rms(tok[:, TOK_CQ:TOK_CQ + MLA_Q_RANK], qng_ref[...]).astype(BF16)
    qm = lax.dot_general(wqt_ref[...], cqn, NT, preferred_element_type=F32)
    nq = MLA_HEADS * MLA_NOPE_DIM
    hr = MLA_ROPE_DIM // 2
    x1 = qm[nq:nq + MLA_HEADS * hr].reshape(MLA_HEADS, hr, tm)
    x2 = qm[nq + MLA_HEADS * hr:nq + 2 * MLA_HEADS * hr].reshape(MLA_HEADS, hr, tm)
    cm_t, sm_t = cm[None], sm[None]
    o1 = (x1 * cm_t - x2 * sm_t) * scale_mla
    o2 = (x2 * cm_t + x1 * sm_t) * scale_mla
    nd = MLA_NOPE_DIM
    q_pad = jnp.zeros((MLA_QK - nd - MLA_ROPE_DIM, tm), BF16)
    for hd in range(MLA_HEADS):
        qtm_ref[0, hd, 0:nd, :] = (qm[hd * nd:(hd + 1) * nd] * scale_mla).astype(BF16)
        qtm_ref[0, hd, nd:nd + hr, :] = o1[hd].astype(BF16)
        qtm_ref[0, hd, nd + hr:nd + 2 * hr, :] = o2[hd].astype(BF16)
        qtm_ref[0, hd, nd + 2 * hr:MLA_QK, :] = q_pad

    hq = NSA_HEADS * half_n
    q1 = tr[TR_Q:TR_Q + hq].reshape(NSA_HEADS, half_n, tm)
    q2 = tr[TR_Q + hq:TR_Q + 2 * hq].reshape(NSA_HEADS, half_n, tm)
    cn_t, sn_t = cn[None], sn[None]
    qt_ref[0, 0:hq, :] = ((q1 * cn_t - q2 * sn_t) * scale_nsa).reshape(hq, tm).astype(BF16)
    qt_ref[0, hq:2 * hq, :] = ((q2 * cn_t + q1 * sn_t) * scale_nsa).reshape(hq, tm).astype(BF16)

    vts = tr[TR_VS:TR_VS + NSA_KV_WIDTH].astype(BF16)
    vtw = tr[TR_VW:TR_VW + NSA_KV_WIDTH].astype(BF16)
    dk = NSA_HEAD_DIM
    gr = dk + ONES_ROWS
    for ii in range(tm // tk):
        for g in range(NSA_KV_GROUPS):
            vts_ref[0, ii, g * gr:g * gr + dk, :] = vts[g * dk:(g + 1) * dk, ii * tk:(ii + 1) * tk]
            vtw_ref[0, ii, g * gr:g * gr + dk, :] = vtw[g * dk:(g + 1) * dk, ii * tk:(ii + 1) * tk]
            vts_ref[0, ii, g * gr + dk:(g + 1) * gr, :] = ones
            vtw_ref[0, ii, g * gr + dk:(g + 1) * gr, :] = ones
    gt_ref[0] = jax.nn.sigmoid(tr[TR_G:TR_G + NSA_KV_GROUPS * GATE_ROWS])
    ztn_ref[0] = _silu(tr[TR_ZN:TR_ZN + NSA_WIDTH]).astype(BF16)
    ztm_ref[0] = _silu(tr[TR_ZM:TR_ZM + MLA_WIDTH]).astype(BF16)


def _proj(x, mod3, ng, wtok, wtr, pos_row, inv_nb, inv_mb, qng, kvng, wqt, wkn, wv, *, tm, tk):
    b, s, d = x.shape
    nt = s // tk
    v_rows_n = NSA_KV_GROUPS * (NSA_HEAD_DIM + ONES_ROWS)
    v_rows_m = MLA_HEADS * (MLA_V_DIM + ONES_ROWS)

    def full(a):
        return pl.BlockSpec(a.shape, lambda bi, i, _n=a.ndim: (0,) * _n)

    in_specs = [pl.BlockSpec((1, tm, d), lambda bi, i: (bi, i, 0)),
                pl.BlockSpec((1, 3, d), lambda bi, i: (bi, 0, 0)),
                full(ng), full(wtok), full(wtr),
                pl.BlockSpec((1, 1, tm), lambda bi, i: (bi, 0, i)),
                full(inv_nb), full(inv_mb),
                full(qng), full(kvng), full(wqt), full(wkn), full(wv)]
    out_shape = [
        jax.ShapeDtypeStruct((b, NSA_WIDTH, s), BF16),
        jax.ShapeDtypeStruct((b, NSA_KV_GROUPS, s, LANES), BF16),
        jax.ShapeDtypeStruct((b, NSA_KV_GROUPS, s, LANES), BF16),
        jax.ShapeDtypeStruct((b, nt, v_rows_n, tk), BF16),
        jax.ShapeDtypeStruct((b, nt, v_rows_n, tk), BF16),
        jax.ShapeDtypeStruct((b, s, NSA_KV_WIDTH), F32),
        jax.ShapeDtypeStruct((b, s, NSA_KV_WIDTH), F32),
        jax.ShapeDtypeStruct((b, NSA_KV_GROUPS * GATE_ROWS, s), F32),
        jax.ShapeDtypeStruct((b, NSA_WIDTH, s), BF16),
        jax.ShapeDtypeStruct((b, MLA_WIDTH, s), BF16),
        jax.ShapeDtypeStruct((b, MLA_HEADS, MLA_QK, s), BF16),
        jax.ShapeDtypeStruct((b, MLA_HEADS, s, MLA_QK), BF16),
        jax.ShapeDtypeStruct((b, nt, v_rows_m, tk), BF16),
    ]
    out_specs = [
        pl.BlockSpec((1, NSA_WIDTH, tm), lambda bi, i: (bi, 0, i)),
        pl.BlockSpec((1, NSA_KV_GROUPS, tm, LANES), lambda bi, i: (bi, 0, i, 0)),
        pl.BlockSpec((1, NSA_KV_GROUPS, tm, LANES), lambda bi, i: (bi, 0, i, 0)),
        pl.BlockSpec((1, tm // tk, v_rows_n, tk), lambda bi, i: (bi, i, 0, 0)),
        pl.BlockSpec((1, tm // tk, v_rows_n, tk), lambda bi, i: (bi, i, 0, 0)),
        pl.BlockSpec((1, tm, NSA_KV_WIDTH), lambda bi, i: (bi, i, 0)),
        pl.BlockSpec((1, tm, NSA_KV_WIDTH), lambda bi, i: (bi, i, 0)),
        pl.BlockSpec((1, NSA_KV_GROUPS * GATE_ROWS, tm), lambda bi, i: (bi, 0, i)),
        pl.BlockSpec((1, NSA_WIDTH, tm), lambda bi, i: (bi, 0, i)),
        pl.BlockSpec((1, MLA_WIDTH, tm), lambda bi, i: (bi, 0, i)),
        pl.BlockSpec((1, MLA_HEADS, MLA_QK, tm), lambda bi, i: (bi, 0, 0, i)),
        pl.BlockSpec((1, MLA_HEADS, tm, MLA_QK), lambda bi, i: (bi, 0, i, 0)),
        pl.BlockSpec((1, tm // tk, v_rows_m, tk), lambda bi, i: (bi, i, 0, 0)),
    ]
    kern = functools.partial(_proj_kernel, tm=tm, tk=tk, scale_nsa=NSA_HEAD_DIM ** -0.5 * LOG2E,
                             scale_mla=(MLA_NOPE_DIM + MLA_ROPE_DIM) ** -0.5 * LOG2E)
    return pl.pallas_call(
        kern, grid=(b, s // tm), in_specs=in_specs, out_specs=out_specs, out_shape=out_shape,
        compiler_params=pltpu.CompilerParams(vmem_limit_bytes=VMEM_LIMIT_BYTES),
        name="proj",
    )(x, mod3, ng, wtok, wtr, pos_row, inv_nb, inv_mb, qng, kvng, wqt, wkn, wv)


def _compress_kernel(k_ref, v_ref, ptop_ref, pbot_ref, wk1_ref, wv1_ref, wk2_ref, wv2t_ref, posc_ref, invl_ref,
                     signl_ref, kc_ref, vct_ref):
    ncp = k_ref.shape[1] // CMP_STRIDE
    gw = NSA_KV_GROUPS * CMP_HIDDEN
    lane = lax.broadcasted_iota(I32, (ncp, LANES), 1)

    def hidden(r_ref, w1_ref):
        r = jnp.concatenate([r_ref[0, pl.ds(t, ncp, stride=CMP_STRIDE), :] for t in range(CMP_STRIDE)], axis=1)
        a = jnp.dot((r + ptop_ref[...]).astype(BF16), w1_ref[:, 0:gw], preferred_element_type=F32)
        bt = jnp.dot((r + pbot_ref[...]).astype(BF16), w1_ref[:, gw:2 * gw], preferred_element_type=F32)
        return _silu(a + pltpu.roll(bt, ncp - 1, axis=0))

    cc, sc = _rope_lanes(posc_ref[...], invl_ref, signl_ref)
    hk = hidden(k_ref, wk1_ref)
    hv = hidden(v_ref, wv1_ref)
    for g in range(NSA_KV_GROUPS):
        hkg = hk[:, g * CMP_HIDDEN:(g + 1) * CMP_HIDDEN].astype(BF16)
        kc = jnp.dot(hkg, wk2_ref[...], preferred_element_type=F32)
        kc_ref[0, g] = _rope_tok(kc, cc, sc, NSA_HEAD_DIM // 2, lane).astype(BF16)
        hvg = hv[:, g * CMP_HIDDEN:(g + 1) * CMP_HIDDEN].astype(BF16)
        vct_ref[0, g] = lax.dot_general(wv2t_ref[...], hvg, NT, preferred_element_type=F32).astype(BF16)


def _compress(kcmp, vcmp, ptop, pbot, wk1, wv1, wk2, wv2t, pos_c, inv_l, sign_l):
    b, s, width = kcmp.shape
    ncp = s // CMP_STRIDE

    def full(a):
        return pl.BlockSpec(a.shape, lambda bi, _n=a.ndim: (0,) * _n)

    blk = pl.BlockSpec((1, s, width), lambda bi: (bi, 0, 0))
    tab = pl.BlockSpec((ncp, LANES), lambda bi: (bi, 0))
    return pl.pallas_call(
        _compress_kernel, grid=(b,),
        in_specs=[blk, blk, full(ptop), full(pbot), full(wk1), full(wv1), full(wk2), full(wv2t), tab,
                  full(inv_l), full(sign_l)],
        out_specs=[pl.BlockSpec((1, NSA_KV_GROUPS, ncp, LANES), lambda bi: (bi, 0, 0, 0)),
                   pl.BlockSpec((1, NSA_KV_GROUPS, NSA_HEAD_DIM, ncp), lambda bi: (bi, 0, 0, 0))],
        out_shape=[jax.ShapeDtypeStruct((b, NSA_KV_GROUPS, ncp, LANES), BF16),
                   jax.ShapeDtypeStruct((b, NSA_KV_GROUPS, NSA_HEAD_DIM, ncp), BF16)],
        compiler_params=pltpu.CompilerParams(vmem_limit_bytes=VMEM_LIMIT_BYTES),
        name="compress",
    )(kcmp, vcmp, ptop, pbot, wk1, wv1, wk2, wv2t, pos_c, inv_l, sign_l)


def _nsa_kernel(q1_ref, q2_ref, kc_ref, vct_ref, kaug_ref, vts_ref, kwin_ref, vtw_ref, g_ref, z_ref, mt_ref,
                o_ref, qaug_ref, sa_ref, sb_ref, m_ref, acc_ref, tot_ref, imp_ref, rank_ref, *, tq, tk, n_sel):
    i = pl.program_id(1)
    nh, ng, hpg, dk, half = NSA_HEADS, NSA_KV_GROUPS, NSA_HPG, NSA_HEAD_DIM, NSA_HEAD_DIM // 2
    vr = dk + ONES_ROWS
    group = [h // hpg for h in range(nh)]
    for h in range(nh):
        qaug_ref[h, 0:half, :] = q1_ref[0, h * half:(h + 1) * half, :]
        qaug_ref[h, half:dk, :] = q2_ref[0, h * half:(h + 1) * half, :]

    def gate(h, branch):
        row = group[h] * GATE_ROWS + (h % hpg) * N_BRANCH + branch
        return g_ref[0, row:row + 1, :]

    row_k = lax.broadcasted_iota(I32, (tk, tq), 0)
    col_q = lax.broadcasted_iota(I32, (tk, tq), 1)
    causal = row_k <= col_q

    def reset():
        m_ref[...] = jnp.full_like(m_ref, NEG_INF)
        acc_ref[...] = jnp.zeros_like(acc_ref)

    def add_branch(branch):
        for h in range(nh):
            inv_l = 1.0 / acc_ref[h, dk:dk + 1, :]
            tot_ref[h] = tot_ref[h] + (gate(h, branch) * inv_l) * acc_ref[h, 0:dk, :]

    def q_cols(h):
        return qaug_ref[h, 0:dk, :]

    ncp = kc_ref.shape[2]
    t_row = i * tq + lax.broadcasted_iota(I32, (1, tq), 1)
    last_n = (t_row - (CMP_BLOCK - 1)) >> CMP_SHIFT
    valid = lax.broadcasted_iota(I32, (ncp, tq), 0) <= last_n
    col_ok = last_n >= 0
    p_heads = {}

    def cmp_update(h, s_c):
        s_c = jnp.where(valid, s_c, NEG_INF)
        e = jnp.exp2(s_c - jnp.max(s_c, axis=0, keepdims=True))
        l_c = jnp.sum(e, axis=0, keepdims=True)
        p_c = e * jnp.where(col_ok, 1.0 / l_c, 0.0)
        o_c = jnp.dot(vct_ref[0, group[h]], p_c.astype(BF16), preferred_element_type=F32)
        tot_ref[h] = gate(h, 0) * o_c
        p_heads[h] = p_c

    nb = mt_ref.shape[0]

    def importance(g, _):
        psum = functools.reduce(lambda a, b: a + b, [p_heads[h] for h in range(nh) if group[h] == g])
        hi = psum.astype(BF16)
        lo = (psum - hi.astype(F32)).astype(BF16)
        mt = mt_ref[...]
        imp = jnp.dot(mt, hi, preferred_element_type=F32) + jnp.dot(mt, lo, preferred_element_type=F32)
        j_idx = lax.broadcasted_iota(I32, (nb, tq), 0)
        cur = (i * tq + lax.broadcasted_iota(I32, (nb, tq), 1)) >> SLC_SHIFT
        forced = (j_idx == 0) | (j_idx == cur) | (j_idx == cur - 1)
        imp_ref[g] = jnp.where(forced, FORCED_SCORE, jnp.where(j_idx > cur, -FORCED_SCORE, imp))

    stages = []
    for g in range(ng):
        kc = kc_ref[0, g, :, 0:dk]
        stages += [(functools.partial(jnp.dot, kc, q_cols(h), preferred_element_type=F32),
                    functools.partial(cmp_update, h)) for h in range(nh) if group[h] == g]
        stages.append((None, functools.partial(importance, g)))

    reset()
    n_back = WINDOW // tk
    for back in range(n_back + 1):
        jb = jnp.maximum(i - back, 0)
        if back == 0:
            keep = causal
        elif back == n_back:
            keep = (row_k > col_q) & (i >= back)
        else:
            keep = jnp.broadcast_to(i >= back, (tk, tq))
        for g in range(ng):
            kt_b = kwin_ref[0, g, pl.ds(pl.multiple_of(jb * tk, tk), tk), 0:dk]
            vt_b = vtw_ref[0, jb, g * vr:(g + 1) * vr, :]
            for h in range(nh):
                if group[h] == g:
                    stages.append((functools.partial(jnp.dot, kt_b, q_cols(h), preferred_element_type=F32),
                                   functools.partial(_chain_update, v_t=vt_b, m_ref=m_ref, acc_ref=acc_ref,
                                                     ch=h, keep=keep)))
    _pipeline(stages, lookahead=5)
    add_branch(2)

    rank_ref[...] = jnp.zeros_like(rank_ref)
    sub = lax.broadcasted_iota(I32, (SUBLANES, tq), 0)
    last_group = ((i + 1) * (tq // SLC_BLOCK) - 1) // SUBLANES

    def count(g, c, v):
        blk = imp_ref[g, v * SUBLANES:(v + 1) * SUBLANES, :]
        cnt = rank_ref[g, v * SUBLANES:(v + 1) * SUBLANES, :]
        for rr in range(SUBLANES):
            row = imp_ref[g, c * SUBLANES + rr:c * SUBLANES + rr + 1, :]
            if c < v:
                beats = row >= blk
            elif c > v:
                beats = row > blk
            else:
                beats = (row > blk) | ((row == blk) & (sub > rr))
            cnt = cnt + beats.astype(I32)
        rank_ref[g, v * SUBLANES:(v + 1) * SUBLANES, :] = cnt

    for lvl in range(nb // SUBLANES):
        @pl.when(lvl <= last_group)
        def _(lvl=lvl):
            for g in range(ng):
                for v in range(lvl + 1):
                    count(g, lvl, v)
                for c in range(lvl):
                    count(g, c, lvl)

    for g in range(ng):
        bias = jnp.where(rank_ref[g] < n_sel, 0.0, SEL_BIAS).astype(BF16)
        for h in range(nh):
            if group[h] == g:
                qaug_ref[h, dk:dk + nb, :] = bias

    reset()
    _causal_sweep(lambda j, g: kaug_ref[0, g, pl.ds(pl.multiple_of(j * tk, tk), tk), :],
                  lambda j, g: vts_ref[0, j, g * vr:(g + 1) * vr, :],
                  group, group, qaug_ref, sa_ref, sb_ref, m_ref, acc_ref, i, causal)
    add_branch(1)

    for h in range(nh):
        zz = z_ref[0, h * dk:(h + 1) * dk, :].astype(F32)
        o_ref[0, h * dk:(h + 1) * dk, :] = (tot_ref[h] * zz).astype(BF16)


def _nsa(qt, kc, vct, kaug, vts, kwin, vtw, gt, ztn, mt, *, tq, tk, n_sel):
    b, _, s = qt.shape
    nt = s // tk
    ncp = kc.shape[2]
    ng, nh = NSA_KV_GROUPS, NSA_HEADS
    hq = nh * (NSA_HEAD_DIM // 2)
    nb = mt.shape[0]
    vr = NSA_HEAD_DIM + ONES_ROWS
    in_specs = [
        pl.BlockSpec((1, hq, tq), lambda bi, i: (bi, 0, i)),
        pl.BlockSpec((1, hq, tq), lambda bi, i: (bi, 1, i)),
        pl.BlockSpec((1, ng, ncp, LANES), lambda bi, i: (bi, 0, 0, 0)),
        pl.BlockSpec((1, ng, NSA_HEAD_DIM, ncp), lambda bi, i: (bi, 0, 0, 0)),
        pl.BlockSpec((1, ng, s, LANES), lambda bi, i: (bi, 0, 0, 0)),
        pl.BlockSpec((1, nt, ng * vr, tk), lambda bi, i: (bi, 0, 0, 0)),
        pl.BlockSpec((1, ng, s, LANES), lambda bi, i: (bi, 0, 0, 0)),
        pl.BlockSpec((1, nt, ng * vr, tk), lambda bi, i: (bi, 0, 0, 0)),
        pl.BlockSpec((1, ng * GATE_ROWS, tq), lambda bi, i: (bi, 0, i)),
        pl.BlockSpec((1, NSA_WIDTH, tq), lambda bi, i: (bi, 0, i)),
        pl.BlockSpec(mt.shape, lambda bi, i: (0, 0)),
    ]
    kern = functools.partial(_nsa_kernel, tq=tq, tk=tk, n_sel=n_sel)
    return pl.pallas_call(
        kern, grid=(b, s // tq), in_specs=in_specs,
        out_specs=pl.BlockSpec((1, NSA_WIDTH, tq), lambda bi, i: (bi, 0, i)),
        out_shape=jax.ShapeDtypeStruct((b, NSA_WIDTH, s), BF16),
        scratch_shapes=[pltpu.VMEM((nh, NSA_HEAD_DIM + nb, tq), BF16),
                        pltpu.VMEM((nh, tk, tq), F32), pltpu.VMEM((nh, tk, tq), F32),
                        pltpu.VMEM((nh, 1, tq), F32), pltpu.VMEM((nh, vr, tq), F32),
                        pltpu.VMEM((nh, NSA_HEAD_DIM, tq), F32),
                        pltpu.VMEM((ng, nb, tq), F32), pltpu.VMEM((ng, nb, tq), I32)],
        compiler_params=pltpu.CompilerParams(dimension_semantics=("arbitrary", "arbitrary"),
                                             vmem_limit_bytes=VMEM_LIMIT_BYTES),
        name="nsa",
    )(qt, qt, kc, vct, kaug, vts, kwin, vtw, gt, ztn, mt)


def _mla_kernel(q_ref, qn_ref, k_ref, vt_ref, z_ref, o_ref, sa_ref, sb_ref, m_ref, acc_ref, *, tq, tk):
    i = pl.program_id(1)
    dv, vr = MLA_V_DIM, MLA_V_DIM + ONES_ROWS
    m_ref[...] = jnp.full_like(m_ref, NEG_INF)
    acc_ref[...] = jnp.zeros_like(acc_ref)

    causal = lax.broadcasted_iota(I32, (tk, tq), 0) <= lax.broadcasted_iota(I32, (tk, tq), 1)
    heads = list(range(MLA_HEADS))
    _causal_sweep(lambda j, hd: k_ref[0, hd, pl.ds(pl.multiple_of(j * tk, tk), tk), :],
                  lambda j, hd: vt_ref[0, j, hd * vr:(hd + 1) * vr, :],
                  heads, heads, q_ref.at[0], sa_ref, sb_ref, m_ref, acc_ref, i, causal,
                  q_next_ref=qn_ref.at[0], first=i == 0)
    for hd in range(MLA_HEADS):
        o_h = acc_ref[hd, 0:dv, :] * (1.0 / acc_ref[hd, dv:dv + 1, :])
        zz = z_ref[0, hd * dv:(hd + 1) * dv, :].astype(F32)
        o_ref[0, hd * dv:(hd + 1) * dv, :] = (o_h * zz).astype(BF16)


def _mla(qtm, kmla, vtm, ztm, *, tq, tk):
    b, _, s, _ = kmla.shape
    nt = s // tk
    vrows = MLA_HEADS * (MLA_V_DIM + ONES_ROWS)
    kern = functools.partial(_mla_kernel, tq=tq, tk=tk)
    return pl.pallas_call(
        kern, grid=(b, s // tq),
        in_specs=[pl.BlockSpec((1, MLA_HEADS, MLA_QK, tq), lambda bi, i: (bi, 0, 0, i)),
                  pl.BlockSpec((1, MLA_HEADS, MLA_QK, tq), lambda bi, i: (bi, 0, 0, jnp.minimum(i + 1, s // tq - 1))),
                  pl.BlockSpec((1, MLA_HEADS, s, MLA_QK), lambda bi, i: (bi, 0, 0, 0)),
                  pl.BlockSpec((1, nt, vrows, tk), lambda bi, i: (bi, 0, 0, 0)),
                  pl.BlockSpec((1, MLA_WIDTH, tq), lambda bi, i: (bi, 0, i))],
        out_specs=pl.BlockSpec((1, MLA_WIDTH, tq), lambda bi, i: (bi, 0, i)),
        out_shape=jax.ShapeDtypeStruct((b, MLA_WIDTH, s), BF16),
        scratch_shapes=[pltpu.VMEM((MLA_HEADS, tk, tq), F32), pltpu.VMEM((MLA_HEADS, tk, tq), F32),
                        pltpu.VMEM((MLA_HEADS, 1, tq), F32), pltpu.VMEM((MLA_HEADS, MLA_V_DIM + ONES_ROWS, tq), F32)],
        compiler_params=pltpu.CompilerParams(dimension_semantics=("arbitrary", "arbitrary"),
                                             vmem_limit_bytes=VMEM_LIMIT_BYTES),
        name="mla",
    )(qtm, qtm, kmla, vtm, ztm)


def _out_kernel(x_ref, mn_ref, mm_ref, w_ref, mod_ref, fg_ref, o_ref, *, final):
    y = lax.dot_general(mn_ref[0], w_ref[0:NSA_WIDTH, :], TN, preferred_element_type=F32)
    y = y + lax.dot_general(mm_ref[0], w_ref[NSA_WIDTH:MIX_WIDTH, :], TN, preferred_element_type=F32)
    x2 = x_ref[0] + mod_ref[0][2:3] * y
    o_ref[0] = _rms(x2, fg_ref[...]) if final else x2


def _out(x, mn, mm, w_out, mod3, fg, *, tm, final):
    b, s, d = x.shape
    return pl.pallas_call(
        functools.partial(_out_kernel, final=final), grid=(b, s // tm),
        in_specs=[pl.BlockSpec((1, tm, d), lambda bi, i: (bi, i, 0)),
                  pl.BlockSpec((1, NSA_WIDTH, tm), lambda bi, i: (bi, 0, i)),
                  pl.BlockSpec((1, MLA_WIDTH, tm), lambda bi, i: (bi, 0, i)),
                  pl.BlockSpec(w_out.shape, lambda bi, i: (0, 0)),
                  pl.BlockSpec((1, 3, d), lambda bi, i: (bi, 0, 0)),
                  pl.BlockSpec((1, d), lambda bi, i: (0, 0))],
        out_specs=pl.BlockSpec((1, tm, d), lambda bi, i: (bi, i, 0)),
        out_shape=jax.ShapeDtypeStruct((b, s, d), F32),
        compiler_params=pltpu.CompilerParams(vmem_limit_bytes=VMEM_LIMIT_BYTES),
        name="out_proj",
    )(x, mn, mm, w_out, mod3, fg)


def _cmp_to_slc_t(ncp, nc, nslc, nb):
    start = np.arange(nc)[:, None] * CMP_STRIDE
    bstart = np.arange(nslc)[None, :] * SLC_BLOCK
    ov = np.minimum(start + CMP_BLOCK, bstart + SLC_BLOCK) - np.maximum(start, bstart)
    m = (np.clip(ov, 0, None) / CMP_BLOCK).astype(np.float32)
    out = np.zeros((nb, ncp), np.float32)
    out[:nslc, :nc] = m.T
    return out


def _layout_w_in(w):
    d = w.shape[0]
    o = (0,) + IN_OFFSETS + (w.shape[1],)
    q_n, ks_n, vs_n, kw_n, vw_n, gl_n, kr_m = (w[:, o[k]:o[k + 1]] for k in (0, 3, 4, 5, 6, 7, 11))
    dk = NSA_HEAD_DIM

    def per_group(kx):
        return jnp.pad(kx.reshape(d, NSA_KV_GROUPS, dk), ((0, 0), (0, 0), (0, LANES - dk))).reshape(d, -1)

    ks_kr = jnp.concatenate([ks_n[:, :dk], jnp.pad(kr_m, ((0, 0), (0, LANES - dk - MLA_ROPE_DIM))),
                             per_group(ks_n)[:, LANES:]], axis=1)
    wtok = jnp.concatenate([ks_kr, kw_n, w[:, o[1]:o[3]], w[:, o[9]:o[11]]], axis=1)
    qr = q_n.reshape(d, NSA_HEADS, 2, dk // 2)
    q_perm = jnp.swapaxes(qr, 1, 2).reshape(d, -1)
    gl = gl_n.reshape(d, NSA_KV_GROUPS, NSA_HPG * N_BRANCH)
    gl = jnp.pad(gl, ((0, 0), (0, 0), (0, GATE_ROWS - NSA_HPG * N_BRANCH))).reshape(d, -1)
    wtr = jnp.concatenate([q_perm, vs_n, vw_n, gl, w[:, o[8]:o[9]], w[:, o[12]:o[13]]], axis=1).T
    assert wtok.shape[1] == TOK_COLS and wtr.shape[0] == TR_ROWS
    return wtok.astype(BF16), wtr.astype(BF16)


def _layout_w1(w1):
    hid = w1.shape[1]
    ng = NSA_KV_GROUPS
    w1r = w1.reshape(2, CMP_STRIDE, 1, NSA_HEAD_DIM, 1, hid)
    blocks = [jnp.pad(w1r, ((0, 0), (0, 0), (0, 0), (0, 0), (g, ng - 1 - g), (0, 0))) for g in range(ng)]
    full = jnp.concatenate(blocks, axis=2)
    full = full.reshape(2, CMP_STRIDE * NSA_KV_WIDTH, ng * hid)
    return jnp.concatenate([full[0], full[1]], axis=1).astype(BF16)


def kernel(x, c, positions, ada_w, ada_b, norm_g, w_in, cmp_pos, cmp_k_w1, cmp_k_w2, cmp_v_w1, cmp_v_w2,
           q_norm_g, w_q_up, kv_norm_g, w_kv_up, w_out, final_norm_g):
    b, s, d = x.shape
    depth = ada_w.shape[0]
    tm, tq = PROJ_TILE, ATT_TILE
    tk = tq
    assert s % tm == 0 and tm % tk == 0 and WINDOW % tk == 0 and (tq & (tq - 1)) == 0 and s % OUT_TILE == 0
    assert CMP_BLOCK == 2 * CMP_STRIDE and s % SLC_BLOCK == 0
    nslc = s // SLC_BLOCK
    nb = LANES - NSA_HEAD_DIM
    assert nslc <= nb
    ncp = s // CMP_STRIDE
    nc = ncp - 1

    half_n, half_m = NSA_HEAD_DIM // 2, MLA_ROPE_DIM // 2
    inv_n = ROPE_THETA ** (-jnp.arange(half_n, dtype=F32) / half_n)
    inv_m = ROPE_THETA ** (-jnp.arange(half_m, dtype=F32) / half_m)
    ones_n, ones_m = jnp.ones((half_n,), F32), jnp.ones((half_m,), F32)
    pad = jnp.zeros((LANES - NSA_HEAD_DIM - MLA_ROPE_DIM,), F32)
    inv_l = jnp.concatenate([inv_n, inv_n, inv_m, inv_m, pad])[None]
    sign_l = jnp.concatenate([-ones_n, ones_n, -ones_m, ones_m, pad])[None]
    pos_f = positions.astype(F32)
    pos_row = pos_f.reshape(b, 1, s)
    cmp_end = np.minimum(np.arange(ncp) * CMP_STRIDE + CMP_BLOCK - 1, s - 1)
    pos_c = jnp.broadcast_to(pos_f[:, cmp_end].reshape(b * ncp, 1), (b * ncp, LANES))
    inv_nb = jnp.broadcast_to(inv_n[:, None], (half_n, tm))
    inv_mb = jnp.broadcast_to(inv_m[:, None], (half_m, tm))

    mt = jnp.asarray(_cmp_to_slc_t(ncp, nc, nslc, nb), dtype=BF16)
    bp = -(-b // SUBLANES) * SUBLANES
    c_pad = jnp.pad(c, ((0, bp - b), (0, 0)))

    for l in range(depth):
        mod = _adaln(c_pad, ada_w[l], ada_b[l].reshape(1, -1))
        mod3 = mod[:b].reshape(b, 3, d)
        wtok, wtr = _layout_w_in(w_in[l])
        wq = w_q_up[l].reshape(MLA_Q_RANK, MLA_HEADS, MLA_NOPE_DIM + MLA_ROPE_DIM)
        wqt = jnp.concatenate([wq[:, :, :MLA_NOPE_DIM].reshape(MLA_Q_RANK, -1),
                               wq[:, :, MLA_NOPE_DIM:MLA_NOPE_DIM + half_m].reshape(MLA_Q_RANK, -1),
                               wq[:, :, MLA_NOPE_DIM + half_m:].reshape(MLA_Q_RANK, -1)], axis=1).T.astype(BF16)
        wkv = w_kv_up[l].reshape(MLA_KV_RANK, MLA_HEADS, MLA_NOPE_DIM + MLA_V_DIM)
        wkn = jnp.pad(wkv[:, :, :MLA_NOPE_DIM], ((0, 0), (0, 0), (0, LANES - MLA_NOPE_DIM))).reshape(
            MLA_KV_RANK, MLA_HEADS * LANES).astype(BF16)
        wv = wkv[:, :, MLA_NOPE_DIM:].transpose(1, 2, 0).reshape(MLA_WIDTH, MLA_KV_RANK).astype(BF16)

        (qt, kaug, kwin, vts, vtw, kcmp, vcmp, gt, ztn, ztm, qtm, kmla, vtm) = _proj(
            x, mod3, norm_g[l].reshape(1, d), wtok, wtr, pos_row, inv_nb, inv_mb,
            q_norm_g[l].reshape(1, -1), kv_norm_g[l].reshape(1, -1), wqt, wkn, wv, tm=tm, tk=tk)

        pos_l = cmp_pos[l]
        ptop = jnp.broadcast_to(pos_l[:CMP_STRIDE, None, :], (CMP_STRIDE, NSA_KV_GROUPS, NSA_HEAD_DIM)).reshape(1, -1)
        pbot = jnp.broadcast_to(pos_l[CMP_STRIDE:, None, :], (CMP_STRIDE, NSA_KV_GROUPS, NSA_HEAD_DIM)).reshape(1, -1)
        wk2 = jnp.pad(cmp_k_w2[l], ((0, 0), (0, LANES - NSA_HEAD_DIM))).astype(BF16)
        kc, vct = _compress(kcmp, vcmp, ptop, pbot,
                            _layout_w1(cmp_k_w1[l]), _layout_w1(cmp_v_w1[l]), wk2,
                            cmp_v_w2[l].T.astype(BF16), pos_c, inv_l, sign_l)

        mix_n = _nsa(qt, kc, vct, kaug, vts, kwin, vtw, gt, ztn, mt, tq=tq, tk=tk, n_sel=min(SLC_TOPK, nslc))
        mix_m = _mla(qtm, kmla, vtm, ztm, tq=tq, tk=tk)
        x = _out(x, mix_n, mix_m, w_out[l].astype(BF16), mod3, final_norm_g.reshape(1, d), tm=OUT_TILE,
                 final=(l == depth - 1))
    return x
```

```python
import functools

import numpy as np
import jax
import jax.numpy as jnp
from jax import lax
from jax.experimental import pallas as pl
from jax.experimental.pallas import tpu as pltpu

F32 = jnp.float32
BF16 = jnp.bfloat16
I32 = jnp.int32

NSA_HEADS = 8
NSA_KV_GROUPS = 2
NSA_HPG = NSA_HEADS // NSA_KV_GROUPS
NSA_HEAD_DIM = 64
NSA_WIDTH = NSA_HEADS * NSA_HEAD_DIM
NSA_KV_WIDTH = NSA_KV_GROUPS * NSA_HEAD_DIM
CMP_BLOCK = 32
CMP_STRIDE = 16
CMP_HIDDEN = 128
SLC_BLOCK = 64
SLC_TOPK = 16
WINDOW = 512
N_BRANCH = 3
FORCED_SCORE = 1.0e4
MLA_HEADS = 8
MLA_NOPE_DIM = 64
MLA_ROPE_DIM = 32
MLA_V_DIM = 64
MLA_WIDTH = MLA_HEADS * MLA_V_DIM
MLA_Q_RANK = 256
MLA_KV_RANK = 128
MIX_WIDTH = NSA_WIDTH + MLA_WIDTH
ROPE_THETA = 10000.0
NORM_EPS = 1e-6
NEG_INF = -1e30
IN_SIZES = (NSA_WIDTH, NSA_KV_WIDTH, NSA_KV_WIDTH, NSA_KV_WIDTH, NSA_KV_WIDTH, NSA_KV_WIDTH, NSA_KV_WIDTH,
            NSA_HEADS * N_BRANCH, NSA_WIDTH, MLA_Q_RANK, MLA_KV_RANK, MLA_ROPE_DIM, MLA_WIDTH)
IN_OFFSETS = tuple(int(o) for o in np.cumsum(IN_SIZES)[:-1])

LANES = 128
SUBLANES = 8
VMEM_LIMIT_BYTES = 56 * 1024 * 1024

PROJ_TILE = 1024
OUT_TILE = 1024
ATT_TILE = 256
SEL_BIAS = NEG_INF
GATE_ROWS = 16
ONES_ROWS = 16
MLA_QK = LANES
LOG2E = float(np.log2(np.e))
SWEEP_UNROLL = 4
SLC_SHIFT = SLC_BLOCK.bit_length() - 1
CMP_SHIFT = CMP_STRIDE.bit_length() - 1
assert 1 << SLC_SHIFT == SLC_BLOCK and 1 << CMP_SHIFT == CMP_STRIDE

NT = (((1,), (1,)), ((), ()))
TN = (((0,), (0,)), ((), ()))


def _silu(v):
    return v * jax.nn.sigmoid(v)


def _rms(v, g):
    ms = jnp.mean(v * v, axis=-1, keepdims=True)
    return v * lax.rsqrt(ms + NORM_EPS) * g


def _rope_tok(v, c, s_signed, half, lane, base=0):
    up = pltpu.roll(v, LANES - half, axis=1)
    dn = pltpu.roll(v, half, axis=1)
    return v * c + jnp.where(lane < base + half, up, dn) * s_signed


def _rope_lanes(pos_b, inv_ref, sign_ref):
    ang = pos_b * inv_ref[...]
    return jnp.cos(ang), jnp.sin(ang) * sign_ref[...]


def _chain_update(s_t, v_t, m_ref, acc_ref, ch, keep=None):
    if keep is not None:
        s_t = jnp.where(keep, s_t, NEG_INF)
    m_prev = m_ref[ch]
    m_new = jnp.maximum(m_prev, jnp.max(s_t, axis=0, keepdims=True))
    alpha = jnp.exp2(m_prev - m_new)
    p = jnp.exp2(s_t - m_new)
    acc_ref[ch] = alpha * acc_ref[ch] + jnp.dot(v_t, p.astype(BF16), preferred_element_type=F32)
    m_ref[ch] = m_new


def _causal_sweep(k_tile, v_tile, k_group, v_group, q_ref, sa_ref, sb_ref, m_ref, acc_ref, last, keep_last,
                  lookahead=2, q_next_ref=None, first=None):
    n_chains = len(k_group)

    def loader(tile_fn, j):
        cache = {}
        return lambda g: cache.setdefault(g, tile_fn(j, g))

    def phase(j_next, s_next_ref, s_cur_ref, j_cur, keep=None, ahead=False):
        k_next = loader(k_tile, j_next) if j_next is not None else None
        k_zero = loader(k_tile, 0) if ahead else None
        v_cur = loader(v_tile, j_cur)
        for n in range(n_chains + lookahead):
            if k_next is not None and n < n_chains:
                s_next_ref[n] = jnp.dot(k_next(k_group[n]), q_ref[n], preferred_element_type=F32)
            if n >= lookahead:
                ch = n - lookahead
                _chain_update(s_cur_ref[ch], v_cur(v_group[ch]), m_ref, acc_ref, ch, keep=keep)
                if ahead:
                    sa_ref[ch] = jnp.dot(k_zero(k_group[ch]), q_next_ref[ch], preferred_element_type=F32)

    bufs = (sa_ref, sb_ref)

    def prologue():
        k_0 = loader(k_tile, 0)
        for ch in range(n_chains):
            sa_ref[ch] = jnp.dot(k_0(k_group[ch]), q_ref[ch], preferred_element_type=F32)

    if q_next_ref is None:
        prologue()
    else:
        pl.when(first)(prologue)

    def run(j0, count):
        for u in range(count):
            phase(j0 + u + 1, bufs[(u + 1) % 2], bufs[u % 2], j0 + u)

    def body(jj, carry):
        run(SWEEP_UNROLL * jj, SWEEP_UNROLL)
        return carry

    lax.fori_loop(0, last // SWEEP_UNROLL, body, 0)
    rem = last % SWEEP_UNROLL
    for r in range(SWEEP_UNROLL):
        @pl.when(rem == r)
        def _(r=r):
            run(last - r, r)
            phase(None, None, bufs[r % 2], last, keep=keep_last, ahead=q_next_ref is not None)


def _pipeline(stages, lookahead):
    pending = {}
    for n in range(len(stages) + lookahead):
        if n < len(stages) and stages[n][0] is not None:
            pending[n] = stages[n][0]()
        if n >= lookahead:
            stages[n - lookahead][1](pending.pop(n - lookahead, None))


def _adaln_kernel(c_ref, w_ref, b_ref, o_ref):
    sc = _silu(c_ref[...])
    o_ref[...] = jnp.dot(sc.astype(BF16), w_ref[...].astype(BF16), preferred_element_type=F32) + b_ref[...]


def _adaln(c, w, b):
    bsz, d = c.shape
    n = w.shape[1] // d
    return pl.pallas_call(
        _adaln_kernel,
        grid=(n,),
        in_specs=[pl.BlockSpec((bsz, d), lambda j: (0, 0)),
                  pl.BlockSpec((d, d), lambda j: (0, j)),
                  pl.BlockSpec((1, d), lambda j: (0, j))],
        out_specs=pl.BlockSpec((None, bsz, d), lambda j: (j, 0, 0)),
        out_shape=jax.ShapeDtypeStruct((n, bsz, d), F32),
        name="adaln",
    )(c, w, b)


TOK_KS, TOK_KW, TOK_KC, TOK_VC, TOK_CQ, TOK_CKV, TOK_COLS = (int(o) for o in np.cumsum(
    [0, NSA_KV_GROUPS * LANES, NSA_KV_WIDTH, NSA_KV_WIDTH, NSA_KV_WIDTH, MLA_Q_RANK, MLA_KV_RANK]))
TOK_KR = TOK_KS
KR_LANE = NSA_HEAD_DIM
assert NSA_KV_WIDTH == LANES and KR_LANE + MLA_ROPE_DIM <= LANES
TR_Q, TR_VS, TR_VW, TR_G, TR_ZN, TR_ZM, TR_ROWS = (int(o) for o in np.cumsum(
    [0, NSA_WIDTH, NSA_KV_WIDTH, NSA_KV_WIDTH, NSA_KV_GROUPS * GATE_ROWS, NSA_WIDTH, MLA_WIDTH]))


def _proj_kernel(x_ref, mod_ref, ng_ref, wtok_ref, wtr_ref, posr_ref,
                 invn_ref, invm_ref, qng_ref, kvng_ref, wqt_ref, wkn_ref, wv_ref,
                 qt_ref, kaug_ref, kwin_ref, vts_ref, vtw_ref, kcmp_ref, vcmp_ref, gt_ref,
                 ztn_ref, ztm_ref, qtm_ref, kmla_ref, vtm_ref, *, tm, tk, scale_nsa, scale_mla):
    i = pl.program_id(1)
    bi = pl.program_id(0)
    shift, scale = mod_ref[0, pl.ds(bi, 1), :], mod_ref[1, pl.ds(bi, 1), :]
    h = _rms(x_ref[0], ng_ref[...]) * (1.0 + scale) + shift
    hb = h.astype(BF16)
    tok = jnp.dot(hb, wtok_ref[...], preferred_element_type=F32)
    tr = lax.dot_general(wtr_ref[...], hb, NT, preferred_element_type=F32)

    lane = lax.broadcasted_iota(I32, (tm, LANES), 1)
    row = lax.broadcasted_iota(I32, (tm, LANES), 0)
    blk = (i * tm + row) >> SLC_SHIFT
    onehot = (lane - NSA_HEAD_DIM == blk).astype(F32)
    ang_n = invn_ref[...] * posr_ref[0]
    cn, sn = jnp.cos(ang_n), jnp.sin(ang_n)
    ang_m = invm_ref[...] * posr_ref[0]
    cm, sm = jnp.cos(ang_m), jnp.sin(ang_m)
    zpad = jnp.zeros((LANES - NSA_HEAD_DIM - MLA_ROPE_DIM, tm), F32)
    ct = jnp.concatenate([cn, cn, cm, cm, zpad], axis=0).T
    st = jnp.concatenate([-sn, sn, -sm, sm, zpad], axis=0).T
    half_n = NSA_HEAD_DIM // 2
    for g in range(NSA_KV_GROUPS):
        ks = _rope_tok(tok[:, TOK_KS + LANES * g:TOK_KS + LANES * (g + 1)], ct, st, half_n, lane)
        kaug_ref[0, g] = jnp.where(lane >= NSA_HEAD_DIM, onehot, ks).astype(BF16)
    ctw = jnp.concatenate([cn, cn] * NSA_KV_GROUPS, axis=0).T
    stw = jnp.concatenate([-sn, sn] * NSA_KV_GROUPS, axis=0).T
    kwp = tok[:, TOK_KW:TOK_KW + LANES]
    first_half = (lane & (NSA_HEAD_DIM - 1)) < half_n
    kw = kwp * ctw + jnp.where(first_half, pltpu.roll(kwp, LANES - half_n, axis=1),
                               pltpu.roll(kwp, half_n, axis=1)) * stw
    for g in range(NSA_KV_GROUPS):
        kwin_ref[0, g] = (kw if g == 0 else pltpu.roll(kw, LANES - g * NSA_HEAD_DIM, axis=1)).astype(BF16)
    kcmp_ref[0] = tok[:, TOK_KC:TOK_KC + LANES]
    vcmp_ref[0] = tok[:, TOK_VC:TOK_VC + LANES]

    ckvn = _rms(tok[:, TOK_CKV:TOK_CKV + MLA_KV_RANK], kvng_ref[...])
    krr = _rope_tok(tok[:, TOK_KR:TOK_KR + LANES], ct, st, MLA_ROPE_DIM // 2, lane, base=KR_LANE)
    ckvb = ckvn.astype(BF16)
    kn = jnp.dot(ckvb, wkn_ref[...], preferred_element_type=F32)
    is_rot = (lane >= KR_LANE) & (lane < KR_LANE + MLA_ROPE_DIM)
    for hd in range(MLA_HEADS):
        kmla_ref[0, hd] = jnp.where(is_rot, krr, kn[:, hd * LANES:(hd + 1) * LANES]).astype(BF16)
    vtm = lax.dot_general(wv_ref[...], ckvb, NT, preferred_element_type=F32).astype(BF16)
    ones = jnp.ones((ONES_ROWS, tk), BF16)
    vr = MLA_V_DIM + ONES_ROWS
    for ii in range(tm // tk):
        for hd in range(MLA_HEADS):
            vtm_ref[0, ii, hd * vr:hd * vr + MLA_V_DIM, :] = vtm[hd * MLA_V_DIM:(hd + 1) * MLA_V_DIM,
                                                                 ii * tk:(ii + 1) * tk]
            vtm_ref[0, ii, hd * vr + MLA_V_DIM:(hd + 1) * vr, :] = ones

    cqn = _rms(tok[:, TOK_CQ:TOK_CQ + MLA_Q_RANK], qng_ref[...]).astype(BF16)
    qm = lax.dot_general(wqt_ref[...], cqn, NT, preferred_element_type=F32)
    nq = MLA_HEADS * MLA_NOPE_DIM
    hr = MLA_ROPE_DIM // 2
    x1 = qm[nq:nq + MLA_HEADS * hr].reshape(MLA_HEADS, hr, tm)
    x2 = qm[nq + MLA_HEADS * hr:nq + 2 * MLA_HEADS * hr].reshape(MLA_HEADS, hr, tm)
    cm_t, sm_t = cm[None], sm[None]
    o1 = (x1 * cm_t - x2 * sm_t) * scale_mla
    o2 = (x2 * cm_t + x1 * sm_t) * scale_mla
    nd = MLA_NOPE_DIM
    q_pad = jnp.zeros((MLA_QK - nd - MLA_ROPE_DIM, tm), BF16)
    for hd in range(MLA_HEADS):
        qtm_ref[0, hd, 0:nd, :] = (qm[hd * nd:(hd + 1) * nd] * scale_mla).astype(BF16)
        qtm_ref[0, hd, nd:nd + hr, :] = o1[hd].astype(BF16)
        qtm_ref[0, hd, nd + hr:nd + 2 * hr, :] = o2[hd].astype(BF16)
        qtm_ref[0, hd, nd + 2 * hr:MLA_QK, :] = q_pad

    hq = NSA_HEADS * half_n
    q1 = tr[TR_Q:TR_Q + hq].reshape(NSA_HEADS, half_n, tm)
    q2 = tr[TR_Q + hq:TR_Q + 2 * hq].reshape(NSA_HEADS, half_n, tm)
    cn_t, sn_t = cn[None], sn[None]
    qt_ref[0, 0:hq, :] = ((q1 * cn_t - q2 * sn_t) * scale_nsa).reshape(hq, tm).astype(BF16)
    qt_ref[0, hq:2 * hq, :] = ((q2 * cn_t + q1 * sn_t) * scale_nsa).reshape(hq, tm).astype(BF16)

    vts = tr[TR_VS:TR_VS + NSA_KV_WIDTH].astype(BF16)
    vtw = tr[TR_VW:TR_VW + NSA_KV_WIDTH].astype(BF16)
    dk = NSA_HEAD_DIM
    gr = dk + ONES_ROWS
    for ii in range(tm // tk):
        for g in range(NSA_KV_GROUPS):
            vts_ref[0, ii, g * gr:g * gr + dk, :] = vts[g * dk:(g + 1) * dk, ii * tk:(ii + 1) * tk]
            vtw_ref[0, ii, g * gr:g * gr + dk, :] = vtw[g * dk:(g + 1) * dk, ii * tk:(ii + 1) * tk]
            vts_ref[0, ii, g * gr + dk:(g + 1) * gr, :] = ones
            vtw_ref[0, ii, g * gr + dk:(g + 1) * gr, :] = ones
    gt_ref[0] = jax.nn.sigmoid(tr[TR_G:TR_G + NSA_KV_GROUPS * GATE_ROWS])
    ztn_ref[0] = _silu(tr[TR_ZN:TR_ZN + NSA_WIDTH]).astype(BF16)
    ztm_ref[0] = _silu(tr[TR_ZM:TR_ZM + MLA_WIDTH]).astype(BF16)


def _proj(x, mod3, ng, wtok, wtr, pos_row, inv_nb, inv_mb, qng, kvng, wqt, wkn, wv, *, tm, tk):
    b, s, d = x.shape
    nt = s // tk
    v_rows_n = NSA_KV_GROUPS * (NSA_HEAD_DIM + ONES_ROWS)
    v_rows_m = MLA_HEADS * (MLA_V_DIM + ONES_ROWS)

    def full(a):
        return pl.BlockSpec(a.shape, lambda bi, i, _n=a.ndim: (0,) * _n)

    in_specs = [pl.BlockSpec((1, tm, d), lambda bi, i: (bi, i, 0)),
                pl.BlockSpec(mod3.shape, lambda bi, i: (0, 0, 0)),
                full(ng), full(wtok), full(wtr),
                pl.BlockSpec((1, 1, tm), lambda bi, i: (bi, 0, i)),
                full(inv_nb), full(inv_mb),
                full(qng), full(kvng), full(wqt), full(wkn), full(wv)]
    out_shape = [
        jax.ShapeDtypeStruct((b, NSA_WIDTH, s), BF16),
        jax.ShapeDtypeStruct((b, NSA_KV_GROUPS, s, LANES), BF16),
        jax.ShapeDtypeStruct((b, NSA_KV_GROUPS, s, LANES), BF16),
        jax.ShapeDtypeStruct((b, nt, v_rows_n, tk), BF16),
        jax.ShapeDtypeStruct((b, nt, v_rows_n, tk), BF16),
        jax.ShapeDtypeStruct((b, s, NSA_KV_WIDTH), F32),
        jax.ShapeDtypeStruct((b, s, NSA_KV_WIDTH), F32),
        jax.ShapeDtypeStruct((b, NSA_KV_GROUPS * GATE_ROWS, s), F32),
        jax.ShapeDtypeStruct((b, NSA_WIDTH, s), BF16),
        jax.ShapeDtypeStruct((b, MLA_WIDTH, s), BF16),
        jax.ShapeDtypeStruct((b, MLA_HEADS, MLA_QK, s), BF16),
        jax.ShapeDtypeStruct((b, MLA_HEADS, s, MLA_QK), BF16),
        jax.ShapeDtypeStruct((b, nt, v_rows_m, tk), BF16),
    ]
    out_specs = [
        pl.BlockSpec((1, NSA_WIDTH, tm), lambda bi, i: (bi, 0, i)),
        pl.BlockSpec((1, NSA_KV_GROUPS, tm, LANES), lambda bi, i: (bi, 0, i, 0)),
        pl.BlockSpec((1, NSA_KV_GROUPS, tm, LANES), lambda bi, i: (bi, 0, i, 0)),
        pl.BlockSpec((1, tm // tk, v_rows_n, tk), lambda bi, i: (bi, i, 0, 0)),
        pl.BlockSpec((1, tm // tk, v_rows_n, tk), lambda bi, i: (bi, i, 0, 0)),
        pl.BlockSpec((1, tm, NSA_KV_WIDTH), lambda bi, i: (bi, i, 0)),
        pl.BlockSpec((1, tm, NSA_KV_WIDTH), lambda bi, i: (bi, i, 0)),
        pl.BlockSpec((1, NSA_KV_GROUPS * GATE_ROWS, tm), lambda bi, i: (bi, 0, i)),
        pl.BlockSpec((1, NSA_WIDTH, tm), lambda bi, i: (bi, 0, i)),
        pl.BlockSpec((1, MLA_WIDTH, tm), lambda bi, i: (bi, 0, i)),
        pl.BlockSpec((1, MLA_HEADS, MLA_QK, tm), lambda bi, i: (bi, 0, 0, i)),
        pl.BlockSpec((1, MLA_HEADS, tm, MLA_QK), lambda bi, i: (bi, 0, i, 0)),
        pl.BlockSpec((1, tm // tk, v_rows_m, tk), lambda bi, i: (bi, i, 0, 0)),
    ]
    kern = functools.partial(_proj_kernel, tm=tm, tk=tk, scale_nsa=NSA_HEAD_DIM ** -0.5 * LOG2E,
                             scale_mla=(MLA_NOPE_DIM + MLA_ROPE_DIM) ** -0.5 * LOG2E)
    return pl.pallas_call(
        kern, grid=(b, s // tm), in_specs=in_specs, out_specs=out_specs, out_shape=out_shape,
        compiler_params=pltpu.CompilerParams(vmem_limit_bytes=VMEM_LIMIT_BYTES),
        name="proj",
    )(x, mod3, ng, wtok, wtr, pos_row, inv_nb, inv_mb, qng, kvng, wqt, wkn, wv)


def _compress_kernel(k_ref, v_ref, pemb_ref, wk1_ref, wv1_ref, wk2_ref, wv2t_ref, posc_ref, invl_ref,
                     signl_ref, kc_ref, vct_ref):
    ncp = k_ref.shape[1] // CMP_STRIDE
    lane = lax.broadcasted_iota(I32, (ncp, LANES), 1)

    def hidden(r_ref, w1_ref):
        r = jnp.concatenate([r_ref[0, pl.ds(t, ncp, stride=CMP_STRIDE), :] for t in range(CMP_STRIDE)], axis=1)
        a = jnp.dot((r + pemb_ref[0]).astype(BF16), w1_ref[0], preferred_element_type=F32)
        bt = jnp.dot((r + pemb_ref[1]).astype(BF16), w1_ref[1], preferred_element_type=F32)
        return _silu(a + pltpu.roll(bt, ncp - 1, axis=0))

    cc, sc = _rope_lanes(posc_ref[...], invl_ref, signl_ref)
    hk = hidden(k_ref, wk1_ref)
    hv = hidden(v_ref, wv1_ref)
    for g in range(NSA_KV_GROUPS):
        hkg = hk[:, g * CMP_HIDDEN:(g + 1) * CMP_HIDDEN].astype(BF16)
        kc = jnp.dot(hkg, wk2_ref[...], preferred_element_type=F32)
        kc_ref[0, g] = _rope_tok(kc, cc, sc, NSA_HEAD_DIM // 2, lane).astype(BF16)
        hvg = hv[:, g * CMP_HIDDEN:(g + 1) * CMP_HIDDEN].astype(BF16)
        vct_ref[0, g] = lax.dot_general(wv2t_ref[...], hvg, NT, preferred_element_type=F32).astype(BF16)


def _compress(kcmp, vcmp, pemb, wk1, wv1, wk2, wv2t, pos_c, inv_l, sign_l):
    b, s, width = kcmp.shape
    ncp = s // CMP_STRIDE

    def full(a):
        return pl.BlockSpec(a.shape, lambda bi, _n=a.ndim: (0,) * _n)

    blk = pl.BlockSpec((1, s, width), lambda bi: (bi, 0, 0))
    tab = pl.BlockSpec((ncp, LANES), lambda bi: (bi, 0))
    return pl.pallas_call(
        _compress_kernel, grid=(b,),
        in_specs=[blk, blk, full(pemb), full(wk1), full(wv1), full(wk2), full(wv2t), tab,
                  full(inv_l), full(sign_l)],
        out_specs=[pl.BlockSpec((1, NSA_KV_GROUPS, ncp, LANES), lambda bi: (bi, 0, 0, 0)),
                   pl.BlockSpec((1, NSA_KV_GROUPS, NSA_HEAD_DIM, ncp), lambda bi: (bi, 0, 0, 0))],
        out_shape=[jax.ShapeDtypeStruct((b, NSA_KV_GROUPS, ncp, LANES), BF16),
                   jax.ShapeDtypeStruct((b, NSA_KV_GROUPS, NSA_HEAD_DIM, ncp), BF16)],
        compiler_params=pltpu.CompilerParams(vmem_limit_bytes=VMEM_LIMIT_BYTES),
        name="compress",
    )(kcmp, vcmp, pemb, wk1, wv1, wk2, wv2t, pos_c, inv_l, sign_l)


def _nsa_kernel(q1_ref, q2_ref, kc_ref, vct_ref, kaug_ref, vts_ref, kwin_ref, vtw_ref, g_ref, z_ref, mt_ref,
                o_ref, qaug_ref, sa_ref, sb_ref, m_ref, acc_ref, tot_ref, imp_ref, rank_ref, *, tq, tk, n_sel):
    i = pl.program_id(1)
    nh, ng, hpg, dk, half = NSA_HEADS, NSA_KV_GROUPS, NSA_HPG, NSA_HEAD_DIM, NSA_HEAD_DIM // 2
    vr = dk + ONES_ROWS
    group = [h // hpg for h in range(nh)]
    for h in range(nh):
        qaug_ref[h, 0:half, :] = q1_ref[0, h * half:(h + 1) * half, :]
        qaug_ref[h, half:dk, :] = q2_ref[0, h * half:(h + 1) * half, :]

    def gate(h, branch):
        row = group[h] * GATE_ROWS + (h % hpg) * N_BRANCH + branch
        return g_ref[0, row:row + 1, :]

    row_k = lax.broadcasted_iota(I32, (tk, tq), 0)
    col_q = lax.broadcasted_iota(I32, (tk, tq), 1)
    causal = row_k <= col_q

    def reset():
        m_ref[...] = jnp.full_like(m_ref, NEG_INF)
        acc_ref[...] = jnp.zeros_like(acc_ref)

    def add_branch(branch):
        for h in range(nh):
            inv_l = 1.0 / acc_ref[h, dk:dk + 1, :]
            tot_ref[h] = tot_ref[h] + (gate(h, branch) * inv_l) * acc_ref[h, 0:dk, :]

    def q_cols(h):
        return qaug_ref[h, 0:dk, :]

    ncp = kc_ref.shape[2]
    t_row = i * tq + lax.broadcasted_iota(I32, (1, tq), 1)
    last_n = (t_row - (CMP_BLOCK - 1)) >> CMP_SHIFT
    valid = lax.broadcasted_iota(I32, (ncp, tq), 0) <= last_n
    col_ok = last_n >= 0
    p_heads = {}

    def cmp_update(h, s_c):
        s_c = jnp.where(valid, s_c, NEG_INF)
        e = jnp.exp2(s_c - jnp.max(s_c, axis=0, keepdims=True))
        l_c = jnp.sum(e, axis=0, keepdims=True)
        p_c = e * jnp.where(col_ok, 1.0 / l_c, 0.0)
        o_c = jnp.dot(vct_ref[0, group[h]], p_c.astype(BF16), preferred_element_type=F32)
        tot_ref[h] = gate(h, 0) * o_c
        p_heads[h] = p_c

    nb = mt_ref.shape[0]

    def importance(g, _):
        psum = functools.reduce(lambda a, b: a + b, [p_heads[h] for h in range(nh) if group[h] == g])
        hi = psum.astype(BF16)
        lo = (psum - hi.astype(F32)).astype(BF16)
        mt = mt_ref[...]
        imp = jnp.dot(mt, hi, preferred_element_type=F32) + jnp.dot(mt, lo, preferred_element_type=F32)
        j_idx = lax.broadcasted_iota(I32, (nb, tq), 0)
        cur = (i * tq + lax.broadcasted_iota(I32, (nb, tq), 1)) >> SLC_SHIFT
        forced = (j_idx == 0) | (j_idx == cur) | (j_idx == cur - 1)
        imp_ref[g] = jnp.where(forced, FORCED_SCORE, jnp.where(j_idx > cur, -FORCED_SCORE, imp))

    stages = []
    for g in range(ng):
        kc = kc_ref[0, g, :, 0:dk]
        stages += [(functools.partial(jnp.dot, kc, q_cols(h), preferred_element_type=F32),
                    functools.partial(cmp_update, h)) for h in range(nh) if group[h] == g]
        stages.append((None, functools.partial(importance, g)))

    reset()
    n_back = WINDOW // tk
    for back in range(n_back + 1):
        jb = jnp.maximum(i - back, 0)
        if back == 0:
            keep = causal
        elif back == n_back:
            keep = (row_k > col_q) & (i >= back)
        else:
            keep = jnp.broadcast_to(i >= back, (tk, tq))
        for g in range(ng):
            kt_b = kwin_ref[0, g, pl.ds(pl.multiple_of(jb * tk, tk), tk), 0:dk]
            vt_b = vtw_ref[0, jb, g * vr:(g + 1) * vr, :]
            for h in range(nh):
                if group[h] == g:
                    stages.append((functools.partial(jnp.dot, kt_b, q_cols(h), preferred_element_type=F32),
                                   functools.partial(_chain_update, v_t=vt_b, m_ref=m_ref, acc_ref=acc_ref,
                                                     ch=h, keep=keep)))
    _pipeline(stages, lookahead=5)
    add_branch(2)

    rank_ref[...] = jnp.zeros_like(rank_ref)
    sub = lax.broadcasted_iota(I32, (SUBLANES, tq), 0)
    last_group = ((i + 1) * (tq // SLC_BLOCK) - 1) // SUBLANES

    def count(g, c, v):
        blk = imp_ref[g, v * SUBLANES:(v + 1) * SUBLANES, :]
        cnt = rank_ref[g, v * SUBLANES:(v + 1) * SUBLANES, :]
        for rr in range(SUBLANES):
            row = imp_ref[g, c * SUBLANES + rr:c * SUBLANES + rr + 1, :]
            if c < v:
                beats = row >= blk
            elif c > v:
                beats = row > blk
            else:
                beats = (row > blk) | ((row == blk) & (sub > rr))
            cnt = cnt + beats.astype(I32)
        rank_ref[g, v * SUBLANES:(v + 1) * SUBLANES, :] = cnt

    for lvl in range(nb // SUBLANES):
        @pl.when(lvl <= last_group)
        def _(lvl=lvl):
            for g in range(ng):
                for v in range(lvl + 1):
                    count(g, lvl, v)
                for c in range(lvl):
                    count(g, c, lvl)

    for g in range(ng):
        bias = jnp.where(rank_ref[g] < n_sel, 0.0, SEL_BIAS).astype(BF16)
        for h in range(nh):
            if group[h] == g:
                qaug_ref[h, dk:dk + nb, :] = bias

    reset()
    _causal_sweep(lambda j, g: kaug_ref[0, g, pl.ds(pl.multiple_of(j * tk, tk), tk), :],
                  lambda j, g: vts_ref[0, j, g * vr:(g + 1) * vr, :],
                  group, group, qaug_ref, sa_ref, sb_ref, m_ref, acc_ref, i, causal)
    add_branch(1)

    for h in range(nh):
        zz = z_ref[0, h * dk:(h + 1) * dk, :].astype(F32)
        o_ref[0, h * dk:(h + 1) * dk, :] = (tot_ref[h] * zz).astype(BF16)


def _nsa(qt, kc, vct, kaug, vts, kwin, vtw, gt, ztn, mt, *, tq, tk, n_sel):
    b, _, s = qt.shape
    nt = s // tk
    ncp = kc.shape[2]
    ng, nh = NSA_KV_GROUPS, NSA_HEADS
    hq = nh * (NSA_HEAD_DIM // 2)
    nb = mt.shape[0]
    vr = NSA_HEAD_DIM + ONES_ROWS
    in_specs = [
        pl.BlockSpec((1, hq, tq), lambda bi, i: (bi, 0, i)),
        pl.BlockSpec((1, hq, tq), lambda bi, i: (bi, 1, i)),
        pl.BlockSpec((1, ng, ncp, LANES), lambda bi, i: (bi, 0, 0, 0)),
        pl.BlockSpec((1, ng, NSA_HEAD_DIM, ncp), lambda bi, i: (bi, 0, 0, 0)),
        pl.BlockSpec((1, ng, s, LANES), lambda bi, i: (bi, 0, 0, 0)),
        pl.BlockSpec((1, nt, ng * vr, tk), lambda bi, i: (bi, 0, 0, 0)),
        pl.BlockSpec((1, ng, s, LANES), lambda bi, i: (bi, 0, 0, 0)),
        pl.BlockSpec((1, nt, ng * vr, tk), lambda bi, i: (bi, 0, 0, 0)),
        pl.BlockSpec((1, ng * GATE_ROWS, tq), lambda bi, i: (bi, 0, i)),
        pl.BlockSpec((1, NSA_WIDTH, tq), lambda bi, i: (bi, 0, i)),
        pl.BlockSpec(mt.shape, lambda bi, i: (0, 0)),
    ]
    kern = functools.partial(_nsa_kernel, tq=tq, tk=tk, n_sel=n_sel)
    return pl.pallas_call(
        kern, grid=(b, s // tq), in_specs=in_specs,
        out_specs=pl.BlockSpec((1, NSA_WIDTH, tq), lambda bi, i: (bi, 0, i)),
        out_shape=jax.ShapeDtypeStruct((b, NSA_WIDTH, s), BF16),
        scratch_shapes=[pltpu.VMEM((nh, NSA_HEAD_DIM + nb, tq), BF16),
                        pltpu.VMEM((nh, tk, tq), F32), pltpu.VMEM((nh, tk, tq), F32),
                        pltpu.VMEM((nh, 1, tq), F32), pltpu.VMEM((nh, vr, tq), F32),
                        pltpu.VMEM((nh, NSA_HEAD_DIM, tq), F32),
                        pltpu.VMEM((ng, nb, tq), F32), pltpu.VMEM((ng, nb, tq), I32)],
        compiler_params=pltpu.CompilerParams(dimension_semantics=("arbitrary", "arbitrary"),
                                             vmem_limit_bytes=VMEM_LIMIT_BYTES),
        name="nsa",
    )(qt, qt, kc, vct, kaug, vts, kwin, vtw, gt, ztn, mt)


def _mla_kernel(q_ref, qn_ref, k_ref, vt_ref, z_ref, o_ref, sa_ref, sb_ref, m_ref, acc_ref, *, tq, tk):
    i = pl.program_id(1)
    dv, vr = MLA_V_DIM, MLA_V_DIM + ONES_ROWS
    m_ref[...] = jnp.full_like(m_ref, NEG_INF)
    acc_ref[...] = jnp.zeros_like(acc_ref)

    causal = lax.broadcasted_iota(I32, (tk, tq), 0) <= lax.broadcasted_iota(I32, (tk, tq), 1)
    heads = list(range(MLA_HEADS))
    _causal_sweep(lambda j, hd: k_ref[0, hd, pl.ds(pl.multiple_of(j * tk, tk), tk), :],
                  lambda j, hd: vt_ref[0, j, hd * vr:(hd + 1) * vr, :],
                  heads, heads, q_ref.at[0], sa_ref, sb_ref, m_ref, acc_ref, i, causal,
                  q_next_ref=qn_ref.at[0], first=i == 0)
    for hd in range(MLA_HEADS):
        o_h = acc_ref[hd, 0:dv, :] * (1.0 / acc_ref[hd, dv:dv + 1, :])
        zz = z_ref[0, hd * dv:(hd + 1) * dv, :].astype(F32)
        o_ref[0, hd * dv:(hd + 1) * dv, :] = (o_h * zz).astype(BF16)


def _mla(qtm, kmla, vtm, ztm, *, tq, tk):
    b, _, s, _ = kmla.shape
    nt = s // tk
    vrows = MLA_HEADS * (MLA_V_DIM + ONES_ROWS)
    kern = functools.partial(_mla_kernel, tq=tq, tk=tk)
    return pl.pallas_call(
        kern, grid=(b, s // tq),
        in_specs=[pl.BlockSpec((1, MLA_HEADS, MLA_QK, tq), lambda bi, i: (bi, 0, 0, i)),
                  pl.BlockSpec((1, MLA_HEADS, MLA_QK, tq), lambda bi, i: (bi, 0, 0, jnp.minimum(i + 1, s // tq - 1))),
                  pl.BlockSpec((1, MLA_HEADS, s, MLA_QK), lambda bi, i: (bi, 0, 0, 0)),
                  pl.BlockSpec((1, nt, vrows, tk), lambda bi, i: (bi, 0, 0, 0)),
                  pl.BlockSpec((1, MLA_WIDTH, tq), lambda bi, i: (bi, 0, i))],
        out_specs=pl.BlockSpec((1, MLA_WIDTH, tq), lambda bi, i: (bi, 0, i)),
        out_shape=jax.ShapeDtypeStruct((b, MLA_WIDTH, s), BF16),
        scratch_shapes=[pltpu.VMEM((MLA_HEADS, tk, tq), F32), pltpu.VMEM((MLA_HEADS, tk, tq), F32),
                        pltpu.VMEM((MLA_HEADS, 1, tq), F32), pltpu.VMEM((MLA_HEADS, MLA_V_DIM + ONES_ROWS, tq), F32)],
        compiler_params=pltpu.CompilerParams(dimension_semantics=("arbitrary", "arbitrary"),
                                             vmem_limit_bytes=VMEM_LIMIT_BYTES),
        name="mla",
    )(qtm, qtm, kmla, vtm, ztm)


def _out_kernel(x_ref, mn_ref, mm_ref, w_ref, mod_ref, fg_ref, o_ref, *, final):
    y = lax.dot_general(mn_ref[0], w_ref[0:NSA_WIDTH, :], TN, preferred_element_type=F32)
    y = y + lax.dot_general(mm_ref[0], w_ref[NSA_WIDTH:MIX_WIDTH, :], TN, preferred_element_type=F32)
    x2 = x_ref[0] + mod_ref[2, pl.ds(pl.program_id(0), 1), :] * y
    o_ref[0] = _rms(x2, fg_ref[...]) if final else x2


def _out(x, mn, mm, w_out, mod3, fg, *, tm, final):
    b, s, d = x.shape
    return pl.pallas_call(
        functools.partial(_out_kernel, final=final), grid=(b, s // tm),
        in_specs=[pl.BlockSpec((1, tm, d), lambda bi, i: (bi, i, 0)),
                  pl.BlockSpec((1, NSA_WIDTH, tm), lambda bi, i: (bi, 0, i)),
                  pl.BlockSpec((1, MLA_WIDTH, tm), lambda bi, i: (bi, 0, i)),
                  pl.BlockSpec(w_out.shape, lambda bi, i: (0, 0)),
                  pl.BlockSpec(mod3.shape, lambda bi, i: (0, 0, 0)),
                  pl.BlockSpec((1, d), lambda bi, i: (0, 0))],
        out_specs=pl.BlockSpec((1, tm, d), lambda bi, i: (bi, i, 0)),
        out_shape=jax.ShapeDtypeStruct((b, s, d), F32),
        compiler_params=pltpu.CompilerParams(vmem_limit_bytes=VMEM_LIMIT_BYTES),
        name="out_proj",
    )(x, mn, mm, w_out, mod3, fg)


def _cmp_to_slc_t(ncp, nc, nslc, nb):
    start = np.arange(nc)[:, None] * CMP_STRIDE
    bstart = np.arange(nslc)[None, :] * SLC_BLOCK
    ov = np.minimum(start + CMP_BLOCK, bstart + SLC_BLOCK) - np.maximum(start, bstart)
    m = (np.clip(ov, 0, None) / CMP_BLOCK).astype(np.float32)
    out = np.zeros((nb, ncp), np.float32)
    out[:nslc, :nc] = m.T
    return out


def _layout_w_in(w):
    d = w.shape[0]
    o = (0,) + IN_OFFSETS + (w.shape[1],)
    q_n, ks_n, vs_n, kw_n, vw_n, gl_n, kr_m = (w[:, o[k]:o[k + 1]] for k in (0, 3, 4, 5, 6, 7, 11))
    dk = NSA_HEAD_DIM

    def per_group(kx):
        return jnp.pad(kx.reshape(d, NSA_KV_GROUPS, dk), ((0, 0), (0, 0), (0, LANES - dk))).reshape(d, -1)

    ks_kr = jnp.concatenate([ks_n[:, :dk], jnp.pad(kr_m, ((0, 0), (0, LANES - dk - MLA_ROPE_DIM))),
                             per_group(ks_n)[:, LANES:]], axis=1)
    wtok = jnp.concatenate([ks_kr, kw_n, w[:, o[1]:o[3]], w[:, o[9]:o[11]]], axis=1)
    qr = q_n.reshape(d, NSA_HEADS, 2, dk // 2)
    q_perm = jnp.swapaxes(qr, 1, 2).reshape(d, -1)
    gl = gl_n.reshape(d, NSA_KV_GROUPS, NSA_HPG * N_BRANCH)
    gl = jnp.pad(gl, ((0, 0), (0, 0), (0, GATE_ROWS - NSA_HPG * N_BRANCH))).reshape(d, -1)
    wtr = jnp.concatenate([q_perm, vs_n, vw_n, gl, w[:, o[8]:o[9]], w[:, o[12]:o[13]]], axis=1).T
    assert wtok.shape[1] == TOK_COLS and wtr.shape[0] == TR_ROWS
    return wtok.astype(BF16), wtr.astype(BF16)


def _layout_w1(w1):
    hid = w1.shape[1]
    ng = NSA_KV_GROUPS
    w1r = w1.reshape(2, CMP_STRIDE, 1, NSA_HEAD_DIM, hid)
    blocks = [jnp.pad(w1r, ((0, 0), (0, 0), (0, 0), (0, 0), (g * hid, (ng - 1 - g) * hid))) for g in range(ng)]
    full = jnp.concatenate(blocks, axis=2)
    return full.reshape(2, CMP_STRIDE * NSA_KV_WIDTH, ng * hid).astype(BF16)


def kernel(x, c, positions, ada_w, ada_b, norm_g, w_in, cmp_pos, cmp_k_w1, cmp_k_w2, cmp_v_w1, cmp_v_w2,
           q_norm_g, w_q_up, kv_norm_g, w_kv_up, w_out, final_norm_g):
    b, s, d = x.shape
    depth = ada_w.shape[0]
    tm, tq = PROJ_TILE, ATT_TILE
    tk = tq
    assert s % tm == 0 and tm % tk == 0 and WINDOW % tk == 0 and (tq & (tq - 1)) == 0 and s % OUT_TILE == 0
    assert CMP_BLOCK == 2 * CMP_STRIDE and s % SLC_BLOCK == 0
    nslc = s // SLC_BLOCK
    nb = LANES - NSA_HEAD_DIM
    assert nslc <= nb
    ncp = s // CMP_STRIDE
    nc = ncp - 1

    half_n, half_m = NSA_HEAD_DIM // 2, MLA_ROPE_DIM // 2
    inv_n = ROPE_THETA ** (-jnp.arange(half_n, dtype=F32) / half_n)
    inv_m = ROPE_THETA ** (-jnp.arange(half_m, dtype=F32) / half_m)
    ones_n, ones_m = jnp.ones((half_n,), F32), jnp.ones((half_m,), F32)
    pad = jnp.zeros((LANES - NSA_HEAD_DIM - MLA_ROPE_DIM,), F32)
    inv_l = jnp.concatenate([inv_n, inv_n, inv_m, inv_m, pad])[None]
    sign_l = jnp.concatenate([-ones_n, ones_n, -ones_m, ones_m, pad])[None]
    pos_f = positions.astype(F32)
    pos_row = pos_f.reshape(b, 1, s)
    cmp_end = np.minimum(np.arange(ncp) * CMP_STRIDE + CMP_BLOCK - 1, s - 1)
    pos_c = jnp.broadcast_to(pos_f[:, cmp_end].reshape(b * ncp, 1), (b * ncp, LANES))
    inv_nb, inv_mb = inv_n[:, None], inv_m[:, None]

    mt = jnp.asarray(_cmp_to_slc_t(ncp, nc, nslc, nb), dtype=BF16)

    for l in range(depth):
        mod3 = _adaln(c, ada_w[l], ada_b[l].reshape(1, -1))
        wtok, wtr = _layout_w_in(w_in[l])
        wq = w_q_up[l].reshape(MLA_Q_RANK, MLA_HEADS, MLA_NOPE_DIM + MLA_ROPE_DIM)
        wqt = jnp.concatenate([wq[:, :, :MLA_NOPE_DIM].reshape(MLA_Q_RANK, -1),
                               wq[:, :, MLA_NOPE_DIM:MLA_NOPE_DIM + half_m].reshape(MLA_Q_RANK, -1),
                               wq[:, :, MLA_NOPE_DIM + half_m:].reshape(MLA_Q_RANK, -1)], axis=1).T.astype(BF16)
        wkv = w_kv_up[l].reshape(MLA_KV_RANK, MLA_HEADS, MLA_NOPE_DIM + MLA_V_DIM)
        wkn = jnp.pad(wkv[:, :, :MLA_NOPE_DIM], ((0, 0), (0, 0), (0, LANES - MLA_NOPE_DIM))).reshape(
            MLA_KV_RANK, MLA_HEADS * LANES).astype(BF16)
        wv = wkv[:, :, MLA_NOPE_DIM:].transpose(1, 2, 0).reshape(MLA_WIDTH, MLA_KV_RANK).astype(BF16)

        (qt, kaug, kwin, vts, vtw, kcmp, vcmp, gt, ztn, ztm, qtm, kmla, vtm) = _proj(
            x, mod3, norm_g[l].reshape(1, d), wtok, wtr, pos_row, inv_nb, inv_mb,
            q_norm_g[l].reshape(1, -1), kv_norm_g[l].reshape(1, -1), wqt, wkn, wv, tm=tm, tk=tk)

        pos_l = cmp_pos[l]
        pemb = jnp.broadcast_to(pos_l.reshape(2, CMP_STRIDE, 1, NSA_HEAD_DIM),
                                (2, CMP_STRIDE, NSA_KV_GROUPS, NSA_HEAD_DIM)).reshape(2, 1, -1)
        wk2 = jnp.pad(cmp_k_w2[l], ((0, 0), (0, LANES - NSA_HEAD_DIM))).astype(BF16)
        kc, vct = _compress(kcmp, vcmp, pemb,
                            _layout_w1(cmp_k_w1[l]), _layout_w1(cmp_v_w1[l]), wk2,
                            cmp_v_w2[l].T.astype(BF16), pos_c, inv_l, sign_l)

        mix_n = _nsa(qt, kc, vct, kaug, vts, kwin, vtw, gt, ztn, mt, tq=tq, tk=tk, n_sel=min(SLC_TOPK, nslc))
        mix_m = _mla(qtm, kmla, vtm, ztm, tq=tq, tk=tk)
        x = _out(x, mix_n, mix_m, w_out[l].astype(BF16), mod3, final_norm_g.reshape(1, d), tm=OUT_TILE,
                 final=(l == depth - 1))
    return x
```

```python
import functools

import numpy as np
import jax
import jax.numpy as jnp
from jax import lax
from jax.experimental import pallas as pl
from jax.experimental.pallas import tpu as pltpu

F32 = jnp.float32
BF16 = jnp.bfloat16
I32 = jnp.int32

NSA_HEADS = 8
NSA_KV_GROUPS = 2
NSA_HPG = NSA_HEADS // NSA_KV_GROUPS
NSA_HEAD_DIM = 64
NSA_WIDTH = NSA_HEADS * NSA_HEAD_DIM
NSA_KV_WIDTH = NSA_KV_GROUPS * NSA_HEAD_DIM
CMP_BLOCK = 32
CMP_STRIDE = 16
CMP_HIDDEN = 128
SLC_BLOCK = 64
SLC_TOPK = 16
WINDOW = 512
N_BRANCH = 3
FORCED_SCORE = 1.0e4
MLA_HEADS = 8
MLA_NOPE_DIM = 64
MLA_ROPE_DIM = 32
MLA_V_DIM = 64
MLA_WIDTH = MLA_HEADS * MLA_V_DIM
MLA_Q_RANK = 256
MLA_KV_RANK = 128
MIX_WIDTH = NSA_WIDTH + MLA_WIDTH
ROPE_THETA = 10000.0
NORM_EPS = 1e-6
NEG_INF = -1e30
IN_SIZES = (NSA_WIDTH, NSA_KV_WIDTH, NSA_KV_WIDTH, NSA_KV_WIDTH, NSA_KV_WIDTH, NSA_KV_WIDTH, NSA_KV_WIDTH,
            NSA_HEADS * N_BRANCH, NSA_WIDTH, MLA_Q_RANK, MLA_KV_RANK, MLA_ROPE_DIM, MLA_WIDTH)
IN_OFFSETS = tuple(int(o) for o in np.cumsum(IN_SIZES)[:-1])

LANES = 128
SUBLANES = 8
VMEM_LIMIT_BYTES = 56 * 1024 * 1024

PROJ_TILE = 1024
OUT_TILE = 1024
ATT_TILE = 256
SEL_BIAS = NEG_INF
GATE_ROWS = NSA_HPG * N_BRANCH
ONES_ROWS = 16
MLA_QK = LANES
LOG2E = float(np.log2(np.e))
SWEEP_UNROLL = 4
SLC_SHIFT = SLC_BLOCK.bit_length() - 1
CMP_SHIFT = CMP_STRIDE.bit_length() - 1
assert 1 << SLC_SHIFT == SLC_BLOCK and 1 << CMP_SHIFT == CMP_STRIDE

NT = (((1,), (1,)), ((), ()))
TN = (((0,), (0,)), ((), ()))


def _silu(v):
    return v * jax.nn.sigmoid(v)


def _rms(v, g):
    ms = jnp.mean(v * v, axis=-1, keepdims=True)
    return v * lax.rsqrt(ms + NORM_EPS) * g


def _rope_tok(v, c, s_signed, half, lane, base=0):
    up = pltpu.roll(v, LANES - half, axis=1)
    dn = pltpu.roll(v, half, axis=1)
    return v * c + jnp.where(lane < base + half, up, dn) * s_signed


def _rope_lanes(pos_b, inv_ref, sign_ref):
    ang = pos_b * inv_ref[...]
    return jnp.cos(ang), jnp.sin(ang) * sign_ref[...]


def _chain_update(s_t, v_t, m_ref, acc_ref, ch, keep=None):
    if keep is not None:
        s_t = jnp.where(keep, s_t, NEG_INF)
    m_prev = m_ref[ch]
    m_new = jnp.maximum(m_prev, jnp.max(s_t, axis=0, keepdims=True))
    alpha = jnp.exp2(m_prev - m_new)
    p = jnp.exp2(s_t - m_new)
    acc_ref[ch] = alpha * acc_ref[ch] + jnp.dot(v_t, p.astype(BF16), preferred_element_type=F32)
    m_ref[ch] = m_new


def _causal_sweep(k_tile, v_tile, k_group, v_group, q_ref, sa_ref, sb_ref, m_ref, acc_ref, last, keep_last,
                  lookahead=2, q_next_ref=None, first=None):
    n_chains = len(k_group)

    def loader(tile_fn, j):
        cache = {}
        return lambda g: cache.setdefault(g, tile_fn(j, g))

    def phase(j_next, s_next_ref, s_cur_ref, j_cur, keep=None, ahead=False):
        k_next = loader(k_tile, j_next) if j_next is not None else None
        k_zero = loader(k_tile, 0) if ahead else None
        v_cur = loader(v_tile, j_cur)
        for n in range(n_chains + lookahead):
            if k_next is not None and n < n_chains:
                s_next_ref[n] = jnp.dot(k_next(k_group[n]), q_ref[n], preferred_element_type=F32)
            if n >= lookahead:
                ch = n - lookahead
                _chain_update(s_cur_ref[ch], v_cur(v_group[ch]), m_ref, acc_ref, ch, keep=keep)
                if ahead:
                    sa_ref[ch] = jnp.dot(k_zero(k_group[ch]), q_next_ref[ch], preferred_element_type=F32)

    bufs = (sa_ref, sb_ref)

    def prologue():
        k_0 = loader(k_tile, 0)
        for ch in range(n_chains):
            sa_ref[ch] = jnp.dot(k_0(k_group[ch]), q_ref[ch], preferred_element_type=F32)

    if q_next_ref is None:
        prologue()
    else:
        pl.when(first)(prologue)

    def run(j0, count):
        for u in range(count):
            phase(j0 + u + 1, bufs[(u + 1) % 2], bufs[u % 2], j0 + u)

    def body(jj, carry):
        run(SWEEP_UNROLL * jj, SWEEP_UNROLL)
        return carry

    lax.fori_loop(0, last // SWEEP_UNROLL, body, 0)
    rem = last % SWEEP_UNROLL
    for r in range(SWEEP_UNROLL):
        @pl.when(rem == r)
        def _(r=r):
            run(last - r, r)
            phase(None, None, bufs[r % 2], last, keep=keep_last, ahead=q_next_ref is not None)


def _pipeline(stages, lookahead):
    pending = {}
    for n in range(len(stages) + lookahead):
        if n < len(stages) and stages[n][0] is not None:
            pending[n] = stages[n][0]()
        if n >= lookahead:
            stages[n - lookahead][1](pending.pop(n - lookahead, None))


def _adaln_kernel(c_ref, w_ref, b_ref, o_ref):
    sc = _silu(c_ref[...])
    o_ref[...] = jnp.dot(sc.astype(BF16), w_ref[...].astype(BF16), preferred_element_type=F32) + b_ref[...]


def _adaln(c, w, b):
    bsz, d = c.shape
    n = w.shape[1] // d
    return pl.pallas_call(
        _adaln_kernel,
        grid=(n,),
        in_specs=[pl.BlockSpec((bsz, d), lambda j: (0, 0)),
                  pl.BlockSpec((d, d), lambda j: (0, j)),
                  pl.BlockSpec((1, d), lambda j: (0, j))],
        out_specs=pl.BlockSpec((None, bsz, d), lambda j: (j, 0, 0)),
        out_shape=jax.ShapeDtypeStruct((n, bsz, d), F32),
        name="adaln",
    )(c, w, b)


TOK_KS, TOK_KW, TOK_KC, TOK_VC, TOK_CQ, TOK_CKV, TOK_COLS = (int(o) for o in np.cumsum(
    [0, NSA_KV_GROUPS * LANES, NSA_KV_WIDTH, NSA_KV_WIDTH, NSA_KV_WIDTH, MLA_Q_RANK, MLA_KV_RANK]))
TOK_KR = TOK_KS
KR_LANE = NSA_HEAD_DIM
assert NSA_KV_WIDTH == LANES and KR_LANE + MLA_ROPE_DIM <= LANES
TR_Q, TR_VS, TR_VW, TR_G, TR_ZN, TR_ZM, TR_ROWS = (int(o) for o in np.cumsum(
    [0, NSA_WIDTH, NSA_KV_WIDTH, NSA_KV_WIDTH, NSA_KV_GROUPS * GATE_ROWS, NSA_WIDTH, MLA_WIDTH]))


def _proj_kernel(x_ref, mod_ref, ng_ref, wtok_ref, wtr_ref, posr_ref,
                 invn_ref, invm_ref, qng_ref, kvng_ref, wqt_ref, wkn_ref, wv_ref,
                 qt_ref, kaug_ref, kwin_ref, vts_ref, vtw_ref, kcmp_ref, vcmp_ref, gt_ref,
                 ztn_ref, ztm_ref, qtm_ref, kmla_ref, vtm_ref, *, tm, tk, scale_nsa, scale_mla):
    i = pl.program_id(1)
    bi = pl.program_id(0)
    shift, scale = mod_ref[0, pl.ds(bi, 1), :], mod_ref[1, pl.ds(bi, 1), :]
    h = _rms(x_ref[0], ng_ref[...]) * (1.0 + scale) + shift
    hb = h.astype(BF16)
    tok = lax.dot_general(hb, wtok_ref[...], NT, preferred_element_type=F32)
    tr = lax.dot_general(wtr_ref[...], hb, NT, preferred_element_type=F32)

    lane = lax.broadcasted_iota(I32, (tm, LANES), 1)
    row = lax.broadcasted_iota(I32, (tm, LANES), 0)
    blk = (i * tm + row) >> SLC_SHIFT
    onehot = (lane - NSA_HEAD_DIM == blk).astype(F32)
    ang_n = invn_ref[...] * posr_ref[0]
    cn, sn = jnp.cos(ang_n), jnp.sin(ang_n)
    ang_m = invm_ref[...] * posr_ref[0]
    cm, sm = jnp.cos(ang_m), jnp.sin(ang_m)
    zpad = jnp.zeros((LANES - NSA_HEAD_DIM - MLA_ROPE_DIM, tm), F32)
    ct = jnp.concatenate([cn, cn, cm, cm, zpad], axis=0).T
    st = jnp.concatenate([-sn, sn, -sm, sm, zpad], axis=0).T
    half_n = NSA_HEAD_DIM // 2
    for g in range(NSA_KV_GROUPS):
        ks = _rope_tok(tok[:, TOK_KS + LANES * g:TOK_KS + LANES * (g + 1)], ct, st, half_n, lane)
        kaug_ref[0, g] = jnp.where(lane >= NSA_HEAD_DIM, onehot, ks).astype(BF16)
    ctw = jnp.concatenate([cn, cn] * NSA_KV_GROUPS, axis=0).T
    stw = jnp.concatenate([-sn, sn] * NSA_KV_GROUPS, axis=0).T
    kwp = tok[:, TOK_KW:TOK_KW + LANES]
    first_half = (lane & (NSA_HEAD_DIM - 1)) < half_n
    kw = kwp * ctw + jnp.where(first_half, pltpu.roll(kwp, LANES - half_n, axis=1),
                               pltpu.roll(kwp, half_n, axis=1)) * stw
    for g in range(NSA_KV_GROUPS):
        kwin_ref[0, g] = (kw if g == 0 else pltpu.roll(kw, LANES - g * NSA_HEAD_DIM, axis=1)).astype(BF16)
    kcmp_ref[0] = tok[:, TOK_KC:TOK_KC + LANES]
    vcmp_ref[0] = tok[:, TOK_VC:TOK_VC + LANES]

    ckvn = _rms(tok[:, TOK_CKV:TOK_CKV + MLA_KV_RANK], kvng_ref[...])
    krr = _rope_tok(tok[:, TOK_KR:TOK_KR + LANES], ct, st, MLA_ROPE_DIM // 2, lane, base=KR_LANE)
    ckvb = ckvn.astype(BF16)
    kn = jnp.dot(ckvb, wkn_ref[...], preferred_element_type=F32)
    is_rot = (lane >= KR_LANE) & (lane < KR_LANE + MLA_ROPE_DIM)
    for hd in range(MLA_HEADS):
        kmla_ref[0, hd] = jnp.where(is_rot, krr, kn[:, hd * LANES:(hd + 1) * LANES]).astype(BF16)
    vtm = lax.dot_general(wv_ref[...], ckvb, NT, preferred_element_type=F32).astype(BF16)
    ones = jnp.ones((ONES_ROWS, tk), BF16)
    vr = MLA_V_DIM + ONES_ROWS
    for ii in range(tm // tk):
        for hd in range(MLA_HEADS):
            vtm_ref[0, ii, hd * vr:hd * vr + MLA_V_DIM, :] = vtm[hd * MLA_V_DIM:(hd + 1) * MLA_V_DIM,
                                                                 ii * tk:(ii + 1) * tk]
            vtm_ref[0, ii, hd * vr + MLA_V_DIM:(hd + 1) * vr, :] = ones

    cqn = _rms(tok[:, TOK_CQ:TOK_CQ + MLA_Q_RANK], qng_ref[...]).astype(BF16)
    qm = lax.dot_general(wqt_ref[...], cqn, NT, preferred_element_type=F32)
    nq = MLA_HEADS * MLA_NOPE_DIM
    hr = MLA_ROPE_DIM // 2
    x1 = qm[nq:nq + MLA_HEADS * hr].reshape(MLA_HEADS, hr, tm)
    x2 = qm[nq + MLA_HEADS * hr:nq + 2 * MLA_HEADS * hr].reshape(MLA_HEADS, hr, tm)
    cm_t, sm_t = cm[None], sm[None]
    o1 = (x1 * cm_t - x2 * sm_t) * scale_mla
    o2 = (x2 * cm_t + x1 * sm_t) * scale_mla
    nd = MLA_NOPE_DIM
    q_pad = jnp.zeros((MLA_QK - nd - MLA_ROPE_DIM, tm), BF16)
    for hd in range(MLA_HEADS):
        qtm_ref[0, hd, 0:nd, :] = (qm[hd * nd:(hd + 1) * nd] * scale_mla).astype(BF16)
        qtm_ref[0, hd, nd:nd + hr, :] = o1[hd].astype(BF16)
        qtm_ref[0, hd, nd + hr:nd + 2 * hr, :] = o2[hd].astype(BF16)
        qtm_ref[0, hd, nd + 2 * hr:MLA_QK, :] = q_pad

    hq = NSA_HEADS * half_n
    q1 = tr[TR_Q:TR_Q + hq].reshape(NSA_HEADS, half_n, tm)
    q2 = tr[TR_Q + hq:TR_Q + 2 * hq].reshape(NSA_HEADS, half_n, tm)
    cn_t, sn_t = cn[None], sn[None]
    qt_ref[0, 0:hq, :] = ((q1 * cn_t - q2 * sn_t) * scale_nsa).reshape(hq, tm).astype(BF16)
    qt_ref[0, hq:2 * hq, :] = ((q2 * cn_t + q1 * sn_t) * scale_nsa).reshape(hq, tm).astype(BF16)

    vts = tr[TR_VS:TR_VS + NSA_KV_WIDTH].astype(BF16)
    vtw = tr[TR_VW:TR_VW + NSA_KV_WIDTH].astype(BF16)
    dk = NSA_HEAD_DIM
    gr = dk + ONES_ROWS
    for ii in range(tm // tk):
        for g in range(NSA_KV_GROUPS):
            vts_ref[0, ii, g * gr:g * gr + dk, :] = vts[g * dk:(g + 1) * dk, ii * tk:(ii + 1) * tk]
            vtw_ref[0, ii, g * gr:g * gr + dk, :] = vtw[g * dk:(g + 1) * dk, ii * tk:(ii + 1) * tk]
            vts_ref[0, ii, g * gr + dk:(g + 1) * gr, :] = ones
            vtw_ref[0, ii, g * gr + dk:(g + 1) * gr, :] = ones
    gt_ref[0] = jax.nn.sigmoid(tr[TR_G:TR_G + NSA_KV_GROUPS * GATE_ROWS])
    ztn_ref[0] = _silu(tr[TR_ZN:TR_ZN + NSA_WIDTH]).astype(BF16)
    ztm_ref[0] = _silu(tr[TR_ZM:TR_ZM + MLA_WIDTH]).astype(BF16)


def _proj(x, mod3, ng, wtok, wtr, pos_row, inv_nb, inv_mb, qng, kvng, wqt, wkn, wv, *, tm, tk):
    b, s, d = x.shape
    nt = s // tk
    v_rows_n = NSA_KV_GROUPS * (NSA_HEAD_DIM + ONES_ROWS)
    v_rows_m = MLA_HEADS * (MLA_V_DIM + ONES_ROWS)

    def full(a):
        return pl.BlockSpec(a.shape, lambda bi, i, _n=a.ndim: (0,) * _n)

    in_specs = [pl.BlockSpec((1, tm, d), lambda bi, i: (bi, i, 0)),
                pl.BlockSpec(mod3.shape, lambda bi, i: (0, 0, 0)),
                full(ng), full(wtok), full(wtr),
                pl.BlockSpec((1, 1, tm), lambda bi, i: (bi, 0, i)),
                full(inv_nb), full(inv_mb),
                full(qng), full(kvng), full(wqt), full(wkn), full(wv)]
    out_shape = [
        jax.ShapeDtypeStruct((b, NSA_WIDTH, s), BF16),
        jax.ShapeDtypeStruct((b, NSA_KV_GROUPS, s, LANES), BF16),
        jax.ShapeDtypeStruct((b, NSA_KV_GROUPS, s, LANES), BF16),
        jax.ShapeDtypeStruct((b, nt, v_rows_n, tk), BF16),
        jax.ShapeDtypeStruct((b, nt, v_rows_n, tk), BF16),
        jax.ShapeDtypeStruct((b, s, NSA_KV_WIDTH), F32),
        jax.ShapeDtypeStruct((b, s, NSA_KV_WIDTH), F32),
        jax.ShapeDtypeStruct((b, NSA_KV_GROUPS * GATE_ROWS, s), F32),
        jax.ShapeDtypeStruct((b, NSA_WIDTH, s), BF16),
        jax.ShapeDtypeStruct((b, MLA_WIDTH, s), BF16),
        jax.ShapeDtypeStruct((b, MLA_HEADS, MLA_QK, s), BF16),
        jax.ShapeDtypeStruct((b, MLA_HEADS, s, MLA_QK), BF16),
        jax.ShapeDtypeStruct((b, nt, v_rows_m, tk), BF16),
    ]
    out_specs = [
        pl.BlockSpec((1, NSA_WIDTH, tm), lambda bi, i: (bi, 0, i)),
        pl.BlockSpec((1, NSA_KV_GROUPS, tm, LANES), lambda bi, i: (bi, 0, i, 0)),
        pl.BlockSpec((1, NSA_KV_GROUPS, tm, LANES), lambda bi, i: (bi, 0, i, 0)),
        pl.BlockSpec((1, tm // tk, v_rows_n, tk), lambda bi, i: (bi, i, 0, 0)),
        pl.BlockSpec((1, tm // tk, v_rows_n, tk), lambda bi, i: (bi, i, 0, 0)),
        pl.BlockSpec((1, tm, NSA_KV_WIDTH), lambda bi, i: (bi, i, 0)),
        pl.BlockSpec((1, tm, NSA_KV_WIDTH), lambda bi, i: (bi, i, 0)),
        pl.BlockSpec((1, NSA_KV_GROUPS * GATE_ROWS, tm), lambda bi, i: (bi, 0, i)),
        pl.BlockSpec((1, NSA_WIDTH, tm), lambda bi, i: (bi, 0, i)),
        pl.BlockSpec((1, MLA_WIDTH, tm), lambda bi, i: (bi, 0, i)),
        pl.BlockSpec((1, MLA_HEADS, MLA_QK, tm), lambda bi, i: (bi, 0, 0, i)),
        pl.BlockSpec((1, MLA_HEADS, tm, MLA_QK), lambda bi, i: (bi, 0, i, 0)),
        pl.BlockSpec((1, tm // tk, v_rows_m, tk), lambda bi, i: (bi, i, 0, 0)),
    ]
    kern = functools.partial(_proj_kernel, tm=tm, tk=tk, scale_nsa=NSA_HEAD_DIM ** -0.5 * LOG2E,
                             scale_mla=(MLA_NOPE_DIM + MLA_ROPE_DIM) ** -0.5 * LOG2E)
    return pl.pallas_call(
        kern, grid=(b, s // tm), in_specs=in_specs, out_specs=out_specs, out_shape=out_shape,
        compiler_params=pltpu.CompilerParams(vmem_limit_bytes=VMEM_LIMIT_BYTES),
        name="proj",
    )(x, mod3, ng, wtok, wtr, pos_row, inv_nb, inv_mb, qng, kvng, wqt, wkn, wv)


def _compress_kernel(k_ref, v_ref, pemb_ref, wk1_ref, wv1_ref, wk2_ref, wv2t_ref, posc_ref, invl_ref,
                     signl_ref, kc_ref, vct_ref):
    ncp = k_ref.shape[1] // CMP_STRIDE
    lane = lax.broadcasted_iota(I32, (ncp, LANES), 1)

    def hidden(r_ref, w1_ref):
        r = jnp.concatenate([r_ref[0, pl.ds(t, ncp, stride=CMP_STRIDE), :] for t in range(CMP_STRIDE)], axis=1)
        a = jnp.dot((r + pemb_ref[0]).astype(BF16), w1_ref[0], preferred_element_type=F32)
        bt = jnp.dot((r + pemb_ref[1]).astype(BF16), w1_ref[1], preferred_element_type=F32)
        return _silu(a + pltpu.roll(bt, ncp - 1, axis=0))

    cc, sc = _rope_lanes(posc_ref[...], invl_ref, signl_ref)
    hk = hidden(k_ref, wk1_ref)
    hv = hidden(v_ref, wv1_ref)
    for g in range(NSA_KV_GROUPS):
        hkg = hk[:, g * CMP_HIDDEN:(g + 1) * CMP_HIDDEN].astype(BF16)
        kc = jnp.dot(hkg, wk2_ref[...], preferred_element_type=F32)
        kc_ref[0, g] = _rope_tok(kc, cc, sc, NSA_HEAD_DIM // 2, lane).astype(BF16)
        hvg = hv[:, g * CMP_HIDDEN:(g + 1) * CMP_HIDDEN].astype(BF16)
        vct_ref[0, g] = lax.dot_general(wv2t_ref[...], hvg, NT, preferred_element_type=F32).astype(BF16)


def _compress(kcmp, vcmp, pemb, wk1, wv1, wk2, wv2t, pos_c, inv_l, sign_l):
    b, s, width = kcmp.shape
    ncp = s // CMP_STRIDE

    def full(a):
        return pl.BlockSpec(a.shape, lambda bi, _n=a.ndim: (0,) * _n)

    blk = pl.BlockSpec((1, s, width), lambda bi: (bi, 0, 0))
    tab = pl.BlockSpec((ncp, LANES), lambda bi: (bi, 0))
    return pl.pallas_call(
        _compress_kernel, grid=(b,),
        in_specs=[blk, blk, full(pemb), full(wk1), full(wv1), full(wk2), full(wv2t), tab,
                  full(inv_l), full(sign_l)],
        out_specs=[pl.BlockSpec((1, NSA_KV_GROUPS, ncp, LANES), lambda bi: (bi, 0, 0, 0)),
                   pl.BlockSpec((1, NSA_KV_GROUPS, NSA_HEAD_DIM, ncp), lambda bi: (bi, 0, 0, 0))],
        out_shape=[jax.ShapeDtypeStruct((b, NSA_KV_GROUPS, ncp, LANES), BF16),
                   jax.ShapeDtypeStruct((b, NSA_KV_GROUPS, NSA_HEAD_DIM, ncp), BF16)],
        compiler_params=pltpu.CompilerParams(vmem_limit_bytes=VMEM_LIMIT_BYTES),
        name="compress",
    )(kcmp, vcmp, pemb, wk1, wv1, wk2, wv2t, pos_c, inv_l, sign_l)


def _nsa_kernel(q1_ref, q2_ref, kc_ref, vct_ref, kaug_ref, vts_ref, kwin_ref, vtw_ref, g_ref, z_ref, mt_ref,
                o_ref, qaug_ref, sa_ref, sb_ref, m_ref, acc_ref, tot_ref, imp_ref, rank_ref, *, tq, tk, n_sel):
    i = pl.program_id(1)
    nh, ng, hpg, dk, half = NSA_HEADS, NSA_KV_GROUPS, NSA_HPG, NSA_HEAD_DIM, NSA_HEAD_DIM // 2
    vr = dk + ONES_ROWS
    group = [h // hpg for h in range(nh)]
    for h in range(nh):
        qaug_ref[h, 0:half, :] = q1_ref[0, h * half:(h + 1) * half, :]
        qaug_ref[h, half:dk, :] = q2_ref[0, h * half:(h + 1) * half, :]

    def gate(h, branch):
        row = group[h] * GATE_ROWS + (h % hpg) * N_BRANCH + branch
        return g_ref[0, row:row + 1, :]

    row_k = lax.broadcasted_iota(I32, (tk, tq), 0)
    col_q = lax.broadcasted_iota(I32, (tk, tq), 1)
    causal = row_k <= col_q

    def reset():
        m_ref[...] = jnp.full_like(m_ref, NEG_INF)
        acc_ref[...] = jnp.zeros_like(acc_ref)

    def add_branch(branch):
        for h in range(nh):
            inv_l = 1.0 / acc_ref[h, dk:dk + 1, :]
            tot_ref[h] = tot_ref[h] + (gate(h, branch) * inv_l) * acc_ref[h, 0:dk, :]

    def q_cols(h):
        return qaug_ref[h, 0:dk, :]

    ncp = kc_ref.shape[2]
    t_row = i * tq + lax.broadcasted_iota(I32, (1, tq), 1)
    last_n = (t_row - (CMP_BLOCK - 1)) >> CMP_SHIFT
    valid = lax.broadcasted_iota(I32, (ncp, tq), 0) <= last_n
    col_ok = last_n >= 0
    p_heads = {}

    def cmp_update(h, s_c):
        s_c = jnp.where(valid, s_c, NEG_INF)
        e = jnp.exp2(s_c - jnp.max(s_c, axis=0, keepdims=True))
        l_c = jnp.sum(e, axis=0, keepdims=True)
        p_c = e * jnp.where(col_ok, 1.0 / l_c, 0.0)
        o_c = jnp.dot(vct_ref[0, group[h]], p_c.astype(BF16), preferred_element_type=F32)
        tot_ref[h] = gate(h, 0) * o_c
        p_heads[h] = p_c

    nb = mt_ref.shape[0]

    def importance(g, _):
        psum = functools.reduce(lambda a, b: a + b, [p_heads[h] for h in range(nh) if group[h] == g])
        hi = psum.astype(BF16)
        lo = (psum - hi.astype(F32)).astype(BF16)
        mt = mt_ref[...]
        imp = jnp.dot(mt, hi, preferred_element_type=F32) + jnp.dot(mt, lo, preferred_element_type=F32)
        j_idx = lax.broadcasted_iota(I32, (nb, tq), 0)
        cur = (i * tq + lax.broadcasted_iota(I32, (nb, tq), 1)) >> SLC_SHIFT
        forced = (j_idx == 0) | (j_idx == cur) | (j_idx == cur - 1)
        imp_ref[g] = jnp.where(forced, FORCED_SCORE, jnp.where(j_idx > cur, -FORCED_SCORE, imp))

    stages = []
    for g in range(ng):
        kc = kc_ref[0, g, :, 0:dk]
        stages += [(functools.partial(jnp.dot, kc, q_cols(h), preferred_element_type=F32),
                    functools.partial(cmp_update, h)) for h in range(nh) if group[h] == g]
        stages.append((None, functools.partial(importance, g)))

    reset()
    n_back = WINDOW // tk
    for back in range(n_back + 1):
        jb = jnp.maximum(i - back, 0)
        if back == 0:
            keep = causal
        elif back == n_back:
            keep = (row_k > col_q) & (i >= back)
        else:
            keep = jnp.broadcast_to(i >= back, (tk, tq))
        for g in range(ng):
            kt_b = kwin_ref[0, g, pl.ds(pl.multiple_of(jb * tk, tk), tk), 0:dk]
            vt_b = vtw_ref[0, jb, g * vr:(g + 1) * vr, :]
            for h in range(nh):
                if group[h] == g:
                    stages.append((functools.partial(jnp.dot, kt_b, q_cols(h), preferred_element_type=F32),
                                   functools.partial(_chain_update, v_t=vt_b, m_ref=m_ref, acc_ref=acc_ref,
                                                     ch=h, keep=keep)))
    _pipeline(stages, lookahead=5)
    add_branch(2)

    rank_ref[...] = jnp.zeros_like(rank_ref)
    sub = lax.broadcasted_iota(I32, (SUBLANES, tq), 0)
    last_group = ((i + 1) * (tq // SLC_BLOCK) - 1) // SUBLANES

    def count(g, c, v):
        blk = imp_ref[g, v * SUBLANES:(v + 1) * SUBLANES, :]
        cnt = rank_ref[g, v * SUBLANES:(v + 1) * SUBLANES, :]
        for rr in range(SUBLANES):
            row = imp_ref[g, c * SUBLANES + rr:c * SUBLANES + rr + 1, :]
            if c < v:
                beats = row >= blk
            elif c > v:
                beats = row > blk
            else:
                beats = (row > blk) | ((row == blk) & (sub > rr))
            cnt = cnt + beats.astype(I32)
        rank_ref[g, v * SUBLANES:(v + 1) * SUBLANES, :] = cnt

    for lvl in range(nb // SUBLANES):
        @pl.when(lvl <= last_group)
        def _(lvl=lvl):
            for g in range(ng):
                for v in range(lvl + 1):
                    count(g, lvl, v)
                for c in range(lvl):
                    count(g, c, lvl)

    for g in range(ng):
        bias = jnp.where(rank_ref[g] < n_sel, 0.0, SEL_BIAS).astype(BF16)
        for h in range(nh):
            if group[h] == g:
                qaug_ref[h, dk:dk + nb, :] = bias

    reset()
    _causal_sweep(lambda j, g: kaug_ref[0, g, pl.ds(pl.multiple_of(j * tk, tk), tk), :],
                  lambda j, g: vts_ref[0, j, g * vr:(g + 1) * vr, :],
                  group, group, qaug_ref, sa_ref, sb_ref, m_ref, acc_ref, i, causal)
    add_branch(1)

    for h in range(nh):
        zz = z_ref[0, h * dk:(h + 1) * dk, :].astype(F32)
        o_ref[0, h * dk:(h + 1) * dk, :] = (tot_ref[h] * zz).astype(BF16)


def _nsa(qt, kc, vct, kaug, vts, kwin, vtw, gt, ztn, mt, *, tq, tk, n_sel):
    b, _, s = qt.shape
    nt = s // tk
    ncp = kc.shape[2]
    ng, nh = NSA_KV_GROUPS, NSA_HEADS
    hq = nh * (NSA_HEAD_DIM // 2)
    nb = mt.shape[0]
    vr = NSA_HEAD_DIM + ONES_ROWS
    in_specs = [
        pl.BlockSpec((1, hq, tq), lambda bi, i: (bi, 0, i)),
        pl.BlockSpec((1, hq, tq), lambda bi, i: (bi, 1, i)),
        pl.BlockSpec((1, ng, ncp, LANES), lambda bi, i: (bi, 0, 0, 0)),
        pl.BlockSpec((1, ng, NSA_HEAD_DIM, ncp), lambda bi, i: (bi, 0, 0, 0)),
        pl.BlockSpec((1, ng, s, LANES), lambda bi, i: (bi, 0, 0, 0)),
        pl.BlockSpec((1, nt, ng * vr, tk), lambda bi, i: (bi, 0, 0, 0)),
        pl.BlockSpec((1, ng, s, LANES), lambda bi, i: (bi, 0, 0, 0)),
        pl.BlockSpec((1, nt, ng * vr, tk), lambda bi, i: (bi, 0, 0, 0)),
        pl.BlockSpec((1, ng * GATE_ROWS, tq), lambda bi, i: (bi, 0, i)),
        pl.BlockSpec((1, NSA_WIDTH, tq), lambda bi, i: (bi, 0, i)),
        pl.BlockSpec(mt.shape, lambda bi, i: (0, 0)),
    ]
    kern = functools.partial(_nsa_kernel, tq=tq, tk=tk, n_sel=n_sel)
    return pl.pallas_call(
        kern, grid=(b, s // tq), in_specs=in_specs,
        out_specs=pl.BlockSpec((1, NSA_WIDTH, tq), lambda bi, i: (bi, 0, i)),
        out_shape=jax.ShapeDtypeStruct((b, NSA_WIDTH, s), BF16),
        scratch_shapes=[pltpu.VMEM((nh, NSA_HEAD_DIM + nb, tq), BF16),
                        pltpu.VMEM((nh, tk, tq), F32), pltpu.VMEM((nh, tk, tq), F32),
                        pltpu.VMEM((nh, 1, tq), F32), pltpu.VMEM((nh, vr, tq), F32),
                        pltpu.VMEM((nh, NSA_HEAD_DIM, tq), F32),
                        pltpu.VMEM((ng, nb, tq), F32), pltpu.VMEM((ng, nb, tq), I32)],
        compiler_params=pltpu.CompilerParams(dimension_semantics=("arbitrary", "arbitrary"),
                                             vmem_limit_bytes=VMEM_LIMIT_BYTES),
        name="nsa",
    )(qt, qt, kc, vct, kaug, vts, kwin, vtw, gt, ztn, mt)


def _mla_kernel(q_ref, qn_ref, k_ref, vt_ref, z_ref, o_ref, sa_ref, sb_ref, m_ref, acc_ref, *, tq, tk):
    i = pl.program_id(1)
    dv, vr = MLA_V_DIM, MLA_V_DIM + ONES_ROWS
    m_ref[...] = jnp.full_like(m_ref, NEG_INF)
    acc_ref[...] = jnp.zeros_like(acc_ref)

    causal = lax.broadcasted_iota(I32, (tk, tq), 0) <= lax.broadcasted_iota(I32, (tk, tq), 1)
    heads = list(range(MLA_HEADS))
    _causal_sweep(lambda j, hd: k_ref[0, hd, pl.ds(pl.multiple_of(j * tk, tk), tk), :],
                  lambda j, hd: vt_ref[0, j, hd * vr:(hd + 1) * vr, :],
                  heads, heads, q_ref.at[0], sa_ref, sb_ref, m_ref, acc_ref, i, causal,
                  q_next_ref=qn_ref.at[0], first=i == 0)
    for hd in range(MLA_HEADS):
        o_h = acc_ref[hd, 0:dv, :] * (1.0 / acc_ref[hd, dv:dv + 1, :])
        zz = z_ref[0, hd * dv:(hd + 1) * dv, :].astype(F32)
        o_ref[0, hd * dv:(hd + 1) * dv, :] = (o_h * zz).astype(BF16)


def _mla(qtm, kmla, vtm, ztm, *, tq, tk):
    b, _, s, _ = kmla.shape
    nt = s // tk
    vrows = MLA_HEADS * (MLA_V_DIM + ONES_ROWS)
    kern = functools.partial(_mla_kernel, tq=tq, tk=tk)
    return pl.pallas_call(
        kern, grid=(b, s // tq),
        in_specs=[pl.BlockSpec((1, MLA_HEADS, MLA_QK, tq), lambda bi, i: (bi, 0, 0, i)),
                  pl.BlockSpec((1, MLA_HEADS, MLA_QK, tq), lambda bi, i: (bi, 0, 0, jnp.minimum(i + 1, s // tq - 1))),
                  pl.BlockSpec((1, MLA_HEADS, s, MLA_QK), lambda bi, i: (bi, 0, 0, 0)),
                  pl.BlockSpec((1, nt, vrows, tk), lambda bi, i: (bi, 0, 0, 0)),
                  pl.BlockSpec((1, MLA_WIDTH, tq), lambda bi, i: (bi, 0, i))],
        out_specs=pl.BlockSpec((1, MLA_WIDTH, tq), lambda bi, i: (bi, 0, i)),
        out_shape=jax.ShapeDtypeStruct((b, MLA_WIDTH, s), BF16),
        scratch_shapes=[pltpu.VMEM((MLA_HEADS, tk, tq), F32), pltpu.VMEM((MLA_HEADS, tk, tq), F32),
                        pltpu.VMEM((MLA_HEADS, 1, tq), F32), pltpu.VMEM((MLA_HEADS, MLA_V_DIM + ONES_ROWS, tq), F32)],
        compiler_params=pltpu.CompilerParams(dimension_semantics=("arbitrary", "arbitrary"),
                                             vmem_limit_bytes=VMEM_LIMIT_BYTES),
        name="mla",
    )(qtm, qtm, kmla, vtm, ztm)


def _out_kernel(x_ref, mn_ref, mm_ref, w_ref, mod_ref, fg_ref, o_ref, *, final):
    y = lax.dot_general(mn_ref[0], w_ref[0:NSA_WIDTH, :], TN, preferred_element_type=F32)
    y = y + lax.dot_general(mm_ref[0], w_ref[NSA_WIDTH:MIX_WIDTH, :], TN, preferred_element_type=F32)
    x2 = x_ref[0] + mod_ref[2, pl.ds(pl.program_id(0), 1), :] * y
    o_ref[0] = _rms(x2, fg_ref[...]) if final else x2


def _out(x, mn, mm, w_out, mod3, fg, *, tm, final):
    b, s, d = x.shape
    return pl.pallas_call(
        functools.partial(_out_kernel, final=final), grid=(b, s // tm),
        in_specs=[pl.BlockSpec((1, tm, d), lambda bi, i: (bi, i, 0)),
                  pl.BlockSpec((1, NSA_WIDTH, tm), lambda bi, i: (bi, 0, i)),
                  pl.BlockSpec((1, MLA_WIDTH, tm), lambda bi, i: (bi, 0, i)),
                  pl.BlockSpec(w_out.shape, lambda bi, i: (0, 0)),
                  pl.BlockSpec(mod3.shape, lambda bi, i: (0, 0, 0)),
                  pl.BlockSpec((1, d), lambda bi, i: (0, 0))],
        out_specs=pl.BlockSpec((1, tm, d), lambda bi, i: (bi, i, 0)),
        out_shape=jax.ShapeDtypeStruct((b, s, d), F32),
        compiler_params=pltpu.CompilerParams(vmem_limit_bytes=VMEM_LIMIT_BYTES),
        name="out_proj",
    )(x, mn, mm, w_out, mod3, fg)


def _cmp_to_slc_t(ncp, nc, nslc, nb):
    start = np.arange(nc)[:, None] * CMP_STRIDE
    bstart = np.arange(nslc)[None, :] * SLC_BLOCK
    ov = np.minimum(start + CMP_BLOCK, bstart + SLC_BLOCK) - np.maximum(start, bstart)
    m = (np.clip(ov, 0, None) / CMP_BLOCK).astype(np.float32)
    out = np.zeros((nb, ncp), np.float32)
    out[:nslc, :nc] = m.T
    return out


def _layout_w_in(w):
    d = w.shape[0]
    o = (0,) + IN_OFFSETS + (w.shape[1],)
    q_n, ks_n, vs_n, kw_n, kr_m = (w[:, o[k]:o[k + 1]] for k in (0, 3, 4, 5, 11))
    dk = NSA_HEAD_DIM

    def per_group(kx):
        return jnp.pad(kx.reshape(d, NSA_KV_GROUPS, dk), ((0, 0), (0, 0), (0, LANES - dk))).reshape(d, -1)

    ks_kr = jnp.concatenate([ks_n[:, :dk], jnp.pad(kr_m, ((0, 0), (0, LANES - dk - MLA_ROPE_DIM))),
                             per_group(ks_n)[:, LANES:]], axis=1)
    wtok = jnp.concatenate([ks_kr, kw_n, w[:, o[1]:o[3]], w[:, o[9]:o[11]]], axis=1)
    qr = q_n.reshape(d, NSA_HEADS, 2, dk // 2)
    q_perm = jnp.swapaxes(qr, 1, 2).reshape(d, -1)
    wtr = jnp.concatenate([q_perm, vs_n, w[:, o[6]:o[9]], w[:, o[12]:o[13]]], axis=1).T
    assert wtok.shape[1] == TOK_COLS and wtr.shape[0] == TR_ROWS
    return wtok.T.astype(BF16), wtr.astype(BF16)


def _layout_w1(w1):
    hid = w1.shape[1]
    ng = NSA_KV_GROUPS
    w1r = w1.reshape(2, CMP_STRIDE, 1, NSA_HEAD_DIM, hid)
    blocks = [jnp.pad(w1r, ((0, 0), (0, 0), (0, 0), (0, 0), (g * hid, (ng - 1 - g) * hid))) for g in range(ng)]
    full = jnp.concatenate(blocks, axis=2)
    return full.reshape(2, CMP_STRIDE * NSA_KV_WIDTH, ng * hid).astype(BF16)


def kernel(x, c, positions, ada_w, ada_b, norm_g, w_in, cmp_pos, cmp_k_w1, cmp_k_w2, cmp_v_w1, cmp_v_w2,
           q_norm_g, w_q_up, kv_norm_g, w_kv_up, w_out, final_norm_g):
    b, s, d = x.shape
    depth = ada_w.shape[0]
    tm, tq = PROJ_TILE, ATT_TILE
    tk = tq
    assert s % tm == 0 and tm % tk == 0 and WINDOW % tk == 0 and (tq & (tq - 1)) == 0 and s % OUT_TILE == 0
    assert CMP_BLOCK == 2 * CMP_STRIDE and s % SLC_BLOCK == 0
    nslc = s // SLC_BLOCK
    nb = LANES - NSA_HEAD_DIM
    assert nslc <= nb
    ncp = s // CMP_STRIDE
    nc = ncp - 1

    half_n, half_m = NSA_HEAD_DIM // 2, MLA_ROPE_DIM // 2
    inv_n = ROPE_THETA ** (-jnp.arange(half_n, dtype=F32) / half_n)
    inv_m = ROPE_THETA ** (-jnp.arange(half_m, dtype=F32) / half_m)
    ones_n, ones_m = jnp.ones((half_n,), F32), jnp.ones((half_m,), F32)
    pad = jnp.zeros((LANES - NSA_HEAD_DIM - MLA_ROPE_DIM,), F32)
    inv_l = jnp.concatenate([inv_n, inv_n, inv_m, inv_m, pad])[None]
    sign_l = jnp.concatenate([-ones_n, ones_n, -ones_m, ones_m, pad])[None]
    pos_f = positions.astype(F32)
    pos_row = pos_f.reshape(b, 1, s)
    cmp_end = np.minimum(np.arange(ncp) * CMP_STRIDE + CMP_BLOCK - 1, s - 1)
    pos_c = jnp.broadcast_to(pos_f[:, cmp_end].reshape(b * ncp, 1), (b * ncp, LANES))
    inv_nb, inv_mb = inv_n[:, None], inv_m[:, None]

    mt = jnp.asarray(_cmp_to_slc_t(ncp, nc, nslc, nb), dtype=BF16)

    for l in range(depth):
        mod3 = _adaln(c, ada_w[l], ada_b[l].reshape(1, -1))
        wtok, wtr = _layout_w_in(w_in[l])
        wq = w_q_up[l].reshape(MLA_Q_RANK, MLA_HEADS, MLA_NOPE_DIM + MLA_ROPE_DIM)
        wqt = jnp.concatenate([wq[:, :, :MLA_NOPE_DIM].reshape(MLA_Q_RANK, -1),
                               wq[:, :, MLA_NOPE_DIM:MLA_NOPE_DIM + half_m].reshape(MLA_Q_RANK, -1),
                               wq[:, :, MLA_NOPE_DIM + half_m:].reshape(MLA_Q_RANK, -1)], axis=1).T.astype(BF16)
        wkv = w_kv_up[l].reshape(MLA_KV_RANK, MLA_HEADS, MLA_NOPE_DIM + MLA_V_DIM)
        wkn = jnp.pad(wkv[:, :, :MLA_NOPE_DIM], ((0, 0), (0, 0), (0, LANES - MLA_NOPE_DIM))).reshape(
            MLA_KV_RANK, MLA_HEADS * LANES).astype(BF16)
        wv = wkv[:, :, MLA_NOPE_DIM:].transpose(1, 2, 0).reshape(MLA_WIDTH, MLA_KV_RANK).astype(BF16)

        (qt, kaug, kwin, vts, vtw, kcmp, vcmp, gt, ztn, ztm, qtm, kmla, vtm) = _proj(
            x, mod3, norm_g[l].reshape(1, d), wtok, wtr, pos_row, inv_nb, inv_mb,
            q_norm_g[l].reshape(1, -1), kv_norm_g[l].reshape(1, -1), wqt, wkn, wv, tm=tm, tk=tk)

        pos_l = cmp_pos[l]
        pemb = jnp.broadcast_to(pos_l.reshape(2, CMP_STRIDE, 1, NSA_HEAD_DIM),
                                (2, CMP_STRIDE, NSA_KV_GROUPS, NSA_HEAD_DIM)).reshape(2, 1, -1)
        wk2 = jnp.pad(cmp_k_w2[l], ((0, 0), (0, LANES - NSA_HEAD_DIM))).astype(BF16)
        kc, vct = _compress(kcmp, vcmp, pemb,
                            _layout_w1(cmp_k_w1[l]), _layout_w1(cmp_v_w1[l]), wk2,
                            cmp_v_w2[l].T.astype(BF16), pos_c, inv_l, sign_l)

        mix_n = _nsa(qt, kc, vct, kaug, vts, kwin, vtw, gt, ztn, mt, tq=tq, tk=tk, n_sel=min(SLC_TOPK, nslc))
        mix_m = _mla(qtm, kmla, vtm, ztm, tq=tq, tk=tk)
        x = _out(x, mix_n, mix_m, w_out[l].astype(BF16), mod3, final_norm_g.reshape(1, d), tm=OUT_TILE,
                 final=(l == depth - 1))
    return x
```

```python
import functools

import numpy as np
import jax
import jax.numpy as jnp
from jax import lax
from jax.experimental import pallas as pl
from jax.experimental.pallas import tpu as pltpu

F32 = jnp.float32
BF16 = jnp.bfloat16
I32 = jnp.int32

NSA_HEADS = 8
NSA_KV_GROUPS = 2
NSA_HPG = NSA_HEADS // NSA_KV_GROUPS
NSA_HEAD_DIM = 64
NSA_WIDTH = NSA_HEADS * NSA_HEAD_DIM
NSA_KV_WIDTH = NSA_KV_GROUPS * NSA_HEAD_DIM
CMP_BLOCK = 32
CMP_STRIDE = 16
CMP_HIDDEN = 128
SLC_BLOCK = 64
SLC_TOPK = 16
WINDOW = 512
N_BRANCH = 3
FORCED_SCORE = 1.0e4
MLA_HEADS = 8
MLA_NOPE_DIM = 64
MLA_ROPE_DIM = 32
MLA_V_DIM = 64
MLA_WIDTH = MLA_HEADS * MLA_V_DIM
MLA_Q_RANK = 256
MLA_KV_RANK = 128
MIX_WIDTH = NSA_WIDTH + MLA_WIDTH
ROPE_THETA = 10000.0
NORM_EPS = 1e-6
NEG_INF = -1e30
IN_SIZES = (NSA_WIDTH, NSA_KV_WIDTH, NSA_KV_WIDTH, NSA_KV_WIDTH, NSA_KV_WIDTH, NSA_KV_WIDTH, NSA_KV_WIDTH,
            NSA_HEADS * N_BRANCH, NSA_WIDTH, MLA_Q_RANK, MLA_KV_RANK, MLA_ROPE_DIM, MLA_WIDTH)
IN_OFFSETS = tuple(int(o) for o in np.cumsum(IN_SIZES)[:-1])

LANES = 128
SUBLANES = 8
VMEM_LIMIT_BYTES = 56 * 1024 * 1024

PROJ_TILE = 1024
OUT_TILE = 1024
ATT_TILE = 256
SEL_BIAS = NEG_INF
GATE_ROWS = NSA_HPG * N_BRANCH
ONES_ROWS = 16
MLA_QK = LANES
LOG2E = float(np.log2(np.e))
SWEEP_UNROLL = 4
SLC_SHIFT = SLC_BLOCK.bit_length() - 1
CMP_SHIFT = CMP_STRIDE.bit_length() - 1
assert 1 << SLC_SHIFT == SLC_BLOCK and 1 << CMP_SHIFT == CMP_STRIDE

NT = (((1,), (1,)), ((), ()))
TN = (((0,), (0,)), ((), ()))


def _silu(v):
    return v * jax.nn.sigmoid(v)


def _rms(v, g):
    ms = jnp.mean(v * v, axis=-1, keepdims=True)
    return v * lax.rsqrt(ms + NORM_EPS) * g


def _rope_tok(v, c, s_signed, half, lane, base=0):
    up = pltpu.roll(v, LANES - half, axis=1)
    dn = pltpu.roll(v, half, axis=1)
    return v * c + jnp.where(lane < base + half, up, dn) * s_signed


def _chain_update(s_t, v_t, m_ref, acc_ref, ch, keep=None):
    if keep is not None:
        s_t = jnp.where(keep, s_t, NEG_INF)
    m_prev = m_ref[ch]
    m_new = jnp.maximum(m_prev, jnp.max(s_t, axis=0, keepdims=True))
    alpha = jnp.exp2(m_prev - m_new)
    p = jnp.exp2(s_t - m_new)
    acc_ref[ch] = alpha * acc_ref[ch] + jnp.dot(v_t, p.astype(BF16), preferred_element_type=F32)
    m_ref[ch] = m_new


def _causal_sweep(k_tile, v_tile, k_group, v_group, q_ref, sa_ref, sb_ref, m_ref, acc_ref, last, keep_last,
                  lookahead=2, q_next_ref=None, first=None):
    n_chains = len(k_group)

    def loader(tile_fn, j):
        cache = {}
        return lambda g: cache.setdefault(g, tile_fn(j, g))

    def phase(j_next, s_next_ref, s_cur_ref, j_cur, keep=None, ahead=False):
        k_next = loader(k_tile, j_next) if j_next is not None else None
        k_zero = loader(k_tile, 0) if ahead else None
        v_cur = loader(v_tile, j_cur)
        for n in range(n_chains + lookahead):
            if k_next is not None and n < n_chains:
                s_next_ref[n] = jnp.dot(k_next(k_group[n]), q_ref[n], preferred_element_type=F32)
            if n >= lookahead:
                ch = n - lookahead
                _chain_update(s_cur_ref[ch], v_cur(v_group[ch]), m_ref, acc_ref, ch, keep=keep)
                if ahead:
                    sa_ref[ch] = jnp.dot(k_zero(k_group[ch]), q_next_ref[ch], preferred_element_type=F32)

    bufs = (sa_ref, sb_ref)

    def prologue():
        k_0 = loader(k_tile, 0)
        for ch in range(n_chains):
            sa_ref[ch] = jnp.dot(k_0(k_group[ch]), q_ref[ch], preferred_element_type=F32)

    if q_next_ref is None:
        prologue()
    else:
        pl.when(first)(prologue)

    def run(j0, count):
        for u in range(count):
            phase(j0 + u + 1, bufs[(u + 1) % 2], bufs[u % 2], j0 + u)

    def body(jj, carry):
        run(SWEEP_UNROLL * jj, SWEEP_UNROLL)
        return carry

    lax.fori_loop(0, last // SWEEP_UNROLL, body, 0)
    rem = last % SWEEP_UNROLL
    for r in range(SWEEP_UNROLL):
        @pl.when(rem == r)
        def _(r=r):
            run(last - r, r)
            phase(None, None, bufs[r % 2], last, keep=keep_last, ahead=q_next_ref is not None)


def _pipeline(stages, lookahead):
    pending = {}
    for n in range(len(stages) + lookahead):
        if n < len(stages) and stages[n][0] is not None:
            pending[n] = stages[n][0]()
        if n >= lookahead:
            stages[n - lookahead][1](pending.pop(n - lookahead, None))


def _adaln_kernel(c_ref, w_ref, b_ref, o_ref):
    sc = _silu(c_ref[...])
    o_ref[...] = jnp.dot(sc.astype(BF16), w_ref[...].astype(BF16), preferred_element_type=F32) + b_ref[...]


def _adaln(c, w, b):
    bsz, d = c.shape
    n = w.shape[1] // d
    return pl.pallas_call(
        _adaln_kernel,
        grid=(n,),
        in_specs=[pl.BlockSpec((bsz, d), lambda j: (0, 0)),
                  pl.BlockSpec((d, d), lambda j: (0, j)),
                  pl.BlockSpec((1, d), lambda j: (0, j))],
        out_specs=pl.BlockSpec((None, bsz, d), lambda j: (j, 0, 0)),
        out_shape=jax.ShapeDtypeStruct((n, bsz, d), F32),
        name="adaln",
    )(c, w, b)


TOK_KS, TOK_KW, TOK_KC, TOK_VC, TOK_CQ, TOK_CKV, TOK_COLS = (int(o) for o in np.cumsum(
    [0, NSA_KV_GROUPS * LANES, NSA_KV_WIDTH, NSA_KV_WIDTH, NSA_KV_WIDTH, MLA_Q_RANK, MLA_KV_RANK]))
TOK_KR = TOK_KS
KR_LANE = NSA_HEAD_DIM
assert NSA_KV_WIDTH == LANES and KR_LANE + MLA_ROPE_DIM <= LANES
TR_Q, TR_VS, TR_VW, TR_G, TR_ZN, TR_ZM, TR_ROWS = (int(o) for o in np.cumsum(
    [0, NSA_WIDTH, NSA_KV_WIDTH, NSA_KV_WIDTH, NSA_KV_GROUPS * GATE_ROWS, NSA_WIDTH, MLA_WIDTH]))


def _proj_kernel(x_ref, mod_ref, ng_ref, wtok_ref, wtr_ref, posr_ref,
                 invn_ref, invm_ref, qng_ref, kvng_ref, wqt_ref, wkn_ref, wv_ref,
                 qt_ref, kaug_ref, kwin_ref, vts_ref, vtw_ref, kcmp_ref, vcmp_ref, gt_ref,
                 ztn_ref, ztm_ref, qtm_ref, kmla_ref, vtm_ref, *, tm, tk, scale_nsa, scale_mla):
    i = pl.program_id(1)
    bi = pl.program_id(0)
    shift, scale = mod_ref[0, pl.ds(bi, 1), :], mod_ref[1, pl.ds(bi, 1), :]
    h = _rms(x_ref[0], ng_ref[...]) * (1.0 + scale) + shift
    hb = h.astype(BF16)
    tok = lax.dot_general(hb, wtok_ref[...], NT, preferred_element_type=F32)
    tr = lax.dot_general(wtr_ref[...], hb, NT, preferred_element_type=F32)

    lane = lax.broadcasted_iota(I32, (tm, LANES), 1)
    row = lax.broadcasted_iota(I32, (tm, LANES), 0)
    blk = (i * tm + row) >> SLC_SHIFT
    onehot = (lane - NSA_HEAD_DIM == blk).astype(F32)
    ang_n = invn_ref[...] * posr_ref[0]
    cn, sn = jnp.cos(ang_n), jnp.sin(ang_n)
    ang_m = invm_ref[...] * posr_ref[0]
    cm, sm = jnp.cos(ang_m), jnp.sin(ang_m)
    zpad = jnp.zeros((LANES - NSA_HEAD_DIM - MLA_ROPE_DIM, tm), F32)
    ct = jnp.concatenate([cn, cn, cm, cm, zpad], axis=0).T
    st = jnp.concatenate([-sn, sn, -sm, sm, zpad], axis=0).T
    half_n = NSA_HEAD_DIM // 2
    for g in range(NSA_KV_GROUPS):
        ks = _rope_tok(tok[:, TOK_KS + LANES * g:TOK_KS + LANES * (g + 1)], ct, st, half_n, lane)
        kaug_ref[0, g] = jnp.where(lane >= NSA_HEAD_DIM, onehot, ks).astype(BF16)
    ctw = jnp.concatenate([cn, cn] * NSA_KV_GROUPS, axis=0).T
    stw = jnp.concatenate([-sn, sn] * NSA_KV_GROUPS, axis=0).T
    kwp = tok[:, TOK_KW:TOK_KW + LANES]
    first_half = (lane & (NSA_HEAD_DIM - 1)) < half_n
    kw = kwp * ctw + jnp.where(first_half, pltpu.roll(kwp, LANES - half_n, axis=1),
                               pltpu.roll(kwp, half_n, axis=1)) * stw
    for g in range(NSA_KV_GROUPS):
        kwin_ref[0, g] = (kw if g == 0 else pltpu.roll(kw, LANES - g * NSA_HEAD_DIM, axis=1)).astype(BF16)
    kcmp_ref[0] = tok[:, TOK_KC:TOK_KC + LANES]
    vcmp_ref[0] = tok[:, TOK_VC:TOK_VC + LANES]

    ckvn = _rms(tok[:, TOK_CKV:TOK_CKV + MLA_KV_RANK], kvng_ref[...])
    krr = _rope_tok(tok[:, TOK_KR:TOK_KR + LANES], ct, st, MLA_ROPE_DIM // 2, lane, base=KR_LANE)
    ckvb = ckvn.astype(BF16)
    kn = jnp.dot(ckvb, wkn_ref[...], preferred_element_type=F32)
    is_rot = (lane >= KR_LANE) & (lane < KR_LANE + MLA_ROPE_DIM)
    for hd in range(MLA_HEADS):
        kmla_ref[0, hd] = jnp.where(is_rot, krr, kn[:, hd * LANES:(hd + 1) * LANES]).astype(BF16)
    vtm = lax.dot_general(wv_ref[...], ckvb, NT, preferred_element_type=F32).astype(BF16)
    ones = jnp.ones((ONES_ROWS, tk), BF16)
    vr = MLA_V_DIM + ONES_ROWS
    for ii in range(tm // tk):
        for hd in range(MLA_HEADS):
            vtm_ref[0, ii, hd * vr:hd * vr + MLA_V_DIM, :] = vtm[hd * MLA_V_DIM:(hd + 1) * MLA_V_DIM,
                                                                 ii * tk:(ii + 1) * tk]
            vtm_ref[0, ii, hd * vr + MLA_V_DIM:(hd + 1) * vr, :] = ones

    cqn = _rms(tok[:, TOK_CQ:TOK_CQ + MLA_Q_RANK], qng_ref[...]).astype(BF16)
    qm = lax.dot_general(wqt_ref[...], cqn, NT, preferred_element_type=F32)
    nq = MLA_HEADS * MLA_NOPE_DIM
    hr = MLA_ROPE_DIM // 2
    x1 = qm[nq:nq + MLA_HEADS * hr].reshape(MLA_HEADS, hr, tm)
    x2 = qm[nq + MLA_HEADS * hr:nq + 2 * MLA_HEADS * hr].reshape(MLA_HEADS, hr, tm)
    cm_t, sm_t = cm[None], sm[None]
    o1 = (x1 * cm_t - x2 * sm_t) * scale_mla
    o2 = (x2 * cm_t + x1 * sm_t) * scale_mla
    nd = MLA_NOPE_DIM
    q_pad = jnp.zeros((MLA_QK - nd - MLA_ROPE_DIM, tm), BF16)
    for hd in range(MLA_HEADS):
        qtm_ref[0, hd, 0:nd, :] = (qm[hd * nd:(hd + 1) * nd] * scale_mla).astype(BF16)
        qtm_ref[0, hd, nd:nd + hr, :] = o1[hd].astype(BF16)
        qtm_ref[0, hd, nd + hr:nd + 2 * hr, :] = o2[hd].astype(BF16)
        qtm_ref[0, hd, nd + 2 * hr:MLA_QK, :] = q_pad

    hq = NSA_HEADS * half_n
    q1 = tr[TR_Q:TR_Q + hq].reshape(NSA_HEADS, half_n, tm)
    q2 = tr[TR_Q + hq:TR_Q + 2 * hq].reshape(NSA_HEADS, half_n, tm)
    cn_t, sn_t = cn[None], sn[None]
    qt_ref[0, 0:hq, :] = ((q1 * cn_t - q2 * sn_t) * scale_nsa).reshape(hq, tm).astype(BF16)
    qt_ref[0, hq:2 * hq, :] = ((q2 * cn_t + q1 * sn_t) * scale_nsa).reshape(hq, tm).astype(BF16)

    vts = tr[TR_VS:TR_VS + NSA_KV_WIDTH].astype(BF16)
    vtw = tr[TR_VW:TR_VW + NSA_KV_WIDTH].astype(BF16)
    dk = NSA_HEAD_DIM
    gr = dk + ONES_ROWS
    for ii in range(tm // tk):
        for g in range(NSA_KV_GROUPS):
            vts_ref[0, ii, g * gr:g * gr + dk, :] = vts[g * dk:(g + 1) * dk, ii * tk:(ii + 1) * tk]
            vtw_ref[0, ii, g * gr:g * gr + dk, :] = vtw[g * dk:(g + 1) * dk, ii * tk:(ii + 1) * tk]
            vts_ref[0, ii, g * gr + dk:(g + 1) * gr, :] = ones
            vtw_ref[0, ii, g * gr + dk:(g + 1) * gr, :] = ones
    gt_ref[0] = jax.nn.sigmoid(tr[TR_G:TR_G + NSA_KV_GROUPS * GATE_ROWS])
    ztn_ref[0] = _silu(tr[TR_ZN:TR_ZN + NSA_WIDTH]).astype(BF16)
    ztm_ref[0] = _silu(tr[TR_ZM:TR_ZM + MLA_WIDTH]).astype(BF16)


def _proj(x, mod3, ng, wtok, wtr, pos_row, inv_nb, inv_mb, qng, kvng, wqt, wkn, wv, *, tm, tk):
    b, s, d = x.shape
    nt = s // tk
    v_rows_n = NSA_KV_GROUPS * (NSA_HEAD_DIM + ONES_ROWS)
    v_rows_m = MLA_HEADS * (MLA_V_DIM + ONES_ROWS)

    def full(a):
        return pl.BlockSpec(a.shape, lambda bi, i, _n=a.ndim: (0,) * _n)

    in_specs = [pl.BlockSpec((1, tm, d), lambda bi, i: (bi, i, 0)),
                pl.BlockSpec(mod3.shape, lambda bi, i: (0, 0, 0)),
                full(ng), full(wtok), full(wtr),
                pl.BlockSpec((1, 1, tm), lambda bi, i: (bi, 0, i)),
                full(inv_nb), full(inv_mb),
                full(qng), full(kvng), full(wqt), full(wkn), full(wv)]
    out_shape = [
        jax.ShapeDtypeStruct((b, NSA_WIDTH, s), BF16),
        jax.ShapeDtypeStruct((b, NSA_KV_GROUPS, s, LANES), BF16),
        jax.ShapeDtypeStruct((b, NSA_KV_GROUPS, s, LANES), BF16),
        jax.ShapeDtypeStruct((b, nt, v_rows_n, tk), BF16),
        jax.ShapeDtypeStruct((b, nt, v_rows_n, tk), BF16),
        jax.ShapeDtypeStruct((b, s, NSA_KV_WIDTH), F32),
        jax.ShapeDtypeStruct((b, s, NSA_KV_WIDTH), F32),
        jax.ShapeDtypeStruct((b, NSA_KV_GROUPS * GATE_ROWS, s), F32),
        jax.ShapeDtypeStruct((b, NSA_WIDTH, s), BF16),
        jax.ShapeDtypeStruct((b, MLA_WIDTH, s), BF16),
        jax.ShapeDtypeStruct((b, MLA_HEADS, MLA_QK, s), BF16),
        jax.ShapeDtypeStruct((b, MLA_HEADS, s, MLA_QK), BF16),
        jax.ShapeDtypeStruct((b, nt, v_rows_m, tk), BF16),
    ]
    out_specs = [
        pl.BlockSpec((1, NSA_WIDTH, tm), lambda bi, i: (bi, 0, i)),
        pl.BlockSpec((1, NSA_KV_GROUPS, tm, LANES), lambda bi, i: (bi, 0, i, 0)),
        pl.BlockSpec((1, NSA_KV_GROUPS, tm, LANES), lambda bi, i: (bi, 0, i, 0)),
        pl.BlockSpec((1, tm // tk, v_rows_n, tk), lambda bi, i: (bi, i, 0, 0)),
        pl.BlockSpec((1, tm // tk, v_rows_n, tk), lambda bi, i: (bi, i, 0, 0)),
        pl.BlockSpec((1, tm, NSA_KV_WIDTH), lambda bi, i: (bi, i, 0)),
        pl.BlockSpec((1, tm, NSA_KV_WIDTH), lambda bi, i: (bi, i, 0)),
        pl.BlockSpec((1, NSA_KV_GROUPS * GATE_ROWS, tm), lambda bi, i: (bi, 0, i)),
        pl.BlockSpec((1, NSA_WIDTH, tm), lambda bi, i: (bi, 0, i)),
        pl.BlockSpec((1, MLA_WIDTH, tm), lambda bi, i: (bi, 0, i)),
        pl.BlockSpec((1, MLA_HEADS, MLA_QK, tm), lambda bi, i: (bi, 0, 0, i)),
        pl.BlockSpec((1, MLA_HEADS, tm, MLA_QK), lambda bi, i: (bi, 0, i, 0)),
        pl.BlockSpec((1, tm // tk, v_rows_m, tk), lambda bi, i: (bi, i, 0, 0)),
    ]
    kern = functools.partial(_proj_kernel, tm=tm, tk=tk, scale_nsa=NSA_HEAD_DIM ** -0.5 * LOG2E,
                             scale_mla=(MLA_NOPE_DIM + MLA_ROPE_DIM) ** -0.5 * LOG2E)
    return pl.pallas_call(
        kern, grid=(b, s // tm), in_specs=in_specs, out_specs=out_specs, out_shape=out_shape,
        compiler_params=pltpu.CompilerParams(vmem_limit_bytes=VMEM_LIMIT_BYTES),
        name="proj",
    )(x, mod3, ng, wtok, wtr, pos_row, inv_nb, inv_mb, qng, kvng, wqt, wkn, wv)


def _compress_kernel(k_ref, v_ref, pemb_ref, wk1_ref, wv1_ref, wk2_ref, wv2t_ref, posc_ref, invn_ref,
                     kc_ref, vct_ref):
    ncp = k_ref.shape[1] // CMP_STRIDE
    lane = lax.broadcasted_iota(I32, (ncp, LANES), 1)

    def hidden(r_ref, w1_ref):
        r = jnp.concatenate([r_ref[0, pl.ds(t, ncp, stride=CMP_STRIDE), :] for t in range(CMP_STRIDE)], axis=1)
        a = jnp.dot((r + pemb_ref[0]).astype(BF16), w1_ref[0], preferred_element_type=F32)
        bt = jnp.dot((r + pemb_ref[1]).astype(BF16), w1_ref[1], preferred_element_type=F32)
        return _silu(a + pltpu.roll(bt, ncp - 1, axis=0))

    ang = invn_ref[...] * posc_ref[0]
    cn, sn = jnp.cos(ang), jnp.sin(ang)
    zpad = jnp.zeros((LANES - NSA_HEAD_DIM, ncp), F32)
    cc = jnp.concatenate([cn, cn, zpad], axis=0).T
    sc = jnp.concatenate([-sn, sn, zpad], axis=0).T
    hk = hidden(k_ref, wk1_ref)
    hv = hidden(v_ref, wv1_ref)
    for g in range(NSA_KV_GROUPS):
        hkg = hk[:, g * CMP_HIDDEN:(g + 1) * CMP_HIDDEN].astype(BF16)
        kc = jnp.dot(hkg, wk2_ref[...], preferred_element_type=F32)
        kc_ref[0, g] = _rope_tok(kc, cc, sc, NSA_HEAD_DIM // 2, lane).astype(BF16)
        hvg = hv[:, g * CMP_HIDDEN:(g + 1) * CMP_HIDDEN].astype(BF16)
        vct_ref[0, g] = lax.dot_general(wv2t_ref[...], hvg, NT, preferred_element_type=F32).astype(BF16)


def _compress(kcmp, vcmp, pemb, wk1, wv1, wk2, wv2t, pos_c, inv_nb):
    b, s, width = kcmp.shape
    ncp = s // CMP_STRIDE

    def full(a):
        return pl.BlockSpec(a.shape, lambda bi, _n=a.ndim: (0,) * _n)

    blk = pl.BlockSpec((1, s, width), lambda bi: (bi, 0, 0))
    tab = pl.BlockSpec((1, 1, ncp), lambda bi: (bi, 0, 0))
    return pl.pallas_call(
        _compress_kernel, grid=(b,),
        in_specs=[blk, blk, full(pemb), full(wk1), full(wv1), full(wk2), full(wv2t), tab, full(inv_nb)],
        out_specs=[pl.BlockSpec((1, NSA_KV_GROUPS, ncp, LANES), lambda bi: (bi, 0, 0, 0)),
                   pl.BlockSpec((1, NSA_KV_GROUPS, NSA_HEAD_DIM, ncp), lambda bi: (bi, 0, 0, 0))],
        out_shape=[jax.ShapeDtypeStruct((b, NSA_KV_GROUPS, ncp, LANES), BF16),
                   jax.ShapeDtypeStruct((b, NSA_KV_GROUPS, NSA_HEAD_DIM, ncp), BF16)],
        compiler_params=pltpu.CompilerParams(vmem_limit_bytes=VMEM_LIMIT_BYTES),
        name="compress",
    )(kcmp, vcmp, pemb, wk1, wv1, wk2, wv2t, pos_c, inv_nb)


def _nsa_kernel(q1_ref, q2_ref, kc_ref, vct_ref, kaug_ref, vts_ref, kwin_ref, vtw_ref, g_ref, z_ref, mt_ref,
                o_ref, qaug_ref, sa_ref, sb_ref, m_ref, acc_ref, tot_ref, imp_ref, rank_ref, *, tq, tk, n_sel):
    i = pl.program_id(1)
    nh, ng, hpg, dk, half = NSA_HEADS, NSA_KV_GROUPS, NSA_HPG, NSA_HEAD_DIM, NSA_HEAD_DIM // 2
    vr = dk + ONES_ROWS
    group = [h // hpg for h in range(nh)]
    for h in range(nh):
        qaug_ref[h, 0:half, :] = q1_ref[0, h * half:(h + 1) * half, :]
        qaug_ref[h, half:dk, :] = q2_ref[0, h * half:(h + 1) * half, :]

    def gate(h, branch):
        row = group[h] * GATE_ROWS + (h % hpg) * N_BRANCH + branch
        return g_ref[0, row:row + 1, :]

    row_k = lax.broadcasted_iota(I32, (tk, tq), 0)
    col_q = lax.broadcasted_iota(I32, (tk, tq), 1)
    causal = row_k <= col_q

    def reset():
        m_ref[...] = jnp.full_like(m_ref, NEG_INF)
        acc_ref[...] = jnp.zeros_like(acc_ref)

    def add_branch(branch):
        for h in range(nh):
            inv_l = 1.0 / acc_ref[h, dk:dk + 1, :]
            tot_ref[h] = tot_ref[h] + (gate(h, branch) * inv_l) * acc_ref[h, 0:dk, :]

    def q_cols(h):
        return qaug_ref[h, 0:dk, :]

    ncp = kc_ref.shape[2]
    t_row = i * tq + lax.broadcasted_iota(I32, (1, tq), 1)
    last_n = (t_row - (CMP_BLOCK - 1)) >> CMP_SHIFT
    valid = lax.broadcasted_iota(I32, (ncp, tq), 0) <= last_n
    col_ok = last_n >= 0
    p_heads = {}

    def cmp_update(h, s_c):
        s_c = jnp.where(valid, s_c, NEG_INF)
        e = jnp.exp2(s_c - jnp.max(s_c, axis=0, keepdims=True))
        l_c = jnp.sum(e, axis=0, keepdims=True)
        p_c = e * jnp.where(col_ok, 1.0 / l_c, 0.0)
        o_c = jnp.dot(vct_ref[0, group[h]], p_c.astype(BF16), preferred_element_type=F32)
        tot_ref[h] = gate(h, 0) * o_c
        p_heads[h] = p_c

    nb = mt_ref.shape[0]

    def importance(g, _):
        psum = functools.reduce(lambda a, b: a + b, [p_heads[h] for h in range(nh) if group[h] == g])
        hi = psum.astype(BF16)
        lo = (psum - hi.astype(F32)).astype(BF16)
        mt = mt_ref[...]
        imp = jnp.dot(mt, hi, preferred_element_type=F32) + jnp.dot(mt, lo, preferred_element_type=F32)
        j_idx = lax.broadcasted_iota(I32, (nb, tq), 0)
        cur = (i * tq + lax.broadcasted_iota(I32, (nb, tq), 1)) >> SLC_SHIFT
        forced = (j_idx == 0) | (j_idx == cur) | (j_idx == cur - 1)
        imp_ref[g] = jnp.where(forced, FORCED_SCORE, jnp.where(j_idx > cur, -FORCED_SCORE, imp))

    stages = []
    for g in range(ng):
        kc = kc_ref[0, g, :, 0:dk]
        stages += [(functools.partial(jnp.dot, kc, q_cols(h), preferred_element_type=F32),
                    functools.partial(cmp_update, h)) for h in range(nh) if group[h] == g]
        stages.append((None, functools.partial(importance, g)))

    reset()
    n_back = WINDOW // tk
    for back in range(n_back + 1):
        jb = jnp.maximum(i - back, 0)
        if back == 0:
            keep = causal
        elif back == n_back:
            keep = (row_k > col_q) & (i >= back)
        else:
            keep = jnp.broadcast_to(i >= back, (tk, tq))
        for g in range(ng):
            kt_b = kwin_ref[0, g, pl.ds(pl.multiple_of(jb * tk, tk), tk), 0:dk]
            vt_b = vtw_ref[0, jb, g * vr:(g + 1) * vr, :]
            for h in range(nh):
                if group[h] == g:
                    stages.append((functools.partial(jnp.dot, kt_b, q_cols(h), preferred_element_type=F32),
                                   functools.partial(_chain_update, v_t=vt_b, m_ref=m_ref, acc_ref=acc_ref,
                                                     ch=h, keep=keep)))
    _pipeline(stages, lookahead=5)
    add_branch(2)

    rank_ref[...] = jnp.zeros_like(rank_ref)
    sub = lax.broadcasted_iota(I32, (SUBLANES, tq), 0)
    last_group = ((i + 1) * (tq // SLC_BLOCK) - 1) // SUBLANES

    def count(g, c, v):
        blk = imp_ref[g, v * SUBLANES:(v + 1) * SUBLANES, :]
        cnt = rank_ref[g, v * SUBLANES:(v + 1) * SUBLANES, :]
        for rr in range(SUBLANES):
            row = imp_ref[g, c * SUBLANES + rr:c * SUBLANES + rr + 1, :]
            if c < v:
                beats = row >= blk
            elif c > v:
                beats = row > blk
            else:
                beats = (row > blk) | ((row == blk) & (sub > rr))
            cnt = cnt + beats.astype(I32)
        rank_ref[g, v * SUBLANES:(v + 1) * SUBLANES, :] = cnt

    for lvl in range(nb // SUBLANES):
        @pl.when(lvl <= last_group)
        def _(lvl=lvl):
            for g in range(ng):
                for v in range(lvl + 1):
                    count(g, lvl, v)
                for c in range(lvl):
                    count(g, c, lvl)

    for g in range(ng):
        bias = jnp.where(rank_ref[g] < n_sel, 0.0, SEL_BIAS).astype(BF16)
        for h in range(nh):
            if group[h] == g:
                qaug_ref[h, dk:dk + nb, :] = bias

    reset()
    _causal_sweep(lambda j, g: kaug_ref[0, g, pl.ds(pl.multiple_of(j * tk, tk), tk), :],
                  lambda j, g: vts_ref[0, j, g * vr:(g + 1) * vr, :],
                  group, group, qaug_ref, sa_ref, sb_ref, m_ref, acc_ref, i, causal)
    add_branch(1)

    for h in range(nh):
        zz = z_ref[0, h * dk:(h + 1) * dk, :].astype(F32)
        o_ref[0, h * dk:(h + 1) * dk, :] = (tot_ref[h] * zz).astype(BF16)


def _nsa(qt, kc, vct, kaug, vts, kwin, vtw, gt, ztn, mt, *, tq, tk, n_sel):
    b, _, s = qt.shape
    nt = s // tk
    ncp = kc.shape[2]
    ng, nh = NSA_KV_GROUPS, NSA_HEADS
    hq = nh * (NSA_HEAD_DIM // 2)
    nb = mt.shape[0]
    vr = NSA_HEAD_DIM + ONES_ROWS
    in_specs = [
        pl.BlockSpec((1, hq, tq), lambda bi, i: (bi, 0, i)),
        pl.BlockSpec((1, hq, tq), lambda bi, i: (bi, 1, i)),
        pl.BlockSpec((1, ng, ncp, LANES), lambda bi, i: (bi, 0, 0, 0)),
        pl.BlockSpec((1, ng, NSA_HEAD_DIM, ncp), lambda bi, i: (bi, 0, 0, 0)),
        pl.BlockSpec((1, ng, s, LANES), lambda bi, i: (bi, 0, 0, 0)),
        pl.BlockSpec((1, nt, ng * vr, tk), lambda bi, i: (bi, 0, 0, 0)),
        pl.BlockSpec((1, ng, s, LANES), lambda bi, i: (bi, 0, 0, 0)),
        pl.BlockSpec((1, nt, ng * vr, tk), lambda bi, i: (bi, 0, 0, 0)),
        pl.BlockSpec((1, ng * GATE_ROWS, tq), lambda bi, i: (bi, 0, i)),
        pl.BlockSpec((1, NSA_WIDTH, tq), lambda bi, i: (bi, 0, i)),
        pl.BlockSpec(mt.shape, lambda bi, i: (0, 0)),
    ]
    kern = functools.partial(_nsa_kernel, tq=tq, tk=tk, n_sel=n_sel)
    return pl.pallas_call(
        kern, grid=(b, s // tq), in_specs=in_specs,
        out_specs=pl.BlockSpec((1, NSA_WIDTH, tq), lambda bi, i: (bi, 0, i)),
        out_shape=jax.ShapeDtypeStruct((b, NSA_WIDTH, s), BF16),
        scratch_shapes=[pltpu.VMEM((nh, NSA_HEAD_DIM + nb, tq), BF16),
                        pltpu.VMEM((nh, tk, tq), F32), pltpu.VMEM((nh, tk, tq), F32),
                        pltpu.VMEM((nh, 1, tq), F32), pltpu.VMEM((nh, vr, tq), F32),
                        pltpu.VMEM((nh, NSA_HEAD_DIM, tq), F32),
                        pltpu.VMEM((ng, nb, tq), F32), pltpu.VMEM((ng, nb, tq), I32)],
        compiler_params=pltpu.CompilerParams(dimension_semantics=("arbitrary", "arbitrary"),
                                             vmem_limit_bytes=VMEM_LIMIT_BYTES),
        name="nsa",
    )(qt, qt, kc, vct, kaug, vts, kwin, vtw, gt, ztn, mt)


def _mla_kernel(q_ref, qn_ref, k_ref, vt_ref, z_ref, o_ref, sa_ref, sb_ref, m_ref, acc_ref, *, tq, tk):
    i = pl.program_id(1)
    dv, vr = MLA_V_DIM, MLA_V_DIM + ONES_ROWS
    m_ref[...] = jnp.full_like(m_ref, NEG_INF)
    acc_ref[...] = jnp.zeros_like(acc_ref)

    causal = lax.broadcasted_iota(I32, (tk, tq), 0) <= lax.broadcasted_iota(I32, (tk, tq), 1)
    heads = list(range(MLA_HEADS))
    _causal_sweep(lambda j, hd: k_ref[0, hd, pl.ds(pl.multiple_of(j * tk, tk), tk), :],
                  lambda j, hd: vt_ref[0, j, hd * vr:(hd + 1) * vr, :],
                  heads, heads, q_ref.at[0], sa_ref, sb_ref, m_ref, acc_ref, i, causal,
                  q_next_ref=qn_ref.at[0], first=i == 0)
    for hd in range(MLA_HEADS):
        o_h = acc_ref[hd, 0:dv, :] * (1.0 / acc_ref[hd, dv:dv + 1, :])
        zz = z_ref[0, hd * dv:(hd + 1) * dv, :].astype(F32)
        o_ref[0, hd * dv:(hd + 1) * dv, :] = (o_h * zz).astype(BF16)


def _mla(qtm, kmla, vtm, ztm, *, tq, tk):
    b, _, s, _ = kmla.shape
    nt = s // tk
    vrows = MLA_HEADS * (MLA_V_DIM + ONES_ROWS)
    kern = functools.partial(_mla_kernel, tq=tq, tk=tk)
    return pl.pallas_call(
        kern, grid=(b, s // tq),
        in_specs=[pl.BlockSpec((1, MLA_HEADS, MLA_QK, tq), lambda bi, i: (bi, 0, 0, i)),
                  pl.BlockSpec((1, MLA_HEADS, MLA_QK, tq), lambda bi, i: (bi, 0, 0, jnp.minimum(i + 1, s // tq - 1))),
                  pl.BlockSpec((1, MLA_HEADS, s, MLA_QK), lambda bi, i: (bi, 0, 0, 0)),
                  pl.BlockSpec((1, nt, vrows, tk), lambda bi, i: (bi, 0, 0, 0)),
                  pl.BlockSpec((1, MLA_WIDTH, tq), lambda bi, i: (bi, 0, i))],
        out_specs=pl.BlockSpec((1, MLA_WIDTH, tq), lambda bi, i: (bi, 0, i)),
        out_shape=jax.ShapeDtypeStruct((b, MLA_WIDTH, s), BF16),
        scratch_shapes=[pltpu.VMEM((MLA_HEADS, tk, tq), F32), pltpu.VMEM((MLA_HEADS, tk, tq), F32),
                        pltpu.VMEM((MLA_HEADS, 1, tq), F32), pltpu.VMEM((MLA_HEADS, MLA_V_DIM + ONES_ROWS, tq), F32)],
        compiler_params=pltpu.CompilerParams(dimension_semantics=("arbitrary", "arbitrary"),
                                             vmem_limit_bytes=VMEM_LIMIT_BYTES),
        name="mla",
    )(qtm, qtm, kmla, vtm, ztm)


def _out_kernel(x_ref, mn_ref, mm_ref, w_ref, mod_ref, fg_ref, o_ref, *, final):
    y = lax.dot_general(mn_ref[0], w_ref[0:NSA_WIDTH, :], TN, preferred_element_type=F32)
    y = y + lax.dot_general(mm_ref[0], w_ref[NSA_WIDTH:MIX_WIDTH, :], TN, preferred_element_type=F32)
    x2 = x_ref[0] + mod_ref[2, pl.ds(pl.program_id(0), 1), :] * y
    o_ref[0] = _rms(x2, fg_ref[...]) if final else x2


def _out(x, mn, mm, w_out, mod3, fg, *, tm, final):
    b, s, d = x.shape
    return pl.pallas_call(
        functools.partial(_out_kernel, final=final), grid=(b, s // tm),
        in_specs=[pl.BlockSpec((1, tm, d), lambda bi, i: (bi, i, 0)),
                  pl.BlockSpec((1, NSA_WIDTH, tm), lambda bi, i: (bi, 0, i)),
                  pl.BlockSpec((1, MLA_WIDTH, tm), lambda bi, i: (bi, 0, i)),
                  pl.BlockSpec(w_out.shape, lambda bi, i: (0, 0)),
                  pl.BlockSpec(mod3.shape, lambda bi, i: (0, 0, 0)),
                  pl.BlockSpec((1, d), lambda bi, i: (0, 0))],
        out_specs=pl.BlockSpec((1, tm, d), lambda bi, i: (bi, i, 0)),
        out_shape=jax.ShapeDtypeStruct((b, s, d), F32),
        compiler_params=pltpu.CompilerParams(vmem_limit_bytes=VMEM_LIMIT_BYTES),
        name="out_proj",
    )(x, mn, mm, w_out, mod3, fg)


def _cmp_to_slc_t(ncp, nc, nslc, nb):
    start = np.arange(nc)[:, None] * CMP_STRIDE
    bstart = np.arange(nslc)[None, :] * SLC_BLOCK
    ov = np.minimum(start + CMP_BLOCK, bstart + SLC_BLOCK) - np.maximum(start, bstart)
    m = (np.clip(ov, 0, None) / CMP_BLOCK).astype(np.float32)
    out = np.zeros((nb, ncp), np.float32)
    out[:nslc, :nc] = m.T
    return out


def _layout_w_in(w):
    d = w.shape[0]
    o = (0,) + IN_OFFSETS + (w.shape[1],)
    q_n, ks_n, vs_n, kw_n, kr_m = (w[:, o[k]:o[k + 1]] for k in (0, 3, 4, 5, 11))
    dk = NSA_HEAD_DIM

    def per_group(kx):
        return jnp.pad(kx.reshape(d, NSA_KV_GROUPS, dk), ((0, 0), (0, 0), (0, LANES - dk))).reshape(d, -1)

    ks_kr = jnp.concatenate([ks_n[:, :dk], jnp.pad(kr_m, ((0, 0), (0, LANES - dk - MLA_ROPE_DIM))),
                             per_group(ks_n)[:, LANES:]], axis=1)
    wtok = jnp.concatenate([ks_kr, kw_n, w[:, o[1]:o[3]], w[:, o[9]:o[11]]], axis=1)
    qr = q_n.reshape(d, NSA_HEADS, 2, dk // 2)
    q_perm = jnp.swapaxes(qr, 1, 2).reshape(d, -1)
    wtr = jnp.concatenate([q_perm, vs_n, w[:, o[6]:o[9]], w[:, o[12]:o[13]]], axis=1).T
    assert wtok.shape[1] == TOK_COLS and wtr.shape[0] == TR_ROWS
    return wtok.T.astype(BF16), wtr.astype(BF16)


def _layout_w1(w1):
    hid = w1.shape[1]
    ng = NSA_KV_GROUPS
    w1r = w1.reshape(2, CMP_STRIDE, 1, NSA_HEAD_DIM, hid)
    blocks = [jnp.pad(w1r, ((0, 0), (0, 0), (0, 0), (0, 0), (g * hid, (ng - 1 - g) * hid))) for g in range(ng)]
    full = jnp.concatenate(blocks, axis=2)
    return full.reshape(2, CMP_STRIDE * NSA_KV_WIDTH, ng * hid).astype(BF16)


def kernel(x, c, positions, ada_w, ada_b, norm_g, w_in, cmp_pos, cmp_k_w1, cmp_k_w2, cmp_v_w1, cmp_v_w2,
           q_norm_g, w_q_up, kv_norm_g, w_kv_up, w_out, final_norm_g):
    b, s, d = x.shape
    depth = ada_w.shape[0]
    tm, tq = PROJ_TILE, ATT_TILE
    tk = tq
    assert s % tm == 0 and tm % tk == 0 and WINDOW % tk == 0 and (tq & (tq - 1)) == 0 and s % OUT_TILE == 0
    assert CMP_BLOCK == 2 * CMP_STRIDE and s % SLC_BLOCK == 0
    nslc = s // SLC_BLOCK
    nb = LANES - NSA_HEAD_DIM
    assert nslc <= nb
    ncp = s // CMP_STRIDE
    nc = ncp - 1

    half_n, half_m = NSA_HEAD_DIM // 2, MLA_ROPE_DIM // 2
    inv_n = ROPE_THETA ** (-jnp.arange(half_n, dtype=F32) / half_n)
    inv_m = ROPE_THETA ** (-jnp.arange(half_m, dtype=F32) / half_m)
    inv_nb, inv_mb = inv_n[:, None], inv_m[:, None]
    pos_f = positions.astype(F32)
    pos_row = pos_f.reshape(b, 1, s)
    pos_c = jnp.concatenate([pos_f[:, CMP_BLOCK - 1::CMP_STRIDE], pos_f[:, s - 1:]], axis=1).reshape(b, 1, ncp)

    mt = jnp.asarray(_cmp_to_slc_t(ncp, nc, nslc, nb), dtype=BF16)

    for l in range(depth):
        mod3 = _adaln(c, ada_w[l], ada_b[l].reshape(1, -1))
        wtok, wtr = _layout_w_in(w_in[l])
        wq = w_q_up[l].reshape(MLA_Q_RANK, MLA_HEADS, MLA_NOPE_DIM + MLA_ROPE_DIM)
        wqt = jnp.concatenate([wq[:, :, :MLA_NOPE_DIM].reshape(MLA_Q_RANK, -1),
                               wq[:, :, MLA_NOPE_DIM:MLA_NOPE_DIM + half_m].reshape(MLA_Q_RANK, -1),
                               wq[:, :, MLA_NOPE_DIM + half_m:].reshape(MLA_Q_RANK, -1)], axis=1).T.astype(BF16)
        wkv = w_kv_up[l].reshape(MLA_KV_RANK, MLA_HEADS, MLA_NOPE_DIM + MLA_V_DIM)
        wkn = jnp.pad(wkv[:, :, :MLA_NOPE_DIM], ((0, 0), (0, 0), (0, LANES - MLA_NOPE_DIM))).reshape(
            MLA_KV_RANK, MLA_HEADS * LANES).astype(BF16)
        wv = wkv[:, :, MLA_NOPE_DIM:].transpose(1, 2, 0).reshape(MLA_WIDTH, MLA_KV_RANK).astype(BF16)

        (qt, kaug, kwin, vts, vtw, kcmp, vcmp, gt, ztn, ztm, qtm, kmla, vtm) = _proj(
            x, mod3, norm_g[l].reshape(1, d), wtok, wtr, pos_row, inv_nb, inv_mb,
            q_norm_g[l].reshape(1, -1), kv_norm_g[l].reshape(1, -1), wqt, wkn, wv, tm=tm, tk=tk)

        pos_l = cmp_pos[l]
        pemb = jnp.broadcast_to(pos_l.reshape(2, CMP_STRIDE, 1, NSA_HEAD_DIM),
                                (2, CMP_STRIDE, NSA_KV_GROUPS, NSA_HEAD_DIM)).reshape(2, 1, -1)
        wk2 = jnp.pad(cmp_k_w2[l], ((0, 0), (0, LANES - NSA_HEAD_DIM))).astype(BF16)
        kc, vct = _compress(kcmp, vcmp, pemb,
                            _layout_w1(cmp_k_w1[l]), _layout_w1(cmp_v_w1[l]), wk2,
                            cmp_v_w2[l].T.astype(BF16), pos_c, inv_nb)

        mix_n = _nsa(qt, kc, vct, kaug, vts, kwin, vtw, gt, ztn, mt, tq=tq, tk=tk, n_sel=min(SLC_TOPK, nslc))
        mix_m = _mla(qtm, kmla, vtm, ztm, tq=tq, tk=tk)
        x = _out(x, mix_n, mix_m, w_out[l].astype(BF16), mod3, final_norm_g.reshape(1, d), tm=OUT_TILE,
                 final=(l == depth - 1))
    return x
```

```python
import functools

import numpy as np
import jax
import jax.numpy as jnp
from jax import lax
from jax.experimental import pallas as pl
from jax.experimental.pallas import tpu as pltpu

F32 = jnp.float32
BF16 = jnp.bfloat16
I32 = jnp.int32

NSA_HEADS = 8
NSA_KV_GROUPS = 2
NSA_HPG = NSA_HEADS // NSA_KV_GROUPS
NSA_HEAD_DIM = 64
NSA_WIDTH = NSA_HEADS * NSA_HEAD_DIM
NSA_KV_WIDTH = NSA_KV_GROUPS * NSA_HEAD_DIM
CMP_BLOCK = 32
CMP_STRIDE = 16
CMP_HIDDEN = 128
SLC_BLOCK = 64
SLC_TOPK = 16
WINDOW = 512
N_BRANCH = 3
FORCED_SCORE = 1.0e4
MLA_HEADS = 8
MLA_NOPE_DIM = 64
MLA_ROPE_DIM = 32
MLA_V_DIM = 64
MLA_WIDTH = MLA_HEADS * MLA_V_DIM
MLA_Q_RANK = 256
MLA_KV_RANK = 128
MIX_WIDTH = NSA_WIDTH + MLA_WIDTH
ROPE_THETA = 10000.0
NORM_EPS = 1e-6
NEG_INF = -1e30
IN_SIZES = (NSA_WIDTH, NSA_KV_WIDTH, NSA_KV_WIDTH, NSA_KV_WIDTH, NSA_KV_WIDTH, NSA_KV_WIDTH, NSA_KV_WIDTH,
            NSA_HEADS * N_BRANCH, NSA_WIDTH, MLA_Q_RANK, MLA_KV_RANK, MLA_ROPE_DIM, MLA_WIDTH)
IN_OFFSETS = tuple(int(o) for o in np.cumsum(IN_SIZES)[:-1])

LANES = 128
SUBLANES = 8
VMEM_LIMIT_BYTES = 56 * 1024 * 1024

PROJ_TILE = 1024
OUT_TILE = 1024
ATT_TILE = 256
SEL_BIAS = NEG_INF
GATE_ROWS = NSA_HPG * N_BRANCH
ONES_ROWS = 16
MLA_QK = LANES
LOG2E = float(np.log2(np.e))
SWEEP_UNROLL = 4
SLC_SHIFT = SLC_BLOCK.bit_length() - 1
CMP_SHIFT = CMP_STRIDE.bit_length() - 1
assert 1 << SLC_SHIFT == SLC_BLOCK and 1 << CMP_SHIFT == CMP_STRIDE

NT = (((1,), (1,)), ((), ()))
TN = (((0,), (0,)), ((), ()))


def _silu(v):
    return v * jax.nn.sigmoid(v)


def _rms(v, g):
    ms = jnp.mean(v * v, axis=-1, keepdims=True)
    return v * lax.rsqrt(ms + NORM_EPS) * g


def _rope_tok(v, c, s_signed, half, lane, base=0):
    up = pltpu.roll(v, LANES - half, axis=1)
    dn = pltpu.roll(v, half, axis=1)
    return v * c + jnp.where(lane < base + half, up, dn) * s_signed


def _chain_update(s_t, v_t, m_ref, acc_ref, ch, keep=None):
    if keep is not None:
        s_t = jnp.where(keep, s_t, NEG_INF)
    m_prev = m_ref[ch]
    m_new = jnp.maximum(m_prev, jnp.max(s_t, axis=0, keepdims=True))
    alpha = jnp.exp2(m_prev - m_new)
    p = jnp.exp2(s_t - m_new)
    acc_ref[ch] = alpha * acc_ref[ch] + jnp.dot(v_t, p.astype(BF16), preferred_element_type=F32)
    m_ref[ch] = m_new


def _causal_sweep(k_tile, v_tile, k_group, v_group, q_ref, sa_ref, sb_ref, m_ref, acc_ref, last, keep_last,
                  lookahead=2, q_next_ref=None, first=None):
    n_chains = len(k_group)

    def loader(tile_fn, j):
        cache = {}
        return lambda g: cache.setdefault(g, tile_fn(j, g))

    def phase(j_next, s_next_ref, s_cur_ref, j_cur, keep=None, ahead=False):
        k_next = loader(k_tile, j_next) if j_next is not None else None
        k_zero = loader(k_tile, 0) if ahead else None
        v_cur = loader(v_tile, j_cur)
        for n in range(n_chains + lookahead):
            if k_next is not None and n < n_chains:
                s_next_ref[n] = jnp.dot(k_next(k_group[n]), q_ref[n], preferred_element_type=F32)
            if n >= lookahead:
                ch = n - lookahead
                _chain_update(s_cur_ref[ch], v_cur(v_group[ch]), m_ref, acc_ref, ch, keep=keep)
                if ahead:
                    sa_ref[ch] = jnp.dot(k_zero(k_group[ch]), q_next_ref[ch], preferred_element_type=F32)

    bufs = (sa_ref, sb_ref)

    def prologue():
        k_0 = loader(k_tile, 0)
        for ch in range(n_chains):
            sa_ref[ch] = jnp.dot(k_0(k_group[ch]), q_ref[ch], preferred_element_type=F32)

    if q_next_ref is None:
        prologue()
    else:
        pl.when(first)(prologue)

    def run(j0, count):
        for u in range(count):
            phase(j0 + u + 1, bufs[(u + 1) % 2], bufs[u % 2], j0 + u)

    def body(jj, carry):
        run(SWEEP_UNROLL * jj, SWEEP_UNROLL)
        return carry

    lax.fori_loop(0, last // SWEEP_UNROLL, body, 0)
    rem = last % SWEEP_UNROLL
    for r in range(SWEEP_UNROLL):
        @pl.when(rem == r)
        def _(r=r):
            run(last - r, r)
            phase(None, None, bufs[r % 2], last, keep=keep_last, ahead=q_next_ref is not None)


def _pipeline(stages, lookahead):
    pending = {}
    for n in range(len(stages) + lookahead):
        if n < len(stages) and stages[n][0] is not None:
            pending[n] = stages[n][0]()
        if n >= lookahead:
            stages[n - lookahead][1](pending.pop(n - lookahead, None))


def _adaln_kernel(c_ref, w_ref, b_ref, o_ref):
    sc = _silu(c_ref[...])
    o_ref[...] = jnp.dot(sc.astype(BF16), w_ref[...].astype(BF16), preferred_element_type=F32) + b_ref[...]


def _adaln(c, w, b):
    bsz, d = c.shape
    n = w.shape[1] // d
    return pl.pallas_call(
        _adaln_kernel,
        grid=(n,),
        in_specs=[pl.BlockSpec((bsz, d), lambda j: (0, 0)),
                  pl.BlockSpec((d, d), lambda j: (0, j)),
                  pl.BlockSpec((1, d), lambda j: (0, j))],
        out_specs=pl.BlockSpec((None, bsz, d), lambda j: (j, 0, 0)),
        out_shape=jax.ShapeDtypeStruct((n, bsz, d), F32),
        name="adaln",
    )(c, w, b)


TOK_KS, TOK_KW, TOK_KC, TOK_VC, TOK_CQ, TOK_CKV, TOK_COLS = (int(o) for o in np.cumsum(
    [0, NSA_KV_GROUPS * LANES, NSA_KV_WIDTH, NSA_KV_WIDTH, NSA_KV_WIDTH, MLA_Q_RANK, MLA_KV_RANK]))
TOK_KR = TOK_KS
KR_LANE = NSA_HEAD_DIM
assert NSA_KV_WIDTH == LANES and KR_LANE + MLA_ROPE_DIM <= LANES
TR_Q, TR_VS, TR_VW, TR_G, TR_ZN, TR_ZM, TR_ROWS = (int(o) for o in np.cumsum(
    [0, NSA_WIDTH, NSA_KV_WIDTH, NSA_KV_WIDTH, NSA_KV_GROUPS * GATE_ROWS, NSA_WIDTH, MLA_WIDTH]))


def _proj_kernel(x_ref, mod_ref, ng_ref, wtok_ref, wtr_ref, posr_ref,
                 invn_ref, invm_ref, qng_ref, kvng_ref, wqt_ref, wkn_ref, wv_ref,
                 qt_ref, kaug_ref, kwin_ref, vts_ref, vtw_ref, kcmp_ref, vcmp_ref, gt_ref,
                 ztn_ref, ztm_ref, qtm_ref, kmla_ref, vtm_ref, *, tm, tk, scale_nsa, scale_mla):
    i = pl.program_id(1)
    bi = pl.program_id(0)
    shift, scale = mod_ref[0, pl.ds(bi, 1), :], mod_ref[1, pl.ds(bi, 1), :]
    h = _rms(x_ref[0], ng_ref[...]) * (1.0 + scale) + shift
    hb = h.astype(BF16)
    tok = lax.dot_general(hb, wtok_ref[...], NT, preferred_element_type=F32)
    tr = lax.dot_general(wtr_ref[...], hb, NT, preferred_element_type=F32)

    lane = lax.broadcasted_iota(I32, (tm, LANES), 1)
    row = lax.broadcasted_iota(I32, (tm, LANES), 0)
    blk = (i * tm + row) >> SLC_SHIFT
    onehot = (lane - NSA_HEAD_DIM == blk).astype(F32)
    ang_n = invn_ref[...] * posr_ref[0]
    cn, sn = jnp.cos(ang_n), jnp.sin(ang_n)
    ang_m = invm_ref[...] * posr_ref[0]
    cm, sm = jnp.cos(ang_m), jnp.sin(ang_m)
    zpad = jnp.zeros((LANES - NSA_HEAD_DIM - MLA_ROPE_DIM, tm), F32)
    ct = jnp.concatenate([cn, cn, cm, cm, zpad], axis=0).T
    st = jnp.concatenate([-sn, sn, -sm, sm, zpad], axis=0).T
    half_n = NSA_HEAD_DIM // 2
    for g in range(NSA_KV_GROUPS):
        ks = _rope_tok(tok[:, TOK_KS + LANES * g:TOK_KS + LANES * (g + 1)], ct, st, half_n, lane)
        kaug_ref[0, g] = jnp.where(lane >= NSA_HEAD_DIM, onehot, ks).astype(BF16)
    ctw = jnp.concatenate([cn, cn] * NSA_KV_GROUPS, axis=0).T
    stw = jnp.concatenate([-sn, sn] * NSA_KV_GROUPS, axis=0).T
    kwp = tok[:, TOK_KW:TOK_KW + LANES]
    first_half = (lane & (NSA_HEAD_DIM - 1)) < half_n
    kw = kwp * ctw + jnp.where(first_half, pltpu.roll(kwp, LANES - half_n, axis=1),
                               pltpu.roll(kwp, half_n, axis=1)) * stw
    for g in range(NSA_KV_GROUPS):
        kwin_ref[0, g] = (kw if g == 0 else pltpu.roll(kw, LANES - g * NSA_HEAD_DIM, axis=1)).astype(BF16)
    kcmp_ref[0] = tok[:, TOK_KC:TOK_KC + LANES]
    vcmp_ref[0] = tok[:, TOK_VC:TOK_VC + LANES]

    ckvn = _rms(tok[:, TOK_CKV:TOK_CKV + MLA_KV_RANK], kvng_ref[...])
    krr = _rope_tok(tok[:, TOK_KR:TOK_KR + LANES], ct, st, MLA_ROPE_DIM // 2, lane, base=KR_LANE)
    ckvb = ckvn.astype(BF16)
    kn = jnp.dot(ckvb, wkn_ref[...], preferred_element_type=F32)
    is_rot = (lane >= KR_LANE) & (lane < KR_LANE + MLA_ROPE_DIM)
    for hd in range(MLA_HEADS):
        kmla_ref[0, hd] = jnp.where(is_rot, krr, kn[:, hd * LANES:(hd + 1) * LANES]).astype(BF16)
    vtm = lax.dot_general(wv_ref[...], ckvb, NT, preferred_element_type=F32).astype(BF16)
    ones = jnp.ones((ONES_ROWS, tk), BF16)
    vr = MLA_V_DIM + ONES_ROWS
    for ii in range(tm // tk):
        for hd in range(MLA_HEADS):
            vtm_ref[0, ii, hd * vr:hd * vr + MLA_V_DIM, :] = vtm[hd * MLA_V_DIM:(hd + 1) * MLA_V_DIM,
                                                                 ii * tk:(ii + 1) * tk]
            vtm_ref[0, ii, hd * vr + MLA_V_DIM:(hd + 1) * vr, :] = ones

    cqn = _rms(tok[:, TOK_CQ:TOK_CQ + MLA_Q_RANK], qng_ref[...]).astype(BF16)
    qm = lax.dot_general(wqt_ref[...], cqn, NT, preferred_element_type=F32)
    nq = MLA_HEADS * MLA_NOPE_DIM
    hr = MLA_ROPE_DIM // 2
    x1 = qm[nq:nq + MLA_HEADS * hr].reshape(MLA_HEADS, hr, tm)
    x2 = qm[nq + MLA_HEADS * hr:nq + 2 * MLA_HEADS * hr].reshape(MLA_HEADS, hr, tm)
    cm_t, sm_t = cm[None], sm[None]
    o1 = (x1 * cm_t - x2 * sm_t) * scale_mla
    o2 = (x2 * cm_t + x1 * sm_t) * scale_mla
    nd = MLA_NOPE_DIM
    q_pad = jnp.zeros((MLA_QK - nd - MLA_ROPE_DIM, tm), BF16)
    for hd in range(MLA_HEADS):
        qtm_ref[0, hd, 0:nd, :] = (qm[hd * nd:(hd + 1) * nd] * scale_mla).astype(BF16)
        qtm_ref[0, hd, nd:nd + hr, :] = o1[hd].astype(BF16)
        qtm_ref[0, hd, nd + hr:nd + 2 * hr, :] = o2[hd].astype(BF16)
        qtm_ref[0, hd, nd + 2 * hr:MLA_QK, :] = q_pad

    hq = NSA_HEADS * half_n
    q1 = tr[TR_Q:TR_Q + hq].reshape(NSA_HEADS, half_n, tm)
    q2 = tr[TR_Q + hq:TR_Q + 2 * hq].reshape(NSA_HEADS, half_n, tm)
    cn_t, sn_t = cn[None], sn[None]
    qt_ref[0, 0:hq, :] = ((q1 * cn_t - q2 * sn_t) * scale_nsa).reshape(hq, tm).astype(BF16)
    qt_ref[0, hq:2 * hq, :] = ((q2 * cn_t + q1 * sn_t) * scale_nsa).reshape(hq, tm).astype(BF16)

    vts = tr[TR_VS:TR_VS + NSA_KV_WIDTH].astype(BF16)
    vtw = tr[TR_VW:TR_VW + NSA_KV_WIDTH].astype(BF16)
    dk = NSA_HEAD_DIM
    gr = dk + ONES_ROWS
    for ii in range(tm // tk):
        for g in range(NSA_KV_GROUPS):
            vts_ref[0, ii, g * gr:g * gr + dk, :] = vts[g * dk:(g + 1) * dk, ii * tk:(ii + 1) * tk]
            vtw_ref[0, ii, g * gr:g * gr + dk, :] = vtw[g * dk:(g + 1) * dk, ii * tk:(ii + 1) * tk]
            vts_ref[0, ii, g * gr + dk:(g + 1) * gr, :] = ones
            vtw_ref[0, ii, g * gr + dk:(g + 1) * gr, :] = ones
    gt_ref[0] = jax.nn.sigmoid(tr[TR_G:TR_G + NSA_KV_GROUPS * GATE_ROWS])
    ztn_ref[0] = _silu(tr[TR_ZN:TR_ZN + NSA_WIDTH]).astype(BF16)
    ztm_ref[0] = _silu(tr[TR_ZM:TR_ZM + MLA_WIDTH]).astype(BF16)


def _proj(x, mod3, ng, wtok, wtr, pos_row, inv_nb, inv_mb, qng, kvng, wqt, wkn, wv, *, tm, tk):
    b, s, d = x.shape
    nt = s // tk
    v_rows_n = NSA_KV_GROUPS * (NSA_HEAD_DIM + ONES_ROWS)
    v_rows_m = MLA_HEADS * (MLA_V_DIM + ONES_ROWS)

    def full(a):
        return pl.BlockSpec(a.shape, lambda bi, i, _n=a.ndim: (0,) * _n)

    in_specs = [pl.BlockSpec((1, tm, d), lambda bi, i: (bi, i, 0)),
                pl.BlockSpec(mod3.shape, lambda bi, i: (0, 0, 0)),
                full(ng), full(wtok), full(wtr),
                pl.BlockSpec((1, 1, tm), lambda bi, i: (bi, 0, i)),
                full(inv_nb), full(inv_mb),
                full(qng), full(kvng), full(wqt), full(wkn), full(wv)]
    out_shape = [
        jax.ShapeDtypeStruct((b, NSA_WIDTH, s), BF16),
        jax.ShapeDtypeStruct((b, NSA_KV_GROUPS, s, LANES), BF16),
        jax.ShapeDtypeStruct((b, NSA_KV_GROUPS, s, LANES), BF16),
        jax.ShapeDtypeStruct((b, nt, v_rows_n, tk), BF16),
        jax.ShapeDtypeStruct((b, nt, v_rows_n, tk), BF16),
        jax.ShapeDtypeStruct((b, s, NSA_KV_WIDTH), F32),
        jax.ShapeDtypeStruct((b, s, NSA_KV_WIDTH), F32),
        jax.ShapeDtypeStruct((b, NSA_KV_GROUPS * GATE_ROWS, s), F32),
        jax.ShapeDtypeStruct((b, NSA_WIDTH, s), BF16),
        jax.ShapeDtypeStruct((b, MLA_WIDTH, s), BF16),
        jax.ShapeDtypeStruct((b, MLA_HEADS, MLA_QK, s), BF16),
        jax.ShapeDtypeStruct((b, MLA_HEADS, s, MLA_QK), BF16),
        jax.ShapeDtypeStruct((b, nt, v_rows_m, tk), BF16),
    ]
    out_specs = [
        pl.BlockSpec((1, NSA_WIDTH, tm), lambda bi, i: (bi, 0, i)),
        pl.BlockSpec((1, NSA_KV_GROUPS, tm, LANES), lambda bi, i: (bi, 0, i, 0)),
        pl.BlockSpec((1, NSA_KV_GROUPS, tm, LANES), lambda bi, i: (bi, 0, i, 0)),
        pl.BlockSpec((1, tm // tk, v_rows_n, tk), lambda bi, i: (bi, i, 0, 0)),
        pl.BlockSpec((1, tm // tk, v_rows_n, tk), lambda bi, i: (bi, i, 0, 0)),
        pl.BlockSpec((1, tm, NSA_KV_WIDTH), lambda bi, i: (bi, i, 0)),
        pl.BlockSpec((1, tm, NSA_KV_WIDTH), lambda bi, i: (bi, i, 0)),
        pl.BlockSpec((1, NSA_KV_GROUPS * GATE_ROWS, tm), lambda bi, i: (bi, 0, i)),
        pl.BlockSpec((1, NSA_WIDTH, tm), lambda bi, i: (bi, 0, i)),
        pl.BlockSpec((1, MLA_WIDTH, tm), lambda bi, i: (bi, 0, i)),
        pl.BlockSpec((1, MLA_HEADS, MLA_QK, tm), lambda bi, i: (bi, 0, 0, i)),
        pl.BlockSpec((1, MLA_HEADS, tm, MLA_QK), lambda bi, i: (bi, 0, i, 0)),
        pl.BlockSpec((1, tm // tk, v_rows_m, tk), lambda bi, i: (bi, i, 0, 0)),
    ]
    kern = functools.partial(_proj_kernel, tm=tm, tk=tk, scale_nsa=NSA_HEAD_DIM ** -0.5 * LOG2E,
                             scale_mla=(MLA_NOPE_DIM + MLA_ROPE_DIM) ** -0.5 * LOG2E)
    return pl.pallas_call(
        kern, grid=(b, s // tm), in_specs=in_specs, out_specs=out_specs, out_shape=out_shape,
        compiler_params=pltpu.CompilerParams(vmem_limit_bytes=VMEM_LIMIT_BYTES),
        name="proj",
    )(x, mod3, ng, wtok, wtr, pos_row, inv_nb, inv_mb, qng, kvng, wqt, wkn, wv)


def _compress_kernel(k_ref, v_ref, pemb_ref, wk1_ref, wv1_ref, wk2_ref, wv2t_ref, posc_ref, invn_ref,
                     kc_ref, vct_ref):
    ncp = k_ref.shape[1] // CMP_STRIDE
    lane = lax.broadcasted_iota(I32, (ncp, LANES), 1)

    def hidden(r_ref, w1_ref):
        r = jnp.concatenate([r_ref[0, pl.ds(t, ncp, stride=CMP_STRIDE), :] for t in range(CMP_STRIDE)], axis=1)
        a = jnp.dot((r + pemb_ref[0]).astype(BF16), w1_ref[0], preferred_element_type=F32)
        bt = jnp.dot((r + pemb_ref[1]).astype(BF16), w1_ref[1], preferred_element_type=F32)
        return _silu(a + pltpu.roll(bt, ncp - 1, axis=0))

    ang = invn_ref[...] * posc_ref[0]
    cn, sn = jnp.cos(ang), jnp.sin(ang)
    zpad = jnp.zeros((LANES - NSA_HEAD_DIM, ncp), F32)
    cc = jnp.concatenate([cn, cn, zpad], axis=0).T
    sc = jnp.concatenate([-sn, sn, zpad], axis=0).T
    hk = hidden(k_ref, wk1_ref)
    hv = hidden(v_ref, wv1_ref)
    for g in range(NSA_KV_GROUPS):
        hkg = hk[:, g * CMP_HIDDEN:(g + 1) * CMP_HIDDEN].astype(BF16)
        kc = jnp.dot(hkg, wk2_ref[...], preferred_element_type=F32)
        kc_ref[0, g] = _rope_tok(kc, cc, sc, NSA_HEAD_DIM // 2, lane).astype(BF16)
        hvg = hv[:, g * CMP_HIDDEN:(g + 1) * CMP_HIDDEN].astype(BF16)
        vct_ref[0, g] = lax.dot_general(wv2t_ref[...], hvg, NT, preferred_element_type=F32).astype(BF16)


def _compress(kcmp, vcmp, pemb, wk1, wv1, wk2, wv2t, pos_c, inv_nb):
    b, s, width = kcmp.shape
    ncp = s // CMP_STRIDE

    def full(a):
        return pl.BlockSpec(a.shape, lambda bi, _n=a.ndim: (0,) * _n)

    blk = pl.BlockSpec((1, s, width), lambda bi: (bi, 0, 0))
    tab = pl.BlockSpec((1, 1, ncp), lambda bi: (bi, 0, 0))
    return pl.pallas_call(
        _compress_kernel, grid=(b,),
        in_specs=[blk, blk, full(pemb), full(wk1), full(wv1), full(wk2), full(wv2t), tab, full(inv_nb)],
        out_specs=[pl.BlockSpec((1, NSA_KV_GROUPS, ncp, LANES), lambda bi: (bi, 0, 0, 0)),
                   pl.BlockSpec((1, NSA_KV_GROUPS, NSA_HEAD_DIM, ncp), lambda bi: (bi, 0, 0, 0))],
        out_shape=[jax.ShapeDtypeStruct((b, NSA_KV_GROUPS, ncp, LANES), BF16),
                   jax.ShapeDtypeStruct((b, NSA_KV_GROUPS, NSA_HEAD_DIM, ncp), BF16)],
        compiler_params=pltpu.CompilerParams(vmem_limit_bytes=VMEM_LIMIT_BYTES),
        name="compress",
    )(kcmp, vcmp, pemb, wk1, wv1, wk2, wv2t, pos_c, inv_nb)


def _nsa_kernel(q1_ref, q2_ref, kc_ref, vct_ref, kaug_ref, vts_ref, kwin_ref, vtw_ref, g_ref, z_ref, mt_ref,
                o_ref, qaug_ref, sa_ref, sb_ref, m_ref, acc_ref, tot_ref, imp_ref, rank_ref, *, tq, tk, n_sel):
    i = pl.program_id(1)
    nh, ng, hpg, dk, half = NSA_HEADS, NSA_KV_GROUPS, NSA_HPG, NSA_HEAD_DIM, NSA_HEAD_DIM // 2
    vr = dk + ONES_ROWS
    group = [h // hpg for h in range(nh)]
    for h in range(nh):
        qaug_ref[h, 0:half, :] = q1_ref[0, h * half:(h + 1) * half, :]
        qaug_ref[h, half:dk, :] = q2_ref[0, h * half:(h + 1) * half, :]

    def gate(h, branch):
        row = group[h] * GATE_ROWS + (h % hpg) * N_BRANCH + branch
        return g_ref[0, row:row + 1, :]

    row_k = lax.broadcasted_iota(I32, (tk, tq), 0)
    col_q = lax.broadcasted_iota(I32, (tk, tq), 1)
    causal = row_k <= col_q

    def reset():
        m_ref[...] = jnp.full_like(m_ref, NEG_INF)
        acc_ref[...] = jnp.zeros_like(acc_ref)

    def add_branch(branch):
        for h in range(nh):
            inv_l = 1.0 / acc_ref[h, dk:dk + 1, :]
            tot_ref[h] = tot_ref[h] + (gate(h, branch) * inv_l) * acc_ref[h, 0:dk, :]

    def q_cols(h):
        return qaug_ref[h, 0:dk, :]

    ncp = kc_ref.shape[2]
    t_row = i * tq + lax.broadcasted_iota(I32, (1, tq), 1)
    last_n = (t_row - (CMP_BLOCK - 1)) >> CMP_SHIFT
    valid = lax.broadcasted_iota(I32, (ncp, tq), 0) <= last_n
    col_ok = last_n >= 0
    p_heads = {}

    def cmp_update(h, s_c):
        s_c = jnp.where(valid, s_c, NEG_INF)
        e = jnp.exp2(s_c - jnp.max(s_c, axis=0, keepdims=True))
        l_c = jnp.sum(e, axis=0, keepdims=True)
        p_c = e * jnp.where(col_ok, 1.0 / l_c, 0.0)
        o_c = jnp.dot(vct_ref[0, group[h]], p_c.astype(BF16), preferred_element_type=F32)
        tot_ref[h] = gate(h, 0) * o_c
        p_heads[h] = p_c

    nb = mt_ref.shape[0]

    def importance(g, _):
        psum = functools.reduce(lambda a, b: a + b, [p_heads[h] for h in range(nh) if group[h] == g])
        hi = psum.astype(BF16)
        lo = (psum - hi.astype(F32)).astype(BF16)
        mt = mt_ref[...]
        imp = jnp.dot(mt, hi, preferred_element_type=F32) + jnp.dot(mt, lo, preferred_element_type=F32)
        j_idx = lax.broadcasted_iota(I32, (nb, tq), 0)
        cur = (i * tq + lax.broadcasted_iota(I32, (nb, tq), 1)) >> SLC_SHIFT
        forced = (j_idx == 0) | (j_idx == cur) | (j_idx == cur - 1)
        imp_ref[g] = jnp.where(forced, FORCED_SCORE, jnp.where(j_idx > cur, -FORCED_SCORE, imp))

    stages = []
    for g in range(ng):
        kc = kc_ref[0, g, :, 0:dk]
        stages += [(functools.partial(jnp.dot, kc, q_cols(h), preferred_element_type=F32),
                    functools.partial(cmp_update, h)) for h in range(nh) if group[h] == g]
        stages.append((None, functools.partial(importance, g)))

    reset()
    n_back = WINDOW // tk
    for back in range(n_back + 1):
        jb = jnp.maximum(i - back, 0)
        if back == 0:
            keep = causal
        elif back == n_back:
            keep = (row_k > col_q) & (i >= back)
        else:
            keep = jnp.broadcast_to(i >= back, (tk, tq))
        for g in range(ng):
            kt_b = kwin_ref[0, g, pl.ds(pl.multiple_of(jb * tk, tk), tk), 0:dk]
            vt_b = vtw_ref[0, jb, g * vr:(g + 1) * vr, :]
            for h in range(nh):
                if group[h] == g:
                    stages.append((functools.partial(jnp.dot, kt_b, q_cols(h), preferred_element_type=F32),
                                   functools.partial(_chain_update, v_t=vt_b, m_ref=m_ref, acc_ref=acc_ref,
                                                     ch=h, keep=keep)))
    _pipeline(stages, lookahead=5)
    add_branch(2)

    rank_ref[...] = jnp.zeros_like(rank_ref)
    sub = lax.broadcasted_iota(I32, (SUBLANES, tq), 0)
    last_group = ((i + 1) * (tq // SLC_BLOCK) - 1) // SUBLANES

    def count(g, c, v):
        blk = imp_ref[g, v * SUBLANES:(v + 1) * SUBLANES, :]
        cnt = rank_ref[g, v * SUBLANES:(v + 1) * SUBLANES, :]
        for rr in range(SUBLANES):
            row = imp_ref[g, c * SUBLANES + rr:c * SUBLANES + rr + 1, :]
            if c < v:
                beats = row >= blk
            elif c > v:
                beats = row > blk
            else:
                beats = (row > blk) | ((row == blk) & (sub > rr))
            cnt = cnt + beats.astype(I32)
        rank_ref[g, v * SUBLANES:(v + 1) * SUBLANES, :] = cnt

    for lvl in range(nb // SUBLANES):
        @pl.when(lvl <= last_group)
        def _(lvl=lvl):
            for g in range(ng):
                for v in range(lvl + 1):
                    count(g, lvl, v)
                for c in range(lvl):
                    count(g, c, lvl)

    for g in range(ng):
        bias = jnp.where(rank_ref[g] < n_sel, 0.0, SEL_BIAS).astype(BF16)
        for h in range(nh):
            if group[h] == g:
                qaug_ref[h, dk:dk + nb, :] = bias

    reset()
    _causal_sweep(lambda j, g: kaug_ref[0, g, pl.ds(pl.multiple_of(j * tk, tk), tk), :],
                  lambda j, g: vts_ref[0, j, g * vr:(g + 1) * vr, :],
                  group, group, qaug_ref, sa_ref, sb_ref, m_ref, acc_ref, i, causal)
    add_branch(1)

    for h in range(nh):
        zz = z_ref[0, h * dk:(h + 1) * dk, :].astype(F32)
        o_ref[0, h * dk:(h + 1) * dk, :] = (tot_ref[h] * zz).astype(BF16)


def _nsa(qt, kc, vct, kaug, vts, kwin, vtw, gt, ztn, mt, *, tq, tk, n_sel):
    b, _, s = qt.shape
    nt = s // tk
    ncp = kc.shape[2]
    ng, nh = NSA_KV_GROUPS, NSA_HEADS
    hq = nh * (NSA_HEAD_DIM // 2)
    nb = mt.shape[0]
    vr = NSA_HEAD_DIM + ONES_ROWS
    in_specs = [
        pl.BlockSpec((1, hq, tq), lambda bi, i: (bi, 0, i)),
        pl.BlockSpec((1, hq, tq), lambda bi, i: (bi, 1, i)),
        pl.BlockSpec((1, ng, ncp, LANES), lambda bi, i: (bi, 0, 0, 0)),
        pl.BlockSpec((1, ng, NSA_HEAD_DIM, ncp), lambda bi, i: (bi, 0, 0, 0)),
        pl.BlockSpec((1, ng, s, LANES), lambda bi, i: (bi, 0, 0, 0)),
        pl.BlockSpec((1, nt, ng * vr, tk), lambda bi, i: (bi, 0, 0, 0)),
        pl.BlockSpec((1, ng, s, LANES), lambda bi, i: (bi, 0, 0, 0)),
        pl.BlockSpec((1, nt, ng * vr, tk), lambda bi, i: (bi, 0, 0, 0)),
        pl.BlockSpec((1, ng * GATE_ROWS, tq), lambda bi, i: (bi, 0, i)),
        pl.BlockSpec((1, NSA_WIDTH, tq), lambda bi, i: (bi, 0, i)),
        pl.BlockSpec(mt.shape, lambda bi, i: (0, 0)),
    ]
    kern = functools.partial(_nsa_kernel, tq=tq, tk=tk, n_sel=n_sel)
    return pl.pallas_call(
        kern, grid=(b, s // tq), in_specs=in_specs,
        out_specs=pl.BlockSpec((1, NSA_WIDTH, tq), lambda bi, i: (bi, 0, i)),
        out_shape=jax.ShapeDtypeStruct((b, NSA_WIDTH, s), BF16),
        scratch_shapes=[pltpu.VMEM((nh, NSA_HEAD_DIM + nb, tq), BF16),
                        pltpu.VMEM((nh, tk, tq), F32), pltpu.VMEM((nh, tk, tq), F32),
                        pltpu.VMEM((nh, 1, tq), F32), pltpu.VMEM((nh, vr, tq), F32),
                        pltpu.VMEM((nh, NSA_HEAD_DIM, tq), F32),
                        pltpu.VMEM((ng, nb, tq), F32), pltpu.VMEM((ng, nb, tq), I32)],
        compiler_params=pltpu.CompilerParams(dimension_semantics=("arbitrary", "arbitrary"),
                                             vmem_limit_bytes=VMEM_LIMIT_BYTES),
        name="nsa",
    )(qt, qt, kc, vct, kaug, vts, kwin, vtw, gt, ztn, mt)


def _mla_kernel(q_ref, qn_ref, k_ref, vt_ref, z_ref, o_ref, sa_ref, sb_ref, m_ref, acc_ref, *, tq, tk):
    i = pl.program_id(1)
    dv, vr = MLA_V_DIM, MLA_V_DIM + ONES_ROWS
    m_ref[...] = jnp.full_like(m_ref, NEG_INF)
    acc_ref[...] = jnp.zeros_like(acc_ref)

    causal = lax.broadcasted_iota(I32, (tk, tq), 0) <= lax.broadcasted_iota(I32, (tk, tq), 1)
    heads = list(range(MLA_HEADS))
    _causal_sweep(lambda j, hd: k_ref[0, hd, pl.ds(pl.multiple_of(j * tk, tk), tk), :],
                  lambda j, hd: vt_ref[0, j, hd * vr:(hd + 1) * vr, :],
                  heads, heads, q_ref.at[0], sa_ref, sb_ref, m_ref, acc_ref, i, causal,
                  q_next_ref=qn_ref.at[0], first=i == 0)
    for hd in range(MLA_HEADS):
        o_h = acc_ref[hd, 0:dv, :] * (1.0 / acc_ref[hd, dv:dv + 1, :])
        zz = z_ref[0, hd * dv:(hd + 1) * dv, :].astype(F32)
        o_ref[0, hd * dv:(hd + 1) * dv, :] = (o_h * zz).astype(BF16)


def _mla(qtm, kmla, vtm, ztm, *, tq, tk):
    b, _, s, _ = kmla.shape
    nt = s // tk
    vrows = MLA_HEADS * (MLA_V_DIM + ONES_ROWS)
    kern = functools.partial(_mla_kernel, tq=tq, tk=tk)
    return pl.pallas_call(
        kern, grid=(b, s // tq),
        in_specs=[pl.BlockSpec((1, MLA_HEADS, MLA_QK, tq), lambda bi, i: (bi, 0, 0, i)),
                  pl.BlockSpec((1, MLA_HEADS, MLA_QK, tq), lambda bi, i: (bi, 0, 0, jnp.minimum(i + 1, s // tq - 1))),
                  pl.BlockSpec((1, MLA_HEADS, s, MLA_QK), lambda bi, i: (bi, 0, 0, 0)),
                  pl.BlockSpec((1, nt, vrows, tk), lambda bi, i: (bi, 0, 0, 0)),
                  pl.BlockSpec((1, MLA_WIDTH, tq), lambda bi, i: (bi, 0, i))],
        out_specs=pl.BlockSpec((1, MLA_WIDTH, tq), lambda bi, i: (bi, 0, i)),
        out_shape=jax.ShapeDtypeStruct((b, MLA_WIDTH, s), BF16),
        scratch_shapes=[pltpu.VMEM((MLA_HEADS, tk, tq), F32), pltpu.VMEM((MLA_HEADS, tk, tq), F32),
                        pltpu.VMEM((MLA_HEADS, 1, tq), F32), pltpu.VMEM((MLA_HEADS, MLA_V_DIM + ONES_ROWS, tq), F32)],
        compiler_params=pltpu.CompilerParams(dimension_semantics=("arbitrary", "arbitrary"),
                                             vmem_limit_bytes=VMEM_LIMIT_BYTES),
        name="mla",
    )(qtm, qtm, kmla, vtm, ztm)


X_RING = 3


def _out_kernel(x_hbm, mn_ref, mm_ref, w_ref, mod_ref, fg_ref, o_ref, xbuf, sem, *, final, tm, nt, n_steps):
    t = pl.program_id(0) * nt + pl.program_id(1)

    def x_copy(step):
        slot = lax.rem(step, X_RING)
        row = pl.multiple_of(lax.rem(step, nt) * tm, tm)
        return pltpu.make_async_copy(x_hbm.at[lax.div(step, nt), pl.ds(row, tm), :], xbuf.at[slot], sem.at[slot])

    @pl.when(t == 0)
    def _():
        for k in range(min(X_RING - 1, n_steps)):
            x_copy(t + k).start()

    @pl.when(t + (X_RING - 1) < n_steps)
    def _():
        x_copy(t + (X_RING - 1)).start()

    y = lax.dot_general(mn_ref[0], w_ref[0:NSA_WIDTH, :], TN, preferred_element_type=F32)
    y = y + lax.dot_general(mm_ref[0], w_ref[NSA_WIDTH:MIX_WIDTH, :], TN, preferred_element_type=F32)
    x_copy(t).wait()
    x2 = xbuf[lax.rem(t, X_RING)] + mod_ref[2, pl.ds(pl.program_id(0), 1), :] * y
    o_ref[0] = _rms(x2, fg_ref[...]) if final else x2


def _out(x, mn, mm, w_out, mod3, fg, *, tm, final):
    b, s, d = x.shape
    nt = s // tm
    return pl.pallas_call(
        functools.partial(_out_kernel, final=final, tm=tm, nt=nt, n_steps=b * nt), grid=(b, nt),
        in_specs=[pl.BlockSpec(memory_space=pl.ANY),
                  pl.BlockSpec((1, NSA_WIDTH, tm), lambda bi, i: (bi, 0, i)),
                  pl.BlockSpec((1, MLA_WIDTH, tm), lambda bi, i: (bi, 0, i)),
                  pl.BlockSpec(w_out.shape, lambda bi, i: (0, 0)),
                  pl.BlockSpec(mod3.shape, lambda bi, i: (0, 0, 0)),
                  pl.BlockSpec((1, d), lambda bi, i: (0, 0))],
        out_specs=pl.BlockSpec((1, tm, d), lambda bi, i: (bi, i, 0)),
        out_shape=jax.ShapeDtypeStruct((b, s, d), F32),
        scratch_shapes=[pltpu.VMEM((X_RING, tm, d), F32), pltpu.SemaphoreType.DMA((X_RING,))],
        compiler_params=pltpu.CompilerParams(dimension_semantics=("arbitrary", "arbitrary"),
                                             vmem_limit_bytes=VMEM_LIMIT_BYTES),
        name="out_proj",
    )(x, mn, mm, w_out, mod3, fg)


def _cmp_to_slc_t(ncp, nc, nslc, nb):
    start = np.arange(nc)[:, None] * CMP_STRIDE
    bstart = np.arange(nslc)[None, :] * SLC_BLOCK
    ov = np.minimum(start + CMP_BLOCK, bstart + SLC_BLOCK) - np.maximum(start, bstart)
    m = (np.clip(ov, 0, None) / CMP_BLOCK).astype(np.float32)
    out = np.zeros((nb, ncp), np.float32)
    out[:nslc, :nc] = m.T
    return out


def _layout_w_in(w):
    d = w.shape[0]
    o = (0,) + IN_OFFSETS + (w.shape[1],)
    q_n, ks_n, vs_n, kw_n, kr_m = (w[:, o[k]:o[k + 1]] for k in (0, 3, 4, 5, 11))
    dk = NSA_HEAD_DIM

    def per_group(kx):
        return jnp.pad(kx.reshape(d, NSA_KV_GROUPS, dk), ((0, 0), (0, 0), (0, LANES - dk))).reshape(d, -1)

    ks_kr = jnp.concatenate([ks_n[:, :dk], jnp.pad(kr_m, ((0, 0), (0, LANES - dk - MLA_ROPE_DIM))),
                             per_group(ks_n)[:, LANES:]], axis=1)
    wtok = jnp.concatenate([ks_kr, kw_n, w[:, o[1]:o[3]], w[:, o[9]:o[11]]], axis=1)
    qr = q_n.reshape(d, NSA_HEADS, 2, dk // 2)
    q_perm = jnp.swapaxes(qr, 1, 2).reshape(d, -1)
    wtr = jnp.concatenate([q_perm, vs_n, w[:, o[6]:o[9]], w[:, o[12]:o[13]]], axis=1).T
    assert wtok.shape[1] == TOK_COLS and wtr.shape[0] == TR_ROWS
    return wtok.T.astype(BF16), wtr.astype(BF16)


def _layout_w1(w1):
    hid = w1.shape[1]
    ng = NSA_KV_GROUPS
    w1r = w1.reshape(2, CMP_STRIDE, 1, NSA_HEAD_DIM, hid)
    blocks = [jnp.pad(w1r, ((0, 0), (0, 0), (0, 0), (0, 0), (g * hid, (ng - 1 - g) * hid))) for g in range(ng)]
    full = jnp.concatenate(blocks, axis=2)
    return full.reshape(2, CMP_STRIDE * NSA_KV_WIDTH, ng * hid).astype(BF16)


def kernel(x, c, positions, ada_w, ada_b, norm_g, w_in, cmp_pos, cmp_k_w1, cmp_k_w2, cmp_v_w1, cmp_v_w2,
           q_norm_g, w_q_up, kv_norm_g, w_kv_up, w_out, final_norm_g):
    b, s, d = x.shape
    depth = ada_w.shape[0]
    tm, tq = PROJ_TILE, ATT_TILE
    tk = tq
    assert s % tm == 0 and tm % tk == 0 and WINDOW % tk == 0 and (tq & (tq - 1)) == 0 and s % OUT_TILE == 0
    assert CMP_BLOCK == 2 * CMP_STRIDE and s % SLC_BLOCK == 0
    nslc = s // SLC_BLOCK
    nb = LANES - NSA_HEAD_DIM
    assert nslc <= nb
    ncp = s // CMP_STRIDE
    nc = ncp - 1

    half_n, half_m = NSA_HEAD_DIM // 2, MLA_ROPE_DIM // 2
    inv_n = ROPE_THETA ** (-jnp.arange(half_n, dtype=F32) / half_n)
    inv_m = ROPE_THETA ** (-jnp.arange(half_m, dtype=F32) / half_m)
    inv_nb, inv_mb = inv_n[:, None], inv_m[:, None]
    pos_f = positions.astype(F32)
    pos_row = pos_f.reshape(b, 1, s)
    pos_c = jnp.concatenate([pos_f[:, CMP_BLOCK - 1::CMP_STRIDE], pos_f[:, s - 1:]], axis=1).reshape(b, 1, ncp)

    mt = jnp.asarray(_cmp_to_slc_t(ncp, nc, nslc, nb), dtype=BF16)

    for l in range(depth):
        mod3 = _adaln(c, ada_w[l], ada_b[l].reshape(1, -1))
        wtok, wtr = _layout_w_in(w_in[l])
        wq = w_q_up[l].reshape(MLA_Q_RANK, MLA_HEADS, MLA_NOPE_DIM + MLA_ROPE_DIM)
        wqt = jnp.concatenate([wq[:, :, :MLA_NOPE_DIM].reshape(MLA_Q_RANK, -1),
                               wq[:, :, MLA_NOPE_DIM:MLA_NOPE_DIM + half_m].reshape(MLA_Q_RANK, -1),
                               wq[:, :, MLA_NOPE_DIM + half_m:].reshape(MLA_Q_RANK, -1)], axis=1).T.astype(BF16)
        wkv = w_kv_up[l].reshape(MLA_KV_RANK, MLA_HEADS, MLA_NOPE_DIM + MLA_V_DIM)
        wkn = jnp.pad(wkv[:, :, :MLA_NOPE_DIM], ((0, 0), (0, 0), (0, LANES - MLA_NOPE_DIM))).reshape(
            MLA_KV_RANK, MLA_HEADS * LANES).astype(BF16)
        wv = wkv[:, :, MLA_NOPE_DIM:].transpose(1, 2, 0).reshape(MLA_WIDTH, MLA_KV_RANK).astype(BF16)

        (qt, kaug, kwin, vts, vtw, kcmp, vcmp, gt, ztn, ztm, qtm, kmla, vtm) = _proj(
            x, mod3, norm_g[l].reshape(1, d), wtok, wtr, pos_row, inv_nb, inv_mb,
            q_norm_g[l].reshape(1, -1), kv_norm_g[l].reshape(1, -1), wqt, wkn, wv, tm=tm, tk=tk)

        pos_l = cmp_pos[l]
        pemb = jnp.broadcast_to(pos_l.reshape(2, CMP_STRIDE, 1, NSA_HEAD_DIM),
                                (2, CMP_STRIDE, NSA_KV_GROUPS, NSA_HEAD_DIM)).reshape(2, 1, -1)
        wk2 = jnp.pad(cmp_k_w2[l], ((0, 0), (0, LANES - NSA_HEAD_DIM))).astype(BF16)
        kc, vct = _compress(kcmp, vcmp, pemb,
                            _layout_w1(cmp_k_w1[l]), _layout_w1(cmp_v_w1[l]), wk2,
                            cmp_v_w2[l].T.astype(BF16), pos_c, inv_nb)

        mix_n = _nsa(qt, kc, vct, kaug, vts, kwin, vtw, gt, ztn, mt, tq=tq, tk=tk, n_sel=min(SLC_TOPK, nslc))
        mix_m = _mla(qtm, kmla, vtm, ztm, tq=tq, tk=tk)
        x = _out(x, mix_n, mix_m, w_out[l].astype(BF16), mod3, final_norm_g.reshape(1, d), tm=OUT_TILE,
                 final=(l == depth - 1))
    return x
```
